```python
import jax, jax.numpy as jnp
from jax import lax
import numpy as np

D_MODEL = 1024
BATCH = 2
SEQ = 8192
DEPTH = 1

CHUNK = 64
PLE_DIM = 256
RW_HEADS = 8
RW_HEAD_DIM = 64
RW_WIDTH = RW_HEADS * RW_HEAD_DIM
RW_DECAY_LORA = 64
RW_ICLR_LORA = 64
RW_GATE_LORA = 128
RW_GN_EPS = 64e-5
GLA_HEADS = 4
GLA_KEY_DIM = 64
GLA_VAL_DIM = 128
GLA_QK_WIDTH = GLA_HEADS * GLA_KEY_DIM
GLA_V_WIDTH = GLA_HEADS * GLA_VAL_DIM
GLA_GATE_RANK = 16
GLA_GATE_TEMP = 16.0
MIX_WIDTH = RW_WIDTH + GLA_V_WIDTH
RW_COLS = 3 * RW_WIDTH + RW_DECAY_LORA + RW_ICLR_LORA + RW_GATE_LORA
GLA_COLS = 2 * GLA_QK_WIDTH + 2 * GLA_V_WIDTH + GLA_GATE_RANK
IN_COLS = RW_COLS + GLA_COLS
N_EXPERTS = 32
TOP_K = 4
D_FF = 1024
SWIGLU_ALPHA = 1.702
SWIGLU_LIMIT = 7.0
EXPERT_BLOCK = 256
NORM_EPS = 1e-6

kernel_name = "hybrid_rwkv7_gla_moe_ple_block"


def rmsnorm(x, g):
    xf = x.astype(jnp.float32)
    y = xf * lax.rsqrt(jnp.mean(xf * xf, axis=-1, keepdims=True) + NORM_EPS)
    return (y * g.astype(jnp.float32)).astype(x.dtype)


def token_shift(h, mu):
    prev = jnp.pad(h, ((0, 0), (1, 0), (0, 0)))[:, :-1]
    return h + (prev - h) * mu


def rwkv7_mix(h, w0, w2, a0, a2, g2, k_k, k_a, r_k, gn_w, gn_b):
    B, S, _ = h.shape
    H, D = RW_HEADS, RW_HEAD_DIM
    f32 = jnp.float32
    o0 = 0
    r = h[..., o0:o0 + RW_WIDTH]; o0 += RW_WIDTH
    k = h[..., o0:o0 + RW_WIDTH]; o0 += RW_WIDTH
    v = h[..., o0:o0 + RW_WIDTH]; o0 += RW_WIDTH
    dw = h[..., o0:o0 + RW_DECAY_LORA]; o0 += RW_DECAY_LORA
    da = h[..., o0:o0 + RW_ICLR_LORA]; o0 += RW_ICLR_LORA
    dg = h[..., o0:o0 + RW_GATE_LORA]

    w_log = -jax.nn.softplus(-(w0 + jnp.tanh(dw) @ w2)) - 0.5
    decay = jnp.exp(-jnp.exp(w_log.astype(f32)))
    iclr = jax.nn.sigmoid((a0 + da @ a2).astype(f32))
    gate = jax.nn.sigmoid(dg) @ g2

    kk = (k * k_k).astype(f32).reshape(B, S, H, D)
    kk = kk / jnp.maximum(jnp.sqrt(jnp.sum(kk * kk, -1, keepdims=True)), 1e-12)
    k = k.astype(f32) * (1.0 + (iclr - 1.0) * k_a.astype(f32))

    def heads(t):
        return t.astype(f32).reshape(B, S, H, D)

    r4, k4, v4, w4, a4 = heads(r), heads(k), heads(v), heads(decay), heads(iclr)

    def tm(t):
        return jnp.transpose(t, (1, 0, 2, 3))

    def step(state, inp):
        r_t, w_t, k_t, v_t, kk_t, a_t = inp
        sa = jnp.einsum('bhvk,bhk->bhv', state, -kk_t)
        state = (state * w_t[:, :, None, :]
                 + sa[..., None] * (kk_t * a_t)[:, :, None, :]
                 + v_t[..., None] * k_t[:, :, None, :])
        return state, jnp.einsum('bhvk,bhk->bhv', state, r_t)

    s0 = jnp.zeros((B, H, D, D), f32)
    _, o = lax.scan(step, s0, (tm(r4), tm(w4), tm(k4), tm(v4), tm(kk), tm(a4)))
    o = jnp.transpose(o, (1, 0, 2, 3))

    mu = jnp.mean(o, -1, keepdims=True)
    var = jnp.mean(jnp.square(o - mu), -1, keepdims=True)
    o = ((o - mu) * lax.rsqrt(var + RW_GN_EPS)).reshape(B, S, RW_WIDTH)
    o = o * gn_w.astype(f32) + gn_b.astype(f32)
    bonus = jnp.sum(r4 * k4 * r_k.astype(f32), -1, keepdims=True) * v4
    o = (o + bonus.reshape(B, S, RW_WIDTH)) * gate.astype(f32)
    return o.astype(h.dtype)


def gla_mix(h, gk2, gk_b, norm_g):
    B, S, _ = h.shape
    H, DK, DV = GLA_HEADS, GLA_KEY_DIM, GLA_VAL_DIM
    f32 = jnp.float32
    o0 = 0
    q = h[..., o0:o0 + GLA_QK_WIDTH]; o0 += GLA_QK_WIDTH
    k = h[..., o0:o0 + GLA_QK_WIDTH]; o0 += GLA_QK_WIDTH
    v = h[..., o0:o0 + GLA_V_WIDTH]; o0 += GLA_V_WIDTH
    g = h[..., o0:o0 + GLA_V_WIDTH]; o0 += GLA_V_WIDTH
    dgk = h[..., o0:o0 + GLA_GATE_RANK]

    log_a = jax.nn.log_sigmoid((dgk @ gk2 + gk_b).astype(f32)) / GLA_GATE_TEMP
    q = q.astype(f32) * (DK ** -0.5)
    NC = S // CHUNK

    def chunks(t, d):
        return t.astype(f32).reshape(B, NC, CHUNK, H, d).transpose(1, 0, 3, 2, 4)

    causal = jnp.tril(jnp.ones((CHUNK, CHUNK), bool))[:, :, None]

    def step(state, inp):
        qc, kc, vc, lac = inp
        b = jnp.cumsum(lac, axis=2)
        o_inter = jnp.einsum('bhcd,bhdv->bhcv', qc * jnp.exp(b), state)
        diff = b[:, :, :, None, :] - b[:, :, None, :, :]
        dec = jnp.exp(jnp.where(causal, diff, -jnp.inf))
        att = jnp.einsum('bhid,bhjd,bhijd->bhij', qc, kc, dec)
        o = o_inter + jnp.einsum('bhij,bhjv->bhiv', att, vc)
        b_last = b[:, :, -1:, :]
        state = (jnp.exp(b_last[:, :, 0, :])[..., None] * state
                 + jnp.einsum('bhjd,bhjv->bhdv', kc * jnp.exp(b_last - b), vc))
        return state, o

    s0 = jnp.zeros((B, H, DK, DV), f32)
    _, o = lax.scan(step, s0, (chunks(q, DK), chunks(k, DK), chunks(v, DV), chunks(log_a, DK)))
    o = o.transpose(1, 0, 3, 2, 4).reshape(B, S, H, DV)
    o = o * lax.rsqrt(jnp.mean(o * o, -1, keepdims=True) + NORM_EPS)
    o = o.reshape(B, S, GLA_V_WIDTH) * norm_g.astype(f32) * jax.nn.silu(g.astype(f32))
    return o.astype(h.dtype)


def moe(u, w_router, b_router, w1, b1, w2, b2):
    B, S, D = u.shape
    N = B * S
    A = N * TOP_K
    NB = (A + N_EXPERTS * (EXPERT_BLOCK - 1) + EXPERT_BLOCK - 1) // EXPERT_BLOCK
    xf = u.reshape(N, D)
    logits = (xf @ w_router + b_router).astype(jnp.float32)
    top_v, top_i = lax.top_k(logits, TOP_K)
    gates = jax.nn.softmax(top_v, axis=-1).astype(xf.dtype)

    flat_e = top_i.reshape(-1).astype(jnp.int32)
    flat_tok = jnp.arange(A, dtype=jnp.int32) // TOP_K
    flat_w = gates.reshape(-1)
    order = jnp.argsort(flat_e, stable=True)
    e_s, tok_s, w_s = flat_e[order], flat_tok[order], flat_w[order]

    counts = jax.ops.segment_sum(jnp.ones((A,), jnp.int32), flat_e, num_segments=N_EXPERTS)
    start = jnp.cumsum(counts) - counts
    pcounts = ((counts + EXPERT_BLOCK - 1) // EXPERT_BLOCK) * EXPERT_BLOCK
    pend = jnp.cumsum(pcounts)
    pstart = pend - pcounts
    dest = pstart[e_s] + (jnp.arange(A, dtype=jnp.int32) - start[e_s])

    tok_buf = jnp.zeros((NB * EXPERT_BLOCK,), jnp.int32).at[dest].set(tok_s)
    w_buf = jnp.zeros((NB * EXPERT_BLOCK,), xf.dtype).at[dest].set(w_s)
    block_rows = jnp.arange(NB, dtype=jnp.int32) * EXPERT_BLOCK
    block_e = jnp.clip(jnp.searchsorted(pend, block_rows, side='right'), 0, N_EXPERTS - 1)

    def expert_block(args):
        tok_b, w_b, e = args
        xb = xf[tok_b]
        hgl = xb @ w1[e] + b1[e]
        x_glu = jnp.minimum(hgl[:, :D_FF], SWIGLU_LIMIT)
        x_lin = jnp.clip(hgl[:, D_FF:], -SWIGLU_LIMIT, SWIGLU_LIMIT)
        act = (x_lin + 1.0) * (x_glu * jax.nn.sigmoid(SWIGLU_ALPHA * x_glu))
        return (act @ w2[e] + b2[e]) * w_b[:, None]

    ys = lax.map(expert_block, (tok_buf.reshape(NB, EXPERT_BLOCK),
                                w_buf.reshape(NB, EXPERT_BLOCK), block_e))
    out = jnp.zeros_like(xf).at[tok_buf].add(ys.reshape(-1, D))
    return out.reshape(B, S, D)


def setup_inputs(seed: int = 0) -> dict:
    key = jax.random.key(seed)
    ks = jax.random.split(key, 40)
    f32 = jnp.float32

    def nrm(k, shape, scale):
        return jax.random.normal(k, shape, f32) * scale

    def gain(k, shape):
        return 1.0 + 0.05 * jax.random.normal(k, shape, f32)

    L, D, E, F = DEPTH, D_MODEL, N_EXPERTS, D_FF
    return {
        "x": nrm(ks[0], (BATCH, SEQ, D), 1.0),
        "p": nrm(ks[1], (L, BATCH, SEQ, PLE_DIM), 1.0),
        "norm_mix": gain(ks[2], (L, D)),
        "w_in": nrm(ks[3], (L, D, IN_COLS), D ** -0.5),
        "shift_mu": jax.random.uniform(ks[4], (L, RW_COLS), f32),
        "rw_w0": jax.random.uniform(ks[5], (L, RW_WIDTH), f32, -6.0, -1.0),
        "rw_w2": nrm(ks[6], (L, RW_DECAY_LORA, RW_WIDTH), 0.1),
        "rw_a0": nrm(ks[7], (L, RW_WIDTH), 0.5),
        "rw_a2": nrm(ks[8], (L, RW_ICLR_LORA, RW_WIDTH), 0.5 * RW_ICLR_LORA ** -0.5),
        "rw_g2": nrm(ks[9], (L, RW_GATE_LORA, RW_WIDTH), RW_GATE_LORA ** -0.5),
        "rw_kk": 0.85 + 0.05 * jax.random.normal(ks[10], (L, RW_WIDTH), f32),
        "rw_ka": gain(ks[11], (L, RW_WIDTH)),
        "rw_rk": nrm(ks[12], (L, RW_HEADS, RW_HEAD_DIM), 0.1),
        "rw_gn_w": gain(ks[13], (L, RW_WIDTH)),
        "rw_gn_b": nrm(ks[14], (L, RW_WIDTH), 0.01),
        "gla_gk2": nrm(ks[15], (L, GLA_GATE_RANK, GLA_QK_WIDTH), GLA_GATE_RANK ** -0.5),
        "gla_gk_b": nrm(ks[16], (L, GLA_QK_WIDTH), 0.1),
        "gla_norm": gain(ks[17], (L, GLA_V_WIDTH)),
        "w_out": nrm(ks[18], (L, MIX_WIDTH, D), MIX_WIDTH ** -0.5),
        "norm_ffn": gain(ks[19], (L, D)),
        "w_router": nrm(ks[20], (L, D, E), D ** -0.5),
        "b_router": nrm(ks[21], (L, E), 0.01),
        "w1": nrm(ks[22], (L, E, D, 2 * F), D ** -0.5),
        "b1": nrm(ks[23], (L, E, 2 * F), 0.01),
        "w2": nrm(ks[24], (L, E, F, D), F ** -0.5),
        "b2": nrm(ks[25], (L, E, D), 0.01),
        "norm_ple": gain(ks[26], (L, D)),
        "w_ple_gate": nrm(ks[27], (L, D, D), D ** -0.5),
        "w_ple": nrm(ks[28], (L, PLE_DIM, D), PLE_DIM ** -0.5),
        "norm_final": gain(ks[29], (D,)),
    }


def reference(x, p, norm_mix, w_in, shift_mu, rw_w0, rw_w2, rw_a0, rw_a2, rw_g2,
              rw_kk, rw_ka, rw_rk, rw_gn_w, rw_gn_b, gla_gk2, gla_gk_b, gla_norm,
              w_out, norm_ffn, w_router, b_router, w1, b1, w2, b2,
              norm_ple, w_ple_gate, w_ple, norm_final):
    h = x
    for i in range(DEPTH):
        u = rmsnorm(h, norm_mix[i])
        proj = u @ w_in[i]
        rw_in = token_shift(proj[..., :RW_COLS], shift_mu[i])
        y_rw = rwkv7_mix(rw_in, rw_w0[i], rw_w2[i], rw_a0[i], rw_a2[i], rw_g2[i],
                         rw_kk[i], rw_ka[i], rw_rk[i], rw_gn_w[i], rw_gn_b[i])
        y_gla = gla_mix(proj[..., RW_COLS:], gla_gk2[i], gla_gk_b[i], gla_norm[i])
        h = h + jnp.concatenate([y_rw, y_gla], axis=-1) @ w_out[i]
        u = rmsnorm(h, norm_ffn[i])
        h = h + moe(u, w_router[i], b_router[i], w1[i], b1[i], w2[i], b2[i])
        u = rmsnorm(h, norm_ple[i])
        h = h + jax.nn.sigmoid(u @ w_ple_gate[i]) * (p[i] @ w_ple[i])
    return rmsnorm(h, norm_final)
```

```python
import functools

import jax
import jax.numpy as jnp
from jax import lax
from jax.experimental import pallas as pl
from jax.experimental.pallas import tpu as pltpu

F32 = jnp.float32
BF16 = jnp.bfloat16

CHUNK = 64
RW_HEADS = 8
RW_HEAD_DIM = 64
RW_WIDTH = RW_HEADS * RW_HEAD_DIM
RW_DECAY_LORA = 64
RW_ICLR_LORA = 64
RW_GATE_LORA = 128
RW_COLS = 3 * RW_WIDTH + RW_DECAY_LORA + RW_ICLR_LORA + RW_GATE_LORA
RW_GN_EPS = 64e-5
GLA_HEADS = 4
GLA_KEY_DIM = 64
GLA_VAL_DIM = 128
GLA_QK_WIDTH = GLA_HEADS * GLA_KEY_DIM
GLA_V_WIDTH = GLA_HEADS * GLA_VAL_DIM
GLA_GATE_RANK = 16
GLA_GATE_TEMP = 16.0
GLA_SUB = 16
LANES = 128
GLA_COLS_PAD = 2 * GLA_QK_WIDTH + 2 * GLA_V_WIDTH + LANES
TOP_K = 4
EXPERT_BLOCK = 256
SWIGLU_ALPHA = 1.702
SWIGLU_LIMIT = 7.0
NORM_EPS = 1e-6
NEG_BIG = -1e30
VMEM_LIMIT = 56 * 1024 * 1024


def _mm(a, b):
    return jnp.dot(a.astype(BF16), b.astype(BF16), preferred_element_type=F32)


def _mm_nt(a, b):
    return lax.dot_general(a.astype(BF16), b.astype(BF16), (((1,), (1,)), ((), ())),
                           preferred_element_type=F32)


def _mm_tn(a, b):
    return lax.dot_general(a.astype(BF16), b.astype(BF16), (((0,), (0,)), ((), ())),
                           preferred_element_type=F32)


def _split(a, n):
    parts = []
    rem = a
    for _ in range(n):
        p = rem.astype(BF16)
        parts.append(p)
        rem = rem - p.astype(F32)
    return parts


def _mm_lhs_split(a, b_bf16, n):
    out = None
    for p in _split(a, n):
        t = jnp.dot(p, b_bf16, preferred_element_type=F32)
        out = t if out is None else out + t
    return out


def _mm_rhs_split(a_bf16, b, n):
    out = None
    for p in _split(b, n):
        t = jnp.dot(a_bf16, p, preferred_element_type=F32)
        out = t if out is None else out + t
    return out


def _rmsnorm(x, g):
    return x * lax.rsqrt(jnp.mean(x * x, axis=-1, keepdims=True) + NORM_EPS) * g


def _softplus(x):
    return jnp.maximum(x, 0.0) + jnp.log(1.0 + jnp.exp(-jnp.abs(x)))


def _sigmoid(x):
    return 1.0 / (1.0 + jnp.exp(-x))


def _tri(n, strict):
    r = lax.broadcasted_iota(jnp.int32, (n, n), 0)
    c = lax.broadcasted_iota(jnp.int32, (n, n), 1)
    return (r > c) if strict else (r >= c)


def _inproj_kernel(x_ref, g_ref, wr_ref, wg_ref, rw_ref, gla_ref):
    u = _rmsnorm(x_ref[...], g_ref[...]).astype(BF16)
    rw_ref[...] = jnp.dot(u, wr_ref[...], preferred_element_type=F32)
    gla_ref[...] = jnp.dot(u, wg_ref[...], preferred_element_type=F32)


def _in_proj(xf, g, w_rw, w_gla, tm):
    n, d = xf.shape
    return pl.pallas_call(
        _inproj_kernel,
        grid=(n // tm,),
        in_specs=[
            pl.BlockSpec((tm, d), lambda i: (i, 0)),
            pl.BlockSpec((1, d), lambda i: (0, 0)),
            pl.BlockSpec(w_rw.shape, lambda i: (0, 0)),
            pl.BlockSpec(w_gla.shape, lambda i: (0, 0)),
        ],
        out_specs=[
            pl.BlockSpec((tm, w_rw.shape[1]), lambda i: (i, 0)),
            pl.BlockSpec((tm, w_gla.shape[1]), lambda i: (i, 0)),
        ],
        out_shape=[
            jax.ShapeDtypeStruct((n, w_rw.shape[1]), F32),
            jax.ShapeDtypeStruct((n, w_gla.shape[1]), F32),
        ],
        compiler_params=pltpu.CompilerParams(
            dimension_semantics=("arbitrary",), vmem_limit_bytes=VMEM_LIMIT),
        name="in_proj",
    )(xf, g, w_rw, w_gla)


def _rwkv_kernel(x_ref, mu_ref, w0_ref, w2_ref, a0_ref, a2_ref, g2_ref, kkw_ref, ka_ref,
                 rk_ref, gnw_ref, gnb_ref, bd_ref, y_ref, state_ref, carry_ref, o_ref):
    C, H, D = CHUNK, RW_HEADS, RW_HEAD_DIM

    @pl.when(pl.program_id(1) == 0)
    def _():
        state_ref[...] = jnp.zeros_like(state_ref)
        carry_ref[...] = jnp.zeros_like(carry_ref)

    x = x_ref[0]
    row = lax.broadcasted_iota(jnp.int32, x.shape, 0)
    prev = jnp.where(row == 0, carry_ref[...], pltpu.roll(x, 1, axis=0))
    carry_ref[...] = x[C - 1:C, :]
    h = x + (prev - x) * mu_ref[...]

    W = RW_WIDTH
    r = h[:, 0:W]
    k = h[:, W:2 * W]
    v = h[:, 2 * W:3 * W]
    o0 = 3 * W
    dw = h[:, o0:o0 + RW_DECAY_LORA]
    da = h[:, o0 + RW_DECAY_LORA:o0 + RW_DECAY_LORA + RW_ICLR_LORA]
    dg = h[:, o0 + RW_DECAY_LORA + RW_ICLR_LORA:]

    bd = bd_ref[...]

    def seg_sum(t):
        return _mm_lhs_split(t, bd, 2)

    w_log = -_softplus(-(w0_ref[...] + _mm(jnp.tanh(dw), w2_ref[...]))) - 0.5
    lw = -jnp.exp(w_log)
    iclr = _sigmoid(a0_ref[...] + _mm(da, a2_ref[...]))
    gate = _mm(_sigmoid(dg), g2_ref[...])

    kk = k * kkw_ref[...]
    kk = kk / jnp.maximum(jnp.sqrt(seg_sum(kk * kk)), 1e-12)
    k2 = k * (1.0 + (iclr - 1.0) * ka_ref[...])

    cw = _mm_rhs_split(_tri(C, False).astype(BF16), lw, 3)
    cw_last = cw[C - 1:C, :]
    e_cw = jnp.exp(cw)
    e_ncw = jnp.exp(-cw)
    e_rem = jnp.exp(cw_last - cw)
    kka = kk * iclr
    a_t = -kk * jnp.exp(cw - lw)
    r_t = r * e_cw
    b_t = kka * e_ncw
    k_t = k2 * e_ncw
    b_h = kka * e_rem
    k_h = k2 * e_rem
    w_c = jnp.exp(cw_last)

    strict = _tri(C, True)
    incl = _tri(C, False)
    eye = jnp.where(_tri(C, False) & ~strict, 1.0, 0.0).astype(F32)

    for hd in range(H):
        sl = slice(hd * D, (hd + 1) * D)
        ah, rh, bh, kh, vh = a_t[:, sl], r_t[:, sl], b_t[:, sl], k_t[:, sl], v[:, sl]
        a_ab = jnp.where(strict, _mm_nt(ah, bh), 0.0)
        a_ak = jnp.where(strict, _mm_nt(ah, kh), 0.0)
        a_rb = jnp.where(incl, _mm_nt(rh, bh), 0.0)
        a_rk = jnp.where(incl, _mm_nt(rh, kh), 0.0)
        p = _mm(a_ab, a_ab)
        q = eye + a_ab
        for _ in range(4):
            q, p = q + _mm(p, q), _mm(p, p)
        t_inv = q + _mm(p, q)
        u0 = _mm(t_inv, _mm(a_ak, vh))
        a_hat = _mm(t_inv, ah)
        s = state_ref[hd]
        u = _mm_nt(a_hat, s) + u0
        o_ref[:, sl] = _mm_nt(rh, s) + _mm(a_rb, u) + _mm(a_rk, vh)
        state_ref[hd] = s * w_c[:, sl] + _mm_tn(u, b_h[:, sl]) + _mm_tn(vh, k_h[:, sl])

    o = o_ref[...]
    inv_d = 1.0 / D
    mean = seg_sum(o) * inv_d
    dlt = o - mean
    var = seg_sum(dlt * dlt) * inv_d
    o = dlt * lax.rsqrt(var + RW_GN_EPS) * gnw_ref[...] + gnb_ref[...]
    bonus = seg_sum(r * k2 * rk_ref[...]) * v
    y_ref[0] = ((o + bonus) * gate).astype(y_ref.dtype)


def _rwkv(rw_proj, mu, w0, w2, a0, a2, g2, kkw, ka, rk, gnw, gnb):
    b, s, cols = rw_proj.shape
    W = RW_WIDTH
    bd = jnp.kron(jnp.eye(RW_HEADS, dtype=F32), jnp.ones((RW_HEAD_DIM, RW_HEAD_DIM), F32)).astype(BF16)
    row = lambda a: a.reshape(1, -1)
    full = lambda a: pl.BlockSpec(a.shape, lambda i, j: (0,) * a.ndim)
    args = [row(mu), row(w0), w2.astype(BF16), row(a0), a2.astype(BF16), g2.astype(BF16),
            row(kkw), row(ka), row(rk), row(gnw), row(gnb), bd]
    return pl.pallas_call(
        _rwkv_kernel,
        grid=(b, s // CHUNK),
        in_specs=[pl.BlockSpec((1, CHUNK, cols), lambda i, j: (i, j, 0))] + [full(a) for a in args],
        out_specs=pl.BlockSpec((1, CHUNK, W), lambda i, j: (i, j, 0)),
        out_shape=jax.ShapeDtypeStruct((b, s, W), BF16),
        scratch_shapes=[
            pltpu.VMEM((RW_HEADS, RW_HEAD_DIM, RW_HEAD_DIM), F32),
            pltpu.VMEM((1, cols), F32),
            pltpu.VMEM((CHUNK, W), F32),
        ],
        compiler_params=pltpu.CompilerParams(
            dimension_semantics=("arbitrary", "arbitrary"), vmem_limit_bytes=VMEM_LIMIT),
        name="rwkv7",
    )(rw_proj, *args)


def _gla_kernel(x_ref, gk2_ref, gkb_ref, ng_ref, y_ref, state_ref):
    C, H, DK, DV, SB = CHUNK, GLA_HEADS, GLA_KEY_DIM, GLA_VAL_DIM, GLA_SUB

    @pl.when(pl.program_id(1) == 0)
    def _():
        state_ref[...] = jnp.zeros_like(state_ref)

    x = x_ref[0]
    QW, VW = GLA_QK_WIDTH, GLA_V_WIDTH
    q = x[:, 0:QW] * (DK ** -0.5)
    k = x[:, QW:2 * QW]
    v = x[:, 2 * QW:2 * QW + VW]
    g = x[:, 2 * QW + VW:2 * QW + 2 * VW]
    dgk = x[:, 2 * QW + 2 * VW:]

    la = -_softplus(-(_mm(dgk, gk2_ref[...]) + gkb_ref[...])) * (1.0 / GLA_GATE_TEMP)
    b = _mm_rhs_split(_tri(C, False).astype(BF16), la, 3)
    b_last = b[C - 1:C, :]
    q_e = q * jnp.exp(b)
    k_e = k * jnp.exp(b_last - b)
    w_c = jnp.exp(b_last)

    ii = lax.broadcasted_iota(jnp.int32, (SB, SB, DK), 0)
    jj = lax.broadcasted_iota(jnp.int32, (SB, SB, DK), 1)
    causal3 = jj <= ii

    outs = []
    for hd in range(H):
        sk = slice(hd * DK, (hd + 1) * DK)
        sv = slice(hd * DV, (hd + 1) * DV)
        qh, kh, bh, vh = q[:, sk], k[:, sk], b[:, sk], v[:, sv]
        s = state_ref[hd]
        o_inter = _mm_nt(q_e[:, sk], s)
        rows = []
        for blk in range(C // SB):
            r0 = blk * SB
            qi, ki, bi, vi = qh[r0:r0 + SB], kh[r0:r0 + SB], bh[r0:r0 + SB], vh[r0:r0 + SB]
            dec = jnp.exp(jnp.where(causal3, bi[:, None, :] - bi[None, :, :], NEG_BIG))
            att = jnp.sum(qi[:, None, :] * ki[None, :, :] * dec, axis=-1, keepdims=True)
            o_blk = jnp.sum(att * vi[None, :, :], axis=1)
            if blk > 0:
                ref = bh[r0 - 1:r0]
                q_s = qi * jnp.exp(bi - ref)
                k_s = kh[0:r0] * jnp.exp(ref - bh[0:r0])
                o_blk = o_blk + _mm(_mm_nt(q_s, k_s), vh[0:r0])
            rows.append(o_blk)
        o = o_inter + jnp.concatenate(rows, axis=0)
        state_ref[hd] = s * w_c[:, sk] + _mm_tn(vh, k_e[:, sk])
        o = o * lax.rsqrt(jnp.mean(o * o, axis=-1, keepdims=True) + NORM_EPS)
        outs.append(o)
    o = jnp.concatenate(outs, axis=1)
    y = o * ng_ref[...] * (g * _sigmoid(g))
    y_ref[0] = y.astype(y_ref.dtype)


def _gla(gla_proj, gk2, gkb, ng):
    b, s, cols = gla_proj.shape
    gk2p = jnp.zeros((LANES, GLA_QK_WIDTH), F32).at[:GLA_GATE_RANK].set(gk2).astype(BF16)
    args = [gk2p, gkb.reshape(1, -1), ng.reshape(1, -1)]
    full = lambda a: pl.BlockSpec(a.shape, lambda i, j: (0,) * a.ndim)
    return pl.pallas_call(
        _gla_kernel,
        grid=(b, s // CHUNK),
        in_specs=[pl.BlockSpec((1, CHUNK, cols), lambda i, j: (i, j, 0))] + [full(a) for a in args],
        out_specs=pl.BlockSpec((1, CHUNK, GLA_V_WIDTH), lambda i, j: (i, j, 0)),
        out_shape=jax.ShapeDtypeStruct((b, s, GLA_V_WIDTH), BF16),
        scratch_shapes=[pltpu.VMEM((GLA_HEADS, GLA_VAL_DIM, GLA_KEY_DIM), F32)],
        compiler_params=pltpu.CompilerParams(
            dimension_semantics=("arbitrary", "arbitrary"), vmem_limit_bytes=VMEM_LIMIT),
        name="gla",
    )(gla_proj, *args)


def _outproj_kernel(yr_ref, yg_ref, x_ref, wor_ref, wog_ref, nf_ref, wrh_ref, wrl_ref, br_ref,
                    h_ref, u_ref, gates_ref, idx_ref):
    h = (x_ref[...] + jnp.dot(yr_ref[...], wor_ref[...], preferred_element_type=F32)
         + jnp.dot(yg_ref[...], wog_ref[...], preferred_element_type=F32))
    h_ref[...] = h
    u = _rmsnorm(h, nf_ref[...])
    u_ref[...] = u
    u_hi, u_lo = _split(u, 2)
    logits = (jnp.dot(u_hi, wrh_ref[...], preferred_element_type=F32)
              + jnp.dot(u_hi, wrl_ref[...], preferred_element_type=F32)
              + jnp.dot(u_lo, wrh_ref[...], preferred_element_type=F32)) + br_ref[...]
    lane = lax.broadcasted_iota(jnp.int32, logits.shape, 1)
    rest = logits
    sel = jnp.zeros(logits.shape, jnp.bool_)
    idx_out = jnp.zeros(logits.shape, jnp.int32)
    top = None
    denom = None
    for r in range(TOP_K):
        m = jnp.max(rest, axis=-1, keepdims=True)
        idx = jnp.min(jnp.where(rest == m, lane, LANES), axis=-1, keepdims=True)
        pick = lane == idx
        sel = jnp.logical_or(sel, pick)
        idx_out = jnp.where(lane == r, idx, idx_out)
        rest = jnp.where(pick, -jnp.inf, rest)
        if r == 0:
            top = m
            denom = jnp.ones_like(m)
        else:
            denom = denom + jnp.exp(m - top)
    gates_ref[...] = jnp.where(sel, jnp.exp(logits - top) / denom, 0.0)
    idx_ref[...] = idx_out


def _out_proj(y_rw, y_gla, xf, wo_r, wo_g, nf, wr_hi, wr_lo, br, tm):
    n, d = xf.shape
    full = lambda a: pl.BlockSpec(a.shape, lambda i: (0,) * a.ndim)
    tile = lambda w: pl.BlockSpec((tm, w), lambda i: (i, 0))
    return pl.pallas_call(
        _outproj_kernel,
        grid=(n // tm,),
        in_specs=[tile(y_rw.shape[1]), tile(y_gla.shape[1]), tile(d), full(wo_r), full(wo_g), full(nf),
                  full(wr_hi), full(wr_lo), full(br)],
        out_specs=[tile(d), tile(d), tile(LANES), tile(LANES)],
        out_shape=[jax.ShapeDtypeStruct((n, d), F32), jax.ShapeDtypeStruct((n, d), F32),
                   jax.ShapeDtypeStruct((n, LANES), F32), jax.ShapeDtypeStruct((n, LANES), jnp.int32)],
        compiler_params=pltpu.CompilerParams(
            dimension_semantics=("arbitrary",), vmem_limit_bytes=VMEM_LIMIT),
        name="out_proj",
    )(y_rw, y_gla, xf, wo_r, wo_g, nf, wr_hi, wr_lo, br)


def _moe_kernel(be_ref, tok_ref, nused_ref, u_hbm, w1_ref, b1_ref, w2_ref, b2_ref, wcol_ref,
                ys_ref, xbuf, w1b, w2b, sem):
    i = pl.program_id(0)
    blk = EXPERT_BLOCK
    f = w2_ref.shape[1]
    e = be_ref[i]
    e_prev = be_ref[jnp.maximum(i - 1, 0)]

    @pl.when(jnp.logical_or(i == 0, e != e_prev))
    def _():
        w1b[...] = w1_ref[0].astype(BF16)
        w2b[...] = w2_ref[0].astype(BF16)

    def row_copy(r):
        t = tok_ref[i * blk + r]
        return pltpu.make_async_copy(u_hbm.at[pl.ds(t, 1)], xbuf.at[pl.ds(r, 1)], sem)

    @pl.when(i < nused_ref[0])
    def _():
        def issue(r, c):
            row_copy(r).start()
            return c
        lax.fori_loop(0, blk, issue, 0)

        def drain(r, c):
            row_copy(r).wait()
            return c
        lax.fori_loop(0, blk, drain, 0)

        xb = xbuf[...].astype(BF16)
        hgl = jnp.dot(xb, w1b[...], preferred_element_type=F32) + b1_ref[0]
        x_glu = jnp.minimum(hgl[:, :f], SWIGLU_LIMIT)
        x_lin = jnp.clip(hgl[:, f:], -SWIGLU_LIMIT, SWIGLU_LIMIT)
        act = (x_lin + 1.0) * (x_glu * _sigmoid(SWIGLU_ALPHA * x_glu))
        y = jnp.dot(act.astype(BF16), w2b[...], preferred_element_type=F32) + b2_ref[0]
        ys_ref[...] = y * wcol_ref[...]

    @pl.when(i >= nused_ref[0])
    def _():
        ys_ref[...] = jnp.zeros_like(ys_ref)


def _moe_experts(block_e, tok_buf, n_used, u, w1, b1, w2, b2, w_col):
    n, d = u.shape
    ne, _, f2 = w1.shape
    f = w2.shape[1]
    nb = block_e.shape[0]
    blk = EXPERT_BLOCK
    grid_spec = pltpu.PrefetchScalarGridSpec(
        num_scalar_prefetch=3,
        grid=(nb,),
        in_specs=[
            pl.BlockSpec(memory_space=pl.ANY),
            pl.BlockSpec((1, d, f2), lambda i, be, tok, nu: (be[i], 0, 0)),
            pl.BlockSpec((1, 1, f2), lambda i, be, tok, nu: (be[i], 0, 0)),
            pl.BlockSpec((1, f, d), lambda i, be, tok, nu: (be[i], 0, 0)),
            pl.BlockSpec((1, 1, d), lambda i, be, tok, nu: (be[i], 0, 0)),
            pl.BlockSpec((blk, 1), lambda i, be, tok, nu: (i, 0)),
        ],
        out_specs=pl.BlockSpec((blk, d), lambda i, be, tok, nu: (i, 0)),
        scratch_shapes=[
            pltpu.VMEM((blk, d), F32),
            pltpu.VMEM((d, f2), BF16),
            pltpu.VMEM((f, d), BF16),
            pltpu.SemaphoreType.DMA(()),
        ],
    )
    return pl.pallas_call(
        _moe_kernel,
        grid_spec=grid_spec,
        out_shape=jax.ShapeDtypeStruct((nb * blk, d), F32),
        compiler_params=pltpu.CompilerParams(
            dimension_semantics=("arbitrary",), vmem_limit_bytes=VMEM_LIMIT),
        name="moe_experts",
    )(block_e, tok_buf, n_used, u, w1, b1.reshape(ne, 1, f2), w2, b2.reshape(ne, 1, d), w_col)


def _final_kernel(slot_ref, ys_hbm, h_ref, p_ref, npl_ref, wg_ref, wp_ref, nfin_ref, out_ref, gbuf, sem,
                  *, last_layer):
    i = pl.program_id(0)
    tm = h_ref.shape[0]

    def row_copy(r, kk):
        s = slot_ref[(i * tm + r) * TOP_K + kk]
        return pltpu.make_async_copy(ys_hbm.at[pl.ds(s, 1)], gbuf.at[kk, pl.ds(r, 1)], sem)

    def issue(r, c):
        for kk in range(TOP_K):
            row_copy(r, kk).start()
        return c
    lax.fori_loop(0, tm, issue, 0)

    def drain(r, c):
        for kk in range(TOP_K):
            row_copy(r, kk).wait()
        return c
    lax.fori_loop(0, tm, drain, 0)

    h = h_ref[...] + ((gbuf[0] + gbuf[1]) + (gbuf[2] + gbuf[3]))
    u = _rmsnorm(h, npl_ref[...])
    gate = _sigmoid(_mm(u, wg_ref[...]))
    h = h + gate * _mm(p_ref[...], wp_ref[...])
    out_ref[...] = _rmsnorm(h, nfin_ref[...]) if last_layer else h


def _final(slots, ys, h1, pf, npl, wg, wp, nfin, tm, last_layer):
    n, d = h1.shape
    pd = pf.shape[1]
    grid_spec = pltpu.PrefetchScalarGridSpec(
        num_scalar_prefetch=1,
        grid=(n // tm,),
        in_specs=[
            pl.BlockSpec(memory_space=pl.ANY),
            pl.BlockSpec((tm, d), lambda i, s: (i, 0)),
            pl.BlockSpec((tm, pd), lambda i, s: (i, 0)),
            pl.BlockSpec((1, d), lambda i, s: (0, 0)),
            pl.BlockSpec((d, d), lambda i, s: (0, 0)),
            pl.BlockSpec((pd, d), lambda i, s: (0, 0)),
            pl.BlockSpec((1, d), lambda i, s: (0, 0)),
        ],
        out_specs=pl.BlockSpec((tm, d), lambda i, s: (i, 0)),
        scratch_shapes=[pltpu.VMEM((TOP_K, tm, d), F32), pltpu.SemaphoreType.DMA(())],
    )
    return pl.pallas_call(
        functools.partial(_final_kernel, last_layer=last_layer),
        grid_spec=grid_spec,
        out_shape=jax.ShapeDtypeStruct((n, d), F32),
        compiler_params=pltpu.CompilerParams(
            dimension_semantics=("arbitrary",), vmem_limit_bytes=VMEM_LIMIT),
        name="final",
    )(slots, ys, h1, pf, npl, wg, wp, nfin)


def _routing(gates, idx, n_experts):
    n = gates.shape[0]
    blk = EXPERT_BLOCK
    nb = (n * TOP_K + n_experts * (blk - 1) + blk - 1) // blk
    top_i = idx[:, :TOP_K]
    top_g = jnp.take_along_axis(gates, top_i, axis=1)
    sel = jnp.zeros((n, n_experts), jnp.int32).at[jnp.arange(n)[:, None], top_i].set(1)
    before = jnp.cumsum(sel, axis=0) - sel
    counts = jnp.sum(sel, axis=0)
    pcounts = ((counts + blk - 1) // blk) * blk
    pend = jnp.cumsum(pcounts)
    pstart = pend - pcounts
    slots = jnp.take_along_axis(pstart[None, :] + before, top_i, axis=1)
    flat = slots.reshape(-1)
    tok = jnp.repeat(jnp.arange(n, dtype=jnp.int32), TOP_K)
    tok_buf = jnp.zeros((nb * blk,), jnp.int32).at[flat].set(tok)
    w_col = jnp.zeros((nb * blk,), F32).at[flat].set(top_g.reshape(-1)).reshape(-1, 1)
    block_rows = jnp.arange(nb, dtype=jnp.int32) * blk
    block_e = jnp.clip(jnp.searchsorted(pend, block_rows, side="right"), 0, n_experts - 1).astype(jnp.int32)
    n_used = (pend[-1] // blk).astype(jnp.int32).reshape(1)
    return block_e, tok_buf, n_used, w_col, flat.astype(jnp.int32)


def kernel(x, p, norm_mix, w_in, shift_mu, rw_w0, rw_w2, rw_a0, rw_a2, rw_g2, rw_kk, rw_ka, rw_rk,
           rw_gn_w, rw_gn_b, gla_gk2, gla_gk_b, gla_norm, w_out, norm_ffn, w_router, b_router,
           w1, b1, w2, b2, norm_ple, w_ple_gate, w_ple, norm_final):
    bsz, seq, d = x.shape
    n = bsz * seq
    depth = w_in.shape[0]
    n_experts = w_router.shape[-1]
    tm = 256
    h = x.reshape(n, d)
    for l in range(depth):
        w_rw = w_in[l][:, :RW_COLS].astype(BF16)
        w_gla = jnp.pad(w_in[l][:, RW_COLS:], ((0, 0), (0, LANES - GLA_GATE_RANK))).astype(BF16)
        rw_proj, gla_proj = _in_proj(h, norm_mix[l].reshape(1, d), w_rw, w_gla, tm)
        y_rw = _rwkv(rw_proj.reshape(bsz, seq, -1), shift_mu[l], rw_w0[l], rw_w2[l], rw_a0[l], rw_a2[l],
                     rw_g2[l], rw_kk[l], rw_ka[l], rw_rk[l], rw_gn_w[l], rw_gn_b[l])
        y_gla = _gla(gla_proj.reshape(bsz, seq, -1), gla_gk2[l], gla_gk_b[l], gla_norm[l])

        wr = jnp.pad(w_router[l], ((0, 0), (0, LANES - n_experts)))
        wr_hi = wr.astype(BF16)
        wr_lo = (wr - wr_hi.astype(F32)).astype(BF16)
        br = jnp.pad(b_router[l], (0, LANES - n_experts), constant_values=NEG_BIG).reshape(1, LANES)
        wo = w_out[l].astype(BF16)
        h1, u, gates, idx = _out_proj(y_rw.reshape(n, -1), y_gla.reshape(n, -1), h, wo[:RW_WIDTH], wo[RW_WIDTH:],
                                      norm_ffn[l].reshape(1, d), wr_hi, wr_lo, br, tm)

        block_e, tok_buf, n_used, w_col, slots = _routing(gates, idx, n_experts)
        ys = _moe_experts(block_e, tok_buf, n_used, u, w1[l], b1[l], w2[l], b2[l], w_col)
        h = _final(slots, ys, h1, p[l].reshape(n, -1), norm_ple[l].reshape(1, d), w_ple_gate[l].astype(BF16),
                   w_ple[l].astype(BF16), norm_final.reshape(1, d), tm, l == depth - 1)
    return h.reshape(bsz, seq, d)
```

```python
import functools

import jax
import jax.numpy as jnp
from jax import lax
from jax.experimental import pallas as pl
from jax.experimental.pallas import tpu as pltpu

F32 = jnp.float32
BF16 = jnp.bfloat16

CHUNK = 64
RW_HEADS = 8
RW_HEAD_DIM = 64
RW_WIDTH = RW_HEADS * RW_HEAD_DIM
RW_DECAY_LORA = 64
RW_ICLR_LORA = 64
RW_GATE_LORA = 128
RW_COLS = 3 * RW_WIDTH + RW_DECAY_LORA + RW_ICLR_LORA + RW_GATE_LORA
RW_GN_EPS = 64e-5
GLA_HEADS = 4
GLA_KEY_DIM = 64
GLA_VAL_DIM = 128
GLA_QK_WIDTH = GLA_HEADS * GLA_KEY_DIM
GLA_V_WIDTH = GLA_HEADS * GLA_VAL_DIM
GLA_GATE_RANK = 16
GLA_GATE_TEMP = 16.0
GLA_SUB = 8
LANES = 128
GLA_COLS_PAD = 2 * GLA_QK_WIDTH + 2 * GLA_V_WIDTH + LANES
TOP_K = 4
EXPERT_BLOCK = 256
SWIGLU_ALPHA = 1.702
SWIGLU_LIMIT = 7.0
NORM_EPS = 1e-6
NEG_BIG = -1e30
VMEM_LIMIT = 56 * 1024 * 1024


def _mm(a, b):
    return jnp.dot(a.astype(BF16), b.astype(BF16), preferred_element_type=F32)


def _mm_nt(a, b):
    return lax.dot_general(a.astype(BF16), b.astype(BF16), (((1,), (1,)), ((), ())),
                           preferred_element_type=F32)


def _mm_tn(a, b):
    return lax.dot_general(a.astype(BF16), b.astype(BF16), (((0,), (0,)), ((), ())),
                           preferred_element_type=F32)


def _split(a, n):
    parts = []
    rem = a
    for _ in range(n):
        p = rem.astype(BF16)
        parts.append(p)
        rem = rem - p.astype(F32)
    return parts


def _mm_lhs_split(a, b_bf16, n):
    out = None
    for p in _split(a, n):
        t = jnp.dot(p, b_bf16, preferred_element_type=F32)
        out = t if out is None else out + t
    return out


def _mm_rhs_split(a_bf16, b, n):
    out = None
    for p in _split(b, n):
        t = jnp.dot(a_bf16, p, preferred_element_type=F32)
        out = t if out is None else out + t
    return out


def _rmsnorm(x, g):
    return x * lax.rsqrt(jnp.mean(x * x, axis=-1, keepdims=True) + NORM_EPS) * g


def _softplus(x):
    return jnp.maximum(x, 0.0) + jnp.log(1.0 + jnp.exp(-jnp.abs(x)))


def _sigmoid(x):
    return 1.0 / (1.0 + jnp.exp(-x))


def _tri(n, strict):
    r = lax.broadcasted_iota(jnp.int32, (n, n), 0)
    c = lax.broadcasted_iota(jnp.int32, (n, n), 1)
    return (r > c) if strict else (r >= c)


def _inproj_kernel(x_ref, g_ref, wr_ref, wg_ref, rw_ref, gla_ref):
    u = _rmsnorm(x_ref[...], g_ref[...]).astype(BF16)
    rw_ref[...] = jnp.dot(u, wr_ref[...], preferred_element_type=F32)
    gla_ref[...] = jnp.dot(u, wg_ref[...], preferred_element_type=F32)


def _in_proj(xf, g, w_rw, w_gla, tm):
    n, d = xf.shape
    return pl.pallas_call(
        _inproj_kernel,
        grid=(n // tm,),
        in_specs=[
            pl.BlockSpec((tm, d), lambda i: (i, 0)),
            pl.BlockSpec((1, d), lambda i: (0, 0)),
            pl.BlockSpec(w_rw.shape, lambda i: (0, 0)),
            pl.BlockSpec(w_gla.shape, lambda i: (0, 0)),
        ],
        out_specs=[
            pl.BlockSpec((tm, w_rw.shape[1]), lambda i: (i, 0)),
            pl.BlockSpec((tm, w_gla.shape[1]), lambda i: (i, 0)),
        ],
        out_shape=[
            jax.ShapeDtypeStruct((n, w_rw.shape[1]), F32),
            jax.ShapeDtypeStruct((n, w_gla.shape[1]), F32),
        ],
        compiler_params=pltpu.CompilerParams(
            dimension_semantics=("arbitrary",), vmem_limit_bytes=VMEM_LIMIT),
        name="in_proj",
    )(xf, g, w_rw, w_gla)


def _rwkv_kernel(x_ref, mu_ref, w0_ref, w2_ref, a0_ref, a2_ref, g2_ref, kkw_ref, ka_ref,
                 rk_ref, gnw_ref, gnb_ref, bd_ref, y_ref, state_ref, carry_ref, o_ref):
    C, H, D = CHUNK, RW_HEADS, RW_HEAD_DIM

    @pl.when(pl.program_id(1) == 0)
    def _():
        state_ref[...] = jnp.zeros_like(state_ref)
        carry_ref[...] = jnp.zeros_like(carry_ref)

    x = x_ref[0]
    row = lax.broadcasted_iota(jnp.int32, x.shape, 0)
    prev = jnp.where(row == 0, carry_ref[...], pltpu.roll(x, 1, axis=0))
    carry_ref[...] = x[C - 1:C, :]
    h = x + (prev - x) * mu_ref[...]

    W = RW_WIDTH
    r = h[:, 0:W]
    k = h[:, W:2 * W]
    v = h[:, 2 * W:3 * W]
    o0 = 3 * W
    dw = h[:, o0:o0 + RW_DECAY_LORA]
    da = h[:, o0 + RW_DECAY_LORA:o0 + RW_DECAY_LORA + RW_ICLR_LORA]
    dg = h[:, o0 + RW_DECAY_LORA + RW_ICLR_LORA:]

    bd = bd_ref[...]

    def seg_sum(t):
        return _mm_lhs_split(t, bd, 2)

    w_log = -_softplus(-(w0_ref[...] + _mm(jnp.tanh(dw), w2_ref[...]))) - 0.5
    lw = -jnp.exp(w_log)
    iclr = _sigmoid(a0_ref[...] + _mm(da, a2_ref[...]))
    gate = _mm(_sigmoid(dg), g2_ref[...])

    kk = k * kkw_ref[...]
    kk = kk / jnp.maximum(jnp.sqrt(seg_sum(kk * kk)), 1e-12)
    k2 = k * (1.0 + (iclr - 1.0) * ka_ref[...])

    cw = _mm_rhs_split(_tri(C, False).astype(BF16), lw, 3)
    cw_last = cw[C - 1:C, :]
    e_cw = jnp.exp(cw)
    e_ncw = jnp.exp(-cw)
    e_rem = jnp.exp(cw_last - cw)
    kka = kk * iclr
    a_t = -kk * jnp.exp(cw - lw)
    r_t = r * e_cw
    b_t = kka * e_ncw
    k_t = k2 * e_ncw
    b_h = kka * e_rem
    k_h = k2 * e_rem
    w_c = jnp.exp(cw_last)

    strict = _tri(C, True)
    incl = _tri(C, False)
    eye = jnp.where(_tri(C, False) & ~strict, 1.0, 0.0).astype(F32)

    hs = range(H)
    sls = [slice(hd * D, (hd + 1) * D) for hd in hs]
    states = [state_ref[hd] for hd in hs]
    ah = [a_t[:, sl] for sl in sls]
    rh = [r_t[:, sl] for sl in sls]
    bh = [b_t[:, sl] for sl in sls]
    kh = [k_t[:, sl] for sl in sls]
    vh = [v[:, sl] for sl in sls]
    a_ab = [jnp.where(strict, _mm_nt(ah[i], bh[i]), 0.0) for i in hs]
    a_ak = [jnp.where(strict, _mm_nt(ah[i], kh[i]), 0.0) for i in hs]
    a_rb = [jnp.where(incl, _mm_nt(rh[i], bh[i]), 0.0) for i in hs]
    a_rk = [jnp.where(incl, _mm_nt(rh[i], kh[i]), 0.0) for i in hs]
    p = [_mm(a_ab[i], a_ab[i]) for i in hs]
    q = [eye + a_ab[i] for i in hs]
    for _ in range(4):
        pq = [_mm(p[i], q[i]) for i in hs]
        p = [_mm(p[i], p[i]) for i in hs]
        q = [q[i] + pq[i] for i in hs]
    pq = [_mm(p[i], q[i]) for i in hs]
    t_inv = [q[i] + pq[i] for i in hs]
    akv = [_mm(a_ak[i], vh[i]) for i in hs]
    u0 = [_mm(t_inv[i], akv[i]) for i in hs]
    a_hat = [_mm(t_inv[i], ah[i]) for i in hs]
    u = [_mm_nt(a_hat[i], states[i]) + u0[i] for i in hs]
    o_heads = [_mm_nt(rh[i], states[i]) + _mm(a_rb[i], u[i]) + _mm(a_rk[i], vh[i]) for i in hs]
    new_states = [states[i] * w_c[:, sls[i]] + _mm_tn(u[i], b_h[:, sls[i]]) + _mm_tn(vh[i], k_h[:, sls[i]])
                  for i in hs]
    for hd in hs:
        state_ref[hd] = new_states[hd]
        o_ref[:, sls[hd]] = o_heads[hd]

    o = o_ref[...]
    inv_d = 1.0 / D
    mean = seg_sum(o) * inv_d
    dlt = o - mean
    var = seg_sum(dlt * dlt) * inv_d
    o = dlt * lax.rsqrt(var + RW_GN_EPS) * gnw_ref[...] + gnb_ref[...]
    bonus = seg_sum(r * k2 * rk_ref[...]) * v
    y_ref[0] = ((o + bonus) * gate).astype(y_ref.dtype)


def _rwkv(rw_proj, mu, w0, w2, a0, a2, g2, kkw, ka, rk, gnw, gnb):
    b, s, cols = rw_proj.shape
    W = RW_WIDTH
    bd = jnp.kron(jnp.eye(RW_HEADS, dtype=F32), jnp.ones((RW_HEAD_DIM, RW_HEAD_DIM), F32)).astype(BF16)
    row = lambda a: a.reshape(1, -1)
    full = lambda a: pl.BlockSpec(a.shape, lambda i, j: (0,) * a.ndim)
    args = [row(mu), row(w0), w2.astype(BF16), row(a0), a2.astype(BF16), g2.astype(BF16),
            row(kkw), row(ka), row(rk), row(gnw), row(gnb), bd]
    return pl.pallas_call(
        _rwkv_kernel,
        grid=(b, s // CHUNK),
        in_specs=[pl.BlockSpec((1, CHUNK, cols), lambda i, j: (i, j, 0))] + [full(a) for a in args],
        out_specs=pl.BlockSpec((1, CHUNK, W), lambda i, j: (i, j, 0)),
        out_shape=jax.ShapeDtypeStruct((b, s, W), BF16),
        scratch_shapes=[
            pltpu.VMEM((RW_HEADS, RW_HEAD_DIM, RW_HEAD_DIM), F32),
            pltpu.VMEM((1, cols), F32),
            pltpu.VMEM((CHUNK, W), F32),
        ],
        compiler_params=pltpu.CompilerParams(
            dimension_semantics=("arbitrary", "arbitrary"), vmem_limit_bytes=VMEM_LIMIT),
        name="rwkv7",
    )(rw_proj, *args)


def _gla_kernel(x_ref, gk2_ref, gkb_ref, ng_ref, y_ref, state_ref):
    C, H, DK, DV, SB = CHUNK, GLA_HEADS, GLA_KEY_DIM, GLA_VAL_DIM, GLA_SUB

    @pl.when(pl.program_id(1) == 0)
    def _():
        state_ref[...] = jnp.zeros_like(state_ref)

    x = x_ref[0]
    QW, VW = GLA_QK_WIDTH, GLA_V_WIDTH
    q = x[:, 0:QW] * (DK ** -0.5)
    k = x[:, QW:2 * QW]
    v = x[:, 2 * QW:2 * QW + VW]
    g = x[:, 2 * QW + VW:2 * QW + 2 * VW]
    dgk = x[:, 2 * QW + 2 * VW:]

    la = -_softplus(-(_mm(dgk, gk2_ref[...]) + gkb_ref[...])) * (1.0 / GLA_GATE_TEMP)
    b = _mm_rhs_split(_tri(C, False).astype(BF16), la, 3)
    b_last = b[C - 1:C, :]
    q_e = q * jnp.exp(b)
    k_e = k * jnp.exp(b_last - b)
    w_c = jnp.exp(b_last)

    ii = lax.broadcasted_iota(jnp.int32, (C // SB, SB, SB, 2 * DK), 1)
    jj = lax.broadcasted_iota(jnp.int32, (C // SB, SB, SB, 2 * DK), 2)
    causal4 = jj <= ii

    hs = range(H)
    nblk = C // SB
    sks = [slice(hd * DK, (hd + 1) * DK) for hd in hs]
    svs = [slice(hd * DV, (hd + 1) * DV) for hd in hs]
    states = [state_ref[hd] for hd in hs]
    qh = [q[:, sk] for sk in sks]
    kh = [k[:, sk] for sk in sks]
    bh = [b[:, sk] for sk in sks]
    vh = [v[:, sv] for sv in svs]
    o_inter = [_mm_nt(q_e[:, sks[i]], states[i]) for i in hs]
    new_states = [states[i] * w_c[:, sks[i]] + _mm_tn(vh[i], k_e[:, sks[i]]) for i in hs]
    att_off = {}
    for blk in range(1, nblk):
        r0 = blk * SB
        for i in hs:
            ref = bh[i][r0 - 1:r0]
            q_s = qh[i][r0:r0 + SB] * jnp.exp(bh[i][r0:r0 + SB] - ref)
            k_s = kh[i][0:r0] * jnp.exp(ref - bh[i][0:r0])
            att_off[(i, blk)] = _mm_nt(q_s, k_s)
    o_off = {key: _mm(att, vh[key[0]][0:key[1] * SB]) for key, att in att_off.items()}
    rsel = lax.broadcasted_iota(jnp.int32, (C, C * SB), 0)
    csel = lax.broadcasted_iota(jnp.int32, (C, C * SB), 1)
    sel = jnp.where((csel >= rsel * SB) & (csel < (rsel + 1) * SB), 1.0, 0.0).astype(BF16)
    drow = lax.broadcasted_iota(jnp.int32, (2 * DK, 2 * DV), 0)
    dcol = lax.broadcasted_iota(jnp.int32, (2 * DK, 2 * DV), 1)
    pair_ones = jnp.where((drow >= DK) == (dcol >= DV), 1.0, 0.0).astype(BF16)
    o_diag = []
    for pr in range(H // 2):
        sl2 = slice(2 * pr * DK, 2 * (pr + 1) * DK)
        q2, k2, b2 = q[:, sl2], k[:, sl2], b[:, sl2]
        dec = jnp.exp(jnp.where(causal4, b2.reshape(nblk, SB, 1, 2 * DK) - b2.reshape(nblk, 1, SB, 2 * DK),
                                NEG_BIG))
        pw = q2.reshape(nblk, SB, 1, 2 * DK) * k2.reshape(nblk, 1, SB, 2 * DK) * dec
        att2 = _mm(pw.reshape(C * SB, 2 * DK), pair_ones)
        for t in range(2):
            v_rep = jnp.broadcast_to(vh[2 * pr + t].reshape(nblk, 1, SB, DV), (nblk, SB, SB, DV))
            z = att2[:, t * DV:(t + 1) * DV] * v_rep.reshape(C * SB, DV)
            o_diag.append(_mm(sel, z))
    outs = []
    for i in hs:
        rows = [jnp.zeros((SB, DV), F32)] + [o_off[(i, blk)] for blk in range(1, nblk)]
        o = o_inter[i] + o_diag[i] + jnp.concatenate(rows, axis=0)
        outs.append(o * lax.rsqrt(jnp.mean(o * o, axis=-1, keepdims=True) + NORM_EPS))
    for hd in hs:
        state_ref[hd] = new_states[hd]
    o = jnp.concatenate(outs, axis=1)
    y = o * ng_ref[...] * (g * _sigmoid(g))
    y_ref[0] = y.astype(y_ref.dtype)


def _gla(gla_proj, gk2, gkb, ng):
    b, s, cols = gla_proj.shape
    gk2p = jnp.zeros((LANES, GLA_QK_WIDTH), F32).at[:GLA_GATE_RANK].set(gk2).astype(BF16)
    args = [gk2p, gkb.reshape(1, -1), ng.reshape(1, -1)]
    full = lambda a: pl.BlockSpec(a.shape, lambda i, j: (0,) * a.ndim)
    return pl.pallas_call(
        _gla_kernel,
        grid=(b, s // CHUNK),
        in_specs=[pl.BlockSpec((1, CHUNK, cols), lambda i, j: (i, j, 0))] + [full(a) for a in args],
        out_specs=pl.BlockSpec((1, CHUNK, GLA_V_WIDTH), lambda i, j: (i, j, 0)),
        out_shape=jax.ShapeDtypeStruct((b, s, GLA_V_WIDTH), BF16),
        scratch_shapes=[pltpu.VMEM((GLA_HEADS, GLA_VAL_DIM, GLA_KEY_DIM), F32)],
        compiler_params=pltpu.CompilerParams(
            dimension_semantics=("arbitrary", "arbitrary"), vmem_limit_bytes=VMEM_LIMIT),
        name="gla",
    )(gla_proj, *args)


def _outproj_kernel(yr_ref, yg_ref, x_ref, wor_ref, wog_ref, nf_ref, wrh_ref, wrl_ref, br_ref,
                    h_ref, u3_ref, route_ref, cnt_ref, carry_ref):
    tm, d = x_ref.shape

    @pl.when(pl.program_id(0) == 0)
    def _():
        carry_ref[...] = jnp.zeros_like(carry_ref)

    h = (x_ref[...] + jnp.dot(yr_ref[...], wor_ref[...], preferred_element_type=F32)
         + jnp.dot(yg_ref[...], wog_ref[...], preferred_element_type=F32))
    h_ref[...] = h
    u = _rmsnorm(h, nf_ref[...])
    for s in range(d // LANES):
        u3_ref[s] = u[:, s * LANES:(s + 1) * LANES]
    u_hi, u_lo = _split(u, 2)
    logits = (jnp.dot(u_hi, wrh_ref[...], preferred_element_type=F32)
              + jnp.dot(u_hi, wrl_ref[...], preferred_element_type=F32)
              + jnp.dot(u_lo, wrh_ref[...], preferred_element_type=F32)) + br_ref[...]
    lane = lax.broadcasted_iota(jnp.int32, logits.shape, 1)
    rest = logits
    picks, idxs, vals = [], [], []
    for r in range(TOP_K):
        m = jnp.max(rest, axis=-1, keepdims=True)
        idx = jnp.min(jnp.where(rest == m, lane, LANES), axis=-1, keepdims=True)
        pick = lane == idx
        picks.append(pick)
        idxs.append(idx)
        vals.append(m)
        rest = jnp.where(pick, -jnp.inf, rest)
    denom = jnp.ones_like(vals[0])
    for r in range(1, TOP_K):
        denom = denom + jnp.exp(vals[r] - vals[0])
    sel = jnp.zeros(logits.shape, F32)
    for pick in picks:
        sel = sel + jnp.where(pick, 1.0, 0.0)
    prefix = _mm(jnp.where(_tri(tm, True), 1.0, 0.0), sel) + carry_ref[...]
    carry_ref[...] = carry_ref[...] + jnp.sum(sel, axis=0, keepdims=True)
    cnt_ref[...] = carry_ref[...]
    route = jnp.zeros(logits.shape, F32)
    for r in range(TOP_K):
        gate = jnp.exp(vals[r] - vals[0]) / denom
        rank = jnp.sum(jnp.where(picks[r], prefix, 0.0), axis=-1, keepdims=True)
        route = jnp.where(lane == r, gate, route)
        route = jnp.where(lane == TOP_K + r, idxs[r].astype(F32), route)
        route = jnp.where(lane == 2 * TOP_K + r, rank, route)
    route_ref[...] = route


def _out_proj(y_rw, y_gla, xf, wo_r, wo_g, nf, wr_hi, wr_lo, br, tm):
    n, d = xf.shape
    full = lambda a: pl.BlockSpec(a.shape, lambda i: (0,) * a.ndim)
    tile = lambda w: pl.BlockSpec((tm, w), lambda i: (i, 0))
    return pl.pallas_call(
        _outproj_kernel,
        grid=(n // tm,),
        in_specs=[tile(y_rw.shape[1]), tile(y_gla.shape[1]), tile(d), full(wo_r), full(wo_g), full(nf),
                  full(wr_hi), full(wr_lo), full(br)],
        out_specs=[tile(d), pl.BlockSpec((d // LANES, tm, LANES), lambda i: (i, 0, 0)), tile(LANES),
                   pl.BlockSpec((1, LANES), lambda i: (0, 0))],
        out_shape=[jax.ShapeDtypeStruct((n, d), F32), jax.ShapeDtypeStruct((n // tm * (d // LANES), tm, LANES), F32),
                   jax.ShapeDtypeStruct((n, LANES), F32), jax.ShapeDtypeStruct((1, LANES), F32)],
        scratch_shapes=[pltpu.VMEM((1, LANES), F32)],
        compiler_params=pltpu.CompilerParams(
            dimension_semantics=("arbitrary",), vmem_limit_bytes=VMEM_LIMIT),
        name="out_proj",
    )(y_rw, y_gla, xf, wo_r, wo_g, nf, wr_hi, wr_lo, br)


assert EXPERT_BLOCK & (EXPERT_BLOCK - 1) == 0
_BLOCK_SHIFT = EXPERT_BLOCK.bit_length() - 1


def _slot_block(slot):
    return lax.shift_right_logical(slot, _BLOCK_SHIFT)


def _slot_row(slot):
    return jnp.bitwise_and(slot, EXPERT_BLOCK - 1)


def _dispatch_kernel(row_ref, u3_hbm, xs_in_hbm, xs_hbm, sems, *, tm, ns):
    del xs_in_hbm
    i = pl.program_id(0)
    par = lax.rem(i, 2)

    def issue(r, c):
        src = u3_hbm.at[pl.ds(i * ns, ns), pl.ds(r, 1), :]
        for kk in range(TOP_K):
            slot = row_ref[(i * tm + r) * TOP_K + kk]
            dst = xs_hbm.at[pl.ds(_slot_block(slot) * ns, ns), pl.ds(_slot_row(slot), 1), :]
            pltpu.make_async_copy(src, dst, sems.at[par]).start()
        return c
    lax.fori_loop(0, tm, issue, 0)

    def wait_step(p):
        for _ in range(TOP_K):
            pltpu.make_async_copy(u3_hbm.at[pl.ds(0, ns)], xs_hbm.at[pl.ds(0, ns), pl.ds(0, tm), :],
                                  sems.at[p]).wait()

    @pl.when(i > 0)
    def _():
        wait_step(1 - par)

    @pl.when(i == pl.num_programs(0) - 1)
    def _():
        wait_step(par)


def _dispatch(slots, u3, n_rows, tm, ns):
    ln = u3.shape[2]
    n = u3.shape[0] // ns * tm
    xs0 = jnp.zeros((n_rows // EXPERT_BLOCK * ns, EXPERT_BLOCK, ln), u3.dtype)
    grid_spec = pltpu.PrefetchScalarGridSpec(
        num_scalar_prefetch=1,
        grid=(n // tm,),
        in_specs=[pl.BlockSpec(memory_space=pl.ANY), pl.BlockSpec(memory_space=pl.ANY)],
        out_specs=pl.BlockSpec(memory_space=pl.ANY),
        scratch_shapes=[pltpu.SemaphoreType.DMA((2,))],
    )
    return pl.pallas_call(
        functools.partial(_dispatch_kernel, tm=tm, ns=ns),
        grid_spec=grid_spec,
        out_shape=jax.ShapeDtypeStruct(xs0.shape, xs0.dtype),
        input_output_aliases={2: 0},
        compiler_params=pltpu.CompilerParams(dimension_semantics=("arbitrary",), has_side_effects=True),
        name="moe_dispatch",
    )(slots, u3, xs0)


def _moe_kernel(be_ref, nused_ref, xs_ref, w1_ref, b1_ref, w2_ref, b2_ref, ys_ref, w1b, w2b):
    i = pl.program_id(0)
    f = w2_ref.shape[1]
    ns = xs_ref.shape[0]
    e = be_ref[i]
    e_prev = be_ref[jnp.maximum(i - 1, 0)]

    @pl.when(jnp.logical_or(i == 0, e != e_prev))
    def _():
        w1b[...] = w1_ref[0].astype(BF16)
        w2b[...] = w2_ref[0].astype(BF16)

    @pl.when(i < nused_ref[0])
    def _():
        xb = jnp.concatenate([xs_ref[s] for s in range(ns)], axis=1).astype(BF16)
        hgl = jnp.dot(xb, w1b[...], preferred_element_type=F32) + b1_ref[0]
        x_glu = jnp.minimum(hgl[:, :f], SWIGLU_LIMIT)
        x_lin = jnp.clip(hgl[:, f:], -SWIGLU_LIMIT, SWIGLU_LIMIT)
        act = (x_lin + 1.0) * (x_glu * _sigmoid(SWIGLU_ALPHA * x_glu))
        y = jnp.dot(act.astype(BF16), w2b[...], preferred_element_type=F32) + b2_ref[0]
        for s in range(ns):
            ys_ref[s] = y[:, s * LANES:(s + 1) * LANES]

    @pl.when(i >= nused_ref[0])
    def _():
        ys_ref[...] = jnp.zeros_like(ys_ref)


def _moe_experts(block_e, n_used, xs, w1, b1, w2, b2):
    _, blk, ln = xs.shape
    ne, d, f2 = w1.shape
    f = w2.shape[1]
    nb = block_e.shape[0]
    ns = d // ln
    grid_spec = pltpu.PrefetchScalarGridSpec(
        num_scalar_prefetch=2,
        grid=(nb,),
        in_specs=[
            pl.BlockSpec((ns, blk, ln), lambda i, be, nu: (i, 0, 0)),
            pl.BlockSpec((1, d, f2), lambda i, be, nu: (be[i], 0, 0)),
            pl.BlockSpec((1, 1, f2), lambda i, be, nu: (be[i], 0, 0)),
            pl.BlockSpec((1, f, d), lambda i, be, nu: (be[i], 0, 0)),
            pl.BlockSpec((1, 1, d), lambda i, be, nu: (be[i], 0, 0)),
        ],
        out_specs=pl.BlockSpec((ns, blk, ln), lambda i, be, nu: (i, 0, 0)),
        scratch_shapes=[pltpu.VMEM((d, f2), BF16), pltpu.VMEM((f, d), BF16)],
    )
    return pl.pallas_call(
        _moe_kernel,
        grid_spec=grid_spec,
        out_shape=jax.ShapeDtypeStruct(xs.shape, F32),
        compiler_params=pltpu.CompilerParams(
            dimension_semantics=("arbitrary",), vmem_limit_bytes=VMEM_LIMIT),
        name="moe_experts",
    )(block_e, n_used, xs, w1, b1.reshape(ne, 1, f2), w2, b2.reshape(ne, 1, d))


def _final_kernel(row_ref, ys_hbm, h_ref, route_ref, p_ref, npl_ref, wg_ref, wp_ref, nfin_ref, out_ref,
                  gbuf, sem, *, last_layer):
    i = pl.program_id(0)
    tm = h_ref.shape[0]
    ns = gbuf.shape[1]

    def issue(r, c):
        for kk in range(TOP_K):
            slot = row_ref[(i * tm + r) * TOP_K + kk]
            src = ys_hbm.at[pl.ds(_slot_block(slot) * ns, ns), pl.ds(_slot_row(slot), 1), :]
            pltpu.make_async_copy(src, gbuf.at[kk, :, pl.ds(r, 1), :], sem).start()
        return c
    lax.fori_loop(0, tm, issue, 0)
    for kk in range(TOP_K):
        pltpu.make_async_copy(ys_hbm.at[pl.ds(0, ns), pl.ds(0, tm), :], gbuf.at[kk], sem).wait()

    route = route_ref[...]
    h = h_ref[...]
    for kk in range(TOP_K):
        yk = jnp.concatenate([gbuf[kk, s] for s in range(ns)], axis=1)
        h = h + route[:, kk:kk + 1] * yk
    u = _rmsnorm(h, npl_ref[...])
    gate = _sigmoid(_mm(u, wg_ref[...]))
    h = h + gate * _mm(p_ref[...], wp_ref[...])
    out_ref[...] = _rmsnorm(h, nfin_ref[...]) if last_layer else h


def _final(slots, ys, h1, route, pf, npl, wg, wp, nfin, tm, last_layer):
    n, d = h1.shape
    pd = pf.shape[1]
    grid_spec = pltpu.PrefetchScalarGridSpec(
        num_scalar_prefetch=1,
        grid=(n // tm,),
        in_specs=[
            pl.BlockSpec(memory_space=pl.ANY),
            pl.BlockSpec((tm, d), lambda i, s: (i, 0)),
            pl.BlockSpec((tm, LANES), lambda i, s: (i, 0)),
            pl.BlockSpec((tm, pd), lambda i, s: (i, 0)),
            pl.BlockSpec((1, d), lambda i, s: (0, 0)),
            pl.BlockSpec((d, d), lambda i, s: (0, 0)),
            pl.BlockSpec((pd, d), lambda i, s: (0, 0)),
            pl.BlockSpec((1, d), lambda i, s: (0, 0)),
        ],
        out_specs=pl.BlockSpec((tm, d), lambda i, s: (i, 0)),
        scratch_shapes=[pltpu.VMEM((TOP_K, d // ys.shape[2], tm, ys.shape[2]), F32),
                        pltpu.SemaphoreType.DMA(())],
    )
    return pl.pallas_call(
        functools.partial(_final_kernel, last_layer=last_layer),
        grid_spec=grid_spec,
        out_shape=jax.ShapeDtypeStruct((n, d), F32),
        compiler_params=pltpu.CompilerParams(
            dimension_semantics=("arbitrary",), vmem_limit_bytes=VMEM_LIMIT),
        name="final",
    )(slots, ys, h1, route, pf, npl, wg, wp, nfin)


def _routing(route, counts, n_experts):
    n = route.shape[0]
    blk = EXPERT_BLOCK
    nb = (n * TOP_K + n_experts * (blk - 1) + blk - 1) // blk
    counts = counts[0, :n_experts].astype(jnp.int32)
    pcounts = ((counts + blk - 1) // blk) * blk
    pend = jnp.cumsum(pcounts)
    pstart = pend - pcounts
    top_i = route[:, TOP_K:2 * TOP_K].astype(jnp.int32)
    rank = route[:, 2 * TOP_K:3 * TOP_K].astype(jnp.int32)
    onehot = top_i[:, :, None] == jnp.arange(n_experts, dtype=jnp.int32)[None, None, :]
    slots = jnp.sum(jnp.where(onehot, pstart[None, None, :], 0), axis=-1) + rank
    block_rows = jnp.arange(nb, dtype=jnp.int32) * blk
    block_e = jnp.sum(block_rows[:, None] >= pend[None, :], axis=1)
    block_e = jnp.clip(block_e, 0, n_experts - 1).astype(jnp.int32)
    n_used = (pend[-1] // blk).astype(jnp.int32).reshape(1)
    return block_e, n_used, slots.reshape(-1).astype(jnp.int32), nb * blk


def kernel(x, p, norm_mix, w_in, shift_mu, rw_w0, rw_w2, rw_a0, rw_a2, rw_g2, rw_kk, rw_ka, rw_rk,
           rw_gn_w, rw_gn_b, gla_gk2, gla_gk_b, gla_norm, w_out, norm_ffn, w_router, b_router,
           w1, b1, w2, b2, norm_ple, w_ple_gate, w_ple, norm_final):
    bsz, seq, d = x.shape
    n = bsz * seq
    depth = w_in.shape[0]
    n_experts = w_router.shape[-1]
    tm = 256
    h = x.reshape(n, d)
    for l in range(depth):
        w_rw = w_in[l][:, :RW_COLS].astype(BF16)
        w_gla = jnp.pad(w_in[l][:, RW_COLS:], ((0, 0), (0, LANES - GLA_GATE_RANK))).astype(BF16)
        rw_proj, gla_proj = _in_proj(h, norm_mix[l].reshape(1, d), w_rw, w_gla, tm)
        y_rw = _rwkv(rw_proj.reshape(bsz, seq, -1), shift_mu[l], rw_w0[l], rw_w2[l], rw_a0[l], rw_a2[l],
                     rw_g2[l], rw_kk[l], rw_ka[l], rw_rk[l], rw_gn_w[l], rw_gn_b[l])
        y_gla = _gla(gla_proj.reshape(bsz, seq, -1), gla_gk2[l], gla_gk_b[l], gla_norm[l])

        wr = jnp.pad(w_router[l], ((0, 0), (0, LANES - n_experts)))
        wr_hi = wr.astype(BF16)
        wr_lo = (wr - wr_hi.astype(F32)).astype(BF16)
        br = jnp.pad(b_router[l], (0, LANES - n_experts), constant_values=NEG_BIG).reshape(1, LANES)
        wo = w_out[l].astype(BF16)
        h1, u3, route, counts = _out_proj(y_rw.reshape(n, -1), y_gla.reshape(n, -1), h, wo[:RW_WIDTH],
                                          wo[RW_WIDTH:], norm_ffn[l].reshape(1, d), wr_hi, wr_lo, br, tm)

        ns = d // LANES
        block_e, n_used, slots, n_rows = _routing(route, counts, n_experts)
        xs = _dispatch(slots, u3, n_rows, tm, ns)
        ys = _moe_experts(block_e, n_used, xs, w1[l], b1[l], w2[l], b2[l])
        h = _final(slots, ys, h1, route, p[l].reshape(n, -1), norm_ple[l].reshape(1, d),
                   w_ple_gate[l].astype(BF16), w_ple[l].astype(BF16), norm_final.reshape(1, d), tm, l == depth - 1)
    return h.reshape(bsz, seq, d)
```

```python
import functools

import jax
import jax.numpy as jnp
from jax import lax
from jax.experimental import pallas as pl
from jax.experimental.pallas import tpu as pltpu

F32 = jnp.float32
BF16 = jnp.bfloat16

CHUNK = 64
RW_HEADS = 8
RW_HEAD_DIM = 64
RW_WIDTH = RW_HEADS * RW_HEAD_DIM
RW_DECAY_LORA = 64
RW_ICLR_LORA = 64
RW_GATE_LORA = 128
RW_COLS = 3 * RW_WIDTH + RW_DECAY_LORA + RW_ICLR_LORA + RW_GATE_LORA
RW_GN_EPS = 64e-5
GLA_HEADS = 4
GLA_KEY_DIM = 64
GLA_VAL_DIM = 128
GLA_QK_WIDTH = GLA_HEADS * GLA_KEY_DIM
GLA_V_WIDTH = GLA_HEADS * GLA_VAL_DIM
GLA_GATE_RANK = 16
GLA_GATE_TEMP = 16.0
GLA_SUB = 8
LANES = 128
GLA_COLS_PAD = 2 * GLA_QK_WIDTH + 2 * GLA_V_WIDTH + LANES
TOP_K = 4
EXPERT_BLOCK = 256
SWIGLU_ALPHA = 1.702
SWIGLU_LIMIT = 7.0
NORM_EPS = 1e-6
NEG_BIG = -1e30
VMEM_LIMIT = 56 * 1024 * 1024


def _mm(a, b):
    return jnp.dot(a.astype(BF16), b.astype(BF16), preferred_element_type=F32)


def _mm_nt(a, b):
    return lax.dot_general(a.astype(BF16), b.astype(BF16), (((1,), (1,)), ((), ())),
                           preferred_element_type=F32)


def _mm_tn(a, b):
    return lax.dot_general(a.astype(BF16), b.astype(BF16), (((0,), (0,)), ((), ())),
                           preferred_element_type=F32)


def _split(a, n):
    parts = []
    rem = a
    for _ in range(n):
        p = rem.astype(BF16)
        parts.append(p)
        rem = rem - p.astype(F32)
    return parts


def _mm_lhs_split(a, b_bf16, n):
    out = None
    for p in _split(a, n):
        t = jnp.dot(p, b_bf16, preferred_element_type=F32)
        out = t if out is None else out + t
    return out


def _mm_rhs_split(a_bf16, b, n):
    out = None
    for p in _split(b, n):
        t = jnp.dot(a_bf16, p, preferred_element_type=F32)
        out = t if out is None else out + t
    return out


def _rmsnorm(x, g):
    return x * lax.rsqrt(jnp.mean(x * x, axis=-1, keepdims=True) + NORM_EPS) * g


def _softplus(x):
    return jnp.maximum(x, 0.0) + jnp.log(1.0 + jnp.exp(-jnp.abs(x)))


def _sigmoid(x):
    return 1.0 / (1.0 + jnp.exp(-x))


def _tri(n, strict):
    r = lax.broadcasted_iota(jnp.int32, (n, n), 0)
    c = lax.broadcasted_iota(jnp.int32, (n, n), 1)
    return (r > c) if strict else (r >= c)


def _inproj_kernel(x_ref, g_ref, wr_ref, wg_ref, rw_ref, gla_ref):
    u = _rmsnorm(x_ref[...], g_ref[...]).astype(BF16)
    rw_ref[...] = jnp.dot(u, wr_ref[...], preferred_element_type=F32)
    gla_ref[...] = jnp.dot(u, wg_ref[...], preferred_element_type=F32)


def _in_proj(xf, g, w_rw, w_gla, tm):
    n, d = xf.shape
    return pl.pallas_call(
        _inproj_kernel,
        grid=(n // tm,),
        in_specs=[
            pl.BlockSpec((tm, d), lambda i: (i, 0)),
            pl.BlockSpec((1, d), lambda i: (0, 0)),
            pl.BlockSpec(w_rw.shape, lambda i: (0, 0)),
            pl.BlockSpec(w_gla.shape, lambda i: (0, 0)),
        ],
        out_specs=[
            pl.BlockSpec((tm, w_rw.shape[1]), lambda i: (i, 0)),
            pl.BlockSpec((tm, w_gla.shape[1]), lambda i: (i, 0)),
        ],
        out_shape=[
            jax.ShapeDtypeStruct((n, w_rw.shape[1]), F32),
            jax.ShapeDtypeStruct((n, w_gla.shape[1]), F32),
        ],
        compiler_params=pltpu.CompilerParams(
            dimension_semantics=("arbitrary",), vmem_limit_bytes=VMEM_LIMIT),
        name="in_proj",
    )(xf, g, w_rw, w_gla)


def _rwkv_kernel(x_ref, mu_ref, w0_ref, w2_ref, a0_ref, a2_ref, g2_ref, kkw_ref, ka_ref,
                 rk_ref, gnw_ref, gnb_ref, bd_ref, y_ref, state_ref, carry_ref, o_ref):
    C, H, D = CHUNK, RW_HEADS, RW_HEAD_DIM

    @pl.when(pl.program_id(1) == 0)
    def _():
        state_ref[...] = jnp.zeros_like(state_ref)
        carry_ref[...] = jnp.zeros_like(carry_ref)

    x = x_ref[0]
    row = lax.broadcasted_iota(jnp.int32, x.shape, 0)
    prev = jnp.where(row == 0, carry_ref[...], pltpu.roll(x, 1, axis=0))
    carry_ref[...] = x[C - 1:C, :]
    h = x + (prev - x) * mu_ref[...]

    W = RW_WIDTH
    r = h[:, 0:W]
    k = h[:, W:2 * W]
    v = h[:, 2 * W:3 * W]
    o0 = 3 * W
    dw = h[:, o0:o0 + RW_DECAY_LORA]
    da = h[:, o0 + RW_DECAY_LORA:o0 + RW_DECAY_LORA + RW_ICLR_LORA]
    dg = h[:, o0 + RW_DECAY_LORA + RW_ICLR_LORA:]

    bd = bd_ref[...]

    def seg_sum(t):
        return _mm_lhs_split(t, bd, 2)

    w_log = -_softplus(-(w0_ref[...] + _mm(jnp.tanh(dw), w2_ref[...]))) - 0.5
    lw = -jnp.exp(w_log)
    iclr = _sigmoid(a0_ref[...] + _mm(da, a2_ref[...]))
    gate = _mm(_sigmoid(dg), g2_ref[...])

    kk = k * kkw_ref[...]
    kk = kk / jnp.maximum(jnp.sqrt(seg_sum(kk * kk)), 1e-12)
    k2 = k * (1.0 + (iclr - 1.0) * ka_ref[...])

    cw = _mm_rhs_split(_tri(C, False).astype(BF16), lw, 3)
    cw_last = cw[C - 1:C, :]
    e_cw = jnp.exp(cw)
    e_ncw = jnp.exp(-cw)
    e_rem = jnp.exp(cw_last - cw)
    kka = kk * iclr
    a_t = -kk * jnp.exp(cw - lw)
    r_t = r * e_cw
    b_t = kka * e_ncw
    k_t = k2 * e_ncw
    b_h = kka * e_rem
    k_h = k2 * e_rem
    w_c = jnp.exp(cw_last)

    strict = _tri(C, True)
    incl = _tri(C, False)
    eye = jnp.where(_tri(C, False) & ~strict, 1.0, 0.0).astype(F32)

    hs = range(H)
    sls = [slice(hd * D, (hd + 1) * D) for hd in hs]
    states = [state_ref[hd] for hd in hs]
    ah = [a_t[:, sl] for sl in sls]
    rh = [r_t[:, sl] for sl in sls]
    bh = [b_t[:, sl] for sl in sls]
    kh = [k_t[:, sl] for sl in sls]
    vh = [v[:, sl] for sl in sls]
    a_ab = [jnp.where(strict, _mm_nt(ah[i], bh[i]), 0.0) for i in hs]
    a_ak = [jnp.where(strict, _mm_nt(ah[i], kh[i]), 0.0) for i in hs]
    a_rb = [jnp.where(incl, _mm_nt(rh[i], bh[i]), 0.0) for i in hs]
    a_rk = [jnp.where(incl, _mm_nt(rh[i], kh[i]), 0.0) for i in hs]
    p = [_mm(a_ab[i], a_ab[i]) for i in hs]
    q = [eye + a_ab[i] for i in hs]
    for _ in range(4):
        pq = [_mm(p[i], q[i]) for i in hs]
        p = [_mm(p[i], p[i]) for i in hs]
        q = [q[i] + pq[i] for i in hs]
    pq = [_mm(p[i], q[i]) for i in hs]
    t_inv = [q[i] + pq[i] for i in hs]
    akv = [_mm(a_ak[i], vh[i]) for i in hs]
    u0 = [_mm(t_inv[i], akv[i]) for i in hs]
    a_hat = [_mm(t_inv[i], ah[i]) for i in hs]
    u = [_mm_nt(a_hat[i], states[i]) + u0[i] for i in hs]
    o_heads = [_mm_nt(rh[i], states[i]) + _mm(a_rb[i], u[i]) + _mm(a_rk[i], vh[i]) for i in hs]
    new_states = [states[i] * w_c[:, sls[i]] + _mm_tn(u[i], b_h[:, sls[i]]) + _mm_tn(vh[i], k_h[:, sls[i]])
                  for i in hs]
    for hd in hs:
        state_ref[hd] = new_states[hd]
        o_ref[:, sls[hd]] = o_heads[hd]

    o = o_ref[...]
    inv_d = 1.0 / D
    mean = seg_sum(o) * inv_d
    dlt = o - mean
    var = seg_sum(dlt * dlt) * inv_d
    o = dlt * lax.rsqrt(var + RW_GN_EPS) * gnw_ref[...] + gnb_ref[...]
    bonus = seg_sum(r * k2 * rk_ref[...]) * v
    y_ref[0] = ((o + bonus) * gate).astype(y_ref.dtype)


def _rwkv(rw_proj, mu, w0, w2, a0, a2, g2, kkw, ka, rk, gnw, gnb):
    b, s, cols = rw_proj.shape
    W = RW_WIDTH
    bd = jnp.kron(jnp.eye(RW_HEADS, dtype=F32), jnp.ones((RW_HEAD_DIM, RW_HEAD_DIM), F32)).astype(BF16)
    row = lambda a: a.reshape(1, -1)
    full = lambda a: pl.BlockSpec(a.shape, lambda i, j: (0,) * a.ndim)
    args = [row(mu), row(w0), w2.astype(BF16), row(a0), a2.astype(BF16), g2.astype(BF16),
            row(kkw), row(ka), row(rk), row(gnw), row(gnb), bd]
    return pl.pallas_call(
        _rwkv_kernel,
        grid=(b, s // CHUNK),
        in_specs=[pl.BlockSpec((1, CHUNK, cols), lambda i, j: (i, j, 0))] + [full(a) for a in args],
        out_specs=pl.BlockSpec((1, CHUNK, W), lambda i, j: (i, j, 0)),
        out_shape=jax.ShapeDtypeStruct((b, s, W), BF16),
        scratch_shapes=[
            pltpu.VMEM((RW_HEADS, RW_HEAD_DIM, RW_HEAD_DIM), F32),
            pltpu.VMEM((1, cols), F32),
            pltpu.VMEM((CHUNK, W), F32),
        ],
        compiler_params=pltpu.CompilerParams(
            dimension_semantics=("arbitrary", "arbitrary"), vmem_limit_bytes=VMEM_LIMIT),
        name="rwkv7",
    )(rw_proj, *args)


def _gla_kernel(x_ref, gk2_ref, gkb_ref, ng_ref, y_ref, state_ref):
    C, H, DK, DV, SB = CHUNK, GLA_HEADS, GLA_KEY_DIM, GLA_VAL_DIM, GLA_SUB

    @pl.when(pl.program_id(1) == 0)
    def _():
        state_ref[...] = jnp.zeros_like(state_ref)

    x = x_ref[0]
    QW, VW = GLA_QK_WIDTH, GLA_V_WIDTH
    q = x[:, 0:QW] * (DK ** -0.5)
    k = x[:, QW:2 * QW]
    v = x[:, 2 * QW:2 * QW + VW]
    g = x[:, 2 * QW + VW:2 * QW + 2 * VW]
    dgk = x[:, 2 * QW + 2 * VW:]

    la = -_softplus(-(_mm(dgk, gk2_ref[...]) + gkb_ref[...])) * (1.0 / GLA_GATE_TEMP)
    b = _mm_rhs_split(_tri(C, False).astype(BF16), la, 3)
    b_last = b[C - 1:C, :]
    q_e = q * jnp.exp(b)
    k_e = k * jnp.exp(b_last - b)
    w_c = jnp.exp(b_last)

    ii = lax.broadcasted_iota(jnp.int32, (C // SB, SB, SB, 2 * DK), 1)
    jj = lax.broadcasted_iota(jnp.int32, (C // SB, SB, SB, 2 * DK), 2)
    causal4 = jj <= ii

    hs = range(H)
    nblk = C // SB
    sks = [slice(hd * DK, (hd + 1) * DK) for hd in hs]
    svs = [slice(hd * DV, (hd + 1) * DV) for hd in hs]
    states = [state_ref[hd] for hd in hs]
    qh = [q[:, sk] for sk in sks]
    kh = [k[:, sk] for sk in sks]
    bh = [b[:, sk] for sk in sks]
    vh = [v[:, sv] for sv in svs]
    o_inter = [_mm_nt(q_e[:, sks[i]], states[i]) for i in hs]
    new_states = [states[i] * w_c[:, sks[i]] + _mm_tn(vh[i], k_e[:, sks[i]]) for i in hs]
    att_off = {}
    for blk in range(1, nblk):
        r0 = blk * SB
        for i in hs:
            ref = bh[i][r0 - 1:r0]
            q_s = qh[i][r0:r0 + SB] * jnp.exp(bh[i][r0:r0 + SB] - ref)
            k_s = kh[i][0:r0] * jnp.exp(ref - bh[i][0:r0])
            att_off[(i, blk)] = _mm_nt(q_s, k_s)
    o_off = {key: _mm(att, vh[key[0]][0:key[1] * SB]) for key, att in att_off.items()}
    rsel = lax.broadcasted_iota(jnp.int32, (C, C * SB), 0)
    csel = lax.broadcasted_iota(jnp.int32, (C, C * SB), 1)
    sel = jnp.where((csel >= rsel * SB) & (csel < (rsel + 1) * SB), 1.0, 0.0).astype(BF16)
    drow = lax.broadcasted_iota(jnp.int32, (2 * DK, 2 * DV), 0)
    dcol = lax.broadcasted_iota(jnp.int32, (2 * DK, 2 * DV), 1)
    pair_ones = jnp.where((drow >= DK) == (dcol >= DV), 1.0, 0.0).astype(BF16)
    o_diag = []
    for pr in range(H // 2):
        sl2 = slice(2 * pr * DK, 2 * (pr + 1) * DK)
        q2, k2, b2 = q[:, sl2], k[:, sl2], b[:, sl2]
        dec = jnp.exp(jnp.where(causal4, b2.reshape(nblk, SB, 1, 2 * DK) - b2.reshape(nblk, 1, SB, 2 * DK),
                                NEG_BIG))
        pw = q2.reshape(nblk, SB, 1, 2 * DK) * k2.reshape(nblk, 1, SB, 2 * DK) * dec
        att2 = _mm(pw.reshape(C * SB, 2 * DK), pair_ones)
        for t in range(2):
            v_rep = jnp.broadcast_to(vh[2 * pr + t].reshape(nblk, 1, SB, DV), (nblk, SB, SB, DV))
            z = att2[:, t * DV:(t + 1) * DV] * v_rep.reshape(C * SB, DV)
            o_diag.append(_mm(sel, z))
    outs = []
    for i in hs:
        rows = [jnp.zeros((SB, DV), F32)] + [o_off[(i, blk)] for blk in range(1, nblk)]
        o = o_inter[i] + o_diag[i] + jnp.concatenate(rows, axis=0)
        outs.append(o * lax.rsqrt(jnp.mean(o * o, axis=-1, keepdims=True) + NORM_EPS))
    for hd in hs:
        state_ref[hd] = new_states[hd]
    o = jnp.concatenate(outs, axis=1)
    y = o * ng_ref[...] * (g * _sigmoid(g))
    y_ref[0] = y.astype(y_ref.dtype)


def _gla(gla_proj, gk2, gkb, ng):
    b, s, cols = gla_proj.shape
    gk2p = jnp.zeros((LANES, GLA_QK_WIDTH), F32).at[:GLA_GATE_RANK].set(gk2).astype(BF16)
    args = [gk2p, gkb.reshape(1, -1), ng.reshape(1, -1)]
    full = lambda a: pl.BlockSpec(a.shape, lambda i, j: (0,) * a.ndim)
    return pl.pallas_call(
        _gla_kernel,
        grid=(b, s // CHUNK),
        in_specs=[pl.BlockSpec((1, CHUNK, cols), lambda i, j: (i, j, 0))] + [full(a) for a in args],
        out_specs=pl.BlockSpec((1, CHUNK, GLA_V_WIDTH), lambda i, j: (i, j, 0)),
        out_shape=jax.ShapeDtypeStruct((b, s, GLA_V_WIDTH), BF16),
        scratch_shapes=[pltpu.VMEM((GLA_HEADS, GLA_VAL_DIM, GLA_KEY_DIM), F32)],
        compiler_params=pltpu.CompilerParams(
            dimension_semantics=("arbitrary", "arbitrary"), vmem_limit_bytes=VMEM_LIMIT),
        name="gla",
    )(gla_proj, *args)


def _outproj_kernel(yr_ref, yg_ref, x_ref, wor_ref, wog_ref, nf_ref, wrh_ref, wrl_ref, br_ref,
                    h_ref, u3_ref, route_ref, cnt_ref, carry_ref):
    tm, d = x_ref.shape

    @pl.when(pl.program_id(0) == 0)
    def _():
        carry_ref[...] = jnp.zeros_like(carry_ref)

    h = (x_ref[...] + jnp.dot(yr_ref[...], wor_ref[...], preferred_element_type=F32)
         + jnp.dot(yg_ref[...], wog_ref[...], preferred_element_type=F32))
    h_ref[...] = h
    u = _rmsnorm(h, nf_ref[...])
    for s in range(d // LANES):
        u3_ref[s] = u[:, s * LANES:(s + 1) * LANES]
    u_hi, u_lo = _split(u, 2)
    logits = (jnp.dot(u_hi, wrh_ref[...], preferred_element_type=F32)
              + jnp.dot(u_hi, wrl_ref[...], preferred_element_type=F32)
              + jnp.dot(u_lo, wrh_ref[...], preferred_element_type=F32)) + br_ref[...]
    lane = lax.broadcasted_iota(jnp.int32, logits.shape, 1)
    rest = logits
    picks, idxs, vals = [], [], []
    for r in range(TOP_K):
        m = jnp.max(rest, axis=-1, keepdims=True)
        idx = jnp.min(jnp.where(rest == m, lane, LANES), axis=-1, keepdims=True)
        pick = lane == idx
        picks.append(pick)
        idxs.append(idx)
        vals.append(m)
        rest = jnp.where(pick, -jnp.inf, rest)
    denom = jnp.ones_like(vals[0])
    for r in range(1, TOP_K):
        denom = denom + jnp.exp(vals[r] - vals[0])
    sel = jnp.zeros(logits.shape, F32)
    for pick in picks:
        sel = sel + jnp.where(pick, 1.0, 0.0)
    prefix = _mm(jnp.where(_tri(tm, True), 1.0, 0.0), sel) + carry_ref[...]
    carry_ref[...] = carry_ref[...] + jnp.sum(sel, axis=0, keepdims=True)
    cnt_ref[...] = carry_ref[...]
    route = jnp.zeros(logits.shape, F32)
    for r in range(TOP_K):
        gate = jnp.exp(vals[r] - vals[0]) / denom
        rank = jnp.sum(jnp.where(picks[r], prefix, 0.0), axis=-1, keepdims=True)
        route = jnp.where(lane == r, gate, route)
        route = jnp.where(lane == TOP_K + r, idxs[r].astype(F32), route)
        route = jnp.where(lane == 2 * TOP_K + r, rank, route)
    route_ref[...] = route


def _out_proj(y_rw, y_gla, xf, wo_r, wo_g, nf, wr_hi, wr_lo, br, tm):
    n, d = xf.shape
    full = lambda a: pl.BlockSpec(a.shape, lambda i: (0,) * a.ndim)
    tile = lambda w: pl.BlockSpec((tm, w), lambda i: (i, 0))
    return pl.pallas_call(
        _outproj_kernel,
        grid=(n // tm,),
        in_specs=[tile(y_rw.shape[1]), tile(y_gla.shape[1]), tile(d), full(wo_r), full(wo_g), full(nf),
                  full(wr_hi), full(wr_lo), full(br)],
        out_specs=[tile(d), pl.BlockSpec((d // LANES, tm, LANES), lambda i: (i, 0, 0)), tile(LANES),
                   pl.BlockSpec((1, LANES), lambda i: (0, 0))],
        out_shape=[jax.ShapeDtypeStruct((n, d), F32), jax.ShapeDtypeStruct((n // tm * (d // LANES), tm, LANES), F32),
                   jax.ShapeDtypeStruct((n, LANES), F32), jax.ShapeDtypeStruct((1, LANES), F32)],
        scratch_shapes=[pltpu.VMEM((1, LANES), F32)],
        compiler_params=pltpu.CompilerParams(
            dimension_semantics=("arbitrary",), vmem_limit_bytes=VMEM_LIMIT),
        name="out_proj",
    )(y_rw, y_gla, xf, wo_r, wo_g, nf, wr_hi, wr_lo, br)


assert EXPERT_BLOCK & (EXPERT_BLOCK - 1) == 0
_BLOCK_SHIFT = EXPERT_BLOCK.bit_length() - 1


def _slot_block(slot):
    return lax.shift_right_logical(slot, _BLOCK_SHIFT)


def _slot_row(slot):
    return jnp.bitwise_and(slot, EXPERT_BLOCK - 1)


_DISPATCH_BUFS = 3


def _dispatch_kernel(row_ref, u3_hbm, xs_in_hbm, xs_hbm, ubuf, in_sems, out_sems, *, tm, ns):
    del xs_in_hbm
    i = pl.program_id(0)
    nsteps = pl.num_programs(0)
    nbuf = _DISPATCH_BUFS

    def load(step, slot_):
        return pltpu.make_async_copy(u3_hbm.at[pl.ds(step * ns, ns)], ubuf.at[slot_], in_sems.at[slot_])

    def wait_scatter(slot_):
        for _ in range(TOP_K):
            pltpu.make_async_copy(ubuf.at[slot_], xs_hbm.at[pl.ds(0, ns), pl.ds(0, tm), :],
                                  out_sems.at[slot_]).wait()

    cur = lax.rem(i, nbuf)
    nxt = lax.rem(i + 1, nbuf)

    @pl.when(i == 0)
    def _():
        load(0, 0).start()

    @pl.when(i >= nbuf - 1)
    def _():
        wait_scatter(nxt)

    @pl.when(i + 1 < nsteps)
    def _():
        load(i + 1, nxt).start()

    load(i, cur).wait()

    def issue(r, c):
        src = ubuf.at[cur, :, pl.ds(r, 1), :]
        for kk in range(TOP_K):
            slot = row_ref[(i * tm + r) * TOP_K + kk]
            dst = xs_hbm.at[pl.ds(_slot_block(slot) * ns, ns), pl.ds(_slot_row(slot), 1), :]
            pltpu.make_async_copy(src, dst, out_sems.at[cur]).start()
        return c
    lax.fori_loop(0, tm, issue, 0)

    @pl.when(i == nsteps - 1)
    def _():
        for back in range(nbuf - 1):
            @pl.when(i - back >= 0)
            def _():
                wait_scatter(lax.rem(i - back + nbuf, nbuf))


def _dispatch(slots, u3, n_rows, tm, ns):
    ln = u3.shape[2]
    n = u3.shape[0] // ns * tm
    xs0 = jnp.zeros((n_rows // EXPERT_BLOCK * ns, EXPERT_BLOCK, ln), u3.dtype)
    grid_spec = pltpu.PrefetchScalarGridSpec(
        num_scalar_prefetch=1,
        grid=(n // tm,),
        in_specs=[pl.BlockSpec(memory_space=pl.ANY), pl.BlockSpec(memory_space=pl.ANY)],
        out_specs=pl.BlockSpec(memory_space=pl.ANY),
        scratch_shapes=[pltpu.VMEM((_DISPATCH_BUFS, ns, tm, ln), u3.dtype),
                        pltpu.SemaphoreType.DMA((_DISPATCH_BUFS,)),
                        pltpu.SemaphoreType.DMA((_DISPATCH_BUFS,))],
    )
    return pl.pallas_call(
        functools.partial(_dispatch_kernel, tm=tm, ns=ns),
        grid_spec=grid_spec,
        out_shape=jax.ShapeDtypeStruct(xs0.shape, xs0.dtype),
        input_output_aliases={2: 0},
        compiler_params=pltpu.CompilerParams(dimension_semantics=("arbitrary",), has_side_effects=True),
        name="moe_dispatch",
    )(slots, u3, xs0)


def _moe_kernel(be_ref, nused_ref, xs_ref, w1_ref, b1_ref, w2_ref, b2_ref, ys_ref, w1b, w2b):
    i = pl.program_id(0)
    f = w2_ref.shape[1]
    ns = xs_ref.shape[0]
    e = be_ref[i]
    e_prev = be_ref[jnp.maximum(i - 1, 0)]

    @pl.when(jnp.logical_or(i == 0, e != e_prev))
    def _():
        w1b[...] = w1_ref[0].astype(BF16)
        w2b[...] = w2_ref[0].astype(BF16)

    @pl.when(i < nused_ref[0])
    def _():
        xb = jnp.concatenate([xs_ref[s] for s in range(ns)], axis=1).astype(BF16)
        hgl = jnp.dot(xb, w1b[...], preferred_element_type=F32) + b1_ref[0]
        x_glu = jnp.minimum(hgl[:, :f], SWIGLU_LIMIT)
        x_lin = jnp.clip(hgl[:, f:], -SWIGLU_LIMIT, SWIGLU_LIMIT)
        act = (x_lin + 1.0) * (x_glu * _sigmoid(SWIGLU_ALPHA * x_glu))
        y = jnp.dot(act.astype(BF16), w2b[...], preferred_element_type=F32) + b2_ref[0]
        for s in range(ns):
            ys_ref[s] = y[:, s * LANES:(s + 1) * LANES]

    @pl.when(i >= nused_ref[0])
    def _():
        ys_ref[...] = jnp.zeros_like(ys_ref)


def _moe_experts(block_e, n_used, xs, w1, b1, w2, b2):
    _, blk, ln = xs.shape
    ne, d, f2 = w1.shape
    f = w2.shape[1]
    nb = block_e.shape[0]
    ns = d // ln
    grid_spec = pltpu.PrefetchScalarGridSpec(
        num_scalar_prefetch=2,
        grid=(nb,),
        in_specs=[
            pl.BlockSpec((ns, blk, ln), lambda i, be, nu: (i, 0, 0)),
            pl.BlockSpec((1, d, f2), lambda i, be, nu: (be[i], 0, 0)),
            pl.BlockSpec((1, 1, f2), lambda i, be, nu: (be[i], 0, 0)),
            pl.BlockSpec((1, f, d), lambda i, be, nu: (be[i], 0, 0)),
            pl.BlockSpec((1, 1, d), lambda i, be, nu: (be[i], 0, 0)),
        ],
        out_specs=pl.BlockSpec((ns, blk, ln), lambda i, be, nu: (i, 0, 0)),
        scratch_shapes=[pltpu.VMEM((d, f2), BF16), pltpu.VMEM((f, d), BF16)],
    )
    return pl.pallas_call(
        _moe_kernel,
        grid_spec=grid_spec,
        out_shape=jax.ShapeDtypeStruct(xs.shape, F32),
        compiler_params=pltpu.CompilerParams(
            dimension_semantics=("arbitrary",), vmem_limit_bytes=VMEM_LIMIT),
        name="moe_experts",
    )(block_e, n_used, xs, w1, b1.reshape(ne, 1, f2), w2, b2.reshape(ne, 1, d))


def _final_kernel(row_ref, ys_hbm, h_ref, route_ref, p_ref, npl_ref, wg_ref, wp_ref, nfin_ref, out_ref,
                  gbuf, sem, *, last_layer):
    i = pl.program_id(0)
    tm = h_ref.shape[0]
    ns = gbuf.shape[1]

    def issue(r, c):
        for kk in range(TOP_K):
            slot = row_ref[(i * tm + r) * TOP_K + kk]
            src = ys_hbm.at[pl.ds(_slot_block(slot) * ns, ns), pl.ds(_slot_row(slot), 1), :]
            pltpu.make_async_copy(src, gbuf.at[kk, :, pl.ds(r, 1), :], sem).start()
        return c
    lax.fori_loop(0, tm, issue, 0)
    for kk in range(TOP_K):
        pltpu.make_async_copy(ys_hbm.at[pl.ds(0, ns), pl.ds(0, tm), :], gbuf.at[kk], sem).wait()

    route = route_ref[...]
    h = h_ref[...]
    for kk in range(TOP_K):
        yk = jnp.concatenate([gbuf[kk, s] for s in range(ns)], axis=1)
        h = h + route[:, kk:kk + 1] * yk
    u = _rmsnorm(h, npl_ref[...])
    gate = _sigmoid(_mm(u, wg_ref[...]))
    h = h + gate * _mm(p_ref[...], wp_ref[...])
    out_ref[...] = _rmsnorm(h, nfin_ref[...]) if last_layer else h


def _final(slots, ys, h1, route, pf, npl, wg, wp, nfin, tm, last_layer):
    n, d = h1.shape
    pd = pf.shape[1]
    grid_spec = pltpu.PrefetchScalarGridSpec(
        num_scalar_prefetch=1,
        grid=(n // tm,),
        in_specs=[
            pl.BlockSpec(memory_space=pl.ANY),
            pl.BlockSpec((tm, d), lambda i, s: (i, 0)),
            pl.BlockSpec((tm, LANES), lambda i, s: (i, 0)),
            pl.BlockSpec((tm, pd), lambda i, s: (i, 0)),
            pl.BlockSpec((1, d), lambda i, s: (0, 0)),
            pl.BlockSpec((d, d), lambda i, s: (0, 0)),
            pl.BlockSpec((pd, d), lambda i, s: (0, 0)),
            pl.BlockSpec((1, d), lambda i, s: (0, 0)),
        ],
        out_specs=pl.BlockSpec((tm, d), lambda i, s: (i, 0)),
        scratch_shapes=[pltpu.VMEM((TOP_K, d // ys.shape[2], tm, ys.shape[2]), F32),
                        pltpu.SemaphoreType.DMA(())],
    )
    return pl.pallas_call(
        functools.partial(_final_kernel, last_layer=last_layer),
        grid_spec=grid_spec,
        out_shape=jax.ShapeDtypeStruct((n, d), F32),
        compiler_params=pltpu.CompilerParams(
            dimension_semantics=("arbitrary",), vmem_limit_bytes=VMEM_LIMIT),
        name="final",
    )(slots, ys, h1, route, pf, npl, wg, wp, nfin)


def _routing(route, counts, n_experts):
    n = route.shape[0]
    blk = EXPERT_BLOCK
    nb = (n * TOP_K + n_experts * (blk - 1) + blk - 1) // blk
    counts = counts[0, :n_experts].astype(jnp.int32)
    pcounts = ((counts + blk - 1) // blk) * blk
    pend = jnp.cumsum(pcounts)
    pstart = pend - pcounts
    top_i = route[:, TOP_K:2 * TOP_K].astype(jnp.int32)
    rank = route[:, 2 * TOP_K:3 * TOP_K].astype(jnp.int32)
    onehot = top_i[:, :, None] == jnp.arange(n_experts, dtype=jnp.int32)[None, None, :]
    slots = jnp.sum(jnp.where(onehot, pstart[None, None, :], 0), axis=-1) + rank
    block_rows = jnp.arange(nb, dtype=jnp.int32) * blk
    block_e = jnp.sum(block_rows[:, None] >= pend[None, :], axis=1)
    block_e = jnp.clip(block_e, 0, n_experts - 1).astype(jnp.int32)
    n_used = (pend[-1] // blk).astype(jnp.int32).reshape(1)
    return block_e, n_used, slots.reshape(-1).astype(jnp.int32), nb * blk


def kernel(x, p, norm_mix, w_in, shift_mu, rw_w0, rw_w2, rw_a0, rw_a2, rw_g2, rw_kk, rw_ka, rw_rk,
           rw_gn_w, rw_gn_b, gla_gk2, gla_gk_b, gla_norm, w_out, norm_ffn, w_router, b_router,
           w1, b1, w2, b2, norm_ple, w_ple_gate, w_ple, norm_final):
    bsz, seq, d = x.shape
    n = bsz * seq
    depth = w_in.shape[0]
    n_experts = w_router.shape[-1]
    tm = 256
    h = x.reshape(n, d)
    for l in range(depth):
        w_rw = w_in[l][:, :RW_COLS].astype(BF16)
        w_gla = jnp.pad(w_in[l][:, RW_COLS:], ((0, 0), (0, LANES - GLA_GATE_RANK))).astype(BF16)
        rw_proj, gla_proj = _in_proj(h, norm_mix[l].reshape(1, d), w_rw, w_gla, tm)
        y_rw = _rwkv(rw_proj.reshape(bsz, seq, -1), shift_mu[l], rw_w0[l], rw_w2[l], rw_a0[l], rw_a2[l],
                     rw_g2[l], rw_kk[l], rw_ka[l], rw_rk[l], rw_gn_w[l], rw_gn_b[l])
        y_gla = _gla(gla_proj.reshape(bsz, seq, -1), gla_gk2[l], gla_gk_b[l], gla_norm[l])

        wr = jnp.pad(w_router[l], ((0, 0), (0, LANES - n_experts)))
        wr_hi = wr.astype(BF16)
        wr_lo = (wr - wr_hi.astype(F32)).astype(BF16)
        br = jnp.pad(b_router[l], (0, LANES - n_experts), constant_values=NEG_BIG).reshape(1, LANES)
        wo = w_out[l].astype(BF16)
        h1, u3, route, counts = _out_proj(y_rw.reshape(n, -1), y_gla.reshape(n, -1), h, wo[:RW_WIDTH],
                                          wo[RW_WIDTH:], norm_ffn[l].reshape(1, d), wr_hi, wr_lo, br, tm)

        ns = d // LANES
        block_e, n_used, slots, n_rows = _routing(route, counts, n_experts)
        xs = _dispatch(slots, u3, n_rows, tm, ns)
        ys = _moe_experts(block_e, n_used, xs, w1[l], b1[l], w2[l], b2[l])
        h = _final(slots, ys, h1, route, p[l].reshape(n, -1), norm_ple[l].reshape(1, d),
                   w_ple_gate[l].astype(BF16), w_ple[l].astype(BF16), norm_final.reshape(1, d), tm, l == depth - 1)
    return h.reshape(bsz, seq, d)
```

```python
import functools

import jax
import jax.numpy as jnp
from jax import lax
from jax.experimental import pallas as pl
from jax.experimental.pallas import tpu as pltpu

F32 = jnp.float32
BF16 = jnp.bfloat16

CHUNK = 64
RW_HEADS = 8
RW_HEAD_DIM = 64
RW_WIDTH = RW_HEADS * RW_HEAD_DIM
RW_DECAY_LORA = 64
RW_ICLR_LORA = 64
RW_GATE_LORA = 128
RW_COLS = 3 * RW_WIDTH + RW_DECAY_LORA + RW_ICLR_LORA + RW_GATE_LORA
RW_GN_EPS = 64e-5
GLA_HEADS = 4
GLA_KEY_DIM = 64
GLA_VAL_DIM = 128
GLA_QK_WIDTH = GLA_HEADS * GLA_KEY_DIM
GLA_V_WIDTH = GLA_HEADS * GLA_VAL_DIM
GLA_GATE_RANK = 16
GLA_GATE_TEMP = 16.0
GLA_SUB = 8
LANES = 128
GLA_COLS_PAD = 2 * GLA_QK_WIDTH + 2 * GLA_V_WIDTH + LANES
TOP_K = 4
EXPERT_BLOCK = 256
SWIGLU_ALPHA = 1.702
SWIGLU_LIMIT = 7.0
NORM_EPS = 1e-6
NEG_BIG = -1e30
VMEM_LIMIT = 56 * 1024 * 1024


def _mm(a, b):
    return jnp.dot(a.astype(BF16), b.astype(BF16), preferred_element_type=F32)


def _mm_nt(a, b):
    return lax.dot_general(a.astype(BF16), b.astype(BF16), (((1,), (1,)), ((), ())),
                           preferred_element_type=F32)


def _mm_tn(a, b):
    return lax.dot_general(a.astype(BF16), b.astype(BF16), (((0,), (0,)), ((), ())),
                           preferred_element_type=F32)


def _split(a, n):
    parts = []
    rem = a
    for _ in range(n):
        p = rem.astype(BF16)
        parts.append(p)
        rem = rem - p.astype(F32)
    return parts


def _mm_lhs_split(a, b_bf16, n):
    out = None
    for p in _split(a, n):
        t = jnp.dot(p, b_bf16, preferred_element_type=F32)
        out = t if out is None else out + t
    return out


def _mm_rhs_split(a_bf16, b, n):
    out = None
    for p in _split(b, n):
        t = jnp.dot(a_bf16, p, preferred_element_type=F32)
        out = t if out is None else out + t
    return out


def _rmsnorm(x, g):
    return x * lax.rsqrt(jnp.mean(x * x, axis=-1, keepdims=True) + NORM_EPS) * g


def _softplus(x):
    return jnp.maximum(x, 0.0) + jnp.log(1.0 + jnp.exp(-jnp.abs(x)))


def _sigmoid(x):
    return 1.0 / (1.0 + jnp.exp(-x))


def _tri(n, strict):
    r = lax.broadcasted_iota(jnp.int32, (n, n), 0)
    c = lax.broadcasted_iota(jnp.int32, (n, n), 1)
    return (r > c) if strict else (r >= c)


def _inproj_kernel(x_ref, g_ref, wr_ref, wg_ref, rw_ref, gla_ref):
    u = _rmsnorm(x_ref[...], g_ref[...]).astype(BF16)
    rw_ref[...] = jnp.dot(u, wr_ref[...], preferred_element_type=F32)
    gla_ref[...] = jnp.dot(u, wg_ref[...], preferred_element_type=F32)


def _in_proj(xf, g, w_rw, w_gla, tm):
    n, d = xf.shape
    return pl.pallas_call(
        _inproj_kernel,
        grid=(n // tm,),
        in_specs=[
            pl.BlockSpec((tm, d), lambda i: (i, 0)),
            pl.BlockSpec((1, d), lambda i: (0, 0)),
            pl.BlockSpec(w_rw.shape, lambda i: (0, 0)),
            pl.BlockSpec(w_gla.shape, lambda i: (0, 0)),
        ],
        out_specs=[
            pl.BlockSpec((tm, w_rw.shape[1]), lambda i: (i, 0)),
            pl.BlockSpec((tm, w_gla.shape[1]), lambda i: (i, 0)),
        ],
        out_shape=[
            jax.ShapeDtypeStruct((n, w_rw.shape[1]), F32),
            jax.ShapeDtypeStruct((n, w_gla.shape[1]), F32),
        ],
        compiler_params=pltpu.CompilerParams(
            dimension_semantics=("arbitrary",), vmem_limit_bytes=VMEM_LIMIT),
        name="in_proj",
    )(xf, g, w_rw, w_gla)


def _rwkv_kernel_unpacked(x_ref, mu_ref, w0_ref, w2_ref, a0_ref, a2_ref, g2_ref, kkw_ref, ka_ref,
                          rk_ref, gnw_ref, gnb_ref, bd_ref, y_ref, state_ref, carry_ref, o_ref):
    C, H, D = CHUNK, RW_HEADS, RW_HEAD_DIM

    @pl.when(pl.program_id(1) == 0)
    def _():
        state_ref[...] = jnp.zeros_like(state_ref)
        carry_ref[...] = jnp.zeros_like(carry_ref)

    x = x_ref[0]
    row = lax.broadcasted_iota(jnp.int32, x.shape, 0)
    prev = jnp.where(row == 0, carry_ref[...], pltpu.roll(x, 1, axis=0))
    carry_ref[...] = x[C - 1:C, :]
    h = x + (prev - x) * mu_ref[...]

    W = RW_WIDTH
    r = h[:, 0:W]
    k = h[:, W:2 * W]
    v = h[:, 2 * W:3 * W]
    o0 = 3 * W
    dw = h[:, o0:o0 + RW_DECAY_LORA]
    da = h[:, o0 + RW_DECAY_LORA:o0 + RW_DECAY_LORA + RW_ICLR_LORA]
    dg = h[:, o0 + RW_DECAY_LORA + RW_ICLR_LORA:]

    bd = bd_ref[...]

    def seg_sum(t):
        return _mm_lhs_split(t, bd, 2)

    w_log = -_softplus(-(w0_ref[...] + _mm(jnp.tanh(dw), w2_ref[...]))) - 0.5
    lw = -jnp.exp(w_log)
    iclr = _sigmoid(a0_ref[...] + _mm(da, a2_ref[...]))
    gate = _mm(_sigmoid(dg), g2_ref[...])

    kk = k * kkw_ref[...]
    kk = kk / jnp.maximum(jnp.sqrt(seg_sum(kk * kk)), 1e-12)
    k2 = k * (1.0 + (iclr - 1.0) * ka_ref[...])

    cw = _mm_rhs_split(_tri(C, False).astype(BF16), lw, 3)
    cw_last = cw[C - 1:C, :]
    e_cw = jnp.exp(cw)
    e_ncw = jnp.exp(-cw)
    e_rem = jnp.exp(cw_last - cw)
    kka = kk * iclr
    a_t = -kk * jnp.exp(cw - lw)
    r_t = r * e_cw
    b_t = kka * e_ncw
    k_t = k2 * e_ncw
    b_h = kka * e_rem
    k_h = k2 * e_rem
    w_c = jnp.exp(cw_last)

    strict = _tri(C, True)
    incl = _tri(C, False)
    eye = jnp.where(_tri(C, False) & ~strict, 1.0, 0.0).astype(F32)

    hs = range(H)
    sls = [slice(hd * D, (hd + 1) * D) for hd in hs]
    states = [state_ref[hd] for hd in hs]
    ah = [a_t[:, sl] for sl in sls]
    rh = [r_t[:, sl] for sl in sls]
    bh = [b_t[:, sl] for sl in sls]
    kh = [k_t[:, sl] for sl in sls]
    vh = [v[:, sl] for sl in sls]
    a_ab = [jnp.where(strict, _mm_nt(ah[i], bh[i]), 0.0) for i in hs]
    a_ak = [jnp.where(strict, _mm_nt(ah[i], kh[i]), 0.0) for i in hs]
    a_rb = [jnp.where(incl, _mm_nt(rh[i], bh[i]), 0.0) for i in hs]
    a_rk = [jnp.where(incl, _mm_nt(rh[i], kh[i]), 0.0) for i in hs]
    p = [_mm(a_ab[i], a_ab[i]) for i in hs]
    q = [eye + a_ab[i] for i in hs]
    for _ in range(4):
        pq = [_mm(p[i], q[i]) for i in hs]
        p = [_mm(p[i], p[i]) for i in hs]
        q = [q[i] + pq[i] for i in hs]
    pq = [_mm(p[i], q[i]) for i in hs]
    t_inv = [q[i] + pq[i] for i in hs]
    akv = [_mm(a_ak[i], vh[i]) for i in hs]
    u0 = [_mm(t_inv[i], akv[i]) for i in hs]
    a_hat = [_mm(t_inv[i], ah[i]) for i in hs]
    u = [_mm_nt(a_hat[i], states[i]) + u0[i] for i in hs]
    o_heads = [_mm_nt(rh[i], states[i]) + _mm(a_rb[i], u[i]) + _mm(a_rk[i], vh[i]) for i in hs]
    new_states = [states[i] * w_c[:, sls[i]] + _mm_tn(u[i], b_h[:, sls[i]]) + _mm_tn(vh[i], k_h[:, sls[i]])
                  for i in hs]
    for hd in hs:
        state_ref[hd] = new_states[hd]
        o_ref[:, sls[hd]] = o_heads[hd]

    o = o_ref[...]
    inv_d = 1.0 / D
    mean = seg_sum(o) * inv_d
    dlt = o - mean
    var = seg_sum(dlt * dlt) * inv_d
    o = dlt * lax.rsqrt(var + RW_GN_EPS) * gnw_ref[...] + gnb_ref[...]
    bonus = seg_sum(r * k2 * rk_ref[...]) * v
    y_ref[0] = ((o + bonus) * gate).astype(y_ref.dtype)


def _rwkv_unpacked(rw_proj, mu, w0, w2, a0, a2, g2, kkw, ka, rk, gnw, gnb):
    b, s, cols = rw_proj.shape
    W = RW_WIDTH
    bd = jnp.kron(jnp.eye(RW_HEADS, dtype=F32), jnp.ones((RW_HEAD_DIM, RW_HEAD_DIM), F32)).astype(BF16)
    row = lambda a: a.reshape(1, -1)
    full = lambda a: pl.BlockSpec(a.shape, lambda i, j: (0,) * a.ndim)
    args = [row(mu), row(w0), w2.astype(BF16), row(a0), a2.astype(BF16), g2.astype(BF16),
            row(kkw), row(ka), row(rk), row(gnw), row(gnb), bd]
    return pl.pallas_call(
        _rwkv_kernel_unpacked,
        grid=(b, s // CHUNK),
        in_specs=[pl.BlockSpec((1, CHUNK, cols), lambda i, j: (i, j, 0))] + [full(a) for a in args],
        out_specs=pl.BlockSpec((1, CHUNK, W), lambda i, j: (i, j, 0)),
        out_shape=jax.ShapeDtypeStruct((b, s, W), BF16),
        scratch_shapes=[
            pltpu.VMEM((RW_HEADS, RW_HEAD_DIM, RW_HEAD_DIM), F32),
            pltpu.VMEM((1, cols), F32),
            pltpu.VMEM((CHUNK, W), F32),
        ],
        compiler_params=pltpu.CompilerParams(
            dimension_semantics=("arbitrary", "arbitrary"), vmem_limit_bytes=VMEM_LIMIT),
        name="rwkv7",
    )(rw_proj, *args)


RW_GROUP = 4
RW_GW = RW_GROUP * RW_HEAD_DIM
RW_STEP_CHUNKS = 2


def _rwkv_kernel(x_ref, mu_ref, w0_ref, w2_ref, a0_ref, a2_ref, g2_ref, kkw_ref, ka_ref,
                 rk_ref, gnw_ref, gnb_ref, bd_ref, y_ref, state_ref, carry_ref):
    C, D, W, GW = CHUNK, RW_HEAD_DIM, RW_WIDTH, RW_GW
    nb, ct = x_ref.shape[0], x_ref.shape[1]
    nch = ct // C
    R = nb * ct

    @pl.when(pl.program_id(0) == 0)
    def _():
        state_ref[...] = jnp.zeros_like(state_ref)
        carry_ref[...] = jnp.zeros_like(carry_ref)

    x = x_ref[...].reshape(R, x_ref.shape[2])
    row = lax.broadcasted_iota(jnp.int32, x.shape, 0)
    prev = pltpu.roll(x, 1, axis=0)
    for b in range(nb):
        prev = jnp.where(row == b * ct, carry_ref[b:b + 1, :], prev)
        carry_ref[b:b + 1, :] = x[(b + 1) * ct - 1:(b + 1) * ct, :]
    h = x + (prev - x) * mu_ref[...]

    r = h[:, 0:W]
    k = h[:, W:2 * W]
    v = h[:, 2 * W:3 * W]
    o0 = 3 * W
    dw = h[:, o0:o0 + RW_DECAY_LORA]
    da = h[:, o0 + RW_DECAY_LORA:o0 + RW_DECAY_LORA + RW_ICLR_LORA]
    dg = h[:, o0 + RW_DECAY_LORA + RW_ICLR_LORA:]

    bd_g = bd_ref[...]

    def seg_sum(t):
        return jnp.concatenate(
            [_mm_lhs_split(t[:, g * GW:(g + 1) * GW], bd_g, 2) for g in range(W // GW)], axis=1)

    w_log = -_softplus(-(w0_ref[...] + _mm(jnp.tanh(dw), w2_ref[...]))) - 0.5
    lw = -jnp.exp(w_log)
    iclr = _sigmoid(a0_ref[...] + _mm(da, a2_ref[...]))
    gate = _mm(_sigmoid(dg), g2_ref[...])

    kk = k * kkw_ref[...]
    kk = kk / jnp.maximum(jnp.sqrt(seg_sum(kk * kk)), 1e-12)
    k2 = k * (1.0 + (iclr - 1.0) * ka_ref[...])

    rr = lax.broadcasted_iota(jnp.int32, (R, R), 0)
    cc = lax.broadcasted_iota(jnp.int32, (R, R), 1)
    tri_seq = jnp.where((rr >= cc) & (rr // C == cc // C), 1.0, 0.0).astype(BF16)
    cw = _mm_rhs_split(tri_seq, lw, 3)
    cw_last = jnp.concatenate(
        [jnp.broadcast_to(cw[(j + 1) * C - 1:(j + 1) * C, :], (C, W)) for j in range(R // C)], axis=0)
    e_cw = jnp.exp(cw)
    e_ncw = jnp.exp(-cw)
    e_rem = jnp.exp(cw_last - cw)
    kka = kk * iclr
    a_t = -kk * jnp.exp(cw - lw)
    r_t = r * e_cw
    b_t = kka * e_ncw
    k_t = k2 * e_ncw
    b_h = kka * e_rem
    k_h = k2 * e_rem
    w_c = jnp.exp(cw_last)

    ti = lax.broadcasted_iota(jnp.int32, (C, GW), 0)
    si = jnp.bitwise_and(lax.broadcasted_iota(jnp.int32, (C, GW), 1), D - 1)
    strict = ti > si
    incl = ti >= si
    eye = jnp.where(ti == si, 1.0, 0.0)

    def bdiag(y):
        yb = y.astype(BF16)
        return jnp.concatenate([yb] * RW_GROUP, axis=0) * bd_g

    ng = W // GW
    chains = [(b, g, ch) for ch in range(nch) for b in range(nb) for g in range(ng)]

    def part(t, c):
        b, g, ch = c
        r0 = b * ct + ch * C
        return t[r0:r0 + C, g * GW:(g + 1) * GW]

    n = range(len(chains))
    a4 = [part(a_t, c) for c in chains]
    r4 = [part(r_t, c) for c in chains]
    v4 = [part(v, c) for c in chains]
    ar = [jnp.concatenate([a4[i], r4[i]], axis=0) for i in n]
    bd_b = [bdiag(part(b_t, c)) for c in chains]
    bd_k = [bdiag(part(k_t, c)) for c in chains]
    bd_v = [bdiag(v4[i]) for i in n]
    m_b = [_mm_nt(ar[i], bd_b[i]) for i in n]
    m_k = [_mm_nt(ar[i], bd_k[i]) for i in n]
    a_ab = [jnp.where(strict, m_b[i][0:C], 0.0) for i in n]
    a_rb = [jnp.where(incl, m_b[i][C:2 * C], 0.0) for i in n]
    a_ak = [jnp.where(strict, m_k[i][0:C], 0.0) for i in n]
    a_rk = [jnp.where(incl, m_k[i][C:2 * C], 0.0) for i in n]
    akv = [_mm(a_ak[i], bd_v[i]) for i in n]
    o_kv = [_mm(a_rk[i], bd_v[i]) for i in n]
    p = [_mm(a_ab[i], bdiag(a_ab[i])) for i in n]
    q = [eye + a_ab[i] for i in n]
    for _ in range(4):
        pq = [_mm(p[i], bdiag(q[i])) for i in n]
        p = [_mm(p[i], bdiag(p[i])) for i in n]
        q = [q[i] + pq[i] for i in n]
    pq = [_mm(p[i], bdiag(q[i])) for i in n]
    t_inv = [q[i] + pq[i] for i in n]
    u0 = [_mm(t_inv[i], bdiag(akv[i])) for i in n]
    a_hat = [_mm(t_inv[i], bdiag(a4[i])) for i in n]
    seqs = [(b, g) for b in range(nb) for g in range(ng)]
    s4 = [state_ref[b, g] for b, g in seqs]
    o_g = {}
    for ch in range(nch):
        idx = [chains.index((b, g, ch)) for b, g in seqs]
        bd_s = [bdiag(s) for s in s4]
        o_s = [_mm_nt(r4[i], bd_s[j]) for j, i in enumerate(idx)]
        u = [_mm_nt(a_hat[i], bd_s[j]) + u0[i] for j, i in enumerate(idx)]
        for j, i in enumerate(idx):
            o_g[chains[i]] = o_s[j] + _mm(a_rb[i], bdiag(u[j])) + o_kv[i]
        nxt = []
        for j, i in enumerate(idx):
            c = chains[i]
            uv = jnp.concatenate([u[j], v4[i]], axis=0)
            bk = jnp.concatenate([part(b_h, c), part(k_h, c)], axis=0)
            full = _mm_tn(uv, bk) * bd_g
            upd = full[0:D]
            for hd in range(1, RW_GROUP):
                upd = upd + full[hd * D:(hd + 1) * D]
            nxt.append(s4[j] * part(w_c, c)[0:D] + upd)
        s4 = nxt
    for j, (b, g) in enumerate(seqs):
        state_ref[b, g] = s4[j]

    o = jnp.concatenate(
        [jnp.concatenate([o_g[(b, g, ch)] for g in range(ng)], axis=1) for b in range(nb) for ch in range(nch)],
        axis=0)
    inv_d = 1.0 / D
    mean = seg_sum(o) * inv_d
    dlt = o - mean
    var = seg_sum(dlt * dlt) * inv_d
    o = dlt * lax.rsqrt(var + RW_GN_EPS) * gnw_ref[...] + gnb_ref[...]
    bonus = seg_sum(r * k2 * rk_ref[...]) * v
    y_ref[...] = ((o + bonus) * gate).astype(y_ref.dtype).reshape(y_ref.shape)


def _rwkv(rw_proj, mu, w0, w2, a0, a2, g2, kkw, ka, rk, gnw, gnb):
    b, s, cols = rw_proj.shape
    W = RW_WIDTH
    bd = jnp.kron(jnp.eye(RW_GROUP, dtype=F32), jnp.ones((RW_HEAD_DIM, RW_HEAD_DIM), F32)).astype(BF16)
    row = lambda a: a.reshape(1, -1)
    full = lambda a: pl.BlockSpec(a.shape, lambda j: (0,) * a.ndim)
    args = [row(mu), row(w0), w2.astype(BF16), row(a0), a2.astype(BF16), g2.astype(BF16),
            row(kkw), row(ka), row(rk), row(gnw), row(gnb), bd]
    ct = CHUNK * RW_STEP_CHUNKS
    return pl.pallas_call(
        _rwkv_kernel,
        grid=(s // ct,),
        in_specs=[pl.BlockSpec((b, ct, cols), lambda j: (0, j, 0))] + [full(a) for a in args],
        out_specs=pl.BlockSpec((b, ct, W), lambda j: (0, j, 0)),
        out_shape=jax.ShapeDtypeStruct((b, s, W), BF16),
        scratch_shapes=[
            pltpu.VMEM((b, W // RW_GW, RW_HEAD_DIM, RW_GW), F32),
            pltpu.VMEM((b, cols), F32),
        ],
        compiler_params=pltpu.CompilerParams(
            dimension_semantics=("arbitrary",), vmem_limit_bytes=VMEM_LIMIT),
        name="rwkv7",
    )(rw_proj, *args)


def _gla_kernel(x_ref, gk2_ref, gkb_ref, ng_ref, y_ref, state_ref):
    C, H, DK, DV, SB = CHUNK, GLA_HEADS, GLA_KEY_DIM, GLA_VAL_DIM, GLA_SUB

    @pl.when(pl.program_id(1) == 0)
    def _():
        state_ref[...] = jnp.zeros_like(state_ref)

    x = x_ref[0]
    QW, VW = GLA_QK_WIDTH, GLA_V_WIDTH
    q = x[:, 0:QW] * (DK ** -0.5)
    k = x[:, QW:2 * QW]
    v = x[:, 2 * QW:2 * QW + VW]
    g = x[:, 2 * QW + VW:2 * QW + 2 * VW]
    dgk = x[:, 2 * QW + 2 * VW:]

    la = -_softplus(-(_mm(dgk, gk2_ref[...]) + gkb_ref[...])) * (1.0 / GLA_GATE_TEMP)
    b = _mm_rhs_split(_tri(C, False).astype(BF16), la, 3)
    b_last = b[C - 1:C, :]
    q_e = q * jnp.exp(b)
    k_e = k * jnp.exp(b_last - b)
    w_c = jnp.exp(b_last)

    ii = lax.broadcasted_iota(jnp.int32, (C // SB, SB, SB, 2 * DK), 1)
    jj = lax.broadcasted_iota(jnp.int32, (C // SB, SB, SB, 2 * DK), 2)
    causal4 = jj <= ii

    hs = range(H)
    nblk = C // SB
    sks = [slice(hd * DK, (hd + 1) * DK) for hd in hs]
    svs = [slice(hd * DV, (hd + 1) * DV) for hd in hs]
    states = [state_ref[hd] for hd in hs]
    qh = [q[:, sk] for sk in sks]
    kh = [k[:, sk] for sk in sks]
    bh = [b[:, sk] for sk in sks]
    vh = [v[:, sv] for sv in svs]
    o_inter = [_mm_nt(q_e[:, sks[i]], states[i]) for i in hs]
    new_states = [states[i] * w_c[:, sks[i]] + _mm_tn(vh[i], k_e[:, sks[i]]) for i in hs]
    att_off = {}
    for blk in range(1, nblk):
        r0 = blk * SB
        for i in hs:
            ref = bh[i][r0 - 1:r0]
            q_s = qh[i][r0:r0 + SB] * jnp.exp(bh[i][r0:r0 + SB] - ref)
            k_s = kh[i][0:r0] * jnp.exp(ref - bh[i][0:r0])
            att_off[(i, blk)] = _mm_nt(q_s, k_s)
    o_off = {key: _mm(att, vh[key[0]][0:key[1] * SB]) for key, att in att_off.items()}
    rsel = lax.broadcasted_iota(jnp.int32, (C, C * SB), 0)
    csel = lax.broadcasted_iota(jnp.int32, (C, C * SB), 1)
    sel = jnp.where((csel >= rsel * SB) & (csel < (rsel + 1) * SB), 1.0, 0.0).astype(BF16)
    drow = lax.broadcasted_iota(jnp.int32, (2 * DK, 2 * DV), 0)
    dcol = lax.broadcasted_iota(jnp.int32, (2 * DK, 2 * DV), 1)
    pair_ones = jnp.where((drow >= DK) == (dcol >= DV), 1.0, 0.0).astype(BF16)
    o_diag = []
    for pr in range(H // 2):
        sl2 = slice(2 * pr * DK, 2 * (pr + 1) * DK)
        q2, k2, b2 = q[:, sl2], k[:, sl2], b[:, sl2]
        dec = jnp.exp(jnp.where(causal4, b2.reshape(nblk, SB, 1, 2 * DK) - b2.reshape(nblk, 1, SB, 2 * DK),
                                NEG_BIG))
        pw = q2.reshape(nblk, SB, 1, 2 * DK) * k2.reshape(nblk, 1, SB, 2 * DK) * dec
        att2 = _mm(pw.reshape(C * SB, 2 * DK), pair_ones)
        for t in range(2):
            v_rep = jnp.broadcast_to(vh[2 * pr + t].reshape(nblk, 1, SB, DV), (nblk, SB, SB, DV))
            z = att2[:, t * DV:(t + 1) * DV] * v_rep.reshape(C * SB, DV)
            o_diag.append(_mm(sel, z))
    outs = []
    for i in hs:
        rows = [jnp.zeros((SB, DV), F32)] + [o_off[(i, blk)] for blk in range(1, nblk)]
        o = o_inter[i] + o_diag[i] + jnp.concatenate(rows, axis=0)
        outs.append(o * lax.rsqrt(jnp.mean(o * o, axis=-1, keepdims=True) + NORM_EPS))
    for hd in hs:
        state_ref[hd] = new_states[hd]
    o = jnp.concatenate(outs, axis=1)
    y = o * ng_ref[...] * (g * _sigmoid(g))
    y_ref[0] = y.astype(y_ref.dtype)


def _gla(gla_proj, gk2, gkb, ng):
    b, s, cols = gla_proj.shape
    gk2p = jnp.zeros((LANES, GLA_QK_WIDTH), F32).at[:GLA_GATE_RANK].set(gk2).astype(BF16)
    args = [gk2p, gkb.reshape(1, -1), ng.reshape(1, -1)]
    full = lambda a: pl.BlockSpec(a.shape, lambda i, j: (0,) * a.ndim)
    return pl.pallas_call(
        _gla_kernel,
        grid=(b, s // CHUNK),
        in_specs=[pl.BlockSpec((1, CHUNK, cols), lambda i, j: (i, j, 0))] + [full(a) for a in args],
        out_specs=pl.BlockSpec((1, CHUNK, GLA_V_WIDTH), lambda i, j: (i, j, 0)),
        out_shape=jax.ShapeDtypeStruct((b, s, GLA_V_WIDTH), BF16),
        scratch_shapes=[pltpu.VMEM((GLA_HEADS, GLA_VAL_DIM, GLA_KEY_DIM), F32)],
        compiler_params=pltpu.CompilerParams(
            dimension_semantics=("arbitrary", "arbitrary"), vmem_limit_bytes=VMEM_LIMIT),
        name="gla",
    )(gla_proj, *args)


def _outproj_kernel(yr_ref, yg_ref, x_ref, wor_ref, wog_ref, nf_ref, wrh_ref, wrl_ref, br_ref,
                    h_ref, u3_ref, route_ref, cnt_ref, carry_ref):
    tm, d = x_ref.shape

    @pl.when(pl.program_id(0) == 0)
    def _():
        carry_ref[...] = jnp.zeros_like(carry_ref)

    h = (x_ref[...] + jnp.dot(yr_ref[...], wor_ref[...], preferred_element_type=F32)
         + jnp.dot(yg_ref[...], wog_ref[...], preferred_element_type=F32))
    h_ref[...] = h
    u = _rmsnorm(h, nf_ref[...])
    for s in range(d // LANES):
        u3_ref[s] = u[:, s * LANES:(s + 1) * LANES]
    u_hi, u_lo = _split(u, 2)
    logits = (jnp.dot(u_hi, wrh_ref[...], preferred_element_type=F32)
              + jnp.dot(u_hi, wrl_ref[...], preferred_element_type=F32)
              + jnp.dot(u_lo, wrh_ref[...], preferred_element_type=F32)) + br_ref[...]
    lane = lax.broadcasted_iota(jnp.int32, logits.shape, 1)
    rest = logits
    picks, idxs, vals = [], [], []
    for r in range(TOP_K):
        m = jnp.max(rest, axis=-1, keepdims=True)
        idx = jnp.min(jnp.where(rest == m, lane, LANES), axis=-1, keepdims=True)
        pick = lane == idx
        picks.append(pick)
        idxs.append(idx)
        vals.append(m)
        rest = jnp.where(pick, -jnp.inf, rest)
    denom = jnp.ones_like(vals[0])
    for r in range(1, TOP_K):
        denom = denom + jnp.exp(vals[r] - vals[0])
    sel = jnp.zeros(logits.shape, F32)
    for pick in picks:
        sel = sel + jnp.where(pick, 1.0, 0.0)
    prefix = _mm(jnp.where(_tri(tm, True), 1.0, 0.0), sel) + carry_ref[...]
    carry_ref[...] = carry_ref[...] + jnp.sum(sel, axis=0, keepdims=True)
    cnt_ref[...] = carry_ref[...]
    route = jnp.zeros(logits.shape, F32)
    for r in range(TOP_K):
        gate = jnp.exp(vals[r] - vals[0]) / denom
        rank = jnp.sum(jnp.where(picks[r], prefix, 0.0), axis=-1, keepdims=True)
        route = jnp.where(lane == r, gate, route)
        route = jnp.where(lane == TOP_K + r, idxs[r].astype(F32), route)
        route = jnp.where(lane == 2 * TOP_K + r, rank, route)
    route_ref[...] = route


def _out_proj(y_rw, y_gla, xf, wo_r, wo_g, nf, wr_hi, wr_lo, br, tm):
    n, d = xf.shape
    full = lambda a: pl.BlockSpec(a.shape, lambda i: (0,) * a.ndim)
    tile = lambda w: pl.BlockSpec((tm, w), lambda i: (i, 0))
    return pl.pallas_call(
        _outproj_kernel,
        grid=(n // tm,),
        in_specs=[tile(y_rw.shape[1]), tile(y_gla.shape[1]), tile(d), full(wo_r), full(wo_g), full(nf),
                  full(wr_hi), full(wr_lo), full(br)],
        out_specs=[tile(d), pl.BlockSpec((d // LANES, tm, LANES), lambda i: (i, 0, 0)), tile(LANES),
                   pl.BlockSpec((1, LANES), lambda i: (0, 0))],
        out_shape=[jax.ShapeDtypeStruct((n, d), F32), jax.ShapeDtypeStruct((n // tm * (d // LANES), tm, LANES), F32),
                   jax.ShapeDtypeStruct((n, LANES), F32), jax.ShapeDtypeStruct((1, LANES), F32)],
        scratch_shapes=[pltpu.VMEM((1, LANES), F32)],
        compiler_params=pltpu.CompilerParams(
            dimension_semantics=("arbitrary",), vmem_limit_bytes=VMEM_LIMIT),
        name="out_proj",
    )(y_rw, y_gla, xf, wo_r, wo_g, nf, wr_hi, wr_lo, br)


assert EXPERT_BLOCK & (EXPERT_BLOCK - 1) == 0
_BLOCK_SHIFT = EXPERT_BLOCK.bit_length() - 1


def _slot_block(slot):
    return lax.shift_right_logical(slot, _BLOCK_SHIFT)


def _slot_row(slot):
    return jnp.bitwise_and(slot, EXPERT_BLOCK - 1)


_DISPATCH_BUFS = 3


def _dispatch_kernel(row_ref, u3_hbm, xs_in_hbm, xs_hbm, ubuf, in_sems, out_sems, *, tm, ns):
    del xs_in_hbm
    i = pl.program_id(0)
    nsteps = pl.num_programs(0)
    nbuf = _DISPATCH_BUFS

    def load(step, slot_):
        return pltpu.make_async_copy(u3_hbm.at[pl.ds(step * ns, ns)], ubuf.at[slot_], in_sems.at[slot_])

    def wait_scatter(slot_):
        for _ in range(TOP_K):
            pltpu.make_async_copy(ubuf.at[slot_], xs_hbm.at[pl.ds(0, ns), pl.ds(0, tm), :],
                                  out_sems.at[slot_]).wait()

    cur = lax.rem(i, nbuf)
    nxt = lax.rem(i + 1, nbuf)

    @pl.when(i == 0)
    def _():
        load(0, 0).start()

    @pl.when(i >= nbuf - 1)
    def _():
        wait_scatter(nxt)

    @pl.when(i + 1 < nsteps)
    def _():
        load(i + 1, nxt).start()

    load(i, cur).wait()

    def issue(r, c):
        src = ubuf.at[cur, :, pl.ds(r, 1), :]
        for kk in range(TOP_K):
            slot = row_ref[(i * tm + r) * TOP_K + kk]
            dst = xs_hbm.at[pl.ds(_slot_block(slot) * ns, ns), pl.ds(_slot_row(slot), 1), :]
            pltpu.make_async_copy(src, dst, out_sems.at[cur]).start()
        return c
    lax.fori_loop(0, tm, issue, 0)

    @pl.when(i == nsteps - 1)
    def _():
        for back in range(nbuf - 1):
            @pl.when(i - back >= 0)
            def _():
                wait_scatter(lax.rem(i - back + nbuf, nbuf))


def _dispatch(slots, u3, n_rows, tm, ns):
    ln = u3.shape[2]
    n = u3.shape[0] // ns * tm
    xs0 = jnp.zeros((n_rows // EXPERT_BLOCK * ns, EXPERT_BLOCK, ln), u3.dtype)
    grid_spec = pltpu.PrefetchScalarGridSpec(
        num_scalar_prefetch=1,
        grid=(n // tm,),
        in_specs=[pl.BlockSpec(memory_space=pl.ANY), pl.BlockSpec(memory_space=pl.ANY)],
        out_specs=pl.BlockSpec(memory_space=pl.ANY),
        scratch_shapes=[pltpu.VMEM((_DISPATCH_BUFS, ns, tm, ln), u3.dtype),
                        pltpu.SemaphoreType.DMA((_DISPATCH_BUFS,)),
                        pltpu.SemaphoreType.DMA((_DISPATCH_BUFS,))],
    )
    return pl.pallas_call(
        functools.partial(_dispatch_kernel, tm=tm, ns=ns),
        grid_spec=grid_spec,
        out_shape=jax.ShapeDtypeStruct(xs0.shape, xs0.dtype),
        input_output_aliases={2: 0},
        compiler_params=pltpu.CompilerParams(dimension_semantics=("arbitrary",), has_side_effects=True),
        name="moe_dispatch",
    )(slots, u3, xs0)


def _moe_kernel(be_ref, nused_ref, xs_ref, w1_ref, b1_ref, w2_ref, b2_ref, ys_ref, w1b, w2b):
    i = pl.program_id(0)
    f = w2_ref.shape[1]
    ns = xs_ref.shape[0]
    e = be_ref[i]
    e_prev = be_ref[jnp.maximum(i - 1, 0)]

    @pl.when(jnp.logical_or(i == 0, e != e_prev))
    def _():
        w1b[...] = w1_ref[0].astype(BF16)
        w2b[...] = w2_ref[0].astype(BF16)

    @pl.when(i < nused_ref[0])
    def _():
        xb = jnp.concatenate([xs_ref[s] for s in range(ns)], axis=1).astype(BF16)
        hgl = jnp.dot(xb, w1b[...], preferred_element_type=F32) + b1_ref[0]
        x_glu = jnp.minimum(hgl[:, :f], SWIGLU_LIMIT)
        x_lin = jnp.clip(hgl[:, f:], -SWIGLU_LIMIT, SWIGLU_LIMIT)
        act = (x_lin + 1.0) * (x_glu * _sigmoid(SWIGLU_ALPHA * x_glu))
        y = jnp.dot(act.astype(BF16), w2b[...], preferred_element_type=F32) + b2_ref[0]
        for s in range(ns):
            ys_ref[s] = y[:, s * LANES:(s + 1) * LANES]

    @pl.when(i >= nused_ref[0])
    def _():
        ys_ref[...] = jnp.zeros_like(ys_ref)


def _moe_experts(block_e, n_used, xs, w1, b1, w2, b2):
    _, blk, ln = xs.shape
    ne, d, f2 = w1.shape
    f = w2.shape[1]
    nb = block_e.shape[0]
    ns = d // ln
    grid_spec = pltpu.PrefetchScalarGridSpec(
        num_scalar_prefetch=2,
        grid=(nb,),
        in_specs=[
            pl.BlockSpec((ns, blk, ln), lambda i, be, nu: (i, 0, 0)),
            pl.BlockSpec((1, d, f2), lambda i, be, nu: (be[i], 0, 0)),
            pl.BlockSpec((1, 1, f2), lambda i, be, nu: (be[i], 0, 0)),
            pl.BlockSpec((1, f, d), lambda i, be, nu: (be[i], 0, 0)),
            pl.BlockSpec((1, 1, d), lambda i, be, nu: (be[i], 0, 0)),
        ],
        out_specs=pl.BlockSpec((ns, blk, ln), lambda i, be, nu: (i, 0, 0)),
        scratch_shapes=[pltpu.VMEM((d, f2), BF16), pltpu.VMEM((f, d), BF16)],
    )
    return pl.pallas_call(
        _moe_kernel,
        grid_spec=grid_spec,
        out_shape=jax.ShapeDtypeStruct(xs.shape, F32),
        compiler_params=pltpu.CompilerParams(
            dimension_semantics=("arbitrary",), vmem_limit_bytes=VMEM_LIMIT),
        name="moe_experts",
    )(block_e, n_used, xs, w1, b1.reshape(ne, 1, f2), w2, b2.reshape(ne, 1, d))


def _final_kernel(row_ref, ys_hbm, h_ref, route_ref, p_ref, npl_ref, wg_ref, wp_ref, nfin_ref, out_ref,
                  gbuf, sem, *, last_layer):
    i = pl.program_id(0)
    tm = h_ref.shape[0]
    ns = gbuf.shape[1]

    def issue(r, c):
        for kk in range(TOP_K):
            slot = row_ref[(i * tm + r) * TOP_K + kk]
            src = ys_hbm.at[pl.ds(_slot_block(slot) * ns, ns), pl.ds(_slot_row(slot), 1), :]
            pltpu.make_async_copy(src, gbuf.at[kk, :, pl.ds(r, 1), :], sem).start()
        return c
    lax.fori_loop(0, tm, issue, 0)
    for kk in range(TOP_K):
        pltpu.make_async_copy(ys_hbm.at[pl.ds(0, ns), pl.ds(0, tm), :], gbuf.at[kk], sem).wait()

    route = route_ref[...]
    h = h_ref[...]
    for kk in range(TOP_K):
        yk = jnp.concatenate([gbuf[kk, s] for s in range(ns)], axis=1)
        h = h + route[:, kk:kk + 1] * yk
    u = _rmsnorm(h, npl_ref[...])
    gate = _sigmoid(_mm(u, wg_ref[...]))
    h = h + gate * _mm(p_ref[...], wp_ref[...])
    out_ref[...] = _rmsnorm(h, nfin_ref[...]) if last_layer else h


def _final(slots, ys, h1, route, pf, npl, wg, wp, nfin, tm, last_layer):
    n, d = h1.shape
    pd = pf.shape[1]
    grid_spec = pltpu.PrefetchScalarGridSpec(
        num_scalar_prefetch=1,
        grid=(n // tm,),
        in_specs=[
            pl.BlockSpec(memory_space=pl.ANY),
            pl.BlockSpec((tm, d), lambda i, s: (i, 0)),
            pl.BlockSpec((tm, LANES), lambda i, s: (i, 0)),
            pl.BlockSpec((tm, pd), lambda i, s: (i, 0)),
            pl.BlockSpec((1, d), lambda i, s: (0, 0)),
            pl.BlockSpec((d, d), lambda i, s: (0, 0)),
            pl.BlockSpec((pd, d), lambda i, s: (0, 0)),
            pl.BlockSpec((1, d), lambda i, s: (0, 0)),
        ],
        out_specs=pl.BlockSpec((tm, d), lambda i, s: (i, 0)),
        scratch_shapes=[pltpu.VMEM((TOP_K, d // ys.shape[2], tm, ys.shape[2]), F32),
                        pltpu.SemaphoreType.DMA(())],
    )
    return pl.pallas_call(
        functools.partial(_final_kernel, last_layer=last_layer),
        grid_spec=grid_spec,
        out_shape=jax.ShapeDtypeStruct((n, d), F32),
        compiler_params=pltpu.CompilerParams(
            dimension_semantics=("arbitrary",), vmem_limit_bytes=VMEM_LIMIT),
        name="final",
    )(slots, ys, h1, route, pf, npl, wg, wp, nfin)


def _routing(route, counts, n_experts):
    n = route.shape[0]
    blk = EXPERT_BLOCK
    nb = (n * TOP_K + n_experts * (blk - 1) + blk - 1) // blk
    counts = counts[0, :n_experts].astype(jnp.int32)
    pcounts = ((counts + blk - 1) // blk) * blk
    pend = jnp.cumsum(pcounts)
    pstart = pend - pcounts
    top_i = route[:, TOP_K:2 * TOP_K].astype(jnp.int32)
    rank = route[:, 2 * TOP_K:3 * TOP_K].astype(jnp.int32)
    onehot = top_i[:, :, None] == jnp.arange(n_experts, dtype=jnp.int32)[None, None, :]
    slots = jnp.sum(jnp.where(onehot, pstart[None, None, :], 0), axis=-1) + rank
    block_rows = jnp.arange(nb, dtype=jnp.int32) * blk
    block_e = jnp.sum(block_rows[:, None] >= pend[None, :], axis=1)
    block_e = jnp.clip(block_e, 0, n_experts - 1).astype(jnp.int32)
    n_used = (pend[-1] // blk).astype(jnp.int32).reshape(1)
    return block_e, n_used, slots.reshape(-1).astype(jnp.int32), nb * blk


def kernel(x, p, norm_mix, w_in, shift_mu, rw_w0, rw_w2, rw_a0, rw_a2, rw_g2, rw_kk, rw_ka, rw_rk,
           rw_gn_w, rw_gn_b, gla_gk2, gla_gk_b, gla_norm, w_out, norm_ffn, w_router, b_router,
           w1, b1, w2, b2, norm_ple, w_ple_gate, w_ple, norm_final):
    bsz, seq, d = x.shape
    n = bsz * seq
    depth = w_in.shape[0]
    n_experts = w_router.shape[-1]
    tm = 256
    h = x.reshape(n, d)
    for l in range(depth):
        w_rw = w_in[l][:, :RW_COLS].astype(BF16)
        w_gla = jnp.pad(w_in[l][:, RW_COLS:], ((0, 0), (0, LANES - GLA_GATE_RANK))).astype(BF16)
        rw_proj, gla_proj = _in_proj(h, norm_mix[l].reshape(1, d), w_rw, w_gla, tm)
        y_rw = _rwkv(rw_proj.reshape(bsz, seq, -1), shift_mu[l], rw_w0[l], rw_w2[l], rw_a0[l], rw_a2[l],
                     rw_g2[l], rw_kk[l], rw_ka[l], rw_rk[l], rw_gn_w[l], rw_gn_b[l])
        y_gla = _gla(gla_proj.reshape(bsz, seq, -1), gla_gk2[l], gla_gk_b[l], gla_norm[l])

        wr = jnp.pad(w_router[l], ((0, 0), (0, LANES - n_experts)))
        wr_hi = wr.astype(BF16)
        wr_lo = (wr - wr_hi.astype(F32)).astype(BF16)
        br = jnp.pad(b_router[l], (0, LANES - n_experts), constant_values=NEG_BIG).reshape(1, LANES)
        wo = w_out[l].astype(BF16)
        h1, u3, route, counts = _out_proj(y_rw.reshape(n, -1), y_gla.reshape(n, -1), h, wo[:RW_WIDTH],
                                          wo[RW_WIDTH:], norm_ffn[l].reshape(1, d), wr_hi, wr_lo, br, tm)

        ns = d // LANES
        block_e, n_used, slots, n_rows = _routing(route, counts, n_experts)
        xs = _dispatch(slots, u3, n_rows, tm, ns)
        ys = _moe_experts(block_e, n_used, xs, w1[l], b1[l], w2[l], b2[l])
        h = _final(slots, ys, h1, route, p[l].reshape(n, -1), norm_ple[l].reshape(1, d),
                   w_ple_gate[l].astype(BF16), w_ple[l].astype(BF16), norm_final.reshape(1, d), tm, l == depth - 1)
    return h.reshape(bsz, seq, d)
```

```python
import functools

import jax
import jax.numpy as jnp
from jax import lax
from jax.experimental import pallas as pl
from jax.experimental.pallas import tpu as pltpu

F32 = jnp.float32
BF16 = jnp.bfloat16

CHUNK = 64
RW_HEADS = 8
RW_HEAD_DIM = 64
RW_WIDTH = RW_HEADS * RW_HEAD_DIM
RW_DECAY_LORA = 64
RW_ICLR_LORA = 64
RW_GATE_LORA = 128
RW_COLS = 3 * RW_WIDTH + RW_DECAY_LORA + RW_ICLR_LORA + RW_GATE_LORA
RW_GN_EPS = 64e-5
GLA_HEADS = 4
GLA_KEY_DIM = 64
GLA_VAL_DIM = 128
GLA_QK_WIDTH = GLA_HEADS * GLA_KEY_DIM
GLA_V_WIDTH = GLA_HEADS * GLA_VAL_DIM
GLA_GATE_RANK = 16
GLA_GATE_TEMP = 16.0
GLA_SUB = 8
LANES = 128
GLA_COLS_PAD = 2 * GLA_QK_WIDTH + 2 * GLA_V_WIDTH + LANES
TOP_K = 4
EXPERT_BLOCK = 256
SWIGLU_ALPHA = 1.702
SWIGLU_LIMIT = 7.0
NORM_EPS = 1e-6
NEG_BIG = -1e30
VMEM_LIMIT = 56 * 1024 * 1024


def _mm(a, b):
    return jnp.dot(a.astype(BF16), b.astype(BF16), preferred_element_type=F32)


def _mm_nt(a, b):
    return lax.dot_general(a.astype(BF16), b.astype(BF16), (((1,), (1,)), ((), ())),
                           preferred_element_type=F32)


def _mm_tn(a, b):
    return lax.dot_general(a.astype(BF16), b.astype(BF16), (((0,), (0,)), ((), ())),
                           preferred_element_type=F32)


def _split(a, n):
    parts = []
    rem = a
    for _ in range(n):
        p = rem.astype(BF16)
        parts.append(p)
        rem = rem - p.astype(F32)
    return parts


def _mm_lhs_split(a, b_bf16, n):
    out = None
    for p in _split(a, n):
        t = jnp.dot(p, b_bf16, preferred_element_type=F32)
        out = t if out is None else out + t
    return out


def _mm_rhs_split(a_bf16, b, n):
    out = None
    for p in _split(b, n):
        t = jnp.dot(a_bf16, p, preferred_element_type=F32)
        out = t if out is None else out + t
    return out


def _rmsnorm(x, g):
    return x * lax.rsqrt(jnp.mean(x * x, axis=-1, keepdims=True) + NORM_EPS) * g


def _softplus(x):
    return jnp.maximum(x, 0.0) + jnp.log(1.0 + jnp.exp(-jnp.abs(x)))


def _sigmoid(x):
    return 1.0 / (1.0 + jnp.exp(-x))


def _tri(n, strict):
    r = lax.broadcasted_iota(jnp.int32, (n, n), 0)
    c = lax.broadcasted_iota(jnp.int32, (n, n), 1)
    return (r > c) if strict else (r >= c)


def _inproj_kernel(x_ref, g_ref, wr_ref, wg_ref, rw_ref, gla_ref):
    u = _rmsnorm(x_ref[...], g_ref[...]).astype(BF16)
    rw_ref[...] = jnp.dot(u, wr_ref[...], preferred_element_type=F32)
    gla_ref[...] = jnp.dot(u, wg_ref[...], preferred_element_type=F32)


def _in_proj(xf, g, w_rw, w_gla, tm):
    n, d = xf.shape
    return pl.pallas_call(
        _inproj_kernel,
        grid=(n // tm,),
        in_specs=[
            pl.BlockSpec((tm, d), lambda i: (i, 0)),
            pl.BlockSpec((1, d), lambda i: (0, 0)),
            pl.BlockSpec(w_rw.shape, lambda i: (0, 0)),
            pl.BlockSpec(w_gla.shape, lambda i: (0, 0)),
        ],
        out_specs=[
            pl.BlockSpec((tm, w_rw.shape[1]), lambda i: (i, 0)),
            pl.BlockSpec((tm, w_gla.shape[1]), lambda i: (i, 0)),
        ],
        out_shape=[
            jax.ShapeDtypeStruct((n, w_rw.shape[1]), F32),
            jax.ShapeDtypeStruct((n, w_gla.shape[1]), F32),
        ],
        compiler_params=pltpu.CompilerParams(
            dimension_semantics=("arbitrary",), vmem_limit_bytes=VMEM_LIMIT),
        name="in_proj",
    )(xf, g, w_rw, w_gla)


def _rwkv_kernel_unpacked(x_ref, mu_ref, w0_ref, w2_ref, a0_ref, a2_ref, g2_ref, kkw_ref, ka_ref,
                          rk_ref, gnw_ref, gnb_ref, bd_ref, y_ref, state_ref, carry_ref, o_ref):
    C, H, D = CHUNK, RW_HEADS, RW_HEAD_DIM

    @pl.when(pl.program_id(1) == 0)
    def _():
        state_ref[...] = jnp.zeros_like(state_ref)
        carry_ref[...] = jnp.zeros_like(carry_ref)

    x = x_ref[0]
    row = lax.broadcasted_iota(jnp.int32, x.shape, 0)
    prev = jnp.where(row == 0, carry_ref[...], pltpu.roll(x, 1, axis=0))
    carry_ref[...] = x[C - 1:C, :]
    h = x + (prev - x) * mu_ref[...]

    W = RW_WIDTH
    r = h[:, 0:W]
    k = h[:, W:2 * W]
    v = h[:, 2 * W:3 * W]
    o0 = 3 * W
    dw = h[:, o0:o0 + RW_DECAY_LORA]
    da = h[:, o0 + RW_DECAY_LORA:o0 + RW_DECAY_LORA + RW_ICLR_LORA]
    dg = h[:, o0 + RW_DECAY_LORA + RW_ICLR_LORA:]

    bd = bd_ref[...]

    def seg_sum(t):
        return _mm_lhs_split(t, bd, 2)

    w_log = -_softplus(-(w0_ref[...] + _mm(jnp.tanh(dw), w2_ref[...]))) - 0.5
    lw = -jnp.exp(w_log)
    iclr = _sigmoid(a0_ref[...] + _mm(da, a2_ref[...]))
    gate = _mm(_sigmoid(dg), g2_ref[...])

    kk = k * kkw_ref[...]
    kk = kk / jnp.maximum(jnp.sqrt(seg_sum(kk * kk)), 1e-12)
    k2 = k * (1.0 + (iclr - 1.0) * ka_ref[...])

    cw = _mm_rhs_split(_tri(C, False).astype(BF16), lw, 3)
    cw_last = cw[C - 1:C, :]
    e_cw = jnp.exp(cw)
    e_ncw = jnp.exp(-cw)
    e_rem = jnp.exp(cw_last - cw)
    kka = kk * iclr
    a_t = -kk * jnp.exp(cw - lw)
    r_t = r * e_cw
    b_t = kka * e_ncw
    k_t = k2 * e_ncw
    b_h = kka * e_rem
    k_h = k2 * e_rem
    w_c = jnp.exp(cw_last)

    strict = _tri(C, True)
    incl = _tri(C, False)
    eye = jnp.where(_tri(C, False) & ~strict, 1.0, 0.0).astype(F32)

    hs = range(H)
    sls = [slice(hd * D, (hd + 1) * D) for hd in hs]
    states = [state_ref[hd] for hd in hs]
    ah = [a_t[:, sl] for sl in sls]
    rh = [r_t[:, sl] for sl in sls]
    bh = [b_t[:, sl] for sl in sls]
    kh = [k_t[:, sl] for sl in sls]
    vh = [v[:, sl] for sl in sls]
    a_ab = [jnp.where(strict, _mm_nt(ah[i], bh[i]), 0.0) for i in hs]
    a_ak = [jnp.where(strict, _mm_nt(ah[i], kh[i]), 0.0) for i in hs]
    a_rb = [jnp.where(incl, _mm_nt(rh[i], bh[i]), 0.0) for i in hs]
    a_rk = [jnp.where(incl, _mm_nt(rh[i], kh[i]), 0.0) for i in hs]
    p = [_mm(a_ab[i], a_ab[i]) for i in hs]
    q = [eye + a_ab[i] for i in hs]
    for _ in range(4):
        pq = [_mm(p[i], q[i]) for i in hs]
        p = [_mm(p[i], p[i]) for i in hs]
        q = [q[i] + pq[i] for i in hs]
    pq = [_mm(p[i], q[i]) for i in hs]
    t_inv = [q[i] + pq[i] for i in hs]
    akv = [_mm(a_ak[i], vh[i]) for i in hs]
    u0 = [_mm(t_inv[i], akv[i]) for i in hs]
    a_hat = [_mm(t_inv[i], ah[i]) for i in hs]
    u = [_mm_nt(a_hat[i], states[i]) + u0[i] for i in hs]
    o_heads = [_mm_nt(rh[i], states[i]) + _mm(a_rb[i], u[i]) + _mm(a_rk[i], vh[i]) for i in hs]
    new_states = [states[i] * w_c[:, sls[i]] + _mm_tn(u[i], b_h[:, sls[i]]) + _mm_tn(vh[i], k_h[:, sls[i]])
                  for i in hs]
    for hd in hs:
        state_ref[hd] = new_states[hd]
        o_ref[:, sls[hd]] = o_heads[hd]

    o = o_ref[...]
    inv_d = 1.0 / D
    mean = seg_sum(o) * inv_d
    dlt = o - mean
    var = seg_sum(dlt * dlt) * inv_d
    o = dlt * lax.rsqrt(var + RW_GN_EPS) * gnw_ref[...] + gnb_ref[...]
    bonus = seg_sum(r * k2 * rk_ref[...]) * v
    y_ref[0] = ((o + bonus) * gate).astype(y_ref.dtype)


def _rwkv_unpacked(rw_proj, mu, w0, w2, a0, a2, g2, kkw, ka, rk, gnw, gnb):
    b, s, cols = rw_proj.shape
    W = RW_WIDTH
    bd = jnp.kron(jnp.eye(RW_HEADS, dtype=F32), jnp.ones((RW_HEAD_DIM, RW_HEAD_DIM), F32)).astype(BF16)
    row = lambda a: a.reshape(1, -1)
    full = lambda a: pl.BlockSpec(a.shape, lambda i, j: (0,) * a.ndim)
    args = [row(mu), row(w0), w2.astype(BF16), row(a0), a2.astype(BF16), g2.astype(BF16),
            row(kkw), row(ka), row(rk), row(gnw), row(gnb), bd]
    return pl.pallas_call(
        _rwkv_kernel_unpacked,
        grid=(b, s // CHUNK),
        in_specs=[pl.BlockSpec((1, CHUNK, cols), lambda i, j: (i, j, 0))] + [full(a) for a in args],
        out_specs=pl.BlockSpec((1, CHUNK, W), lambda i, j: (i, j, 0)),
        out_shape=jax.ShapeDtypeStruct((b, s, W), BF16),
        scratch_shapes=[
            pltpu.VMEM((RW_HEADS, RW_HEAD_DIM, RW_HEAD_DIM), F32),
            pltpu.VMEM((1, cols), F32),
            pltpu.VMEM((CHUNK, W), F32),
        ],
        compiler_params=pltpu.CompilerParams(
            dimension_semantics=("arbitrary", "arbitrary"), vmem_limit_bytes=VMEM_LIMIT),
        name="rwkv7",
    )(rw_proj, *args)


RW_GROUP = 4
RW_GW = RW_GROUP * RW_HEAD_DIM
RW_STEP_CHUNKS = 2


def _rwkv_kernel(x_ref, mu_ref, w0_ref, w2_ref, a0_ref, a2_ref, g2_ref, kkw_ref, ka_ref,
                 rk_ref, gnw_ref, gnb_ref, bd_ref, y_ref, state_ref, carry_ref):
    C, D, W, GW = CHUNK, RW_HEAD_DIM, RW_WIDTH, RW_GW
    nb, ct = x_ref.shape[0], x_ref.shape[1]
    nch = ct // C
    R = nb * ct

    @pl.when(pl.program_id(0) == 0)
    def _():
        state_ref[...] = jnp.zeros_like(state_ref)
        carry_ref[...] = jnp.zeros_like(carry_ref)

    x = x_ref[...].reshape(R, x_ref.shape[2])
    row = lax.broadcasted_iota(jnp.int32, x.shape, 0)
    prev = pltpu.roll(x, 1, axis=0)
    for b in range(nb):
        prev = jnp.where(row == b * ct, carry_ref[b:b + 1, :], prev)
        carry_ref[b:b + 1, :] = x[(b + 1) * ct - 1:(b + 1) * ct, :]
    h = x + (prev - x) * mu_ref[...]

    r = h[:, 0:W]
    k = h[:, W:2 * W]
    v = h[:, 2 * W:3 * W]
    o0 = 3 * W
    dw = h[:, o0:o0 + RW_DECAY_LORA]
    da = h[:, o0 + RW_DECAY_LORA:o0 + RW_DECAY_LORA + RW_ICLR_LORA]
    dg = h[:, o0 + RW_DECAY_LORA + RW_ICLR_LORA:]

    bd_g = bd_ref[...]

    def seg_sum(t):
        return jnp.concatenate(
            [_mm_lhs_split(t[:, g * GW:(g + 1) * GW], bd_g, 2) for g in range(W // GW)], axis=1)

    w_log = -_softplus(-(w0_ref[...] + _mm(jnp.tanh(dw), w2_ref[...]))) - 0.5
    lw = -jnp.exp(w_log)
    iclr = _sigmoid(a0_ref[...] + _mm(da, a2_ref[...]))
    gate = _mm(_sigmoid(dg), g2_ref[...])

    kk = k * kkw_ref[...]
    kk = kk / jnp.maximum(jnp.sqrt(seg_sum(kk * kk)), 1e-12)
    k2 = k * (1.0 + (iclr - 1.0) * ka_ref[...])

    rr = lax.broadcasted_iota(jnp.int32, (R, R), 0)
    cc = lax.broadcasted_iota(jnp.int32, (R, R), 1)
    tri_seq = jnp.where((rr >= cc) & (rr // C == cc // C), 1.0, 0.0).astype(BF16)
    cw = _mm_rhs_split(tri_seq, lw, 3)
    cw_last = jnp.concatenate(
        [jnp.broadcast_to(cw[(j + 1) * C - 1:(j + 1) * C, :], (C, W)) for j in range(R // C)], axis=0)
    e_cw = jnp.exp(cw)
    e_ncw = jnp.exp(-cw)
    e_rem = jnp.exp(cw_last - cw)
    kka = kk * iclr
    a_t = -kk * jnp.exp(cw - lw)
    r_t = r * e_cw
    b_t = kka * e_ncw
    k_t = k2 * e_ncw
    b_h = kka * e_rem
    k_h = k2 * e_rem
    w_c = jnp.exp(cw_last)

    ti = lax.broadcasted_iota(jnp.int32, (C, GW), 0)
    si = jnp.bitwise_and(lax.broadcasted_iota(jnp.int32, (C, GW), 1), D - 1)
    strict = ti > si
    incl = ti >= si
    eye = jnp.where(ti == si, 1.0, 0.0)

    def bdiag(y):
        yb = y.astype(BF16)
        return jnp.concatenate([yb] * RW_GROUP, axis=0) * bd_g

    ng = W // GW
    chains = [(b, g, ch) for ch in range(nch) for b in range(nb) for g in range(ng)]

    def part(t, c):
        b, g, ch = c
        r0 = b * ct + ch * C
        return t[r0:r0 + C, g * GW:(g + 1) * GW]

    n = range(len(chains))
    a4 = [part(a_t, c) for c in chains]
    r4 = [part(r_t, c) for c in chains]
    v4 = [part(v, c) for c in chains]
    ar = [jnp.concatenate([a4[i], r4[i]], axis=0) for i in n]
    bd_b = [bdiag(part(b_t, c)) for c in chains]
    bd_k = [bdiag(part(k_t, c)) for c in chains]
    bd_v = [bdiag(v4[i]) for i in n]
    m_b = [_mm_nt(ar[i], bd_b[i]) for i in n]
    m_k = [_mm_nt(ar[i], bd_k[i]) for i in n]
    a_ab = [jnp.where(strict, m_b[i][0:C], 0.0) for i in n]
    a_rb = [jnp.where(incl, m_b[i][C:2 * C], 0.0) for i in n]
    a_ak = [jnp.where(strict, m_k[i][0:C], 0.0) for i in n]
    a_rk = [jnp.where(incl, m_k[i][C:2 * C], 0.0) for i in n]
    akv = [_mm(a_ak[i], bd_v[i]) for i in n]
    o_kv = [_mm(a_rk[i], bd_v[i]) for i in n]
    p = [_mm(a_ab[i], bdiag(a_ab[i])) for i in n]
    q = [eye + a_ab[i] for i in n]
    for _ in range(4):
        pq = [_mm(p[i], bdiag(q[i])) for i in n]
        p = [_mm(p[i], bdiag(p[i])) for i in n]
        q = [q[i] + pq[i] for i in n]
    pq = [_mm(p[i], bdiag(q[i])) for i in n]
    t_inv = [q[i] + pq[i] for i in n]
    u0 = [_mm(t_inv[i], bdiag(akv[i])) for i in n]
    a_hat = [_mm(t_inv[i], bdiag(a4[i])) for i in n]
    seqs = [(b, g) for b in range(nb) for g in range(ng)]
    s4 = [state_ref[b, g] for b, g in seqs]
    o_g = {}
    for ch in range(nch):
        idx = [chains.index((b, g, ch)) for b, g in seqs]
        bd_s = [bdiag(s) for s in s4]
        o_s = [_mm_nt(r4[i], bd_s[j]) for j, i in enumerate(idx)]
        u = [_mm_nt(a_hat[i], bd_s[j]) + u0[i] for j, i in enumerate(idx)]
        for j, i in enumerate(idx):
            o_g[chains[i]] = o_s[j] + _mm(a_rb[i], bdiag(u[j])) + o_kv[i]
        nxt = []
        for j, i in enumerate(idx):
            c = chains[i]
            uv = jnp.concatenate([u[j], v4[i]], axis=0)
            bk = jnp.concatenate([part(b_h, c), part(k_h, c)], axis=0)
            full = _mm_tn(uv, bk) * bd_g
            upd = full[0:D]
            for hd in range(1, RW_GROUP):
                upd = upd + full[hd * D:(hd + 1) * D]
            nxt.append(s4[j] * part(w_c, c)[0:D] + upd)
        s4 = nxt
    for j, (b, g) in enumerate(seqs):
        state_ref[b, g] = s4[j]

    o = jnp.concatenate(
        [jnp.concatenate([o_g[(b, g, ch)] for g in range(ng)], axis=1) for b in range(nb) for ch in range(nch)],
        axis=0)
    inv_d = 1.0 / D
    mean = seg_sum(o) * inv_d
    dlt = o - mean
    var = seg_sum(dlt * dlt) * inv_d
    o = dlt * lax.rsqrt(var + RW_GN_EPS) * gnw_ref[...] + gnb_ref[...]
    bonus = seg_sum(r * k2 * rk_ref[...]) * v
    y_ref[...] = ((o + bonus) * gate).astype(y_ref.dtype).reshape(y_ref.shape)


def _rwkv(rw_proj, mu, w0, w2, a0, a2, g2, kkw, ka, rk, gnw, gnb):
    b, s, cols = rw_proj.shape
    W = RW_WIDTH
    bd = jnp.kron(jnp.eye(RW_GROUP, dtype=F32), jnp.ones((RW_HEAD_DIM, RW_HEAD_DIM), F32)).astype(BF16)
    row = lambda a: a.reshape(1, -1)
    full = lambda a: pl.BlockSpec(a.shape, lambda j: (0,) * a.ndim)
    args = [row(mu), row(w0), w2.astype(BF16), row(a0), a2.astype(BF16), g2.astype(BF16),
            row(kkw), row(ka), row(rk), row(gnw), row(gnb), bd]
    ct = CHUNK * RW_STEP_CHUNKS
    return pl.pallas_call(
        _rwkv_kernel,
        grid=(s // ct,),
        in_specs=[pl.BlockSpec((b, ct, cols), lambda j: (0, j, 0))] + [full(a) for a in args],
        out_specs=pl.BlockSpec((b, ct, W), lambda j: (0, j, 0)),
        out_shape=jax.ShapeDtypeStruct((b, s, W), BF16),
        scratch_shapes=[
            pltpu.VMEM((b, W // RW_GW, RW_HEAD_DIM, RW_GW), F32),
            pltpu.VMEM((b, cols), F32),
        ],
        compiler_params=pltpu.CompilerParams(
            dimension_semantics=("arbitrary",), vmem_limit_bytes=VMEM_LIMIT),
        name="rwkv7",
    )(rw_proj, *args)


def _gla_kernel(x_ref, gk2_ref, gkb_ref, ng_ref, y_ref, state_ref):
    C, H, DK, DV, SB = CHUNK, GLA_HEADS, GLA_KEY_DIM, GLA_VAL_DIM, GLA_SUB

    @pl.when(pl.program_id(1) == 0)
    def _():
        state_ref[...] = jnp.zeros_like(state_ref)

    x = x_ref[0]
    QW, VW = GLA_QK_WIDTH, GLA_V_WIDTH
    q = x[:, 0:QW] * (DK ** -0.5)
    k = x[:, QW:2 * QW]
    v = x[:, 2 * QW:2 * QW + VW]
    g = x[:, 2 * QW + VW:2 * QW + 2 * VW]
    dgk = x[:, 2 * QW + 2 * VW:]

    la = -_softplus(-(_mm(dgk, gk2_ref[...]) + gkb_ref[...])) * (1.0 / GLA_GATE_TEMP)
    b = _mm_rhs_split(_tri(C, False).astype(BF16), la, 3)
    b_last = b[C - 1:C, :]
    q_e = q * jnp.exp(b)
    k_e = k * jnp.exp(b_last - b)
    w_c = jnp.exp(b_last)

    ii = lax.broadcasted_iota(jnp.int32, (C // SB, SB, SB, 2 * DK), 1)
    jj = lax.broadcasted_iota(jnp.int32, (C // SB, SB, SB, 2 * DK), 2)
    causal4 = jj <= ii

    hs = range(H)
    nblk = C // SB
    sks = [slice(hd * DK, (hd + 1) * DK) for hd in hs]
    svs = [slice(hd * DV, (hd + 1) * DV) for hd in hs]
    states = [state_ref[hd] for hd in hs]
    qh = [q[:, sk] for sk in sks]
    kh = [k[:, sk] for sk in sks]
    bh = [b[:, sk] for sk in sks]
    vh = [v[:, sv] for sv in svs]
    o_inter = [_mm_nt(q_e[:, sks[i]], states[i]) for i in hs]
    new_states = [states[i] * w_c[:, sks[i]] + _mm_tn(vh[i], k_e[:, sks[i]]) for i in hs]
    att_off = {}
    for blk in range(1, nblk):
        r0 = blk * SB
        for i in hs:
            ref = bh[i][r0 - 1:r0]
            q_s = qh[i][r0:r0 + SB] * jnp.exp(bh[i][r0:r0 + SB] - ref)
            k_s = kh[i][0:r0] * jnp.exp(ref - bh[i][0:r0])
            att_off[(i, blk)] = _mm_nt(q_s, k_s)
    o_off = {key: _mm(att, vh[key[0]][0:key[1] * SB]) for key, att in att_off.items()}
    rsel = lax.broadcasted_iota(jnp.int32, (C, C * SB), 0)
    csel = lax.broadcasted_iota(jnp.int32, (C, C * SB), 1)
    sel = jnp.where((csel >= rsel * SB) & (csel < (rsel + 1) * SB), 1.0, 0.0).astype(BF16)
    drow = lax.broadcasted_iota(jnp.int32, (2 * DK, 2 * DV), 0)
    dcol = lax.broadcasted_iota(jnp.int32, (2 * DK, 2 * DV), 1)
    pair_ones = jnp.where((drow >= DK) == (dcol >= DV), 1.0, 0.0).astype(BF16)
    o_diag = []
    for pr in range(H // 2):
        sl2 = slice(2 * pr * DK, 2 * (pr + 1) * DK)
        q2, k2, b2 = q[:, sl2], k[:, sl2], b[:, sl2]
        dec = jnp.exp(jnp.where(causal4, b2.reshape(nblk, SB, 1, 2 * DK) - b2.reshape(nblk, 1, SB, 2 * DK),
                                NEG_BIG))
        pw = q2.reshape(nblk, SB, 1, 2 * DK) * k2.reshape(nblk, 1, SB, 2 * DK) * dec
        att2 = _mm(pw.reshape(C * SB, 2 * DK), pair_ones)
        for t in range(2):
            v_rep = jnp.broadcast_to(vh[2 * pr + t].reshape(nblk, 1, SB, DV), (nblk, SB, SB, DV))
            z = att2[:, t * DV:(t + 1) * DV] * v_rep.reshape(C * SB, DV)
            o_diag.append(_mm(sel, z))
    outs = []
    for i in hs:
        rows = [jnp.zeros((SB, DV), F32)] + [o_off[(i, blk)] for blk in range(1, nblk)]
        o = o_inter[i] + o_diag[i] + jnp.concatenate(rows, axis=0)
        outs.append(o * lax.rsqrt(jnp.mean(o * o, axis=-1, keepdims=True) + NORM_EPS))
    for hd in hs:
        state_ref[hd] = new_states[hd]
    o = jnp.concatenate(outs, axis=1)
    y = o * ng_ref[...] * (g * _sigmoid(g))
    y_ref[0] = y.astype(y_ref.dtype)


def _gla(gla_proj, gk2, gkb, ng):
    b, s, cols = gla_proj.shape
    gk2p = jnp.zeros((LANES, GLA_QK_WIDTH), F32).at[:GLA_GATE_RANK].set(gk2).astype(BF16)
    args = [gk2p, gkb.reshape(1, -1), ng.reshape(1, -1)]
    full = lambda a: pl.BlockSpec(a.shape, lambda i, j: (0,) * a.ndim)
    return pl.pallas_call(
        _gla_kernel,
        grid=(b, s // CHUNK),
        in_specs=[pl.BlockSpec((1, CHUNK, cols), lambda i, j: (i, j, 0))] + [full(a) for a in args],
        out_specs=pl.BlockSpec((1, CHUNK, GLA_V_WIDTH), lambda i, j: (i, j, 0)),
        out_shape=jax.ShapeDtypeStruct((b, s, GLA_V_WIDTH), BF16),
        scratch_shapes=[pltpu.VMEM((GLA_HEADS, GLA_VAL_DIM, GLA_KEY_DIM), F32)],
        compiler_params=pltpu.CompilerParams(
            dimension_semantics=("arbitrary", "arbitrary"), vmem_limit_bytes=VMEM_LIMIT),
        name="gla",
    )(gla_proj, *args)


def _outproj_kernel(yr_ref, yg_ref, x_ref, wor_ref, wog_ref, nf_ref, wrh_ref, wrl_ref, br_ref,
                    h_ref, u3_ref, route_ref, cnt_ref, carry_ref):
    tm, d = x_ref.shape

    @pl.when(pl.program_id(0) == 0)
    def _():
        carry_ref[...] = jnp.zeros_like(carry_ref)

    h = (x_ref[...] + jnp.dot(yr_ref[...], wor_ref[...], preferred_element_type=F32)
         + jnp.dot(yg_ref[...], wog_ref[...], preferred_element_type=F32))
    h_ref[...] = h
    u = _rmsnorm(h, nf_ref[...])
    for s in range(d // LANES):
        u3_ref[s] = u[:, s * LANES:(s + 1) * LANES]
    u_hi, u_lo = _split(u, 2)
    logits = (jnp.dot(u_hi, wrh_ref[...], preferred_element_type=F32)
              + jnp.dot(u_hi, wrl_ref[...], preferred_element_type=F32)
              + jnp.dot(u_lo, wrh_ref[...], preferred_element_type=F32)) + br_ref[...]
    lane = lax.broadcasted_iota(jnp.int32, logits.shape, 1)
    rest = logits
    picks, idxs, vals = [], [], []
    for r in range(TOP_K):
        m = jnp.max(rest, axis=-1, keepdims=True)
        idx = jnp.min(jnp.where(rest == m, lane, LANES), axis=-1, keepdims=True)
        pick = lane == idx
        picks.append(pick)
        idxs.append(idx)
        vals.append(m)
        rest = jnp.where(pick, -jnp.inf, rest)
    denom = jnp.ones_like(vals[0])
    for r in range(1, TOP_K):
        denom = denom + jnp.exp(vals[r] - vals[0])
    sel = jnp.zeros(logits.shape, F32)
    for pick in picks:
        sel = sel + jnp.where(pick, 1.0, 0.0)
    prefix = _mm(jnp.where(_tri(tm, True), 1.0, 0.0), sel) + carry_ref[...]
    carry_ref[...] = carry_ref[...] + jnp.sum(sel, axis=0, keepdims=True)
    cnt_ref[...] = carry_ref[...]
    route = jnp.zeros(logits.shape, F32)
    for r in range(TOP_K):
        gate = jnp.exp(vals[r] - vals[0]) / denom
        rank = jnp.sum(jnp.where(picks[r], prefix, 0.0), axis=-1, keepdims=True)
        route = jnp.where(lane == r, gate, route)
        route = jnp.where(lane == TOP_K + r, idxs[r].astype(F32), route)
        route = jnp.where(lane == 2 * TOP_K + r, rank, route)
    route_ref[...] = route


def _out_proj(y_rw, y_gla, xf, wo_r, wo_g, nf, wr_hi, wr_lo, br, tm):
    n, d = xf.shape
    full = lambda a: pl.BlockSpec(a.shape, lambda i: (0,) * a.ndim)
    tile = lambda w: pl.BlockSpec((tm, w), lambda i: (i, 0))
    return pl.pallas_call(
        _outproj_kernel,
        grid=(n // tm,),
        in_specs=[tile(y_rw.shape[1]), tile(y_gla.shape[1]), tile(d), full(wo_r), full(wo_g), full(nf),
                  full(wr_hi), full(wr_lo), full(br)],
        out_specs=[tile(d), pl.BlockSpec((d // LANES, tm, LANES), lambda i: (i, 0, 0)), tile(LANES),
                   pl.BlockSpec((1, LANES), lambda i: (0, 0))],
        out_shape=[jax.ShapeDtypeStruct((n, d), F32), jax.ShapeDtypeStruct((n // tm * (d // LANES), tm, LANES), F32),
                   jax.ShapeDtypeStruct((n, LANES), F32), jax.ShapeDtypeStruct((1, LANES), F32)],
        scratch_shapes=[pltpu.VMEM((1, LANES), F32)],
        compiler_params=pltpu.CompilerParams(
            dimension_semantics=("arbitrary",), vmem_limit_bytes=VMEM_LIMIT),
        name="out_proj",
    )(y_rw, y_gla, xf, wo_r, wo_g, nf, wr_hi, wr_lo, br)


assert EXPERT_BLOCK & (EXPERT_BLOCK - 1) == 0
_BLOCK_SHIFT = EXPERT_BLOCK.bit_length() - 1


def _slot_block(slot):
    return lax.shift_right_logical(slot, _BLOCK_SHIFT)


def _slot_row(slot):
    return jnp.bitwise_and(slot, EXPERT_BLOCK - 1)


_DISPATCH_BUFS = 3


def _dispatch_kernel(row_ref, u3_hbm, xs_in_hbm, xs_hbm, ubuf, in_sems, out_sems, *, tm, ns):
    del xs_in_hbm
    i = pl.program_id(0)
    nsteps = pl.num_programs(0)
    nbuf = _DISPATCH_BUFS

    def load(step, slot_):
        return pltpu.make_async_copy(u3_hbm.at[pl.ds(step * ns, ns)], ubuf.at[slot_], in_sems.at[slot_])

    def wait_scatter(slot_):
        for _ in range(TOP_K):
            pltpu.make_async_copy(ubuf.at[slot_], xs_hbm.at[pl.ds(0, ns), pl.ds(0, tm), :],
                                  out_sems.at[slot_]).wait()

    cur = lax.rem(i, nbuf)
    nxt = lax.rem(i + 1, nbuf)

    @pl.when(i == 0)
    def _():
        load(0, 0).start()

    @pl.when(i >= nbuf - 1)
    def _():
        wait_scatter(nxt)

    @pl.when(i + 1 < nsteps)
    def _():
        load(i + 1, nxt).start()

    load(i, cur).wait()

    def issue(r, c):
        src = ubuf.at[cur, :, pl.ds(r, 1), :]
        for kk in range(TOP_K):
            slot = row_ref[(i * tm + r) * TOP_K + kk]
            dst = xs_hbm.at[pl.ds(_slot_block(slot) * ns, ns), pl.ds(_slot_row(slot), 1), :]
            pltpu.make_async_copy(src, dst, out_sems.at[cur]).start()
        return c
    lax.fori_loop(0, tm, issue, 0)

    @pl.when(i == nsteps - 1)
    def _():
        for back in range(nbuf - 1):
            @pl.when(i - back >= 0)
            def _():
                wait_scatter(lax.rem(i - back + nbuf, nbuf))


def _dispatch(slots, u3, n_rows, tm, ns):
    ln = u3.shape[2]
    n = u3.shape[0] // ns * tm
    xs0 = jnp.zeros((n_rows // EXPERT_BLOCK * ns, EXPERT_BLOCK, ln), u3.dtype)
    grid_spec = pltpu.PrefetchScalarGridSpec(
        num_scalar_prefetch=1,
        grid=(n // tm,),
        in_specs=[pl.BlockSpec(memory_space=pl.ANY), pl.BlockSpec(memory_space=pl.ANY)],
        out_specs=pl.BlockSpec(memory_space=pl.ANY),
        scratch_shapes=[pltpu.VMEM((_DISPATCH_BUFS, ns, tm, ln), u3.dtype),
                        pltpu.SemaphoreType.DMA((_DISPATCH_BUFS,)),
                        pltpu.SemaphoreType.DMA((_DISPATCH_BUFS,))],
    )
    return pl.pallas_call(
        functools.partial(_dispatch_kernel, tm=tm, ns=ns),
        grid_spec=grid_spec,
        out_shape=jax.ShapeDtypeStruct(xs0.shape, xs0.dtype),
        input_output_aliases={2: 0},
        compiler_params=pltpu.CompilerParams(dimension_semantics=("arbitrary",), has_side_effects=True),
        name="moe_dispatch",
    )(slots, u3, xs0)


def _moe_kernel(be_ref, nxt_ref, nused_ref, xs_ref, w1_hbm, b1_ref, w2_hbm, b2_ref, ys_ref,
                w1f, w2f, w1b, w2b, sems):
    i = pl.program_id(0)
    f = w2b.shape[0]
    ns = xs_ref.shape[0]
    e = be_ref[i]
    e_prev = be_ref[jnp.maximum(i - 1, 0)]

    def fetch(expert):
        return (pltpu.make_async_copy(w1_hbm.at[expert], w1f, sems.at[0]),
                pltpu.make_async_copy(w2_hbm.at[expert], w2f, sems.at[1]))

    @pl.when(i == 0)
    def _():
        for cp in fetch(e):
            cp.start()

    @pl.when(jnp.logical_or(i == 0, e != e_prev))
    def _():
        for cp in fetch(e):
            cp.wait()
        w1b[...] = w1f[...].astype(BF16)
        w2b[...] = w2f[...].astype(BF16)

        @pl.when(nxt_ref[i] >= 0)
        def _():
            for cp in fetch(nxt_ref[i]):
                cp.start()

    @pl.when(i < nused_ref[0])
    def _():
        xb = jnp.concatenate([xs_ref[s] for s in range(ns)], axis=1).astype(BF16)
        hgl = jnp.dot(xb, w1b[...], preferred_element_type=F32) + b1_ref[0]
        x_glu = jnp.minimum(hgl[:, :f], SWIGLU_LIMIT)
        x_lin = jnp.clip(hgl[:, f:], -SWIGLU_LIMIT, SWIGLU_LIMIT)
        act = (x_lin + 1.0) * (x_glu * _sigmoid(SWIGLU_ALPHA * x_glu))
        y = jnp.dot(act.astype(BF16), w2b[...], preferred_element_type=F32) + b2_ref[0]
        for s in range(ns):
            ys_ref[s] = y[:, s * LANES:(s + 1) * LANES]

    @pl.when(i >= nused_ref[0])
    def _():
        ys_ref[...] = jnp.zeros_like(ys_ref)


def _moe_experts(block_e, next_e, n_used, xs, w1, b1, w2, b2):
    _, blk, ln = xs.shape
    ne, d, f2 = w1.shape
    f = w2.shape[1]
    nb = block_e.shape[0]
    ns = d // ln
    grid_spec = pltpu.PrefetchScalarGridSpec(
        num_scalar_prefetch=3,
        grid=(nb,),
        in_specs=[
            pl.BlockSpec((ns, blk, ln), lambda i, be, nx, nu: (i, 0, 0)),
            pl.BlockSpec(memory_space=pl.ANY),
            pl.BlockSpec((1, 1, f2), lambda i, be, nx, nu: (be[i], 0, 0)),
            pl.BlockSpec(memory_space=pl.ANY),
            pl.BlockSpec((1, 1, d), lambda i, be, nx, nu: (be[i], 0, 0)),
        ],
        out_specs=pl.BlockSpec((ns, blk, ln), lambda i, be, nx, nu: (i, 0, 0)),
        scratch_shapes=[pltpu.VMEM((d, f2), w1.dtype), pltpu.VMEM((f, d), w2.dtype),
                        pltpu.VMEM((d, f2), BF16), pltpu.VMEM((f, d), BF16),
                        pltpu.SemaphoreType.DMA((2,))],
    )
    return pl.pallas_call(
        _moe_kernel,
        grid_spec=grid_spec,
        out_shape=jax.ShapeDtypeStruct(xs.shape, F32),
        compiler_params=pltpu.CompilerParams(
            dimension_semantics=("arbitrary",), vmem_limit_bytes=VMEM_LIMIT),
        name="moe_experts",
    )(block_e, next_e, n_used, xs, w1, b1.reshape(ne, 1, f2), w2, b2.reshape(ne, 1, d))


def _final_kernel(row_ref, ys_hbm, h_ref, route_ref, p_ref, npl_ref, wg_ref, wp_ref, nfin_ref, out_ref,
                  gbuf, sems, *, last_layer):
    i = pl.program_id(0)
    tm = h_ref.shape[0]
    ns = gbuf.shape[2]
    cur = lax.rem(i, 2)

    def gather(step, buf):
        def issue(r, c):
            for kk in range(TOP_K):
                slot = row_ref[(step * tm + r) * TOP_K + kk]
                src = ys_hbm.at[pl.ds(_slot_block(slot) * ns, ns), pl.ds(_slot_row(slot), 1), :]
                pltpu.make_async_copy(src, gbuf.at[buf, kk, :, pl.ds(r, 1), :], sems.at[buf]).start()
            return c
        lax.fori_loop(0, tm, issue, 0)

    @pl.when(i == 0)
    def _():
        gather(0, 0)

    @pl.when(i + 1 < pl.num_programs(0))
    def _():
        gather(i + 1, 1 - cur)

    for kk in range(TOP_K):
        pltpu.make_async_copy(ys_hbm.at[pl.ds(0, ns), pl.ds(0, tm), :], gbuf.at[cur, kk], sems.at[cur]).wait()

    route = route_ref[...]
    h = h_ref[...]
    for kk in range(TOP_K):
        yk = jnp.concatenate([gbuf[cur, kk, s] for s in range(ns)], axis=1)
        h = h + route[:, kk:kk + 1] * yk
    u = _rmsnorm(h, npl_ref[...])
    gate = _sigmoid(_mm(u, wg_ref[...]))
    h = h + gate * _mm(p_ref[...], wp_ref[...])
    out_ref[...] = _rmsnorm(h, nfin_ref[...]) if last_layer else h


def _final(slots, ys, h1, route, pf, npl, wg, wp, nfin, tm, last_layer):
    n, d = h1.shape
    pd = pf.shape[1]
    grid_spec = pltpu.PrefetchScalarGridSpec(
        num_scalar_prefetch=1,
        grid=(n // tm,),
        in_specs=[
            pl.BlockSpec(memory_space=pl.ANY),
            pl.BlockSpec((tm, d), lambda i, s: (i, 0)),
            pl.BlockSpec((tm, LANES), lambda i, s: (i, 0)),
            pl.BlockSpec((tm, pd), lambda i, s: (i, 0)),
            pl.BlockSpec((1, d), lambda i, s: (0, 0)),
            pl.BlockSpec((d, d), lambda i, s: (0, 0)),
            pl.BlockSpec((pd, d), lambda i, s: (0, 0)),
            pl.BlockSpec((1, d), lambda i, s: (0, 0)),
        ],
        out_specs=pl.BlockSpec((tm, d), lambda i, s: (i, 0)),
        scratch_shapes=[pltpu.VMEM((2, TOP_K, d // ys.shape[2], tm, ys.shape[2]), F32),
                        pltpu.SemaphoreType.DMA((2,))],
    )
    return pl.pallas_call(
        functools.partial(_final_kernel, last_layer=last_layer),
        grid_spec=grid_spec,
        out_shape=jax.ShapeDtypeStruct((n, d), F32),
        compiler_params=pltpu.CompilerParams(
            dimension_semantics=("arbitrary",), vmem_limit_bytes=VMEM_LIMIT),
        name="final",
    )(slots, ys, h1, route, pf, npl, wg, wp, nfin)


def _routing(route, counts, n_experts):
    n = route.shape[0]
    blk = EXPERT_BLOCK
    nb = (n * TOP_K + n_experts * (blk - 1) + blk - 1) // blk
    counts = counts[0, :n_experts].astype(jnp.int32)
    pcounts = ((counts + blk - 1) // blk) * blk
    pend = jnp.cumsum(pcounts)
    pstart = pend - pcounts
    top_i = route[:, TOP_K:2 * TOP_K].astype(jnp.int32)
    rank = route[:, 2 * TOP_K:3 * TOP_K].astype(jnp.int32)
    onehot = top_i[:, :, None] == jnp.arange(n_experts, dtype=jnp.int32)[None, None, :]
    slots = jnp.sum(jnp.where(onehot, pstart[None, None, :], 0), axis=-1) + rank
    block_rows = jnp.arange(nb, dtype=jnp.int32) * blk
    block_e = jnp.sum(block_rows[:, None] >= pend[None, :], axis=1)
    block_e = jnp.clip(block_e, 0, n_experts - 1).astype(jnp.int32)
    n_used = (pend[-1] // blk).astype(jnp.int32).reshape(1)
    blocks = jnp.arange(nb, dtype=jnp.int32)
    onehot_b = block_e[:, None] == jnp.arange(n_experts, dtype=jnp.int32)[None, :]
    run_end = jnp.sum(jnp.where(onehot_b, pend[None, :], 0), axis=1) // blk
    run_end = jnp.where(blocks >= n_used[0], nb, run_end)
    follow = jnp.sum(jnp.where(blocks[None, :] == run_end[:, None], block_e[None, :], 0), axis=1)
    next_e = jnp.where((run_end < nb) & (follow != block_e), follow, -1).astype(jnp.int32)
    return block_e, next_e, n_used, slots.reshape(-1).astype(jnp.int32), nb * blk


def kernel(x, p, norm_mix, w_in, shift_mu, rw_w0, rw_w2, rw_a0, rw_a2, rw_g2, rw_kk, rw_ka, rw_rk,
           rw_gn_w, rw_gn_b, gla_gk2, gla_gk_b, gla_norm, w_out, norm_ffn, w_router, b_router,
           w1, b1, w2, b2, norm_ple, w_ple_gate, w_ple, norm_final):
    bsz, seq, d = x.shape
    n = bsz * seq
    depth = w_in.shape[0]
    n_experts = w_router.shape[-1]
    tm = 256
    h = x.reshape(n, d)
    for l in range(depth):
        w_rw = w_in[l][:, :RW_COLS].astype(BF16)
        w_gla = jnp.pad(w_in[l][:, RW_COLS:], ((0, 0), (0, LANES - GLA_GATE_RANK))).astype(BF16)
        rw_proj, gla_proj = _in_proj(h, norm_mix[l].reshape(1, d), w_rw, w_gla, tm)
        y_rw = _rwkv(rw_proj.reshape(bsz, seq, -1), shift_mu[l], rw_w0[l], rw_w2[l], rw_a0[l], rw_a2[l],
                     rw_g2[l], rw_kk[l], rw_ka[l], rw_rk[l], rw_gn_w[l], rw_gn_b[l])
        y_gla = _gla(gla_proj.reshape(bsz, seq, -1), gla_gk2[l], gla_gk_b[l], gla_norm[l])

        wr = jnp.pad(w_router[l], ((0, 0), (0, LANES - n_experts)))
        wr_hi = wr.astype(BF16)
        wr_lo = (wr - wr_hi.astype(F32)).astype(BF16)
        br = jnp.pad(b_router[l], (0, LANES - n_experts), constant_values=NEG_BIG).reshape(1, LANES)
        wo = w_out[l].astype(BF16)
        h1, u3, route, counts = _out_proj(y_rw.reshape(n, -1), y_gla.reshape(n, -1), h, wo[:RW_WIDTH],
                                          wo[RW_WIDTH:], norm_ffn[l].reshape(1, d), wr_hi, wr_lo, br, tm)

        ns = d // LANES
        block_e, next_e, n_used, slots, n_rows = _routing(route, counts, n_experts)
        xs = _dispatch(slots, u3, n_rows, tm, ns)
        ys = _moe_experts(block_e, next_e, n_used, xs, w1[l], b1[l], w2[l], b2[l])
        h = _final(slots, ys, h1, route, p[l].reshape(n, -1), norm_ple[l].reshape(1, d),
                   w_ple_gate[l].astype(BF16), w_ple[l].astype(BF16), norm_final.reshape(1, d), tm, l == depth - 1)
    return h.reshape(bsz, seq, d)
```

```python
import functools

import jax
import jax.numpy as jnp
from jax import lax
from jax.experimental import pallas as pl
from jax.experimental.pallas import tpu as pltpu

F32 = jnp.float32
BF16 = jnp.bfloat16

CHUNK = 64
RW_HEADS = 8
RW_HEAD_DIM = 64
RW_WIDTH = RW_HEADS * RW_HEAD_DIM
RW_DECAY_LORA = 64
RW_ICLR_LORA = 64
RW_GATE_LORA = 128
RW_COLS = 3 * RW_WIDTH + RW_DECAY_LORA + RW_ICLR_LORA + RW_GATE_LORA
RW_GN_EPS = 64e-5
GLA_HEADS = 4
GLA_KEY_DIM = 64
GLA_VAL_DIM = 128
GLA_QK_WIDTH = GLA_HEADS * GLA_KEY_DIM
GLA_V_WIDTH = GLA_HEADS * GLA_VAL_DIM
GLA_GATE_RANK = 16
GLA_GATE_TEMP = 16.0
GLA_SUB = 8
LANES = 128
GLA_COLS_PAD = 2 * GLA_QK_WIDTH + 2 * GLA_V_WIDTH + LANES
TOP_K = 4
EXPERT_BLOCK = 256
SWIGLU_ALPHA = 1.702
SWIGLU_LIMIT = 7.0
NORM_EPS = 1e-6
NEG_BIG = -1e30
VMEM_LIMIT = 56 * 1024 * 1024


def _mm(a, b):
    return jnp.dot(a.astype(BF16), b.astype(BF16), preferred_element_type=F32)


def _mm_nt(a, b):
    return lax.dot_general(a.astype(BF16), b.astype(BF16), (((1,), (1,)), ((), ())),
                           preferred_element_type=F32)


def _mm_tn(a, b):
    return lax.dot_general(a.astype(BF16), b.astype(BF16), (((0,), (0,)), ((), ())),
                           preferred_element_type=F32)


def _split(a, n):
    parts = []
    rem = a
    for _ in range(n):
        p = rem.astype(BF16)
        parts.append(p)
        rem = rem - p.astype(F32)
    return parts


def _mm_lhs_split(a, b_bf16, n):
    out = None
    for p in _split(a, n):
        t = jnp.dot(p, b_bf16, preferred_element_type=F32)
        out = t if out is None else out + t
    return out


def _mm_rhs_split(a_bf16, b, n):
    out = None
    for p in _split(b, n):
        t = jnp.dot(a_bf16, p, preferred_element_type=F32)
        out = t if out is None else out + t
    return out


def _rmsnorm(x, g):
    return x * lax.rsqrt(jnp.mean(x * x, axis=-1, keepdims=True) + NORM_EPS) * g


def _softplus(x):
    return jnp.maximum(x, 0.0) + jnp.log(1.0 + jnp.exp(-jnp.abs(x)))


def _sigmoid(x):
    return 1.0 / (1.0 + jnp.exp(-x))


def _tri(n, strict):
    r = lax.broadcasted_iota(jnp.int32, (n, n), 0)
    c = lax.broadcasted_iota(jnp.int32, (n, n), 1)
    return (r > c) if strict else (r >= c)


def _inproj_kernel(x_ref, g_ref, wr_ref, wg_ref, rw_ref, gla_ref):
    u = _rmsnorm(x_ref[...], g_ref[...]).astype(BF16)
    rw_ref[...] = jnp.dot(u, wr_ref[...], preferred_element_type=F32)
    gla_ref[...] = jnp.dot(u, wg_ref[...], preferred_element_type=F32)


def _in_proj(xf, g, w_rw, w_gla, tm):
    n, d = xf.shape
    return pl.pallas_call(
        _inproj_kernel,
        grid=(n // tm,),
        in_specs=[
            pl.BlockSpec((tm, d), lambda i: (i, 0)),
            pl.BlockSpec((1, d), lambda i: (0, 0)),
            pl.BlockSpec(w_rw.shape, lambda i: (0, 0)),
            pl.BlockSpec(w_gla.shape, lambda i: (0, 0)),
        ],
        out_specs=[
            pl.BlockSpec((tm, w_rw.shape[1]), lambda i: (i, 0)),
            pl.BlockSpec((tm, w_gla.shape[1]), lambda i: (i, 0)),
        ],
        out_shape=[
            jax.ShapeDtypeStruct((n, w_rw.shape[1]), F32),
            jax.ShapeDtypeStruct((n, w_gla.shape[1]), F32),
        ],
        compiler_params=pltpu.CompilerParams(
            dimension_semantics=("arbitrary",), vmem_limit_bytes=VMEM_LIMIT),
        name="in_proj",
    )(xf, g, w_rw, w_gla)


def _rwkv_kernel_unpacked(x_ref, mu_ref, w0_ref, w2_ref, a0_ref, a2_ref, g2_ref, kkw_ref, ka_ref,
                          rk_ref, gnw_ref, gnb_ref, bd_ref, y_ref, state_ref, carry_ref, o_ref):
    C, H, D = CHUNK, RW_HEADS, RW_HEAD_DIM

    @pl.when(pl.program_id(1) == 0)
    def _():
        state_ref[...] = jnp.zeros_like(state_ref)
        carry_ref[...] = jnp.zeros_like(carry_ref)

    x = x_ref[0]
    row = lax.broadcasted_iota(jnp.int32, x.shape, 0)
    prev = jnp.where(row == 0, carry_ref[...], pltpu.roll(x, 1, axis=0))
    carry_ref[...] = x[C - 1:C, :]
    h = x + (prev - x) * mu_ref[...]

    W = RW_WIDTH
    r = h[:, 0:W]
    k = h[:, W:2 * W]
    v = h[:, 2 * W:3 * W]
    o0 = 3 * W
    dw = h[:, o0:o0 + RW_DECAY_LORA]
    da = h[:, o0 + RW_DECAY_LORA:o0 + RW_DECAY_LORA + RW_ICLR_LORA]
    dg = h[:, o0 + RW_DECAY_LORA + RW_ICLR_LORA:]

    bd = bd_ref[...]

    def seg_sum(t):
        return _mm_lhs_split(t, bd, 2)

    w_log = -_softplus(-(w0_ref[...] + _mm(jnp.tanh(dw), w2_ref[...]))) - 0.5
    lw = -jnp.exp(w_log)
    iclr = _sigmoid(a0_ref[...] + _mm(da, a2_ref[...]))
    gate = _mm(_sigmoid(dg), g2_ref[...])

    kk = k * kkw_ref[...]
    kk = kk / jnp.maximum(jnp.sqrt(seg_sum(kk * kk)), 1e-12)
    k2 = k * (1.0 + (iclr - 1.0) * ka_ref[...])

    cw = _mm_rhs_split(_tri(C, False).astype(BF16), lw, 3)
    cw_last = cw[C - 1:C, :]
    e_cw = jnp.exp(cw)
    e_ncw = jnp.exp(-cw)
    e_rem = jnp.exp(cw_last - cw)
    kka = kk * iclr
    a_t = -kk * jnp.exp(cw - lw)
    r_t = r * e_cw
    b_t = kka * e_ncw
    k_t = k2 * e_ncw
    b_h = kka * e_rem
    k_h = k2 * e_rem
    w_c = jnp.exp(cw_last)

    strict = _tri(C, True)
    incl = _tri(C, False)
    eye = jnp.where(_tri(C, False) & ~strict, 1.0, 0.0).astype(F32)

    hs = range(H)
    sls = [slice(hd * D, (hd + 1) * D) for hd in hs]
    states = [state_ref[hd] for hd in hs]
    ah = [a_t[:, sl] for sl in sls]
    rh = [r_t[:, sl] for sl in sls]
    bh = [b_t[:, sl] for sl in sls]
    kh = [k_t[:, sl] for sl in sls]
    vh = [v[:, sl] for sl in sls]
    a_ab = [jnp.where(strict, _mm_nt(ah[i], bh[i]), 0.0) for i in hs]
    a_ak = [jnp.where(strict, _mm_nt(ah[i], kh[i]), 0.0) for i in hs]
    a_rb = [jnp.where(incl, _mm_nt(rh[i], bh[i]), 0.0) for i in hs]
    a_rk = [jnp.where(incl, _mm_nt(rh[i], kh[i]), 0.0) for i in hs]
    p = [_mm(a_ab[i], a_ab[i]) for i in hs]
    q = [eye + a_ab[i] for i in hs]
    for _ in range(4):
        pq = [_mm(p[i], q[i]) for i in hs]
        p = [_mm(p[i], p[i]) for i in hs]
        q = [q[i] + pq[i] for i in hs]
    pq = [_mm(p[i], q[i]) for i in hs]
    t_inv = [q[i] + pq[i] for i in hs]
    akv = [_mm(a_ak[i], vh[i]) for i in hs]
    u0 = [_mm(t_inv[i], akv[i]) for i in hs]
    a_hat = [_mm(t_inv[i], ah[i]) for i in hs]
    u = [_mm_nt(a_hat[i], states[i]) + u0[i] for i in hs]
    o_heads = [_mm_nt(rh[i], states[i]) + _mm(a_rb[i], u[i]) + _mm(a_rk[i], vh[i]) for i in hs]
    new_states = [states[i] * w_c[:, sls[i]] + _mm_tn(u[i], b_h[:, sls[i]]) + _mm_tn(vh[i], k_h[:, sls[i]])
                  for i in hs]
    for hd in hs:
        state_ref[hd] = new_states[hd]
        o_ref[:, sls[hd]] = o_heads[hd]

    o = o_ref[...]
    inv_d = 1.0 / D
    mean = seg_sum(o) * inv_d
    dlt = o - mean
    var = seg_sum(dlt * dlt) * inv_d
    o = dlt * lax.rsqrt(var + RW_GN_EPS) * gnw_ref[...] + gnb_ref[...]
    bonus = seg_sum(r * k2 * rk_ref[...]) * v
    y_ref[0] = ((o + bonus) * gate).astype(y_ref.dtype)


def _rwkv_unpacked(rw_proj, mu, w0, w2, a0, a2, g2, kkw, ka, rk, gnw, gnb):
    b, s, cols = rw_proj.shape
    W = RW_WIDTH
    bd = jnp.kron(jnp.eye(RW_HEADS, dtype=F32), jnp.ones((RW_HEAD_DIM, RW_HEAD_DIM), F32)).astype(BF16)
    row = lambda a: a.reshape(1, -1)
    full = lambda a: pl.BlockSpec(a.shape, lambda i, j: (0,) * a.ndim)
    args = [row(mu), row(w0), w2.astype(BF16), row(a0), a2.astype(BF16), g2.astype(BF16),
            row(kkw), row(ka), row(rk), row(gnw), row(gnb), bd]
    return pl.pallas_call(
        _rwkv_kernel_unpacked,
        grid=(b, s // CHUNK),
        in_specs=[pl.BlockSpec((1, CHUNK, cols), lambda i, j: (i, j, 0))] + [full(a) for a in args],
        out_specs=pl.BlockSpec((1, CHUNK, W), lambda i, j: (i, j, 0)),
        out_shape=jax.ShapeDtypeStruct((b, s, W), BF16),
        scratch_shapes=[
            pltpu.VMEM((RW_HEADS, RW_HEAD_DIM, RW_HEAD_DIM), F32),
            pltpu.VMEM((1, cols), F32),
            pltpu.VMEM((CHUNK, W), F32),
        ],
        compiler_params=pltpu.CompilerParams(
            dimension_semantics=("arbitrary", "arbitrary"), vmem_limit_bytes=VMEM_LIMIT),
        name="rwkv7",
    )(rw_proj, *args)


RW_GROUP = 4
RW_GW = RW_GROUP * RW_HEAD_DIM
RW_STEP_CHUNKS = 2


def _rwkv_kernel(x_ref, mu_ref, w0_ref, w2_ref, a0_ref, a2_ref, g2_ref, kkw_ref, ka_ref,
                 rk_ref, gnw_ref, gnb_ref, bd_ref, y_ref, state_ref, carry_ref):
    C, D, W, GW = CHUNK, RW_HEAD_DIM, RW_WIDTH, RW_GW
    nb, ct = x_ref.shape[0], x_ref.shape[1]
    nch = ct // C
    R = nb * ct

    @pl.when(pl.program_id(0) == 0)
    def _():
        state_ref[...] = jnp.zeros_like(state_ref)
        carry_ref[...] = jnp.zeros_like(carry_ref)

    x = x_ref[...].reshape(R, x_ref.shape[2])
    row = lax.broadcasted_iota(jnp.int32, x.shape, 0)
    prev = pltpu.roll(x, 1, axis=0)
    for b in range(nb):
        prev = jnp.where(row == b * ct, carry_ref[b:b + 1, :], prev)
        carry_ref[b:b + 1, :] = x[(b + 1) * ct - 1:(b + 1) * ct, :]
    h = x + (prev - x) * mu_ref[...]

    r = h[:, 0:W]
    k = h[:, W:2 * W]
    v = h[:, 2 * W:3 * W]
    o0 = 3 * W
    dw = h[:, o0:o0 + RW_DECAY_LORA]
    da = h[:, o0 + RW_DECAY_LORA:o0 + RW_DECAY_LORA + RW_ICLR_LORA]
    dg = h[:, o0 + RW_DECAY_LORA + RW_ICLR_LORA:]

    bd_g = bd_ref[...]

    def seg_sum(t):
        return jnp.concatenate(
            [_mm_lhs_split(t[:, g * GW:(g + 1) * GW], bd_g, 2) for g in range(W // GW)], axis=1)

    w_log = -_softplus(-(w0_ref[...] + _mm(jnp.tanh(dw), w2_ref[...]))) - 0.5
    lw = -jnp.exp(w_log)
    iclr = _sigmoid(a0_ref[...] + _mm(da, a2_ref[...]))
    gate = _mm(_sigmoid(dg), g2_ref[...])

    kk = k * kkw_ref[...]
    kk = kk / jnp.maximum(jnp.sqrt(seg_sum(kk * kk)), 1e-12)
    k2 = k * (1.0 + (iclr - 1.0) * ka_ref[...])

    rr = lax.broadcasted_iota(jnp.int32, (R, R), 0)
    cc = lax.broadcasted_iota(jnp.int32, (R, R), 1)
    tri_seq = jnp.where((rr >= cc) & (rr // C == cc // C), 1.0, 0.0).astype(BF16)
    cw = _mm_rhs_split(tri_seq, lw, 3)
    cw_last = jnp.concatenate(
        [jnp.broadcast_to(cw[(j + 1) * C - 1:(j + 1) * C, :], (C, W)) for j in range(R // C)], axis=0)
    e_cw = jnp.exp(cw)
    e_ncw = jnp.exp(-cw)
    e_rem = jnp.exp(cw_last - cw)
    kka = kk * iclr
    a_t = -kk * jnp.exp(cw - lw)
    r_t = r * e_cw
    b_t = kka * e_ncw
    k_t = k2 * e_ncw
    b_h = kka * e_rem
    k_h = k2 * e_rem
    w_c = jnp.exp(cw_last)

    ti = lax.broadcasted_iota(jnp.int32, (C, GW), 0)
    si = jnp.bitwise_and(lax.broadcasted_iota(jnp.int32, (C, GW), 1), D - 1)
    strict = ti > si
    incl = ti >= si
    eye = jnp.where(ti == si, 1.0, 0.0)

    def bdiag(y):
        yb = y.astype(BF16)
        return jnp.concatenate([yb] * RW_GROUP, axis=0) * bd_g

    ng = W // GW
    chains = [(b, g, ch) for ch in range(nch) for b in range(nb) for g in range(ng)]

    def part(t, c):
        b, g, ch = c
        r0 = b * ct + ch * C
        return t[r0:r0 + C, g * GW:(g + 1) * GW]

    n = range(len(chains))
    a4 = [part(a_t, c) for c in chains]
    r4 = [part(r_t, c) for c in chains]
    v4 = [part(v, c) for c in chains]
    ar = [jnp.concatenate([a4[i], r4[i]], axis=0) for i in n]
    bd_b = [bdiag(part(b_t, c)) for c in chains]
    bd_k = [bdiag(part(k_t, c)) for c in chains]
    bd_v = [bdiag(v4[i]) for i in n]
    m_b = [_mm_nt(ar[i], bd_b[i]) for i in n]
    m_k = [_mm_nt(ar[i], bd_k[i]) for i in n]
    a_ab = [jnp.where(strict, m_b[i][0:C], 0.0) for i in n]
    a_rb = [jnp.where(incl, m_b[i][C:2 * C], 0.0) for i in n]
    a_ak = [jnp.where(strict, m_k[i][0:C], 0.0) for i in n]
    a_rk = [jnp.where(incl, m_k[i][C:2 * C], 0.0) for i in n]
    akv = [_mm(a_ak[i], bd_v[i]) for i in n]
    o_kv = [_mm(a_rk[i], bd_v[i]) for i in n]
    p = [_mm(a_ab[i], bdiag(a_ab[i])) for i in n]
    q = [eye + a_ab[i] for i in n]
    for _ in range(4):
        pq = [_mm(p[i], bdiag(q[i])) for i in n]
        p = [_mm(p[i], bdiag(p[i])) for i in n]
        q = [q[i] + pq[i] for i in n]
    pq = [_mm(p[i], bdiag(q[i])) for i in n]
    t_inv = [q[i] + pq[i] for i in n]
    u0 = [_mm(t_inv[i], bdiag(akv[i])) for i in n]
    a_hat = [_mm(t_inv[i], bdiag(a4[i])) for i in n]
    seqs = [(b, g) for b in range(nb) for g in range(ng)]
    s4 = [state_ref[b, g] for b, g in seqs]
    o_g = {}
    for ch in range(nch):
        idx = [chains.index((b, g, ch)) for b, g in seqs]
        bd_s = [bdiag(s) for s in s4]
        o_s = [_mm_nt(r4[i], bd_s[j]) for j, i in enumerate(idx)]
        u = [_mm_nt(a_hat[i], bd_s[j]) + u0[i] for j, i in enumerate(idx)]
        for j, i in enumerate(idx):
            o_g[chains[i]] = o_s[j] + _mm(a_rb[i], bdiag(u[j])) + o_kv[i]
        nxt = []
        for j, i in enumerate(idx):
            c = chains[i]
            uv = jnp.concatenate([u[j], v4[i]], axis=0)
            bk = jnp.concatenate([part(b_h, c), part(k_h, c)], axis=0)
            full = _mm_tn(uv, bk) * bd_g
            upd = full[0:D]
            for hd in range(1, RW_GROUP):
                upd = upd + full[hd * D:(hd + 1) * D]
            nxt.append(s4[j] * part(w_c, c)[0:D] + upd)
        s4 = nxt
    for j, (b, g) in enumerate(seqs):
        state_ref[b, g] = s4[j]

    o = jnp.concatenate(
        [jnp.concatenate([o_g[(b, g, ch)] for g in range(ng)], axis=1) for b in range(nb) for ch in range(nch)],
        axis=0)
    inv_d = 1.0 / D
    mean = seg_sum(o) * inv_d
    dlt = o - mean
    var = seg_sum(dlt * dlt) * inv_d
    o = dlt * lax.rsqrt(var + RW_GN_EPS) * gnw_ref[...] + gnb_ref[...]
    bonus = seg_sum(r * k2 * rk_ref[...]) * v
    y_ref[...] = ((o + bonus) * gate).astype(y_ref.dtype).reshape(y_ref.shape)


def _rwkv(rw_proj, mu, w0, w2, a0, a2, g2, kkw, ka, rk, gnw, gnb):
    b, s, cols = rw_proj.shape
    W = RW_WIDTH
    bd = jnp.kron(jnp.eye(RW_GROUP, dtype=F32), jnp.ones((RW_HEAD_DIM, RW_HEAD_DIM), F32)).astype(BF16)
    row = lambda a: a.reshape(1, -1)
    full = lambda a: pl.BlockSpec(a.shape, lambda j: (0,) * a.ndim)
    args = [row(mu), row(w0), w2.astype(BF16), row(a0), a2.astype(BF16), g2.astype(BF16),
            row(kkw), row(ka), row(rk), row(gnw), row(gnb), bd]
    ct = CHUNK * RW_STEP_CHUNKS
    return pl.pallas_call(
        _rwkv_kernel,
        grid=(s // ct,),
        in_specs=[pl.BlockSpec((b, ct, cols), lambda j: (0, j, 0))] + [full(a) for a in args],
        out_specs=pl.BlockSpec((b, ct, W), lambda j: (0, j, 0)),
        out_shape=jax.ShapeDtypeStruct((b, s, W), BF16),
        scratch_shapes=[
            pltpu.VMEM((b, W // RW_GW, RW_HEAD_DIM, RW_GW), F32),
            pltpu.VMEM((b, cols), F32),
        ],
        compiler_params=pltpu.CompilerParams(
            dimension_semantics=("arbitrary",), vmem_limit_bytes=VMEM_LIMIT),
        name="rwkv7",
    )(rw_proj, *args)


GLA_STEP_CHUNKS = 2


def _log2(n):
    assert n & (n - 1) == 0
    return n.bit_length() - 1


def _group_mask(rows, row_group, cols, col_group):
    r = lax.shift_right_logical(lax.broadcasted_iota(jnp.int32, (rows, cols), 0), _log2(row_group))
    c = lax.shift_right_logical(lax.broadcasted_iota(jnp.int32, (rows, cols), 1), _log2(col_group))
    return jnp.where(r == c, 1.0, 0.0).astype(BF16)


def _gla_kernel(x_ref, gk2_ref, gkb_ref, ng_ref, y_ref, state_ref):
    C, H, DK, DV, SB = CHUNK, GLA_HEADS, GLA_KEY_DIM, GLA_VAL_DIM, GLA_SUB
    QW, VW = GLA_QK_WIDTH, GLA_V_WIDTH
    nb, ct = x_ref.shape[0], x_ref.shape[1]
    nch = ct // C
    R = nb * ct

    @pl.when(pl.program_id(0) == 0)
    def _():
        state_ref[...] = jnp.zeros_like(state_ref)

    x = x_ref[...].reshape(R, x_ref.shape[2])
    q = x[:, 0:QW] * (DK ** -0.5)
    k = x[:, QW:2 * QW]
    v = x[:, 2 * QW:2 * QW + VW]
    g = x[:, 2 * QW + VW:2 * QW + 2 * VW]
    dgk = x[:, 2 * QW + 2 * VW:]

    la = -_softplus(-(_mm(dgk, gk2_ref[...]) + gkb_ref[...])) * (1.0 / GLA_GATE_TEMP)
    rr = lax.broadcasted_iota(jnp.int32, (R, R), 0)
    cc = lax.broadcasted_iota(jnp.int32, (R, R), 1)
    tri_seq = jnp.where((rr >= cc) & (rr // C == cc // C), 1.0, 0.0).astype(BF16)
    b = _mm_rhs_split(tri_seq, la, 3)
    b_last = jnp.concatenate(
        [jnp.broadcast_to(b[(j + 1) * C - 1:(j + 1) * C, :], (C, QW)) for j in range(R // C)], axis=0)
    q_e = q * jnp.exp(b)
    k_e = k * jnp.exp(b_last - b)
    w_c = jnp.exp(b_last)

    def tile_rows(y, n):
        return jnp.concatenate([y.astype(BF16)] * n, axis=0)

    units = [(bi, ch) for ch in range(nch) for bi in range(nb)]
    row0 = {u: u[0] * ct + u[1] * C for u in units}

    o_off = {u: None for u in units}
    for s in (C // 2, C // 4, C // 8):
        mk = _group_mask(H * s, s, QW, DK)
        mv = _group_mask(H * s, s, VW, DV)
        jobs = [(u, row0[u] + m * 2 * s) for u in units for m in range(C // (2 * s))]
        att = []
        for u, c0 in jobs:
            ref = b[c0 + s - 1:c0 + s]
            q_s = q[c0 + s:c0 + 2 * s] * jnp.exp(b[c0 + s:c0 + 2 * s] - ref)
            k_s = k[c0:c0 + s] * jnp.exp(ref - b[c0:c0 + s])
            att.append(_mm_nt(q_s, tile_rows(k_s, H) * mk))
        outs = [_mm(att[j], tile_rows(v[c0:c0 + s], H) * mv) for j, (u, c0) in enumerate(jobs)]
        zero = jnp.zeros((s, VW), F32)
        for u in units:
            pieces = []
            for j, (uj, c0) in enumerate(jobs):
                if uj == u:
                    pieces += [zero, outs[j]]
            level = jnp.concatenate(pieces, axis=0)
            o_off[u] = level if o_off[u] is None else o_off[u] + level

    nblk = C // SB
    ii = lax.broadcasted_iota(jnp.int32, (nblk, SB, SB, 2 * DK), 1)
    jj = lax.broadcasted_iota(jnp.int32, (nblk, SB, SB, 2 * DK), 2)
    causal4 = jj <= ii
    sel = _group_mask(C, 1, C * SB, SB)
    pair_ones = _group_mask(2 * DK, DK, 2 * DV, DV)
    att2 = {}
    for u in units:
        r0 = row0[u]
        for pr in range(H // 2):
            sl2 = slice(2 * pr * DK, 2 * (pr + 1) * DK)
            q2, k2, b2 = q[r0:r0 + C, sl2], k[r0:r0 + C, sl2], b[r0:r0 + C, sl2]
            dec = jnp.exp(jnp.where(
                causal4, b2.reshape(nblk, SB, 1, 2 * DK) - b2.reshape(nblk, 1, SB, 2 * DK), NEG_BIG))
            pw = q2.reshape(nblk, SB, 1, 2 * DK) * k2.reshape(nblk, 1, SB, 2 * DK) * dec
            att2[(u, pr)] = _mm(pw.reshape(C * SB, 2 * DK), pair_ones)
    o_diag = {}
    for u in units:
        r0 = row0[u]
        heads = []
        for hd in range(H):
            v_h = v[r0:r0 + C, hd * DV:(hd + 1) * DV]
            v_rep = jnp.broadcast_to(v_h.reshape(nblk, 1, SB, DV), (nblk, SB, SB, DV)).reshape(C * SB, DV)
            heads.append(_mm(sel, att2[(u, hd // 2)][:, (hd % 2) * DV:(hd % 2 + 1) * DV] * v_rep))
        o_diag[u] = jnp.concatenate(heads, axis=1)

    ms = _group_mask(H * DV, DV, QW, DK)
    s4 = [state_ref[bi] for bi in range(nb)]
    o_int = {}
    for ch in range(nch):
        for bi in range(nb):
            r0 = row0[(bi, ch)]
            o_int[(bi, ch)] = _mm_nt(q_e[r0:r0 + C], tile_rows(s4[bi], H) * ms)
        nxt = []
        for bi in range(nb):
            r0 = row0[(bi, ch)]
            full = _mm_tn(v[r0:r0 + C], k_e[r0:r0 + C]) * ms
            upd = full[0:DV]
            for hd in range(1, H):
                upd = upd + full[hd * DV:(hd + 1) * DV]
            nxt.append(s4[bi] * w_c[r0:r0 + 1] + upd)
        s4 = nxt
    for bi in range(nb):
        state_ref[bi] = s4[bi]

    rows = []
    for bi in range(nb):
        for ch in range(nch):
            u = (bi, ch)
            o = o_int[u] + o_diag[u] + o_off[u]
            heads = []
            for hd in range(H):
                oh = o[:, hd * DV:(hd + 1) * DV]
                heads.append(oh * lax.rsqrt(jnp.mean(oh * oh, axis=-1, keepdims=True) + NORM_EPS))
            rows.append(jnp.concatenate(heads, axis=1))
    o = jnp.concatenate(rows, axis=0)
    y = o * ng_ref[...] * (g * _sigmoid(g))
    y_ref[...] = y.astype(y_ref.dtype).reshape(y_ref.shape)


def _gla(gla_proj, gk2, gkb, ng):
    b, s, cols = gla_proj.shape
    gk2p = jnp.zeros((LANES, GLA_QK_WIDTH), F32).at[:GLA_GATE_RANK].set(gk2).astype(BF16)
    args = [gk2p, gkb.reshape(1, -1), ng.reshape(1, -1)]
    full = lambda a: pl.BlockSpec(a.shape, lambda j: (0,) * a.ndim)
    ct = CHUNK * GLA_STEP_CHUNKS
    return pl.pallas_call(
        _gla_kernel,
        grid=(s // ct,),
        in_specs=[pl.BlockSpec((b, ct, cols), lambda j: (0, j, 0))] + [full(a) for a in args],
        out_specs=pl.BlockSpec((b, ct, GLA_V_WIDTH), lambda j: (0, j, 0)),
        out_shape=jax.ShapeDtypeStruct((b, s, GLA_V_WIDTH), BF16),
        scratch_shapes=[pltpu.VMEM((b, GLA_VAL_DIM, GLA_QK_WIDTH), F32)],
        compiler_params=pltpu.CompilerParams(
            dimension_semantics=("arbitrary",), vmem_limit_bytes=VMEM_LIMIT),
        name="gla",
    )(gla_proj, *args)


def _gla_kernel_unpacked(x_ref, gk2_ref, gkb_ref, ng_ref, y_ref, state_ref):
    C, H, DK, DV, SB = CHUNK, GLA_HEADS, GLA_KEY_DIM, GLA_VAL_DIM, GLA_SUB

    @pl.when(pl.program_id(1) == 0)
    def _():
        state_ref[...] = jnp.zeros_like(state_ref)

    x = x_ref[0]
    QW, VW = GLA_QK_WIDTH, GLA_V_WIDTH
    q = x[:, 0:QW] * (DK ** -0.5)
    k = x[:, QW:2 * QW]
    v = x[:, 2 * QW:2 * QW + VW]
    g = x[:, 2 * QW + VW:2 * QW + 2 * VW]
    dgk = x[:, 2 * QW + 2 * VW:]

    la = -_softplus(-(_mm(dgk, gk2_ref[...]) + gkb_ref[...])) * (1.0 / GLA_GATE_TEMP)
    b = _mm_rhs_split(_tri(C, False).astype(BF16), la, 3)
    b_last = b[C - 1:C, :]
    q_e = q * jnp.exp(b)
    k_e = k * jnp.exp(b_last - b)
    w_c = jnp.exp(b_last)

    ii = lax.broadcasted_iota(jnp.int32, (C // SB, SB, SB, 2 * DK), 1)
    jj = lax.broadcasted_iota(jnp.int32, (C // SB, SB, SB, 2 * DK), 2)
    causal4 = jj <= ii

    hs = range(H)
    nblk = C // SB
    sks = [slice(hd * DK, (hd + 1) * DK) for hd in hs]
    svs = [slice(hd * DV, (hd + 1) * DV) for hd in hs]
    states = [state_ref[hd] for hd in hs]
    qh = [q[:, sk] for sk in sks]
    kh = [k[:, sk] for sk in sks]
    bh = [b[:, sk] for sk in sks]
    vh = [v[:, sv] for sv in svs]
    o_inter = [_mm_nt(q_e[:, sks[i]], states[i]) for i in hs]
    new_states = [states[i] * w_c[:, sks[i]] + _mm_tn(vh[i], k_e[:, sks[i]]) for i in hs]
    att_off = {}
    for blk in range(1, nblk):
        r0 = blk * SB
        for i in hs:
            ref = bh[i][r0 - 1:r0]
            q_s = qh[i][r0:r0 + SB] * jnp.exp(bh[i][r0:r0 + SB] - ref)
            k_s = kh[i][0:r0] * jnp.exp(ref - bh[i][0:r0])
            att_off[(i, blk)] = _mm_nt(q_s, k_s)
    o_off = {key: _mm(att, vh[key[0]][0:key[1] * SB]) for key, att in att_off.items()}
    rsel = lax.broadcasted_iota(jnp.int32, (C, C * SB), 0)
    csel = lax.broadcasted_iota(jnp.int32, (C, C * SB), 1)
    sel = jnp.where((csel >= rsel * SB) & (csel < (rsel + 1) * SB), 1.0, 0.0).astype(BF16)
    drow = lax.broadcasted_iota(jnp.int32, (2 * DK, 2 * DV), 0)
    dcol = lax.broadcasted_iota(jnp.int32, (2 * DK, 2 * DV), 1)
    pair_ones = jnp.where((drow >= DK) == (dcol >= DV), 1.0, 0.0).astype(BF16)
    o_diag = []
    for pr in range(H // 2):
        sl2 = slice(2 * pr * DK, 2 * (pr + 1) * DK)
        q2, k2, b2 = q[:, sl2], k[:, sl2], b[:, sl2]
        dec = jnp.exp(jnp.where(causal4, b2.reshape(nblk, SB, 1, 2 * DK) - b2.reshape(nblk, 1, SB, 2 * DK),
                                NEG_BIG))
        pw = q2.reshape(nblk, SB, 1, 2 * DK) * k2.reshape(nblk, 1, SB, 2 * DK) * dec
        att2 = _mm(pw.reshape(C * SB, 2 * DK), pair_ones)
        for t in range(2):
            v_rep = jnp.broadcast_to(vh[2 * pr + t].reshape(nblk, 1, SB, DV), (nblk, SB, SB, DV))
            z = att2[:, t * DV:(t + 1) * DV] * v_rep.reshape(C * SB, DV)
            o_diag.append(_mm(sel, z))
    outs = []
    for i in hs:
        rows = [jnp.zeros((SB, DV), F32)] + [o_off[(i, blk)] for blk in range(1, nblk)]
        o = o_inter[i] + o_diag[i] + jnp.concatenate(rows, axis=0)
        outs.append(o * lax.rsqrt(jnp.mean(o * o, axis=-1, keepdims=True) + NORM_EPS))
    for hd in hs:
        state_ref[hd] = new_states[hd]
    o = jnp.concatenate(outs, axis=1)
    y = o * ng_ref[...] * (g * _sigmoid(g))
    y_ref[0] = y.astype(y_ref.dtype)


def _gla_unpacked(gla_proj, gk2, gkb, ng):
    b, s, cols = gla_proj.shape
    gk2p = jnp.zeros((LANES, GLA_QK_WIDTH), F32).at[:GLA_GATE_RANK].set(gk2).astype(BF16)
    args = [gk2p, gkb.reshape(1, -1), ng.reshape(1, -1)]
    full = lambda a: pl.BlockSpec(a.shape, lambda i, j: (0,) * a.ndim)
    return pl.pallas_call(
        _gla_kernel_unpacked,
        grid=(b, s // CHUNK),
        in_specs=[pl.BlockSpec((1, CHUNK, cols), lambda i, j: (i, j, 0))] + [full(a) for a in args],
        out_specs=pl.BlockSpec((1, CHUNK, GLA_V_WIDTH), lambda i, j: (i, j, 0)),
        out_shape=jax.ShapeDtypeStruct((b, s, GLA_V_WIDTH), BF16),
        scratch_shapes=[pltpu.VMEM((GLA_HEADS, GLA_VAL_DIM, GLA_KEY_DIM), F32)],
        compiler_params=pltpu.CompilerParams(
            dimension_semantics=("arbitrary", "arbitrary"), vmem_limit_bytes=VMEM_LIMIT),
        name="gla",
    )(gla_proj, *args)


def _outproj_kernel(yr_ref, yg_ref, x_ref, wor_ref, wog_ref, nf_ref, wrh_ref, wrl_ref, br_ref,
                    h_ref, u3_ref, route_ref, cnt_ref, carry_ref):
    tm, d = x_ref.shape

    @pl.when(pl.program_id(0) == 0)
    def _():
        carry_ref[...] = jnp.zeros_like(carry_ref)

    h = (x_ref[...] + jnp.dot(yr_ref[...], wor_ref[...], preferred_element_type=F32)
         + jnp.dot(yg_ref[...], wog_ref[...], preferred_element_type=F32))
    h_ref[...] = h
    u = _rmsnorm(h, nf_ref[...])
    for s in range(d // LANES):
        u3_ref[s] = u[:, s * LANES:(s + 1) * LANES]
    u_hi, u_lo = _split(u, 2)
    logits = (jnp.dot(u_hi, wrh_ref[...], preferred_element_type=F32)
              + jnp.dot(u_hi, wrl_ref[...], preferred_element_type=F32)
              + jnp.dot(u_lo, wrh_ref[...], preferred_element_type=F32)) + br_ref[...]
    lane = lax.broadcasted_iota(jnp.int32, logits.shape, 1)
    rest = logits
    picks, idxs, vals = [], [], []
    for r in range(TOP_K):
        m = jnp.max(rest, axis=-1, keepdims=True)
        idx = jnp.min(jnp.where(rest == m, lane, LANES), axis=-1, keepdims=True)
        pick = lane == idx
        picks.append(pick)
        idxs.append(idx)
        vals.append(m)
        rest = jnp.where(pick, -jnp.inf, rest)
    denom = jnp.ones_like(vals[0])
    for r in range(1, TOP_K):
        denom = denom + jnp.exp(vals[r] - vals[0])
    sel = jnp.zeros(logits.shape, F32)
    for pick in picks:
        sel = sel + jnp.where(pick, 1.0, 0.0)
    prefix = _mm(jnp.where(_tri(tm, True), 1.0, 0.0), sel) + carry_ref[...]
    carry_ref[...] = carry_ref[...] + jnp.sum(sel, axis=0, keepdims=True)
    cnt_ref[...] = carry_ref[...]
    route = jnp.zeros(logits.shape, F32)
    for r in range(TOP_K):
        gate = jnp.exp(vals[r] - vals[0]) / denom
        rank = jnp.sum(jnp.where(picks[r], prefix, 0.0), axis=-1, keepdims=True)
        route = jnp.where(lane == r, gate, route)
        route = jnp.where(lane == TOP_K + r, idxs[r].astype(F32), route)
        route = jnp.where(lane == 2 * TOP_K + r, rank, route)
    route_ref[...] = route


def _out_proj(y_rw, y_gla, xf, wo_r, wo_g, nf, wr_hi, wr_lo, br, tm):
    n, d = xf.shape
    full = lambda a: pl.BlockSpec(a.shape, lambda i: (0,) * a.ndim)
    tile = lambda w: pl.BlockSpec((tm, w), lambda i: (i, 0))
    return pl.pallas_call(
        _outproj_kernel,
        grid=(n // tm,),
        in_specs=[tile(y_rw.shape[1]), tile(y_gla.shape[1]), tile(d), full(wo_r), full(wo_g), full(nf),
                  full(wr_hi), full(wr_lo), full(br)],
        out_specs=[tile(d), pl.BlockSpec((d // LANES, tm, LANES), lambda i: (i, 0, 0)), tile(LANES),
                   pl.BlockSpec((1, LANES), lambda i: (0, 0))],
        out_shape=[jax.ShapeDtypeStruct((n, d), F32), jax.ShapeDtypeStruct((n // tm * (d // LANES), tm, LANES), F32),
                   jax.ShapeDtypeStruct((n, LANES), F32), jax.ShapeDtypeStruct((1, LANES), F32)],
        scratch_shapes=[pltpu.VMEM((1, LANES), F32)],
        compiler_params=pltpu.CompilerParams(
            dimension_semantics=("arbitrary",), vmem_limit_bytes=VMEM_LIMIT),
        name="out_proj",
    )(y_rw, y_gla, xf, wo_r, wo_g, nf, wr_hi, wr_lo, br)


assert EXPERT_BLOCK & (EXPERT_BLOCK - 1) == 0
_BLOCK_SHIFT = EXPERT_BLOCK.bit_length() - 1


def _slot_block(slot):
    return lax.shift_right_logical(slot, _BLOCK_SHIFT)


def _slot_row(slot):
    return jnp.bitwise_and(slot, EXPERT_BLOCK - 1)


_DISPATCH_BUFS = 3


def _dispatch_kernel(row_ref, u3_hbm, xs_in_hbm, xs_hbm, ubuf, in_sems, out_sems, *, tm, ns):
    del xs_in_hbm
    i = pl.program_id(0)
    nsteps = pl.num_programs(0)
    nbuf = _DISPATCH_BUFS

    def load(step, slot_):
        return pltpu.make_async_copy(u3_hbm.at[pl.ds(step * ns, ns)], ubuf.at[slot_], in_sems.at[slot_])

    def wait_scatter(slot_):
        for _ in range(TOP_K):
            pltpu.make_async_copy(ubuf.at[slot_], xs_hbm.at[pl.ds(0, ns), pl.ds(0, tm), :],
                                  out_sems.at[slot_]).wait()

    cur = lax.rem(i, nbuf)
    nxt = lax.rem(i + 1, nbuf)

    @pl.when(i == 0)
    def _():
        load(0, 0).start()

    @pl.when(i >= nbuf - 1)
    def _():
        wait_scatter(nxt)

    @pl.when(i + 1 < nsteps)
    def _():
        load(i + 1, nxt).start()

    load(i, cur).wait()

    def issue(r, c):
        src = ubuf.at[cur, :, pl.ds(r, 1), :]
        for kk in range(TOP_K):
            slot = row_ref[(i * tm + r) * TOP_K + kk]
            dst = xs_hbm.at[pl.ds(_slot_block(slot) * ns, ns), pl.ds(_slot_row(slot), 1), :]
            pltpu.make_async_copy(src, dst, out_sems.at[cur]).start()
        return c
    lax.fori_loop(0, tm, issue, 0)

    @pl.when(i == nsteps - 1)
    def _():
        for back in range(nbuf - 1):
            @pl.when(i - back >= 0)
            def _():
                wait_scatter(lax.rem(i - back + nbuf, nbuf))


def _dispatch(slots, u3, n_rows, tm, ns):
    ln = u3.shape[2]
    n = u3.shape[0] // ns * tm
    xs0 = jnp.zeros((n_rows // EXPERT_BLOCK * ns, EXPERT_BLOCK, ln), u3.dtype)
    grid_spec = pltpu.PrefetchScalarGridSpec(
        num_scalar_prefetch=1,
        grid=(n // tm,),
        in_specs=[pl.BlockSpec(memory_space=pl.ANY), pl.BlockSpec(memory_space=pl.ANY)],
        out_specs=pl.BlockSpec(memory_space=pl.ANY),
        scratch_shapes=[pltpu.VMEM((_DISPATCH_BUFS, ns, tm, ln), u3.dtype),
                        pltpu.SemaphoreType.DMA((_DISPATCH_BUFS,)),
                        pltpu.SemaphoreType.DMA((_DISPATCH_BUFS,))],
    )
    return pl.pallas_call(
        functools.partial(_dispatch_kernel, tm=tm, ns=ns),
        grid_spec=grid_spec,
        out_shape=jax.ShapeDtypeStruct(xs0.shape, xs0.dtype),
        input_output_aliases={2: 0},
        compiler_params=pltpu.CompilerParams(dimension_semantics=("arbitrary",), has_side_effects=True),
        name="moe_dispatch",
    )(slots, u3, xs0)


def _moe_kernel(be_ref, nxt_ref, nused_ref, xs_ref, w1_hbm, b1_ref, w2_hbm, b2_ref, ys_ref,
                w1f, w2f, w1b, w2b, sems):
    i = pl.program_id(0)
    f = w2b.shape[0]
    ns = xs_ref.shape[0]
    e = be_ref[i]
    e_prev = be_ref[jnp.maximum(i - 1, 0)]

    def fetch(expert):
        return (pltpu.make_async_copy(w1_hbm.at[expert], w1f, sems.at[0]),
                pltpu.make_async_copy(w2_hbm.at[expert], w2f, sems.at[1]))

    @pl.when(i == 0)
    def _():
        for cp in fetch(e):
            cp.start()

    @pl.when(jnp.logical_or(i == 0, e != e_prev))
    def _():
        for cp in fetch(e):
            cp.wait()
        w1b[...] = w1f[...].astype(BF16)
        w2b[...] = w2f[...].astype(BF16)

        @pl.when(nxt_ref[i] >= 0)
        def _():
            for cp in fetch(nxt_ref[i]):
                cp.start()

    @pl.when(i < nused_ref[0])
    def _():
        xb = jnp.concatenate([xs_ref[s] for s in range(ns)], axis=1).astype(BF16)
        hgl = jnp.dot(xb, w1b[...], preferred_element_type=F32) + b1_ref[0]
        x_glu = jnp.minimum(hgl[:, :f], SWIGLU_LIMIT)
        x_lin = jnp.clip(hgl[:, f:], -SWIGLU_LIMIT, SWIGLU_LIMIT)
        act = (x_lin + 1.0) * (x_glu * _sigmoid(SWIGLU_ALPHA * x_glu))
        y = jnp.dot(act.astype(BF16), w2b[...], preferred_element_type=F32) + b2_ref[0]
        for s in range(ns):
            ys_ref[s] = y[:, s * LANES:(s + 1) * LANES]

    @pl.when(i >= nused_ref[0])
    def _():
        ys_ref[...] = jnp.zeros_like(ys_ref)


def _moe_experts(block_e, next_e, n_used, xs, w1, b1, w2, b2):
    _, blk, ln = xs.shape
    ne, d, f2 = w1.shape
    f = w2.shape[1]
    nb = block_e.shape[0]
    ns = d // ln
    grid_spec = pltpu.PrefetchScalarGridSpec(
        num_scalar_prefetch=3,
        grid=(nb,),
        in_specs=[
            pl.BlockSpec((ns, blk, ln), lambda i, be, nx, nu: (i, 0, 0)),
            pl.BlockSpec(memory_space=pl.ANY),
            pl.BlockSpec((1, 1, f2), lambda i, be, nx, nu: (be[i], 0, 0)),
            pl.BlockSpec(memory_space=pl.ANY),
            pl.BlockSpec((1, 1, d), lambda i, be, nx, nu: (be[i], 0, 0)),
        ],
        out_specs=pl.BlockSpec((ns, blk, ln), lambda i, be, nx, nu: (i, 0, 0)),
        scratch_shapes=[pltpu.VMEM((d, f2), w1.dtype), pltpu.VMEM((f, d), w2.dtype),
                        pltpu.VMEM((d, f2), BF16), pltpu.VMEM((f, d), BF16),
                        pltpu.SemaphoreType.DMA((2,))],
    )
    return pl.pallas_call(
        _moe_kernel,
        grid_spec=grid_spec,
        out_shape=jax.ShapeDtypeStruct(xs.shape, F32),
        compiler_params=pltpu.CompilerParams(
            dimension_semantics=("arbitrary",), vmem_limit_bytes=VMEM_LIMIT),
        name="moe_experts",
    )(block_e, next_e, n_used, xs, w1, b1.reshape(ne, 1, f2), w2, b2.reshape(ne, 1, d))


def _final_kernel(row_ref, ys_hbm, h_ref, route_ref, p_ref, npl_ref, wg_ref, wp_ref, nfin_ref, out_ref,
                  gbuf, sems, *, last_layer):
    i = pl.program_id(0)
    tm = h_ref.shape[0]
    ns = gbuf.shape[2]
    cur = lax.rem(i, 2)

    def gather(step, buf):
        def issue(r, c):
            for kk in range(TOP_K):
                slot = row_ref[(step * tm + r) * TOP_K + kk]
                src = ys_hbm.at[pl.ds(_slot_block(slot) * ns, ns), pl.ds(_slot_row(slot), 1), :]
                pltpu.make_async_copy(src, gbuf.at[buf, kk, :, pl.ds(r, 1), :], sems.at[buf]).start()
            return c
        lax.fori_loop(0, tm, issue, 0)

    @pl.when(i == 0)
    def _():
        gather(0, 0)

    @pl.when(i + 1 < pl.num_programs(0))
    def _():
        gather(i + 1, 1 - cur)

    for kk in range(TOP_K):
        pltpu.make_async_copy(ys_hbm.at[pl.ds(0, ns), pl.ds(0, tm), :], gbuf.at[cur, kk], sems.at[cur]).wait()

    route = route_ref[...]
    h = h_ref[...]
    for kk in range(TOP_K):
        yk = jnp.concatenate([gbuf[cur, kk, s] for s in range(ns)], axis=1)
        h = h + route[:, kk:kk + 1] * yk
    u = _rmsnorm(h, npl_ref[...])
    gate = _sigmoid(_mm(u, wg_ref[...]))
    h = h + gate * _mm(p_ref[...], wp_ref[...])
    out_ref[...] = _rmsnorm(h, nfin_ref[...]) if last_layer else h


def _final(slots, ys, h1, route, pf, npl, wg, wp, nfin, tm, last_layer):
    n, d = h1.shape
    pd = pf.shape[1]
    grid_spec = pltpu.PrefetchScalarGridSpec(
        num_scalar_prefetch=1,
        grid=(n // tm,),
        in_specs=[
            pl.BlockSpec(memory_space=pl.ANY),
            pl.BlockSpec((tm, d), lambda i, s: (i, 0)),
            pl.BlockSpec((tm, LANES), lambda i, s: (i, 0)),
            pl.BlockSpec((tm, pd), lambda i, s: (i, 0)),
            pl.BlockSpec((1, d), lambda i, s: (0, 0)),
            pl.BlockSpec((d, d), lambda i, s: (0, 0)),
            pl.BlockSpec((pd, d), lambda i, s: (0, 0)),
            pl.BlockSpec((1, d), lambda i, s: (0, 0)),
        ],
        out_specs=pl.BlockSpec((tm, d), lambda i, s: (i, 0)),
        scratch_shapes=[pltpu.VMEM((2, TOP_K, d // ys.shape[2], tm, ys.shape[2]), F32),
                        pltpu.SemaphoreType.DMA((2,))],
    )
    return pl.pallas_call(
        functools.partial(_final_kernel, last_layer=last_layer),
        grid_spec=grid_spec,
        out_shape=jax.ShapeDtypeStruct((n, d), F32),
        compiler_params=pltpu.CompilerParams(
            dimension_semantics=("arbitrary",), vmem_limit_bytes=VMEM_LIMIT),
        name="final",
    )(slots, ys, h1, route, pf, npl, wg, wp, nfin)


def _routing(route, counts, n_experts):
    n = route.shape[0]
    blk = EXPERT_BLOCK
    nb = (n * TOP_K + n_experts * (blk - 1) + blk - 1) // blk
    counts = counts[0, :n_experts].astype(jnp.int32)
    pcounts = ((counts + blk - 1) // blk) * blk
    pend = jnp.cumsum(pcounts)
    pstart = pend - pcounts
    top_i = route[:, TOP_K:2 * TOP_K].astype(jnp.int32)
    rank = route[:, 2 * TOP_K:3 * TOP_K].astype(jnp.int32)
    onehot = top_i[:, :, None] == jnp.arange(n_experts, dtype=jnp.int32)[None, None, :]
    slots = jnp.sum(jnp.where(onehot, pstart[None, None, :], 0), axis=-1) + rank
    block_rows = jnp.arange(nb, dtype=jnp.int32) * blk
    block_e = jnp.sum(block_rows[:, None] >= pend[None, :], axis=1)
    block_e = jnp.clip(block_e, 0, n_experts - 1).astype(jnp.int32)
    n_used = (pend[-1] // blk).astype(jnp.int32).reshape(1)
    blocks = jnp.arange(nb, dtype=jnp.int32)
    onehot_b = block_e[:, None] == jnp.arange(n_experts, dtype=jnp.int32)[None, :]
    run_end = jnp.sum(jnp.where(onehot_b, pend[None, :], 0), axis=1) // blk
    run_end = jnp.where(blocks >= n_used[0], nb, run_end)
    follow = jnp.sum(jnp.where(blocks[None, :] == run_end[:, None], block_e[None, :], 0), axis=1)
    next_e = jnp.where((run_end < nb) & (follow != block_e), follow, -1).astype(jnp.int32)
    return block_e, next_e, n_used, slots.reshape(-1).astype(jnp.int32), nb * blk


def kernel(x, p, norm_mix, w_in, shift_mu, rw_w0, rw_w2, rw_a0, rw_a2, rw_g2, rw_kk, rw_ka, rw_rk,
           rw_gn_w, rw_gn_b, gla_gk2, gla_gk_b, gla_norm, w_out, norm_ffn, w_router, b_router,
           w1, b1, w2, b2, norm_ple, w_ple_gate, w_ple, norm_final):
    bsz, seq, d = x.shape
    n = bsz * seq
    depth = w_in.shape[0]
    n_experts = w_router.shape[-1]
    tm = 256
    h = x.reshape(n, d)
    for l in range(depth):
        w_rw = w_in[l][:, :RW_COLS].astype(BF16)
        w_gla = jnp.pad(w_in[l][:, RW_COLS:], ((0, 0), (0, LANES - GLA_GATE_RANK))).astype(BF16)
        rw_proj, gla_proj = _in_proj(h, norm_mix[l].reshape(1, d), w_rw, w_gla, tm)
        y_rw = _rwkv(rw_proj.reshape(bsz, seq, -1), shift_mu[l], rw_w0[l], rw_w2[l], rw_a0[l], rw_a2[l],
                     rw_g2[l], rw_kk[l], rw_ka[l], rw_rk[l], rw_gn_w[l], rw_gn_b[l])
        y_gla = _gla(gla_proj.reshape(bsz, seq, -1), gla_gk2[l], gla_gk_b[l], gla_norm[l])

        wr = jnp.pad(w_router[l], ((0, 0), (0, LANES - n_experts)))
        wr_hi = wr.astype(BF16)
        wr_lo = (wr - wr_hi.astype(F32)).astype(BF16)
        br = jnp.pad(b_router[l], (0, LANES - n_experts), constant_values=NEG_BIG).reshape(1, LANES)
        wo = w_out[l].astype(BF16)
        h1, u3, route, counts = _out_proj(y_rw.reshape(n, -1), y_gla.reshape(n, -1), h, wo[:RW_WIDTH],
                                          wo[RW_WIDTH:], norm_ffn[l].reshape(1, d), wr_hi, wr_lo, br, tm)

        ns = d // LANES
        block_e, next_e, n_used, slots, n_rows = _routing(route, counts, n_experts)
        xs = _dispatch(slots, u3, n_rows, tm, ns)
        ys = _moe_experts(block_e, next_e, n_used, xs, w1[l], b1[l], w2[l], b2[l])
        h = _final(slots, ys, h1, route, p[l].reshape(n, -1), norm_ple[l].reshape(1, d),
                   w_ple_gate[l].astype(BF16), w_ple[l].astype(BF16), norm_final.reshape(1, d), tm, l == depth - 1)
    return h.reshape(bsz, seq, d)
```

```python
import functools

import jax
import jax.numpy as jnp
from jax import lax
from jax.experimental import pallas as pl
from jax.experimental.pallas import tpu as pltpu

F32 = jnp.float32
BF16 = jnp.bfloat16

CHUNK = 64
RW_HEADS = 8
RW_HEAD_DIM = 64
RW_WIDTH = RW_HEADS * RW_HEAD_DIM
RW_DECAY_LORA = 64
RW_ICLR_LORA = 64
RW_GATE_LORA = 128
RW_COLS = 3 * RW_WIDTH + RW_DECAY_LORA + RW_ICLR_LORA + RW_GATE_LORA
RW_GN_EPS = 64e-5
GLA_HEADS = 4
GLA_KEY_DIM = 64
GLA_VAL_DIM = 128
GLA_QK_WIDTH = GLA_HEADS * GLA_KEY_DIM
GLA_V_WIDTH = GLA_HEADS * GLA_VAL_DIM
GLA_GATE_RANK = 16
GLA_GATE_TEMP = 16.0
GLA_SUB = 8
LANES = 128
GLA_COLS_PAD = 2 * GLA_QK_WIDTH + 2 * GLA_V_WIDTH + LANES
TOP_K = 4
EXPERT_BLOCK = 256
SWIGLU_ALPHA = 1.702
SWIGLU_LIMIT = 7.0
NORM_EPS = 1e-6
NEG_BIG = -1e30
VMEM_LIMIT = 56 * 1024 * 1024


def _mm(a, b):
    return jnp.dot(a.astype(BF16), b.astype(BF16), preferred_element_type=F32)


def _mm_nt(a, b):
    return lax.dot_general(a.astype(BF16), b.astype(BF16), (((1,), (1,)), ((), ())),
                           preferred_element_type=F32)


def _mm_tn(a, b):
    return lax.dot_general(a.astype(BF16), b.astype(BF16), (((0,), (0,)), ((), ())),
                           preferred_element_type=F32)


def _split(a, n):
    parts = []
    rem = a
    for _ in range(n):
        p = rem.astype(BF16)
        parts.append(p)
        rem = rem - p.astype(F32)
    return parts


def _mm_lhs_split(a, b_bf16, n):
    out = None
    for p in _split(a, n):
        t = jnp.dot(p, b_bf16, preferred_element_type=F32)
        out = t if out is None else out + t
    return out


def _mm_rhs_split(a_bf16, b, n):
    out = None
    for p in _split(b, n):
        t = jnp.dot(a_bf16, p, preferred_element_type=F32)
        out = t if out is None else out + t
    return out


def _rmsnorm(x, g):
    return x * lax.rsqrt(jnp.mean(x * x, axis=-1, keepdims=True) + NORM_EPS) * g


def _softplus(x):
    return jnp.maximum(x, 0.0) + jnp.log(1.0 + jnp.exp(-jnp.abs(x)))


def _sigmoid(x):
    return 1.0 / (1.0 + jnp.exp(-x))


def _log2(n):
    assert n & (n - 1) == 0
    return n.bit_length() - 1


def _head_keep(rows, head_lanes, dtype):
    lane_head = lax.shift_right_logical(lax.broadcasted_iota(jnp.int32, (rows, LANES), 1), _log2(head_lanes))
    return [jnp.where(lane_head == j, 1.0, 0.0).astype(dtype) for j in range(LANES // head_lanes)]


def _block_diag_fn(rows, n_heads, head_lanes):
    n_tiles = n_heads * head_lanes // LANES
    zero = jnp.zeros((rows, LANES), BF16)
    keep = _head_keep(rows, head_lanes, BF16) if head_lanes < LANES else None

    def f(y):
        yb = y.astype(BF16)
        blocks = []
        for hd in range(n_heads):
            tiles = [zero] * n_tiles
            if keep is not None:
                t = hd * head_lanes // LANES
                tiles[t] = yb[:, t * LANES:(t + 1) * LANES] * keep[hd % len(keep)]
            else:
                for t in range(hd * head_lanes // LANES, (hd + 1) * head_lanes // LANES):
                    tiles[t] = yb[:, t * LANES:(t + 1) * LANES]
            blocks.append(jnp.concatenate(tiles, axis=1))
        return jnp.concatenate(blocks, axis=0)
    return f


def _diag_blocks_fn(rows, n_heads, head_lanes):
    assert head_lanes < LANES
    keep = _head_keep(rows, head_lanes, F32)
    per = len(keep)

    def f(full):
        tiles = []
        for t in range(n_heads * head_lanes // LANES):
            acc = None
            for j in range(per):
                hd = t * per + j
                term = full[hd * rows:(hd + 1) * rows, t * LANES:(t + 1) * LANES] * keep[j]
                acc = term if acc is None else acc + term
            tiles.append(acc)
        return jnp.concatenate(tiles, axis=1)
    return f


def _tri(n, strict):
    r = lax.broadcasted_iota(jnp.int32, (n, n), 0)
    c = lax.broadcasted_iota(jnp.int32, (n, n), 1)
    return (r > c) if strict else (r >= c)


def _inproj_kernel(x_ref, g_ref, wr_ref, wg_ref, rw_ref, gla_ref):
    u = _rmsnorm(x_ref[...], g_ref[...]).astype(BF16)
    rw_ref[...] = jnp.dot(u, wr_ref[...], preferred_element_type=F32)
    gla_ref[...] = jnp.dot(u, wg_ref[...], preferred_element_type=F32)


def _in_proj(xf, g, w_rw, w_gla, tm):
    n, d = xf.shape
    return pl.pallas_call(
        _inproj_kernel,
        grid=(n // tm,),
        in_specs=[
            pl.BlockSpec((tm, d), lambda i: (i, 0)),
            pl.BlockSpec((1, d), lambda i: (0, 0)),
            pl.BlockSpec(w_rw.shape, lambda i: (0, 0)),
            pl.BlockSpec(w_gla.shape, lambda i: (0, 0)),
        ],
        out_specs=[
            pl.BlockSpec((tm, w_rw.shape[1]), lambda i: (i, 0)),
            pl.BlockSpec((tm, w_gla.shape[1]), lambda i: (i, 0)),
        ],
        out_shape=[
            jax.ShapeDtypeStruct((n, w_rw.shape[1]), F32),
            jax.ShapeDtypeStruct((n, w_gla.shape[1]), F32),
        ],
        compiler_params=pltpu.CompilerParams(
            dimension_semantics=("arbitrary",), vmem_limit_bytes=VMEM_LIMIT),
        name="in_proj",
    )(xf, g, w_rw, w_gla)


def _rwkv_kernel_unpacked(x_ref, mu_ref, w0_ref, w2_ref, a0_ref, a2_ref, g2_ref, kkw_ref, ka_ref,
                          rk_ref, gnw_ref, gnb_ref, bd_ref, y_ref, state_ref, carry_ref, o_ref):
    C, H, D = CHUNK, RW_HEADS, RW_HEAD_DIM

    @pl.when(pl.program_id(1) == 0)
    def _():
        state_ref[...] = jnp.zeros_like(state_ref)
        carry_ref[...] = jnp.zeros_like(carry_ref)

    x = x_ref[0]
    row = lax.broadcasted_iota(jnp.int32, x.shape, 0)
    prev = jnp.where(row == 0, carry_ref[...], pltpu.roll(x, 1, axis=0))
    carry_ref[...] = x[C - 1:C, :]
    h = x + (prev - x) * mu_ref[...]

    W = RW_WIDTH
    r = h[:, 0:W]
    k = h[:, W:2 * W]
    v = h[:, 2 * W:3 * W]
    o0 = 3 * W
    dw = h[:, o0:o0 + RW_DECAY_LORA]
    da = h[:, o0 + RW_DECAY_LORA:o0 + RW_DECAY_LORA + RW_ICLR_LORA]
    dg = h[:, o0 + RW_DECAY_LORA + RW_ICLR_LORA:]

    bd = bd_ref[...]

    def seg_sum(t):
        return _mm_lhs_split(t, bd, 2)

    w_log = -_softplus(-(w0_ref[...] + _mm(jnp.tanh(dw), w2_ref[...]))) - 0.5
    lw = -jnp.exp(w_log)
    iclr = _sigmoid(a0_ref[...] + _mm(da, a2_ref[...]))
    gate = _mm(_sigmoid(dg), g2_ref[...])

    kk = k * kkw_ref[...]
    kk = kk / jnp.maximum(jnp.sqrt(seg_sum(kk * kk)), 1e-12)
    k2 = k * (1.0 + (iclr - 1.0) * ka_ref[...])

    cw = _mm_rhs_split(_tri(C, False).astype(BF16), lw, 3)
    cw_last = cw[C - 1:C, :]
    e_cw = jnp.exp(cw)
    e_ncw = jnp.exp(-cw)
    e_rem = jnp.exp(cw_last - cw)
    kka = kk * iclr
    a_t = -kk * jnp.exp(cw - lw)
    r_t = r * e_cw
    b_t = kka * e_ncw
    k_t = k2 * e_ncw
    b_h = kka * e_rem
    k_h = k2 * e_rem
    w_c = jnp.exp(cw_last)

    strict = _tri(C, True)
    incl = _tri(C, False)
    eye = jnp.where(_tri(C, False) & ~strict, 1.0, 0.0).astype(F32)

    hs = range(H)
    sls = [slice(hd * D, (hd + 1) * D) for hd in hs]
    states = [state_ref[hd] for hd in hs]
    ah = [a_t[:, sl] for sl in sls]
    rh = [r_t[:, sl] for sl in sls]
    bh = [b_t[:, sl] for sl in sls]
    kh = [k_t[:, sl] for sl in sls]
    vh = [v[:, sl] for sl in sls]
    a_ab = [jnp.where(strict, _mm_nt(ah[i], bh[i]), 0.0) for i in hs]
    a_ak = [jnp.where(strict, _mm_nt(ah[i], kh[i]), 0.0) for i in hs]
    a_rb = [jnp.where(incl, _mm_nt(rh[i], bh[i]), 0.0) for i in hs]
    a_rk = [jnp.where(incl, _mm_nt(rh[i], kh[i]), 0.0) for i in hs]
    p = [_mm(a_ab[i], a_ab[i]) for i in hs]
    q = [eye + a_ab[i] for i in hs]
    for _ in range(4):
        pq = [_mm(p[i], q[i]) for i in hs]
        p = [_mm(p[i], p[i]) for i in hs]
        q = [q[i] + pq[i] for i in hs]
    pq = [_mm(p[i], q[i]) for i in hs]
    t_inv = [q[i] + pq[i] for i in hs]
    akv = [_mm(a_ak[i], vh[i]) for i in hs]
    u0 = [_mm(t_inv[i], akv[i]) for i in hs]
    a_hat = [_mm(t_inv[i], ah[i]) for i in hs]
    u = [_mm_nt(a_hat[i], states[i]) + u0[i] for i in hs]
    o_heads = [_mm_nt(rh[i], states[i]) + _mm(a_rb[i], u[i]) + _mm(a_rk[i], vh[i]) for i in hs]
    new_states = [states[i] * w_c[:, sls[i]] + _mm_tn(u[i], b_h[:, sls[i]]) + _mm_tn(vh[i], k_h[:, sls[i]])
                  for i in hs]
    for hd in hs:
        state_ref[hd] = new_states[hd]
        o_ref[:, sls[hd]] = o_heads[hd]

    o = o_ref[...]
    inv_d = 1.0 / D
    mean = seg_sum(o) * inv_d
    dlt = o - mean
    var = seg_sum(dlt * dlt) * inv_d
    o = dlt * lax.rsqrt(var + RW_GN_EPS) * gnw_ref[...] + gnb_ref[...]
    bonus = seg_sum(r * k2 * rk_ref[...]) * v
    y_ref[0] = ((o + bonus) * gate).astype(y_ref.dtype)


def _rwkv_unpacked(rw_proj, mu, w0, w2, a0, a2, g2, kkw, ka, rk, gnw, gnb):
    b, s, cols = rw_proj.shape
    W = RW_WIDTH
    bd = jnp.kron(jnp.eye(RW_HEADS, dtype=F32), jnp.ones((RW_HEAD_DIM, RW_HEAD_DIM), F32)).astype(BF16)
    row = lambda a: a.reshape(1, -1)
    full = lambda a: pl.BlockSpec(a.shape, lambda i, j: (0,) * a.ndim)
    args = [row(mu), row(w0), w2.astype(BF16), row(a0), a2.astype(BF16), g2.astype(BF16),
            row(kkw), row(ka), row(rk), row(gnw), row(gnb), bd]
    return pl.pallas_call(
        _rwkv_kernel_unpacked,
        grid=(b, s // CHUNK),
        in_specs=[pl.BlockSpec((1, CHUNK, cols), lambda i, j: (i, j, 0))] + [full(a) for a in args],
        out_specs=pl.BlockSpec((1, CHUNK, W), lambda i, j: (i, j, 0)),
        out_shape=jax.ShapeDtypeStruct((b, s, W), BF16),
        scratch_shapes=[
            pltpu.VMEM((RW_HEADS, RW_HEAD_DIM, RW_HEAD_DIM), F32),
            pltpu.VMEM((1, cols), F32),
            pltpu.VMEM((CHUNK, W), F32),
        ],
        compiler_params=pltpu.CompilerParams(
            dimension_semantics=("arbitrary", "arbitrary"), vmem_limit_bytes=VMEM_LIMIT),
        name="rwkv7",
    )(rw_proj, *args)


RW_GROUP = 4
RW_GW = RW_GROUP * RW_HEAD_DIM
RW_STEP_CHUNKS = 2


def _rwkv_kernel(x_ref, mu_ref, w0_ref, w2_ref, a0_ref, a2_ref, g2_ref, kkw_ref, ka_ref,
                 rk_ref, gnw_ref, gnb_ref, bd_ref, y_ref, state_ref, carry_ref):
    C, D, W, GW = CHUNK, RW_HEAD_DIM, RW_WIDTH, RW_GW
    nb, ct = x_ref.shape[0], x_ref.shape[1]
    nch = ct // C
    R = nb * ct

    @pl.when(pl.program_id(0) == 0)
    def _():
        state_ref[...] = jnp.zeros_like(state_ref)
        carry_ref[...] = jnp.zeros_like(carry_ref)

    x = x_ref[...].reshape(R, x_ref.shape[2])
    row = lax.broadcasted_iota(jnp.int32, x.shape, 0)
    prev = pltpu.roll(x, 1, axis=0)
    for b in range(nb):
        prev = jnp.where(row == b * ct, carry_ref[b:b + 1, :], prev)
        carry_ref[b:b + 1, :] = x[(b + 1) * ct - 1:(b + 1) * ct, :]
    h = x + (prev - x) * mu_ref[...]

    r = h[:, 0:W]
    k = h[:, W:2 * W]
    v = h[:, 2 * W:3 * W]
    o0 = 3 * W
    dw = h[:, o0:o0 + RW_DECAY_LORA]
    da = h[:, o0 + RW_DECAY_LORA:o0 + RW_DECAY_LORA + RW_ICLR_LORA]
    dg = h[:, o0 + RW_DECAY_LORA + RW_ICLR_LORA:]

    bd_g = bd_ref[...]

    def seg_sum(t):
        return jnp.concatenate(
            [_mm_lhs_split(t[:, g * GW:(g + 1) * GW], bd_g, 2) for g in range(W // GW)], axis=1)

    w_log = -_softplus(-(w0_ref[...] + _mm(jnp.tanh(dw), w2_ref[...]))) - 0.5
    lw = -jnp.exp(w_log)
    iclr = _sigmoid(a0_ref[...] + _mm(da, a2_ref[...]))
    gate = _mm(_sigmoid(dg), g2_ref[...])

    kk = k * kkw_ref[...]
    kk = kk * jnp.minimum(lax.rsqrt(seg_sum(kk * kk)), 1e12)
    k2 = k * (1.0 + (iclr - 1.0) * ka_ref[...])

    rr = lax.broadcasted_iota(jnp.int32, (R, R), 0)
    cc = lax.broadcasted_iota(jnp.int32, (R, R), 1)
    tri_seq = jnp.where((rr >= cc) & (rr // C == cc // C), 1.0, 0.0).astype(BF16)
    cw = _mm_rhs_split(tri_seq, lw, 3)
    cw_last = jnp.concatenate(
        [jnp.broadcast_to(cw[(j + 1) * C - 1:(j + 1) * C, :], (C, W)) for j in range(R // C)], axis=0)
    e_cw = jnp.exp(cw)
    e_ncw = jnp.exp(-cw)
    e_rem = jnp.exp(cw_last - cw)
    kka = kk * iclr
    a_t = -kk * jnp.exp(cw - lw)
    r_t = r * e_cw
    b_t = kka * e_ncw
    k_t = k2 * e_ncw
    b_h = kka * e_rem
    k_h = k2 * e_rem
    w_c = jnp.exp(cw_last)

    ti = lax.broadcasted_iota(jnp.int32, (C, GW), 0)
    si = jnp.bitwise_and(lax.broadcasted_iota(jnp.int32, (C, GW), 1), D - 1)
    strict = ti > si
    incl = ti >= si
    eye = jnp.where(ti == si, 1.0, 0.0)

    bdiag = _block_diag_fn(C, RW_GROUP, D)
    diag_blocks = _diag_blocks_fn(D, RW_GROUP, D)

    ng = W // GW
    chains = [(b, g, ch) for ch in range(nch) for b in range(nb) for g in range(ng)]

    def part(t, c):
        b, g, ch = c
        r0 = b * ct + ch * C
        return t[r0:r0 + C, g * GW:(g + 1) * GW]

    n = range(len(chains))
    a4 = [part(a_t, c) for c in chains]
    r4 = [part(r_t, c) for c in chains]
    v4 = [part(v, c) for c in chains]
    ar = [jnp.concatenate([a4[i], r4[i]], axis=0) for i in n]
    bd_b = [bdiag(part(b_t, c)) for c in chains]
    bd_k = [bdiag(part(k_t, c)) for c in chains]
    bd_v = [bdiag(v4[i]) for i in n]
    m_b = [_mm_nt(ar[i], bd_b[i]) for i in n]
    m_k = [_mm_nt(ar[i], bd_k[i]) for i in n]
    a_ab = [jnp.where(strict, m_b[i][0:C], 0.0) for i in n]
    a_rb = [jnp.where(incl, m_b[i][C:2 * C], 0.0) for i in n]
    a_ak = [jnp.where(strict, m_k[i][0:C], 0.0) for i in n]
    a_rk = [jnp.where(incl, m_k[i][C:2 * C], 0.0) for i in n]
    akv = [_mm(a_ak[i], bd_v[i]) for i in n]
    o_kv = [_mm(a_rk[i], bd_v[i]) for i in n]
    p = [_mm(a_ab[i], bdiag(a_ab[i])) for i in n]
    q = [eye + a_ab[i] for i in n]
    for _ in range(4):
        pq = [_mm(p[i], bdiag(q[i])) for i in n]
        p = [_mm(p[i], bdiag(p[i])) for i in n]
        q = [q[i] + pq[i] for i in n]
    pq = [_mm(p[i], bdiag(q[i])) for i in n]
    t_inv = [q[i] + pq[i] for i in n]
    u0 = [_mm(t_inv[i], bdiag(akv[i])) for i in n]
    a_hat = [_mm(t_inv[i], bdiag(a4[i])) for i in n]
    seqs = [(b, g) for b in range(nb) for g in range(ng)]
    s4 = [state_ref[b, g] for b, g in seqs]
    o_g = {}
    for ch in range(nch):
        idx = [chains.index((b, g, ch)) for b, g in seqs]
        bd_s = [bdiag(s) for s in s4]
        o_s = [_mm_nt(r4[i], bd_s[j]) for j, i in enumerate(idx)]
        u = [_mm_nt(a_hat[i], bd_s[j]) + u0[i] for j, i in enumerate(idx)]
        for j, i in enumerate(idx):
            o_g[chains[i]] = o_s[j] + _mm(a_rb[i], bdiag(u[j])) + o_kv[i]
        nxt = []
        for j, i in enumerate(idx):
            c = chains[i]
            uv = jnp.concatenate([u[j], v4[i]], axis=0)
            bk = jnp.concatenate([part(b_h, c), part(k_h, c)], axis=0)
            nxt.append(s4[j] * part(w_c, c)[0:D] + diag_blocks(_mm_tn(uv, bk)))
        s4 = nxt
    for j, (b, g) in enumerate(seqs):
        state_ref[b, g] = s4[j]

    o = jnp.concatenate(
        [jnp.concatenate([o_g[(b, g, ch)] for g in range(ng)], axis=1) for b in range(nb) for ch in range(nch)],
        axis=0)
    inv_d = 1.0 / D
    mean = seg_sum(o) * inv_d
    dlt = o - mean
    var = seg_sum(dlt * dlt) * inv_d
    o = dlt * lax.rsqrt(var + RW_GN_EPS) * gnw_ref[...] + gnb_ref[...]
    bonus = seg_sum(r * k2 * rk_ref[...]) * v
    y_ref[...] = ((o + bonus) * gate).astype(y_ref.dtype).reshape(y_ref.shape)


def _rwkv(rw_proj, mu, w0, w2, a0, a2, g2, kkw, ka, rk, gnw, gnb):
    b, s, cols = rw_proj.shape
    W = RW_WIDTH
    bd = jnp.kron(jnp.eye(RW_GROUP, dtype=F32), jnp.ones((RW_HEAD_DIM, RW_HEAD_DIM), F32)).astype(BF16)
    row = lambda a: a.reshape(1, -1)
    full = lambda a: pl.BlockSpec(a.shape, lambda j: (0,) * a.ndim)
    args = [row(mu), row(w0), w2.astype(BF16), row(a0), a2.astype(BF16), g2.astype(BF16),
            row(kkw), row(ka), row(rk), row(gnw), row(gnb), bd]
    ct = CHUNK * RW_STEP_CHUNKS
    return pl.pallas_call(
        _rwkv_kernel,
        grid=(s // ct,),
        in_specs=[pl.BlockSpec((b, ct, cols), lambda j: (0, j, 0))] + [full(a) for a in args],
        out_specs=pl.BlockSpec((b, ct, W), lambda j: (0, j, 0)),
        out_shape=jax.ShapeDtypeStruct((b, s, W), BF16),
        scratch_shapes=[
            pltpu.VMEM((b, W // RW_GW, RW_HEAD_DIM, RW_GW), F32),
            pltpu.VMEM((b, cols), F32),
        ],
        compiler_params=pltpu.CompilerParams(
            dimension_semantics=("arbitrary",), vmem_limit_bytes=VMEM_LIMIT),
        name="rwkv7",
    )(rw_proj, *args)


GLA_STEP_CHUNKS = 2


def _group_mask(rows, row_group, cols, col_group):
    r = lax.shift_right_logical(lax.broadcasted_iota(jnp.int32, (rows, cols), 0), _log2(row_group))
    c = lax.shift_right_logical(lax.broadcasted_iota(jnp.int32, (rows, cols), 1), _log2(col_group))
    return jnp.where(r == c, 1.0, 0.0).astype(BF16)


def _gla_kernel(x_ref, gk2_ref, gkb_ref, ng_ref, y_ref, state_ref):
    C, H, DK, DV, SB = CHUNK, GLA_HEADS, GLA_KEY_DIM, GLA_VAL_DIM, GLA_SUB
    QW, VW = GLA_QK_WIDTH, GLA_V_WIDTH
    nb, ct = x_ref.shape[0], x_ref.shape[1]
    nch = ct // C
    R = nb * ct

    @pl.when(pl.program_id(0) == 0)
    def _():
        state_ref[...] = jnp.zeros_like(state_ref)

    x = x_ref[...].reshape(R, x_ref.shape[2])
    q = x[:, 0:QW] * (DK ** -0.5)
    k = x[:, QW:2 * QW]
    v = x[:, 2 * QW:2 * QW + VW]
    g = x[:, 2 * QW + VW:2 * QW + 2 * VW]
    dgk = x[:, 2 * QW + 2 * VW:]

    la = -_softplus(-(_mm(dgk, gk2_ref[...]) + gkb_ref[...])) * (1.0 / GLA_GATE_TEMP)
    rr = lax.broadcasted_iota(jnp.int32, (R, R), 0)
    cc = lax.broadcasted_iota(jnp.int32, (R, R), 1)
    tri_seq = jnp.where((rr >= cc) & (rr // C == cc // C), 1.0, 0.0).astype(BF16)
    b = _mm_rhs_split(tri_seq, la, 3)
    b_last = jnp.concatenate(
        [jnp.broadcast_to(b[(j + 1) * C - 1:(j + 1) * C, :], (C, QW)) for j in range(R // C)], axis=0)
    q_e = q * jnp.exp(b)
    k_e = k * jnp.exp(b_last - b)
    w_c = jnp.exp(b_last)

    def tile_rows(y, n):
        return jnp.concatenate([y.astype(BF16)] * n, axis=0)

    units = [(bi, ch) for ch in range(nch) for bi in range(nb)]
    row0 = {u: u[0] * ct + u[1] * C for u in units}

    o_off = {u: None for u in units}
    for s in (C // 2, C // 4, C // 8):
        bd_keys = _block_diag_fn(s, H, DK)
        bd_vals = _block_diag_fn(s, H, DV)
        jobs = [(u, row0[u] + m * 2 * s) for u in units for m in range(C // (2 * s))]
        att = []
        for u, c0 in jobs:
            ref = b[c0 + s - 1:c0 + s]
            q_s = q[c0 + s:c0 + 2 * s] * jnp.exp(b[c0 + s:c0 + 2 * s] - ref)
            k_s = k[c0:c0 + s] * jnp.exp(ref - b[c0:c0 + s])
            att.append(_mm_nt(q_s, bd_keys(k_s)))
        outs = [_mm(att[j], bd_vals(v[c0:c0 + s])) for j, (u, c0) in enumerate(jobs)]
        zero = jnp.zeros((s, VW), F32)
        for u in units:
            pieces = []
            for j, (uj, c0) in enumerate(jobs):
                if uj == u:
                    pieces += [zero, outs[j]]
            level = jnp.concatenate(pieces, axis=0)
            o_off[u] = level if o_off[u] is None else o_off[u] + level

    nblk = C // SB
    ii = lax.broadcasted_iota(jnp.int32, (nblk, SB, SB, 2 * DK), 1)
    jj = lax.broadcasted_iota(jnp.int32, (nblk, SB, SB, 2 * DK), 2)
    causal4 = jj <= ii
    sel = _group_mask(C, 1, C * SB, SB)
    pair_ones = _group_mask(2 * DK, DK, 2 * DV, DV)
    att2 = {}
    for u in units:
        r0 = row0[u]
        for pr in range(H // 2):
            sl2 = slice(2 * pr * DK, 2 * (pr + 1) * DK)
            q2, k2, b2 = q[r0:r0 + C, sl2], k[r0:r0 + C, sl2], b[r0:r0 + C, sl2]
            dec = jnp.exp(jnp.where(
                causal4, b2.reshape(nblk, SB, 1, 2 * DK) - b2.reshape(nblk, 1, SB, 2 * DK), NEG_BIG))
            pw = q2.reshape(nblk, SB, 1, 2 * DK) * k2.reshape(nblk, 1, SB, 2 * DK) * dec
            att2[(u, pr)] = _mm(pw.reshape(C * SB, 2 * DK), pair_ones)
    o_diag = {}
    for u in units:
        r0 = row0[u]
        heads = []
        for hd in range(H):
            v_h = v[r0:r0 + C, hd * DV:(hd + 1) * DV]
            v_rep = jnp.broadcast_to(v_h.reshape(nblk, 1, SB, DV), (nblk, SB, SB, DV)).reshape(C * SB, DV)
            heads.append(_mm(sel, att2[(u, hd // 2)][:, (hd % 2) * DV:(hd % 2 + 1) * DV] * v_rep))
        o_diag[u] = jnp.concatenate(heads, axis=1)

    bd_state = _block_diag_fn(DV, H, DK)
    diag_state = _diag_blocks_fn(DV, H, DK)
    s4 = [state_ref[bi] for bi in range(nb)]
    o_int = {}
    for ch in range(nch):
        for bi in range(nb):
            r0 = row0[(bi, ch)]
            o_int[(bi, ch)] = _mm_nt(q_e[r0:r0 + C], bd_state(s4[bi]))
        nxt = []
        for bi in range(nb):
            r0 = row0[(bi, ch)]
            nxt.append(s4[bi] * w_c[r0:r0 + 1] + diag_state(_mm_tn(v[r0:r0 + C], k_e[r0:r0 + C])))
        s4 = nxt
    for bi in range(nb):
        state_ref[bi] = s4[bi]

    rows = []
    for bi in range(nb):
        for ch in range(nch):
            u = (bi, ch)
            o = o_int[u] + o_diag[u] + o_off[u]
            heads = []
            for hd in range(H):
                oh = o[:, hd * DV:(hd + 1) * DV]
                heads.append(oh * lax.rsqrt(jnp.mean(oh * oh, axis=-1, keepdims=True) + NORM_EPS))
            rows.append(jnp.concatenate(heads, axis=1))
    o = jnp.concatenate(rows, axis=0)
    y = o * ng_ref[...] * (g * _sigmoid(g))
    y_ref[...] = y.astype(y_ref.dtype).reshape(y_ref.shape)


def _gla(gla_proj, gk2, gkb, ng):
    b, s, cols = gla_proj.shape
    gk2p = jnp.zeros((LANES, GLA_QK_WIDTH), F32).at[:GLA_GATE_RANK].set(gk2).astype(BF16)
    args = [gk2p, gkb.reshape(1, -1), ng.reshape(1, -1)]
    full = lambda a: pl.BlockSpec(a.shape, lambda j: (0,) * a.ndim)
    ct = CHUNK * GLA_STEP_CHUNKS
    return pl.pallas_call(
        _gla_kernel,
        grid=(s // ct,),
        in_specs=[pl.BlockSpec((b, ct, cols), lambda j: (0, j, 0))] + [full(a) for a in args],
        out_specs=pl.BlockSpec((b, ct, GLA_V_WIDTH), lambda j: (0, j, 0)),
        out_shape=jax.ShapeDtypeStruct((b, s, GLA_V_WIDTH), BF16),
        scratch_shapes=[pltpu.VMEM((b, GLA_VAL_DIM, GLA_QK_WIDTH), F32)],
        compiler_params=pltpu.CompilerParams(
            dimension_semantics=("arbitrary",), vmem_limit_bytes=VMEM_LIMIT),
        name="gla",
    )(gla_proj, *args)


def _gla_kernel_unpacked(x_ref, gk2_ref, gkb_ref, ng_ref, y_ref, state_ref):
    C, H, DK, DV, SB = CHUNK, GLA_HEADS, GLA_KEY_DIM, GLA_VAL_DIM, GLA_SUB

    @pl.when(pl.program_id(1) == 0)
    def _():
        state_ref[...] = jnp.zeros_like(state_ref)

    x = x_ref[0]
    QW, VW = GLA_QK_WIDTH, GLA_V_WIDTH
    q = x[:, 0:QW] * (DK ** -0.5)
    k = x[:, QW:2 * QW]
    v = x[:, 2 * QW:2 * QW + VW]
    g = x[:, 2 * QW + VW:2 * QW + 2 * VW]
    dgk = x[:, 2 * QW + 2 * VW:]

    la = -_softplus(-(_mm(dgk, gk2_ref[...]) + gkb_ref[...])) * (1.0 / GLA_GATE_TEMP)
    b = _mm_rhs_split(_tri(C, False).astype(BF16), la, 3)
    b_last = b[C - 1:C, :]
    q_e = q * jnp.exp(b)
    k_e = k * jnp.exp(b_last - b)
    w_c = jnp.exp(b_last)

    ii = lax.broadcasted_iota(jnp.int32, (C // SB, SB, SB, 2 * DK), 1)
    jj = lax.broadcasted_iota(jnp.int32, (C // SB, SB, SB, 2 * DK), 2)
    causal4 = jj <= ii

    hs = range(H)
    nblk = C // SB
    sks = [slice(hd * DK, (hd + 1) * DK) for hd in hs]
    svs = [slice(hd * DV, (hd + 1) * DV) for hd in hs]
    states = [state_ref[hd] for hd in hs]
    qh = [q[:, sk] for sk in sks]
    kh = [k[:, sk] for sk in sks]
    bh = [b[:, sk] for sk in sks]
    vh = [v[:, sv] for sv in svs]
    o_inter = [_mm_nt(q_e[:, sks[i]], states[i]) for i in hs]
    new_states = [states[i] * w_c[:, sks[i]] + _mm_tn(vh[i], k_e[:, sks[i]]) for i in hs]
    att_off = {}
    for blk in range(1, nblk):
        r0 = blk * SB
        for i in hs:
            ref = bh[i][r0 - 1:r0]
            q_s = qh[i][r0:r0 + SB] * jnp.exp(bh[i][r0:r0 + SB] - ref)
            k_s = kh[i][0:r0] * jnp.exp(ref - bh[i][0:r0])
            att_off[(i, blk)] = _mm_nt(q_s, k_s)
    o_off = {key: _mm(att, vh[key[0]][0:key[1] * SB]) for key, att in att_off.items()}
    rsel = lax.broadcasted_iota(jnp.int32, (C, C * SB), 0)
    csel = lax.broadcasted_iota(jnp.int32, (C, C * SB), 1)
    sel = jnp.where((csel >= rsel * SB) & (csel < (rsel + 1) * SB), 1.0, 0.0).astype(BF16)
    drow = lax.broadcasted_iota(jnp.int32, (2 * DK, 2 * DV), 0)
    dcol = lax.broadcasted_iota(jnp.int32, (2 * DK, 2 * DV), 1)
    pair_ones = jnp.where((drow >= DK) == (dcol >= DV), 1.0, 0.0).astype(BF16)
    o_diag = []
    for pr in range(H // 2):
        sl2 = slice(2 * pr * DK, 2 * (pr + 1) * DK)
        q2, k2, b2 = q[:, sl2], k[:, sl2], b[:, sl2]
        dec = jnp.exp(jnp.where(causal4, b2.reshape(nblk, SB, 1, 2 * DK) - b2.reshape(nblk, 1, SB, 2 * DK),
                                NEG_BIG))
        pw = q2.reshape(nblk, SB, 1, 2 * DK) * k2.reshape(nblk, 1, SB, 2 * DK) * dec
        att2 = _mm(pw.reshape(C * SB, 2 * DK), pair_ones)
        for t in range(2):
            v_rep = jnp.broadcast_to(vh[2 * pr + t].reshape(nblk, 1, SB, DV), (nblk, SB, SB, DV))
            z = att2[:, t * DV:(t + 1) * DV] * v_rep.reshape(C * SB, DV)
            o_diag.append(_mm(sel, z))
    outs = []
    for i in hs:
        rows = [jnp.zeros((SB, DV), F32)] + [o_off[(i, blk)] for blk in range(1, nblk)]
        o = o_inter[i] + o_diag[i] + jnp.concatenate(rows, axis=0)
        outs.append(o * lax.rsqrt(jnp.mean(o * o, axis=-1, keepdims=True) + NORM_EPS))
    for hd in hs:
        state_ref[hd] = new_states[hd]
    o = jnp.concatenate(outs, axis=1)
    y = o * ng_ref[...] * (g * _sigmoid(g))
    y_ref[0] = y.astype(y_ref.dtype)


def _gla_unpacked(gla_proj, gk2, gkb, ng):
    b, s, cols = gla_proj.shape
    gk2p = jnp.zeros((LANES, GLA_QK_WIDTH), F32).at[:GLA_GATE_RANK].set(gk2).astype(BF16)
    args = [gk2p, gkb.reshape(1, -1), ng.reshape(1, -1)]
    full = lambda a: pl.BlockSpec(a.shape, lambda i, j: (0,) * a.ndim)
    return pl.pallas_call(
        _gla_kernel_unpacked,
        grid=(b, s // CHUNK),
        in_specs=[pl.BlockSpec((1, CHUNK, cols), lambda i, j: (i, j, 0))] + [full(a) for a in args],
        out_specs=pl.BlockSpec((1, CHUNK, GLA_V_WIDTH), lambda i, j: (i, j, 0)),
        out_shape=jax.ShapeDtypeStruct((b, s, GLA_V_WIDTH), BF16),
        scratch_shapes=[pltpu.VMEM((GLA_HEADS, GLA_VAL_DIM, GLA_KEY_DIM), F32)],
        compiler_params=pltpu.CompilerParams(
            dimension_semantics=("arbitrary", "arbitrary"), vmem_limit_bytes=VMEM_LIMIT),
        name="gla",
    )(gla_proj, *args)


def _outproj_kernel(yr_ref, yg_ref, x_ref, wor_ref, wog_ref, nf_ref, wrh_ref, wrl_ref, br_ref,
                    h_ref, u3_ref, route_ref, cnt_ref, carry_ref):
    tm, d = x_ref.shape

    @pl.when(pl.program_id(0) == 0)
    def _():
        carry_ref[...] = jnp.zeros_like(carry_ref)

    h = (x_ref[...] + jnp.dot(yr_ref[...], wor_ref[...], preferred_element_type=F32)
         + jnp.dot(yg_ref[...], wog_ref[...], preferred_element_type=F32))
    h_ref[...] = h
    u = _rmsnorm(h, nf_ref[...])
    for s in range(d // LANES):
        u3_ref[s] = u[:, s * LANES:(s + 1) * LANES]
    u_hi, u_lo = _split(u, 2)
    logits = (jnp.dot(u_hi, wrh_ref[...], preferred_element_type=F32)
              + jnp.dot(u_hi, wrl_ref[...], preferred_element_type=F32)
              + jnp.dot(u_lo, wrh_ref[...], preferred_element_type=F32)) + br_ref[...]
    lane = lax.broadcasted_iota(jnp.int32, logits.shape, 1)
    rest = logits
    picks, idxs, vals = [], [], []
    for r in range(TOP_K):
        m = jnp.max(rest, axis=-1, keepdims=True)
        idx = jnp.min(jnp.where(rest == m, lane, LANES), axis=-1, keepdims=True)
        pick = lane == idx
        picks.append(pick)
        idxs.append(idx)
        vals.append(m)
        rest = jnp.where(pick, -jnp.inf, rest)
    denom = jnp.ones_like(vals[0])
    for r in range(1, TOP_K):
        denom = denom + jnp.exp(vals[r] - vals[0])
    sel = jnp.zeros(logits.shape, F32)
    for pick in picks:
        sel = sel + jnp.where(pick, 1.0, 0.0)
    prefix = _mm(jnp.where(_tri(tm, True), 1.0, 0.0), sel) + carry_ref[...]
    carry_ref[...] = carry_ref[...] + jnp.sum(sel, axis=0, keepdims=True)
    cnt_ref[...] = carry_ref[...]
    route = jnp.zeros(logits.shape, F32)
    for r in range(TOP_K):
        gate = jnp.exp(vals[r] - vals[0]) / denom
        rank = jnp.sum(jnp.where(picks[r], prefix, 0.0), axis=-1, keepdims=True)
        route = jnp.where(lane == r, gate, route)
        route = jnp.where(lane == TOP_K + r, idxs[r].astype(F32), route)
        route = jnp.where(lane == 2 * TOP_K + r, rank, route)
    route_ref[...] = route


def _out_proj(y_rw, y_gla, xf, wo_r, wo_g, nf, wr_hi, wr_lo, br, tm):
    n, d = xf.shape
    full = lambda a: pl.BlockSpec(a.shape, lambda i: (0,) * a.ndim)
    tile = lambda w: pl.BlockSpec((tm, w), lambda i: (i, 0))
    return pl.pallas_call(
        _outproj_kernel,
        grid=(n // tm,),
        in_specs=[tile(y_rw.shape[1]), tile(y_gla.shape[1]), tile(d), full(wo_r), full(wo_g), full(nf),
                  full(wr_hi), full(wr_lo), full(br)],
        out_specs=[tile(d), pl.BlockSpec((d // LANES, tm, LANES), lambda i: (i, 0, 0)), tile(LANES),
                   pl.BlockSpec((1, LANES), lambda i: (0, 0))],
        out_shape=[jax.ShapeDtypeStruct((n, d), F32), jax.ShapeDtypeStruct((n // tm * (d // LANES), tm, LANES), F32),
                   jax.ShapeDtypeStruct((n, LANES), F32), jax.ShapeDtypeStruct((1, LANES), F32)],
        scratch_shapes=[pltpu.VMEM((1, LANES), F32)],
        compiler_params=pltpu.CompilerParams(
            dimension_semantics=("arbitrary",), vmem_limit_bytes=VMEM_LIMIT),
        name="out_proj",
    )(y_rw, y_gla, xf, wo_r, wo_g, nf, wr_hi, wr_lo, br)


assert EXPERT_BLOCK & (EXPERT_BLOCK - 1) == 0
_BLOCK_SHIFT = EXPERT_BLOCK.bit_length() - 1


def _slot_block(slot):
    return lax.shift_right_logical(slot, _BLOCK_SHIFT)


def _slot_row(slot):
    return jnp.bitwise_and(slot, EXPERT_BLOCK - 1)


_DISPATCH_BUFS = 3


def _dispatch_kernel(row_ref, pad_ref, u3_hbm, xs_hbm, ubuf, zbuf, in_sems, out_sems, zsem, *, tm, ns):
    i = pl.program_id(0)
    nsteps = pl.num_programs(0)
    nbuf = _DISPATCH_BUFS
    n_blocks = pad_ref.shape[0]

    @pl.when(i == 0)
    def _():
        zbuf[...] = jnp.zeros_like(zbuf)

        def zero_block(blk_i):
            return pltpu.make_async_copy(zbuf, xs_hbm.at[pl.ds(blk_i * ns, ns)], zsem)

        def start(blk_i, c):
            @pl.when(pad_ref[blk_i] != 0)
            def _():
                zero_block(blk_i).start()
            return c
        lax.fori_loop(0, n_blocks, start, 0)

        def finish(blk_i, c):
            @pl.when(pad_ref[blk_i] != 0)
            def _():
                zero_block(blk_i).wait()
            return c
        lax.fori_loop(0, n_blocks, finish, 0)

    def load(step, slot_):
        return pltpu.make_async_copy(u3_hbm.at[pl.ds(step * ns, ns)], ubuf.at[slot_], in_sems.at[slot_])

    def wait_scatter(slot_):
        for _ in range(TOP_K):
            pltpu.make_async_copy(u3_hbm.at[pl.ds(0, ns)], ubuf.at[slot_], out_sems.at[slot_]).wait()

    cur = lax.rem(i, nbuf)
    nxt = lax.rem(i + 1, nbuf)

    @pl.when(i == 0)
    def _():
        load(0, 0).start()

    @pl.when(i >= nbuf - 1)
    def _():
        wait_scatter(nxt)

    @pl.when(i + 1 < nsteps)
    def _():
        load(i + 1, nxt).start()

    load(i, cur).wait()

    def issue(r, c):
        src = ubuf.at[cur, :, pl.ds(r, 1), :]
        for kk in range(TOP_K):
            slot = row_ref[(i * tm + r) * TOP_K + kk]
            dst = xs_hbm.at[pl.ds(_slot_block(slot) * ns, ns), pl.ds(_slot_row(slot), 1), :]
            pltpu.make_async_copy(src, dst, out_sems.at[cur]).start()
        return c
    lax.fori_loop(0, tm, issue, 0, unroll=4)

    @pl.when(i == nsteps - 1)
    def _():
        for back in range(nbuf - 1):
            @pl.when(i - back >= 0)
            def _():
                wait_scatter(lax.rem(i - back + nbuf, nbuf))


def _dispatch(slots, pad_blocks, u3, tm, ns):
    ln = u3.shape[2]
    n = u3.shape[0] // ns * tm
    blk = EXPERT_BLOCK
    grid_spec = pltpu.PrefetchScalarGridSpec(
        num_scalar_prefetch=2,
        grid=(n // tm,),
        in_specs=[pl.BlockSpec(memory_space=pl.ANY)],
        out_specs=pl.BlockSpec(memory_space=pl.ANY),
        scratch_shapes=[pltpu.VMEM((_DISPATCH_BUFS, ns, tm, ln), u3.dtype),
                        pltpu.VMEM((ns, blk, ln), u3.dtype),
                        pltpu.SemaphoreType.DMA((_DISPATCH_BUFS,)),
                        pltpu.SemaphoreType.DMA((_DISPATCH_BUFS,)),
                        pltpu.SemaphoreType.DMA(())],
    )
    return pl.pallas_call(
        functools.partial(_dispatch_kernel, tm=tm, ns=ns),
        grid_spec=grid_spec,
        out_shape=jax.ShapeDtypeStruct((pad_blocks.shape[0] * ns, blk, ln), u3.dtype),
        compiler_params=pltpu.CompilerParams(dimension_semantics=("arbitrary",), has_side_effects=True),
        name="moe_dispatch",
    )(slots, pad_blocks, u3)


def _moe_kernel(be_ref, nxt_ref, nused_ref, xs_ref, w1_hbm, b1_ref, w2_hbm, b2_ref, ys_ref,
                w1f, w2f, w1b, w2b, sems):
    i = pl.program_id(0)
    f = w2b.shape[0]
    ns = xs_ref.shape[0]
    e = be_ref[i]
    e_prev = be_ref[jnp.maximum(i - 1, 0)]

    def fetch(expert):
        return (pltpu.make_async_copy(w1_hbm.at[expert], w1f, sems.at[0]),
                pltpu.make_async_copy(w2_hbm.at[expert], w2f, sems.at[1]))

    @pl.when(i == 0)
    def _():
        for cp in fetch(e):
            cp.start()

    @pl.when(jnp.logical_or(i == 0, e != e_prev))
    def _():
        for cp in fetch(e):
            cp.wait()
        w1b[...] = w1f[...].astype(BF16)
        w2b[...] = w2f[...].astype(BF16)

        @pl.when(nxt_ref[i] >= 0)
        def _():
            for cp in fetch(nxt_ref[i]):
                cp.start()

    @pl.when(i < nused_ref[0])
    def _():
        xb = jnp.concatenate([xs_ref[s] for s in range(ns)], axis=1).astype(BF16)
        hgl = jnp.dot(xb, w1b[...], preferred_element_type=F32) + b1_ref[0]
        x_glu = jnp.minimum(hgl[:, :f], SWIGLU_LIMIT)
        x_lin = jnp.clip(hgl[:, f:], -SWIGLU_LIMIT, SWIGLU_LIMIT)
        act = (x_lin + 1.0) * (x_glu * _sigmoid(SWIGLU_ALPHA * x_glu))
        y = jnp.dot(act.astype(BF16), w2b[...], preferred_element_type=F32) + b2_ref[0]
        for s in range(ns):
            ys_ref[s] = y[:, s * LANES:(s + 1) * LANES]

    @pl.when(i >= nused_ref[0])
    def _():
        ys_ref[...] = jnp.zeros_like(ys_ref)


def _moe_experts(block_e, next_e, n_used, xs, w1, b1, w2, b2):
    _, blk, ln = xs.shape
    ne, d, f2 = w1.shape
    f = w2.shape[1]
    nb = block_e.shape[0]
    ns = d // ln
    grid_spec = pltpu.PrefetchScalarGridSpec(
        num_scalar_prefetch=3,
        grid=(nb,),
        in_specs=[
            pl.BlockSpec((ns, blk, ln), lambda i, be, nx, nu: (i, 0, 0)),
            pl.BlockSpec(memory_space=pl.ANY),
            pl.BlockSpec((1, 1, f2), lambda i, be, nx, nu: (be[i], 0, 0)),
            pl.BlockSpec(memory_space=pl.ANY),
            pl.BlockSpec((1, 1, d), lambda i, be, nx, nu: (be[i], 0, 0)),
        ],
        out_specs=pl.BlockSpec((ns, blk, ln), lambda i, be, nx, nu: (i, 0, 0)),
        scratch_shapes=[pltpu.VMEM((d, f2), w1.dtype), pltpu.VMEM((f, d), w2.dtype),
                        pltpu.VMEM((d, f2), BF16), pltpu.VMEM((f, d), BF16),
                        pltpu.SemaphoreType.DMA((2,))],
    )
    return pl.pallas_call(
        _moe_kernel,
        grid_spec=grid_spec,
        out_shape=jax.ShapeDtypeStruct(xs.shape, F32),
        compiler_params=pltpu.CompilerParams(
            dimension_semantics=("arbitrary",), vmem_limit_bytes=VMEM_LIMIT),
        name="moe_experts",
    )(block_e, next_e, n_used, xs, w1, b1.reshape(ne, 1, f2), w2, b2.reshape(ne, 1, d))


def _final_kernel(row_ref, ys_hbm, h_ref, route_ref, p_ref, npl_ref, wg_ref, wp_ref, nfin_ref, out_ref,
                  gbuf, sems, *, last_layer):
    i = pl.program_id(0)
    tm = h_ref.shape[0]
    ns = gbuf.shape[2]
    cur = lax.rem(i, 2)

    def gather(step, buf):
        def issue(r, c):
            for kk in range(TOP_K):
                slot = row_ref[(step * tm + r) * TOP_K + kk]
                src = ys_hbm.at[pl.ds(_slot_block(slot) * ns, ns), pl.ds(_slot_row(slot), 1), :]
                pltpu.make_async_copy(src, gbuf.at[buf, kk, :, pl.ds(r, 1), :], sems.at[buf]).start()
            return c
        lax.fori_loop(0, tm, issue, 0, unroll=4)

    @pl.when(i == 0)
    def _():
        gather(0, 0)

    @pl.when(i + 1 < pl.num_programs(0))
    def _():
        gather(i + 1, 1 - cur)

    for kk in range(TOP_K):
        pltpu.make_async_copy(ys_hbm.at[pl.ds(0, ns), pl.ds(0, tm), :], gbuf.at[cur, kk], sems.at[cur]).wait()

    route = route_ref[...]
    h = h_ref[...]
    for kk in range(TOP_K):
        yk = jnp.concatenate([gbuf[cur, kk, s] for s in range(ns)], axis=1)
        h = h + route[:, kk:kk + 1] * yk
    u = _rmsnorm(h, npl_ref[...])
    gate = _sigmoid(_mm(u, wg_ref[...]))
    h = h + gate * _mm(p_ref[...], wp_ref[...])
    out_ref[...] = _rmsnorm(h, nfin_ref[...]) if last_layer else h


def _final(slots, ys, h1, route, pf, npl, wg, wp, nfin, tm, last_layer):
    n, d = h1.shape
    pd = pf.shape[1]
    grid_spec = pltpu.PrefetchScalarGridSpec(
        num_scalar_prefetch=1,
        grid=(n // tm,),
        in_specs=[
            pl.BlockSpec(memory_space=pl.ANY),
            pl.BlockSpec((tm, d), lambda i, s: (i, 0)),
            pl.BlockSpec((tm, LANES), lambda i, s: (i, 0)),
            pl.BlockSpec((tm, pd), lambda i, s: (i, 0)),
            pl.BlockSpec((1, d), lambda i, s: (0, 0)),
            pl.BlockSpec((d, d), lambda i, s: (0, 0)),
            pl.BlockSpec((pd, d), lambda i, s: (0, 0)),
            pl.BlockSpec((1, d), lambda i, s: (0, 0)),
        ],
        out_specs=pl.BlockSpec((tm, d), lambda i, s: (i, 0)),
        scratch_shapes=[pltpu.VMEM((2, TOP_K, d // ys.shape[2], tm, ys.shape[2]), F32),
                        pltpu.SemaphoreType.DMA((2,))],
    )
    return pl.pallas_call(
        functools.partial(_final_kernel, last_layer=last_layer),
        grid_spec=grid_spec,
        out_shape=jax.ShapeDtypeStruct((n, d), F32),
        compiler_params=pltpu.CompilerParams(
            dimension_semantics=("arbitrary",), vmem_limit_bytes=VMEM_LIMIT),
        name="final",
    )(slots, ys, h1, route, pf, npl, wg, wp, nfin)


def _routing(route, counts, n_experts):
    n = route.shape[0]
    blk = EXPERT_BLOCK
    nb = (n * TOP_K + n_experts * (blk - 1) + blk - 1) // blk
    counts = counts[0, :n_experts].astype(jnp.int32)
    pcounts = ((counts + blk - 1) // blk) * blk
    pend = jnp.cumsum(pcounts)
    pstart = pend - pcounts
    top_i = route[:, TOP_K:2 * TOP_K].astype(jnp.int32)
    rank = route[:, 2 * TOP_K:3 * TOP_K].astype(jnp.int32)
    onehot = top_i[:, :, None] == jnp.arange(n_experts, dtype=jnp.int32)[None, None, :]
    slots = jnp.sum(jnp.where(onehot, pstart[None, None, :], 0), axis=-1) + rank
    block_rows = jnp.arange(nb, dtype=jnp.int32) * blk
    block_e = jnp.sum(block_rows[:, None] >= pend[None, :], axis=1)
    block_e = jnp.clip(block_e, 0, n_experts - 1).astype(jnp.int32)
    n_used = (pend[-1] // blk).astype(jnp.int32).reshape(1)
    blocks = jnp.arange(nb, dtype=jnp.int32)
    onehot_b = block_e[:, None] == jnp.arange(n_experts, dtype=jnp.int32)[None, :]
    run_end = jnp.sum(jnp.where(onehot_b, pend[None, :], 0), axis=1) // blk
    run_end = jnp.where(blocks >= n_used[0], nb, run_end)
    follow = jnp.sum(jnp.where(blocks[None, :] == run_end[:, None], block_e[None, :], 0), axis=1)
    next_e = jnp.where((run_end < nb) & (follow != block_e), follow, -1).astype(jnp.int32)
    pad_blocks = ((blocks + 1 == run_end) | (blocks >= n_used[0])).astype(jnp.int32)
    return block_e, next_e, n_used, slots.reshape(-1).astype(jnp.int32), pad_blocks


def kernel(x, p, norm_mix, w_in, shift_mu, rw_w0, rw_w2, rw_a0, rw_a2, rw_g2, rw_kk, rw_ka, rw_rk,
           rw_gn_w, rw_gn_b, gla_gk2, gla_gk_b, gla_norm, w_out, norm_ffn, w_router, b_router,
           w1, b1, w2, b2, norm_ple, w_ple_gate, w_ple, norm_final):
    bsz, seq, d = x.shape
    n = bsz * seq
    depth = w_in.shape[0]
    n_experts = w_router.shape[-1]
    tm = 256
    tm_route = min(512, n)
    h = x.reshape(n, d)
    for l in range(depth):
        w_rw = w_in[l][:, :RW_COLS].astype(BF16)
        w_gla = jnp.pad(w_in[l][:, RW_COLS:], ((0, 0), (0, LANES - GLA_GATE_RANK))).astype(BF16)
        rw_proj, gla_proj = _in_proj(h, norm_mix[l].reshape(1, d), w_rw, w_gla, tm)
        y_rw = _rwkv(rw_proj.reshape(bsz, seq, -1), shift_mu[l], rw_w0[l], rw_w2[l], rw_a0[l], rw_a2[l],
                     rw_g2[l], rw_kk[l], rw_ka[l], rw_rk[l], rw_gn_w[l], rw_gn_b[l])
        y_gla = _gla(gla_proj.reshape(bsz, seq, -1), gla_gk2[l], gla_gk_b[l], gla_norm[l])

        wr = jnp.pad(w_router[l], ((0, 0), (0, LANES - n_experts)))
        wr_hi = wr.astype(BF16)
        wr_lo = (wr - wr_hi.astype(F32)).astype(BF16)
        br = jnp.pad(b_router[l], (0, LANES - n_experts), constant_values=NEG_BIG).reshape(1, LANES)
        wo = w_out[l].astype(BF16)
        h1, u3, route, counts = _out_proj(y_rw.reshape(n, -1), y_gla.reshape(n, -1), h, wo[:RW_WIDTH],
                                          wo[RW_WIDTH:], norm_ffn[l].reshape(1, d), wr_hi, wr_lo, br, tm_route)

        ns = d // LANES
        block_e, next_e, n_used, slots, pad_blocks = _routing(route, counts, n_experts)
        xs = _dispatch(slots, pad_blocks, u3, tm_route, ns)
        ys = _moe_experts(block_e, next_e, n_used, xs, w1[l], b1[l], w2[l], b2[l])
        h = _final(slots, ys, h1, route, p[l].reshape(n, -1), norm_ple[l].reshape(1, d),
                   w_ple_gate[l].astype(BF16), w_ple[l].astype(BF16), norm_final.reshape(1, d), tm, l == depth - 1)
    return h.reshape(bsz, seq, d)
```

```python
import functools

import jax
import jax.numpy as jnp
from jax import lax
from jax.experimental import pallas as pl
from jax.experimental.pallas import tpu as pltpu

F32 = jnp.float32
BF16 = jnp.bfloat16

CHUNK = 64
RW_HEADS = 8
RW_HEAD_DIM = 64
RW_WIDTH = RW_HEADS * RW_HEAD_DIM
RW_DECAY_LORA = 64
RW_ICLR_LORA = 64
RW_GATE_LORA = 128
RW_COLS = 3 * RW_WIDTH + RW_DECAY_LORA + RW_ICLR_LORA + RW_GATE_LORA
RW_GN_EPS = 64e-5
GLA_HEADS = 4
GLA_KEY_DIM = 64
GLA_VAL_DIM = 128
GLA_QK_WIDTH = GLA_HEADS * GLA_KEY_DIM
GLA_V_WIDTH = GLA_HEADS * GLA_VAL_DIM
GLA_GATE_RANK = 16
GLA_GATE_TEMP = 16.0
GLA_SUB = 8
LANES = 128
GLA_COLS_PAD = 2 * GLA_QK_WIDTH + 2 * GLA_V_WIDTH + LANES
TOP_K = 4
EXPERT_BLOCK = 256
SWIGLU_ALPHA = 1.702
SWIGLU_LIMIT = 7.0
NORM_EPS = 1e-6
NEG_BIG = -1e30
VMEM_LIMIT = 56 * 1024 * 1024


def _mm(a, b):
    return jnp.dot(a.astype(BF16), b.astype(BF16), preferred_element_type=F32)


def _mm_nt(a, b):
    return lax.dot_general(a.astype(BF16), b.astype(BF16), (((1,), (1,)), ((), ())),
                           preferred_element_type=F32)


def _mm_tn(a, b):
    return lax.dot_general(a.astype(BF16), b.astype(BF16), (((0,), (0,)), ((), ())),
                           preferred_element_type=F32)


def _split(a, n):
    parts = []
    rem = a
    for _ in range(n):
        p = rem.astype(BF16)
        parts.append(p)
        rem = rem - p.astype(F32)
    return parts


def _mm_lhs_split(a, b_bf16, n):
    out = None
    for p in _split(a, n):
        t = jnp.dot(p, b_bf16, preferred_element_type=F32)
        out = t if out is None else out + t
    return out


def _mm_rhs_split(a_bf16, b, n):
    out = None
    for p in _split(b, n):
        t = jnp.dot(a_bf16, p, preferred_element_type=F32)
        out = t if out is None else out + t
    return out


def _rmsnorm(x, g):
    return x * lax.rsqrt(jnp.mean(x * x, axis=-1, keepdims=True) + NORM_EPS) * g


def _softplus(x):
    return jnp.maximum(x, 0.0) + jnp.log(1.0 + jnp.exp(-jnp.abs(x)))


def _sigmoid(x):
    return 1.0 / (1.0 + jnp.exp(-x))


def _log2(n):
    assert n & (n - 1) == 0
    return n.bit_length() - 1


def _head_keep(rows, head_lanes, dtype):
    lane_head = lax.shift_right_logical(lax.broadcasted_iota(jnp.int32, (rows, LANES), 1), _log2(head_lanes))
    return [jnp.where(lane_head == j, 1.0, 0.0).astype(dtype) for j in range(LANES // head_lanes)]


def _block_diag_fn(rows, n_heads, head_lanes):
    n_tiles = n_heads * head_lanes // LANES
    zero = jnp.zeros((rows, LANES), BF16)
    keep = _head_keep(rows, head_lanes, BF16) if head_lanes < LANES else None

    def f(y):
        yb = y.astype(BF16)
        blocks = []
        for hd in range(n_heads):
            tiles = [zero] * n_tiles
            if keep is not None:
                t = hd * head_lanes // LANES
                tiles[t] = yb[:, t * LANES:(t + 1) * LANES] * keep[hd % len(keep)]
            else:
                for t in range(hd * head_lanes // LANES, (hd + 1) * head_lanes // LANES):
                    tiles[t] = yb[:, t * LANES:(t + 1) * LANES]
            blocks.append(jnp.concatenate(tiles, axis=1))
        return jnp.concatenate(blocks, axis=0)
    return f


def _diag_blocks_fn(rows, n_heads, head_lanes):
    assert head_lanes < LANES
    keep = _head_keep(rows, head_lanes, F32)
    per = len(keep)

    def f(full):
        tiles = []
        for t in range(n_heads * head_lanes // LANES):
            acc = None
            for j in range(per):
                hd = t * per + j
                term = full[hd * rows:(hd + 1) * rows, t * LANES:(t + 1) * LANES] * keep[j]
                acc = term if acc is None else acc + term
            tiles.append(acc)
        return jnp.concatenate(tiles, axis=1)
    return f


def _tri(n, strict):
    r = lax.broadcasted_iota(jnp.int32, (n, n), 0)
    c = lax.broadcasted_iota(jnp.int32, (n, n), 1)
    return (r > c) if strict else (r >= c)


def _inproj_kernel(x_ref, g_ref, wr_ref, wg_ref, rw_ref, gla_ref):
    u = _rmsnorm(x_ref[...], g_ref[...]).astype(BF16)
    rw_ref[...] = jnp.dot(u, wr_ref[...], preferred_element_type=F32)
    gla_ref[...] = jnp.dot(u, wg_ref[...], preferred_element_type=F32)


def _in_proj(xf, g, w_rw, w_gla, tm):
    n, d = xf.shape
    return pl.pallas_call(
        _inproj_kernel,
        grid=(n // tm,),
        in_specs=[
            pl.BlockSpec((tm, d), lambda i: (i, 0)),
            pl.BlockSpec((1, d), lambda i: (0, 0)),
            pl.BlockSpec(w_rw.shape, lambda i: (0, 0)),
            pl.BlockSpec(w_gla.shape, lambda i: (0, 0)),
        ],
        out_specs=[
            pl.BlockSpec((tm, w_rw.shape[1]), lambda i: (i, 0)),
            pl.BlockSpec((tm, w_gla.shape[1]), lambda i: (i, 0)),
        ],
        out_shape=[
            jax.ShapeDtypeStruct((n, w_rw.shape[1]), F32),
            jax.ShapeDtypeStruct((n, w_gla.shape[1]), F32),
        ],
        compiler_params=pltpu.CompilerParams(
            dimension_semantics=("arbitrary",), vmem_limit_bytes=VMEM_LIMIT),
        name="in_proj",
    )(xf, g, w_rw, w_gla)


def _rwkv_kernel_unpacked(x_ref, mu_ref, w0_ref, w2_ref, a0_ref, a2_ref, g2_ref, kkw_ref, ka_ref,
                          rk_ref, gnw_ref, gnb_ref, bd_ref, y_ref, state_ref, carry_ref, o_ref):
    C, H, D = CHUNK, RW_HEADS, RW_HEAD_DIM

    @pl.when(pl.program_id(1) == 0)
    def _():
        state_ref[...] = jnp.zeros_like(state_ref)
        carry_ref[...] = jnp.zeros_like(carry_ref)

    x = x_ref[0]
    row = lax.broadcasted_iota(jnp.int32, x.shape, 0)
    prev = jnp.where(row == 0, carry_ref[...], pltpu.roll(x, 1, axis=0))
    carry_ref[...] = x[C - 1:C, :]
    h = x + (prev - x) * mu_ref[...]

    W = RW_WIDTH
    r = h[:, 0:W]
    k = h[:, W:2 * W]
    v = h[:, 2 * W:3 * W]
    o0 = 3 * W
    dw = h[:, o0:o0 + RW_DECAY_LORA]
    da = h[:, o0 + RW_DECAY_LORA:o0 + RW_DECAY_LORA + RW_ICLR_LORA]
    dg = h[:, o0 + RW_DECAY_LORA + RW_ICLR_LORA:]

    bd = bd_ref[...]

    def seg_sum(t):
        return _mm_lhs_split(t, bd, 2)

    w_log = -_softplus(-(w0_ref[...] + _mm(jnp.tanh(dw), w2_ref[...]))) - 0.5
    lw = -jnp.exp(w_log)
    iclr = _sigmoid(a0_ref[...] + _mm(da, a2_ref[...]))
    gate = _mm(_sigmoid(dg), g2_ref[...])

    kk = k * kkw_ref[...]
    kk = kk / jnp.maximum(jnp.sqrt(seg_sum(kk * kk)), 1e-12)
    k2 = k * (1.0 + (iclr - 1.0) * ka_ref[...])

    cw = _mm_rhs_split(_tri(C, False).astype(BF16), lw, 3)
    cw_last = cw[C - 1:C, :]
    e_cw = jnp.exp(cw)
    e_ncw = jnp.exp(-cw)
    e_rem = jnp.exp(cw_last - cw)
    kka = kk * iclr
    a_t = -kk * jnp.exp(cw - lw)
    r_t = r * e_cw
    b_t = kka * e_ncw
    k_t = k2 * e_ncw
    b_h = kka * e_rem
    k_h = k2 * e_rem
    w_c = jnp.exp(cw_last)

    strict = _tri(C, True)
    incl = _tri(C, False)
    eye = jnp.where(_tri(C, False) & ~strict, 1.0, 0.0).astype(F32)

    hs = range(H)
    sls = [slice(hd * D, (hd + 1) * D) for hd in hs]
    states = [state_ref[hd] for hd in hs]
    ah = [a_t[:, sl] for sl in sls]
    rh = [r_t[:, sl] for sl in sls]
    bh = [b_t[:, sl] for sl in sls]
    kh = [k_t[:, sl] for sl in sls]
    vh = [v[:, sl] for sl in sls]
    a_ab = [jnp.where(strict, _mm_nt(ah[i], bh[i]), 0.0) for i in hs]
    a_ak = [jnp.where(strict, _mm_nt(ah[i], kh[i]), 0.0) for i in hs]
    a_rb = [jnp.where(incl, _mm_nt(rh[i], bh[i]), 0.0) for i in hs]
    a_rk = [jnp.where(incl, _mm_nt(rh[i], kh[i]), 0.0) for i in hs]
    p = [_mm(a_ab[i], a_ab[i]) for i in hs]
    q = [eye + a_ab[i] for i in hs]
    for _ in range(4):
        pq = [_mm(p[i], q[i]) for i in hs]
        p = [_mm(p[i], p[i]) for i in hs]
        q = [q[i] + pq[i] for i in hs]
    pq = [_mm(p[i], q[i]) for i in hs]
    t_inv = [q[i] + pq[i] for i in hs]
    akv = [_mm(a_ak[i], vh[i]) for i in hs]
    u0 = [_mm(t_inv[i], akv[i]) for i in hs]
    a_hat = [_mm(t_inv[i], ah[i]) for i in hs]
    u = [_mm_nt(a_hat[i], states[i]) + u0[i] for i in hs]
    o_heads = [_mm_nt(rh[i], states[i]) + _mm(a_rb[i], u[i]) + _mm(a_rk[i], vh[i]) for i in hs]
    new_states = [states[i] * w_c[:, sls[i]] + _mm_tn(u[i], b_h[:, sls[i]]) + _mm_tn(vh[i], k_h[:, sls[i]])
                  for i in hs]
    for hd in hs:
        state_ref[hd] = new_states[hd]
        o_ref[:, sls[hd]] = o_heads[hd]

    o = o_ref[...]
    inv_d = 1.0 / D
    mean = seg_sum(o) * inv_d
    dlt = o - mean
    var = seg_sum(dlt * dlt) * inv_d
    o = dlt * lax.rsqrt(var + RW_GN_EPS) * gnw_ref[...] + gnb_ref[...]
    bonus = seg_sum(r * k2 * rk_ref[...]) * v
    y_ref[0] = ((o + bonus) * gate).astype(y_ref.dtype)


def _rwkv_unpacked(rw_proj, mu, w0, w2, a0, a2, g2, kkw, ka, rk, gnw, gnb):
    b, s, cols = rw_proj.shape
    W = RW_WIDTH
    bd = jnp.kron(jnp.eye(RW_HEADS, dtype=F32), jnp.ones((RW_HEAD_DIM, RW_HEAD_DIM), F32)).astype(BF16)
    row = lambda a: a.reshape(1, -1)
    full = lambda a: pl.BlockSpec(a.shape, lambda i, j: (0,) * a.ndim)
    args = [row(mu), row(w0), w2.astype(BF16), row(a0), a2.astype(BF16), g2.astype(BF16),
            row(kkw), row(ka), row(rk), row(gnw), row(gnb), bd]
    return pl.pallas_call(
        _rwkv_kernel_unpacked,
        grid=(b, s // CHUNK),
        in_specs=[pl.BlockSpec((1, CHUNK, cols), lambda i, j: (i, j, 0))] + [full(a) for a in args],
        out_specs=pl.BlockSpec((1, CHUNK, W), lambda i, j: (i, j, 0)),
        out_shape=jax.ShapeDtypeStruct((b, s, W), BF16),
        scratch_shapes=[
            pltpu.VMEM((RW_HEADS, RW_HEAD_DIM, RW_HEAD_DIM), F32),
            pltpu.VMEM((1, cols), F32),
            pltpu.VMEM((CHUNK, W), F32),
        ],
        compiler_params=pltpu.CompilerParams(
            dimension_semantics=("arbitrary", "arbitrary"), vmem_limit_bytes=VMEM_LIMIT),
        name="rwkv7",
    )(rw_proj, *args)


RW_GROUP = 4
RW_GW = RW_GROUP * RW_HEAD_DIM
RW_STEP_CHUNKS = 2


def _rwkv_kernel(x_ref, mu_ref, w0_ref, w2_ref, a0_ref, a2_ref, g2_ref, kkw_ref, ka_ref,
                 rk_ref, gnw_ref, gnb_ref, bd_ref, y_ref, state_ref, carry_ref):
    C, D, W, GW = CHUNK, RW_HEAD_DIM, RW_WIDTH, RW_GW
    nb, ct = x_ref.shape[0], x_ref.shape[1]
    nch = ct // C
    R = nb * ct

    @pl.when(pl.program_id(0) == 0)
    def _():
        state_ref[...] = jnp.zeros_like(state_ref)
        carry_ref[...] = jnp.zeros_like(carry_ref)

    x = x_ref[...].reshape(R, x_ref.shape[2])
    row = lax.broadcasted_iota(jnp.int32, x.shape, 0)
    prev = pltpu.roll(x, 1, axis=0)
    for b in range(nb):
        prev = jnp.where(row == b * ct, carry_ref[b:b + 1, :], prev)
        carry_ref[b:b + 1, :] = x[(b + 1) * ct - 1:(b + 1) * ct, :]
    h = x + (prev - x) * mu_ref[...]

    r = h[:, 0:W]
    k = h[:, W:2 * W]
    v = h[:, 2 * W:3 * W]
    o0 = 3 * W
    dw = h[:, o0:o0 + RW_DECAY_LORA]
    da = h[:, o0 + RW_DECAY_LORA:o0 + RW_DECAY_LORA + RW_ICLR_LORA]
    dg = h[:, o0 + RW_DECAY_LORA + RW_ICLR_LORA:]

    bd_g = bd_ref[...]

    def seg_sum(t):
        return jnp.concatenate(
            [_mm_lhs_split(t[:, g * GW:(g + 1) * GW], bd_g, 2) for g in range(W // GW)], axis=1)

    w_log = -_softplus(-(w0_ref[...] + _mm(jnp.tanh(dw), w2_ref[...]))) - 0.5
    lw = -jnp.exp(w_log)
    iclr = _sigmoid(a0_ref[...] + _mm(da, a2_ref[...]))
    gate = _mm(_sigmoid(dg), g2_ref[...])

    kk = k * kkw_ref[...]
    kk = kk * jnp.minimum(lax.rsqrt(seg_sum(kk * kk)), 1e12)
    k2 = k * (1.0 + (iclr - 1.0) * ka_ref[...])

    rr = lax.broadcasted_iota(jnp.int32, (R, R), 0)
    cc = lax.broadcasted_iota(jnp.int32, (R, R), 1)
    tri_seq = jnp.where((rr >= cc) & (rr // C == cc // C), 1.0, 0.0).astype(BF16)
    cw = _mm_rhs_split(tri_seq, lw, 3)
    cw_last = jnp.concatenate(
        [jnp.broadcast_to(cw[(j + 1) * C - 1:(j + 1) * C, :], (C, W)) for j in range(R // C)], axis=0)
    e_cw = jnp.exp(cw)
    e_ncw = jnp.exp(-cw)
    e_rem = jnp.exp(cw_last - cw)
    kka = kk * iclr
    a_t = -kk * jnp.exp(cw - lw)
    r_t = r * e_cw
    b_t = kka * e_ncw
    k_t = k2 * e_ncw
    b_h = kka * e_rem
    k_h = k2 * e_rem
    w_c = jnp.exp(cw_last)

    ti = lax.broadcasted_iota(jnp.int32, (C, GW), 0)
    si = jnp.bitwise_and(lax.broadcasted_iota(jnp.int32, (C, GW), 1), D - 1)
    strict = ti > si
    incl = ti >= si
    eye = jnp.where(ti == si, 1.0, 0.0)

    bdiag = _block_diag_fn(C, RW_GROUP, D)
    diag_blocks = _diag_blocks_fn(D, RW_GROUP, D)

    ng = W // GW
    chains = [(b, g, ch) for ch in range(nch) for b in range(nb) for g in range(ng)]

    def part(t, c):
        b, g, ch = c
        r0 = b * ct + ch * C
        return t[r0:r0 + C, g * GW:(g + 1) * GW]

    n = range(len(chains))
    a4 = [part(a_t, c) for c in chains]
    r4 = [part(r_t, c) for c in chains]
    v4 = [part(v, c) for c in chains]
    ar = [jnp.concatenate([a4[i], r4[i]], axis=0) for i in n]
    bd_b = [bdiag(part(b_t, c)) for c in chains]
    bd_k = [bdiag(part(k_t, c)) for c in chains]
    bd_v = [bdiag(v4[i]) for i in n]
    m_b = [_mm_nt(ar[i], bd_b[i]) for i in n]
    m_k = [_mm_nt(ar[i], bd_k[i]) for i in n]
    a_ab = [jnp.where(strict, m_b[i][0:C], 0.0) for i in n]
    a_rb = [jnp.where(incl, m_b[i][C:2 * C], 0.0) for i in n]
    a_ak = [jnp.where(strict, m_k[i][0:C], 0.0) for i in n]
    a_rk = [jnp.where(incl, m_k[i][C:2 * C], 0.0) for i in n]
    akv = [_mm(a_ak[i], bd_v[i]) for i in n]
    o_kv = [_mm(a_rk[i], bd_v[i]) for i in n]
    p = [_mm(a_ab[i], bdiag(a_ab[i])) for i in n]
    q = [eye + a_ab[i] for i in n]
    for _ in range(4):
        pq = [_mm(p[i], bdiag(q[i])) for i in n]
        p = [_mm(p[i], bdiag(p[i])) for i in n]
        q = [q[i] + pq[i] for i in n]
    pq = [_mm(p[i], bdiag(q[i])) for i in n]
    t_inv = [q[i] + pq[i] for i in n]
    u0 = [_mm(t_inv[i], bdiag(akv[i])) for i in n]
    a_hat = [_mm(t_inv[i], bdiag(a4[i])) for i in n]
    seqs = [(b, g) for b in range(nb) for g in range(ng)]
    s4 = [state_ref[b, g] for b, g in seqs]
    o_g = {}
    for ch in range(nch):
        idx = [chains.index((b, g, ch)) for b, g in seqs]
        bd_s = [bdiag(s) for s in s4]
        o_s = [_mm_nt(r4[i], bd_s[j]) for j, i in enumerate(idx)]
        u = [_mm_nt(a_hat[i], bd_s[j]) + u0[i] for j, i in enumerate(idx)]
        for j, i in enumerate(idx):
            o_g[chains[i]] = o_s[j] + _mm(a_rb[i], bdiag(u[j])) + o_kv[i]
        nxt = []
        for j, i in enumerate(idx):
            c = chains[i]
            uv = jnp.concatenate([u[j], v4[i]], axis=0)
            bk = jnp.concatenate([part(b_h, c), part(k_h, c)], axis=0)
            nxt.append(s4[j] * part(w_c, c)[0:D] + diag_blocks(_mm_tn(uv, bk)))
        s4 = nxt
    for j, (b, g) in enumerate(seqs):
        state_ref[b, g] = s4[j]

    o = jnp.concatenate(
        [jnp.concatenate([o_g[(b, g, ch)] for g in range(ng)], axis=1) for b in range(nb) for ch in range(nch)],
        axis=0)
    inv_d = 1.0 / D
    mean = seg_sum(o) * inv_d
    dlt = o - mean
    var = seg_sum(dlt * dlt) * inv_d
    o = dlt * lax.rsqrt(var + RW_GN_EPS) * gnw_ref[...] + gnb_ref[...]
    bonus = seg_sum(r * k2 * rk_ref[...]) * v
    y_ref[...] = ((o + bonus) * gate).astype(y_ref.dtype).reshape(y_ref.shape)


def _rwkv(rw_proj, mu, w0, w2, a0, a2, g2, kkw, ka, rk, gnw, gnb):
    b, s, cols = rw_proj.shape
    W = RW_WIDTH
    bd = jnp.kron(jnp.eye(RW_GROUP, dtype=F32), jnp.ones((RW_HEAD_DIM, RW_HEAD_DIM), F32)).astype(BF16)
    row = lambda a: a.reshape(1, -1)
    full = lambda a: pl.BlockSpec(a.shape, lambda j: (0,) * a.ndim)
    args = [row(mu), row(w0), w2.astype(BF16), row(a0), a2.astype(BF16), g2.astype(BF16),
            row(kkw), row(ka), row(rk), row(gnw), row(gnb), bd]
    ct = CHUNK * RW_STEP_CHUNKS
    return pl.pallas_call(
        _rwkv_kernel,
        grid=(s // ct,),
        in_specs=[pl.BlockSpec((b, ct, cols), lambda j: (0, j, 0))] + [full(a) for a in args],
        out_specs=pl.BlockSpec((b, ct, W), lambda j: (0, j, 0)),
        out_shape=jax.ShapeDtypeStruct((b, s, W), BF16),
        scratch_shapes=[
            pltpu.VMEM((b, W // RW_GW, RW_HEAD_DIM, RW_GW), F32),
            pltpu.VMEM((b, cols), F32),
        ],
        compiler_params=pltpu.CompilerParams(
            dimension_semantics=("arbitrary",), vmem_limit_bytes=VMEM_LIMIT),
        name="rwkv7",
    )(rw_proj, *args)


GLA_STEP_CHUNKS = 2


def _group_mask(rows, row_group, cols, col_group):
    r = lax.shift_right_logical(lax.broadcasted_iota(jnp.int32, (rows, cols), 0), _log2(row_group))
    c = lax.shift_right_logical(lax.broadcasted_iota(jnp.int32, (rows, cols), 1), _log2(col_group))
    return jnp.where(r == c, 1.0, 0.0).astype(BF16)


def _gla_kernel(x_ref, gk2_ref, gkb_ref, ng_ref, y_ref, state_ref):
    C, H, DK, DV, SB = CHUNK, GLA_HEADS, GLA_KEY_DIM, GLA_VAL_DIM, GLA_SUB
    QW, VW = GLA_QK_WIDTH, GLA_V_WIDTH
    nb, ct = x_ref.shape[0], x_ref.shape[1]
    nch = ct // C
    R = nb * ct

    @pl.when(pl.program_id(0) == 0)
    def _():
        state_ref[...] = jnp.zeros_like(state_ref)

    x = x_ref[...].reshape(R, x_ref.shape[2])
    q = x[:, 0:QW] * (DK ** -0.5)
    k = x[:, QW:2 * QW]
    v = x[:, 2 * QW:2 * QW + VW]
    g = x[:, 2 * QW + VW:2 * QW + 2 * VW]
    dgk = x[:, 2 * QW + 2 * VW:]

    la = -_softplus(-(_mm(dgk, gk2_ref[...]) + gkb_ref[...])) * (1.0 / GLA_GATE_TEMP)
    rr = lax.broadcasted_iota(jnp.int32, (R, R), 0)
    cc = lax.broadcasted_iota(jnp.int32, (R, R), 1)
    tri_seq = jnp.where((rr >= cc) & (rr // C == cc // C), 1.0, 0.0).astype(BF16)
    b = _mm_rhs_split(tri_seq, la, 3)
    b_last = jnp.concatenate(
        [jnp.broadcast_to(b[(j + 1) * C - 1:(j + 1) * C, :], (C, QW)) for j in range(R // C)], axis=0)
    q_e = q * jnp.exp(b)
    k_e = k * jnp.exp(b_last - b)
    w_c = jnp.exp(b_last)

    def tile_rows(y, n):
        return jnp.concatenate([y.astype(BF16)] * n, axis=0)

    units = [(bi, ch) for ch in range(nch) for bi in range(nb)]
    row0 = {u: u[0] * ct + u[1] * C for u in units}

    o_off = {u: None for u in units}
    for s in (C // 2, C // 4, C // 8):
        bd_keys = _block_diag_fn(s, H, DK)
        bd_vals = _block_diag_fn(s, H, DV)
        jobs = [(u, row0[u] + m * 2 * s) for u in units for m in range(C // (2 * s))]
        att = []
        for u, c0 in jobs:
            ref = b[c0 + s - 1:c0 + s]
            q_s = q[c0 + s:c0 + 2 * s] * jnp.exp(b[c0 + s:c0 + 2 * s] - ref)
            k_s = k[c0:c0 + s] * jnp.exp(ref - b[c0:c0 + s])
            att.append(_mm_nt(q_s, bd_keys(k_s)))
        outs = [_mm(att[j], bd_vals(v[c0:c0 + s])) for j, (u, c0) in enumerate(jobs)]
        zero = jnp.zeros((s, VW), F32)
        for u in units:
            pieces = []
            for j, (uj, c0) in enumerate(jobs):
                if uj == u:
                    pieces += [zero, outs[j]]
            level = jnp.concatenate(pieces, axis=0)
            o_off[u] = level if o_off[u] is None else o_off[u] + level

    nblk = C // SB
    ii = lax.broadcasted_iota(jnp.int32, (nblk, SB, SB, 2 * DK), 1)
    jj = lax.broadcasted_iota(jnp.int32, (nblk, SB, SB, 2 * DK), 2)
    causal4 = jj <= ii
    sel = _group_mask(C, 1, C * SB, SB)
    pair_ones = _group_mask(2 * DK, DK, 2 * DV, DV)
    att2 = {}
    for u in units:
        r0 = row0[u]
        for pr in range(H // 2):
            sl2 = slice(2 * pr * DK, 2 * (pr + 1) * DK)
            q2, k2, b2 = q[r0:r0 + C, sl2], k[r0:r0 + C, sl2], b[r0:r0 + C, sl2]
            dec = jnp.exp(jnp.where(
                causal4, b2.reshape(nblk, SB, 1, 2 * DK) - b2.reshape(nblk, 1, SB, 2 * DK), NEG_BIG))
            pw = q2.reshape(nblk, SB, 1, 2 * DK) * k2.reshape(nblk, 1, SB, 2 * DK) * dec
            att2[(u, pr)] = _mm(pw.reshape(C * SB, 2 * DK), pair_ones)
    o_diag = {}
    for u in units:
        r0 = row0[u]
        heads = []
        for hd in range(H):
            v_h = v[r0:r0 + C, hd * DV:(hd + 1) * DV]
            v_rep = jnp.broadcast_to(v_h.reshape(nblk, 1, SB, DV), (nblk, SB, SB, DV)).reshape(C * SB, DV)
            heads.append(_mm(sel, att2[(u, hd // 2)][:, (hd % 2) * DV:(hd % 2 + 1) * DV] * v_rep))
        o_diag[u] = jnp.concatenate(heads, axis=1)

    bd_state = _block_diag_fn(DV, H, DK)
    diag_state = _diag_blocks_fn(DV, H, DK)
    s4 = [state_ref[bi] for bi in range(nb)]
    o_int = {}
    for ch in range(nch):
        for bi in range(nb):
            r0 = row0[(bi, ch)]
            o_int[(bi, ch)] = _mm_nt(q_e[r0:r0 + C], bd_state(s4[bi]))
        nxt = []
        for bi in range(nb):
            r0 = row0[(bi, ch)]
            nxt.append(s4[bi] * w_c[r0:r0 + 1] + diag_state(_mm_tn(v[r0:r0 + C], k_e[r0:r0 + C])))
        s4 = nxt
    for bi in range(nb):
        state_ref[bi] = s4[bi]

    rows = []
    for bi in range(nb):
        for ch in range(nch):
            u = (bi, ch)
            o = o_int[u] + o_diag[u] + o_off[u]
            heads = []
            for hd in range(H):
                oh = o[:, hd * DV:(hd + 1) * DV]
                heads.append(oh * lax.rsqrt(jnp.mean(oh * oh, axis=-1, keepdims=True) + NORM_EPS))
            rows.append(jnp.concatenate(heads, axis=1))
    o = jnp.concatenate(rows, axis=0)
    y = o * ng_ref[...] * (g * _sigmoid(g))
    y_ref[...] = y.astype(y_ref.dtype).reshape(y_ref.shape)


def _gla(gla_proj, gk2, gkb, ng):
    b, s, cols = gla_proj.shape
    gk2p = jnp.zeros((LANES, GLA_QK_WIDTH), F32).at[:GLA_GATE_RANK].set(gk2).astype(BF16)
    args = [gk2p, gkb.reshape(1, -1), ng.reshape(1, -1)]
    full = lambda a: pl.BlockSpec(a.shape, lambda j: (0,) * a.ndim)
    ct = CHUNK * GLA_STEP_CHUNKS
    return pl.pallas_call(
        _gla_kernel,
        grid=(s // ct,),
        in_specs=[pl.BlockSpec((b, ct, cols), lambda j: (0, j, 0))] + [full(a) for a in args],
        out_specs=pl.BlockSpec((b, ct, GLA_V_WIDTH), lambda j: (0, j, 0)),
        out_shape=jax.ShapeDtypeStruct((b, s, GLA_V_WIDTH), BF16),
        scratch_shapes=[pltpu.VMEM((b, GLA_VAL_DIM, GLA_QK_WIDTH), F32)],
        compiler_params=pltpu.CompilerParams(
            dimension_semantics=("arbitrary",), vmem_limit_bytes=VMEM_LIMIT),
        name="gla",
    )(gla_proj, *args)


def _gla_kernel_unpacked(x_ref, gk2_ref, gkb_ref, ng_ref, y_ref, state_ref):
    C, H, DK, DV, SB = CHUNK, GLA_HEADS, GLA_KEY_DIM, GLA_VAL_DIM, GLA_SUB

    @pl.when(pl.program_id(1) == 0)
    def _():
        state_ref[...] = jnp.zeros_like(state_ref)

    x = x_ref[0]
    QW, VW = GLA_QK_WIDTH, GLA_V_WIDTH
    q = x[:, 0:QW] * (DK ** -0.5)
    k = x[:, QW:2 * QW]
    v = x[:, 2 * QW:2 * QW + VW]
    g = x[:, 2 * QW + VW:2 * QW + 2 * VW]
    dgk = x[:, 2 * QW + 2 * VW:]

    la = -_softplus(-(_mm(dgk, gk2_ref[...]) + gkb_ref[...])) * (1.0 / GLA_GATE_TEMP)
    b = _mm_rhs_split(_tri(C, False).astype(BF16), la, 3)
    b_last = b[C - 1:C, :]
    q_e = q * jnp.exp(b)
    k_e = k * jnp.exp(b_last - b)
    w_c = jnp.exp(b_last)

    ii = lax.broadcasted_iota(jnp.int32, (C // SB, SB, SB, 2 * DK), 1)
    jj = lax.broadcasted_iota(jnp.int32, (C // SB, SB, SB, 2 * DK), 2)
    causal4 = jj <= ii

    hs = range(H)
    nblk = C // SB
    sks = [slice(hd * DK, (hd + 1) * DK) for hd in hs]
    svs = [slice(hd * DV, (hd + 1) * DV) for hd in hs]
    states = [state_ref[hd] for hd in hs]
    qh = [q[:, sk] for sk in sks]
    kh = [k[:, sk] for sk in sks]
    bh = [b[:, sk] for sk in sks]
    vh = [v[:, sv] for sv in svs]
    o_inter = [_mm_nt(q_e[:, sks[i]], states[i]) for i in hs]
    new_states = [states[i] * w_c[:, sks[i]] + _mm_tn(vh[i], k_e[:, sks[i]]) for i in hs]
    att_off = {}
    for blk in range(1, nblk):
        r0 = blk * SB
        for i in hs:
            ref = bh[i][r0 - 1:r0]
            q_s = qh[i][r0:r0 + SB] * jnp.exp(bh[i][r0:r0 + SB] - ref)
            k_s = kh[i][0:r0] * jnp.exp(ref - bh[i][0:r0])
            att_off[(i, blk)] = _mm_nt(q_s, k_s)
    o_off = {key: _mm(att, vh[key[0]][0:key[1] * SB]) for key, att in att_off.items()}
    rsel = lax.broadcasted_iota(jnp.int32, (C, C * SB), 0)
    csel = lax.broadcasted_iota(jnp.int32, (C, C * SB), 1)
    sel = jnp.where((csel >= rsel * SB) & (csel < (rsel + 1) * SB), 1.0, 0.0).astype(BF16)
    drow = lax.broadcasted_iota(jnp.int32, (2 * DK, 2 * DV), 0)
    dcol = lax.broadcasted_iota(jnp.int32, (2 * DK, 2 * DV), 1)
    pair_ones = jnp.where((drow >= DK) == (dcol >= DV), 1.0, 0.0).astype(BF16)
    o_diag = []
    for pr in range(H // 2):
        sl2 = slice(2 * pr * DK, 2 * (pr + 1) * DK)
        q2, k2, b2 = q[:, sl2], k[:, sl2], b[:, sl2]
        dec = jnp.exp(jnp.where(causal4, b2.reshape(nblk, SB, 1, 2 * DK) - b2.reshape(nblk, 1, SB, 2 * DK),
                                NEG_BIG))
        pw = q2.reshape(nblk, SB, 1, 2 * DK) * k2.reshape(nblk, 1, SB, 2 * DK) * dec
        att2 = _mm(pw.reshape(C * SB, 2 * DK), pair_ones)
        for t in range(2):
            v_rep = jnp.broadcast_to(vh[2 * pr + t].reshape(nblk, 1, SB, DV), (nblk, SB, SB, DV))
            z = att2[:, t * DV:(t + 1) * DV] * v_rep.reshape(C * SB, DV)
            o_diag.append(_mm(sel, z))
    outs = []
    for i in hs:
        rows = [jnp.zeros((SB, DV), F32)] + [o_off[(i, blk)] for blk in range(1, nblk)]
        o = o_inter[i] + o_diag[i] + jnp.concatenate(rows, axis=0)
        outs.append(o * lax.rsqrt(jnp.mean(o * o, axis=-1, keepdims=True) + NORM_EPS))
    for hd in hs:
        state_ref[hd] = new_states[hd]
    o = jnp.concatenate(outs, axis=1)
    y = o * ng_ref[...] * (g * _sigmoid(g))
    y_ref[0] = y.astype(y_ref.dtype)


def _gla_unpacked(gla_proj, gk2, gkb, ng):
    b, s, cols = gla_proj.shape
    gk2p = jnp.zeros((LANES, GLA_QK_WIDTH), F32).at[:GLA_GATE_RANK].set(gk2).astype(BF16)
    args = [gk2p, gkb.reshape(1, -1), ng.reshape(1, -1)]
    full = lambda a: pl.BlockSpec(a.shape, lambda i, j: (0,) * a.ndim)
    return pl.pallas_call(
        _gla_kernel_unpacked,
        grid=(b, s // CHUNK),
        in_specs=[pl.BlockSpec((1, CHUNK, cols), lambda i, j: (i, j, 0))] + [full(a) for a in args],
        out_specs=pl.BlockSpec((1, CHUNK, GLA_V_WIDTH), lambda i, j: (i, j, 0)),
        out_shape=jax.ShapeDtypeStruct((b, s, GLA_V_WIDTH), BF16),
        scratch_shapes=[pltpu.VMEM((GLA_HEADS, GLA_VAL_DIM, GLA_KEY_DIM), F32)],
        compiler_params=pltpu.CompilerParams(
            dimension_semantics=("arbitrary", "arbitrary"), vmem_limit_bytes=VMEM_LIMIT),
        name="gla",
    )(gla_proj, *args)


def _outproj_kernel(yr_ref, yg_ref, x_ref, wor_ref, wog_ref, nf_ref, wrh_ref, wrl_ref, br_ref,
                    h_ref, u3_ref, route_ref, cnt_ref, carry_ref):
    tm, d = x_ref.shape

    @pl.when(pl.program_id(0) == 0)
    def _():
        carry_ref[...] = jnp.zeros_like(carry_ref)

    h = (x_ref[...] + jnp.dot(yr_ref[...], wor_ref[...], preferred_element_type=F32)
         + jnp.dot(yg_ref[...], wog_ref[...], preferred_element_type=F32))
    h_ref[...] = h
    u = _rmsnorm(h, nf_ref[...])
    for s in range(d // LANES):
        u3_ref[s] = u[:, s * LANES:(s + 1) * LANES]
    u_hi, u_lo = _split(u, 2)
    logits = (jnp.dot(u_hi, wrh_ref[...], preferred_element_type=F32)
              + jnp.dot(u_hi, wrl_ref[...], preferred_element_type=F32)
              + jnp.dot(u_lo, wrh_ref[...], preferred_element_type=F32)) + br_ref[...]
    lane = lax.broadcasted_iota(jnp.int32, logits.shape, 1)
    rest = logits
    picks, idxs, vals = [], [], []
    for r in range(TOP_K):
        m = jnp.max(rest, axis=-1, keepdims=True)
        idx = jnp.min(jnp.where(rest == m, lane, LANES), axis=-1, keepdims=True)
        pick = lane == idx
        picks.append(pick)
        idxs.append(idx)
        vals.append(m)
        rest = jnp.where(pick, -jnp.inf, rest)
    denom = jnp.ones_like(vals[0])
    for r in range(1, TOP_K):
        denom = denom + jnp.exp(vals[r] - vals[0])
    sel = jnp.zeros(logits.shape, F32)
    for pick in picks:
        sel = sel + jnp.where(pick, 1.0, 0.0)
    prefix = _mm(jnp.where(_tri(tm, True), 1.0, 0.0), sel) + carry_ref[...]
    carry_ref[...] = carry_ref[...] + jnp.sum(sel, axis=0, keepdims=True)
    cnt_ref[...] = carry_ref[...]
    route = jnp.zeros(logits.shape, F32)
    for r in range(TOP_K):
        gate = jnp.exp(vals[r] - vals[0]) / denom
        rank = jnp.sum(jnp.where(picks[r], prefix, 0.0), axis=-1, keepdims=True)
        route = jnp.where(lane == r, gate, route)
        route = jnp.where(lane == TOP_K + r, idxs[r].astype(F32), route)
        route = jnp.where(lane == 2 * TOP_K + r, rank, route)
    route_ref[...] = route


def _out_proj(y_rw, y_gla, xf, wo_r, wo_g, nf, wr_hi, wr_lo, br, tm):
    n, d = xf.shape
    full = lambda a: pl.BlockSpec(a.shape, lambda i: (0,) * a.ndim)
    tile = lambda w: pl.BlockSpec((tm, w), lambda i: (i, 0))
    return pl.pallas_call(
        _outproj_kernel,
        grid=(n // tm,),
        in_specs=[tile(y_rw.shape[1]), tile(y_gla.shape[1]), tile(d), full(wo_r), full(wo_g), full(nf),
                  full(wr_hi), full(wr_lo), full(br)],
        out_specs=[tile(d), pl.BlockSpec((d // LANES, tm, LANES), lambda i: (i, 0, 0)), tile(LANES),
                   pl.BlockSpec((1, LANES), lambda i: (0, 0))],
        out_shape=[jax.ShapeDtypeStruct((n, d), F32), jax.ShapeDtypeStruct((n // tm * (d // LANES), tm, LANES), F32),
                   jax.ShapeDtypeStruct((n, LANES), F32), jax.ShapeDtypeStruct((1, LANES), F32)],
        scratch_shapes=[pltpu.VMEM((1, LANES), F32)],
        compiler_params=pltpu.CompilerParams(
            dimension_semantics=("arbitrary",), vmem_limit_bytes=VMEM_LIMIT),
        name="out_proj",
    )(y_rw, y_gla, xf, wo_r, wo_g, nf, wr_hi, wr_lo, br)


assert EXPERT_BLOCK & (EXPERT_BLOCK - 1) == 0
_BLOCK_SHIFT = EXPERT_BLOCK.bit_length() - 1


def _slot_block(slot):
    return lax.shift_right_logical(slot, _BLOCK_SHIFT)


def _slot_row(slot):
    return jnp.bitwise_and(slot, EXPERT_BLOCK - 1)


_DISPATCH_BUFS = 3


def _dispatch_kernel(row_ref, pad_ref, u3_hbm, xs_hbm, ubuf, zbuf, in_sems, out_sems, zsem, *, tm, ns):
    i = pl.program_id(0)
    nsteps = pl.num_programs(0)
    nbuf = _DISPATCH_BUFS
    n_blocks = pad_ref.shape[0]

    @pl.when(i == 0)
    def _():
        zbuf[...] = jnp.zeros_like(zbuf)

        def zero_block(blk_i):
            return pltpu.make_async_copy(zbuf, xs_hbm.at[pl.ds(blk_i * ns, ns)], zsem)

        def start(blk_i, c):
            @pl.when(pad_ref[blk_i] != 0)
            def _():
                zero_block(blk_i).start()
            return c
        lax.fori_loop(0, n_blocks, start, 0)

        def finish(blk_i, c):
            @pl.when(pad_ref[blk_i] != 0)
            def _():
                zero_block(blk_i).wait()
            return c
        lax.fori_loop(0, n_blocks, finish, 0)

    def load(step, slot_):
        return pltpu.make_async_copy(u3_hbm.at[pl.ds(step * ns, ns)], ubuf.at[slot_], in_sems.at[slot_])

    def wait_scatter(slot_):
        for _ in range(TOP_K):
            pltpu.make_async_copy(u3_hbm.at[pl.ds(0, ns)], ubuf.at[slot_], out_sems.at[slot_]).wait()

    cur = lax.rem(i, nbuf)
    nxt = lax.rem(i + 1, nbuf)

    @pl.when(i == 0)
    def _():
        load(0, 0).start()

    @pl.when(i >= nbuf - 1)
    def _():
        wait_scatter(nxt)

    @pl.when(i + 1 < nsteps)
    def _():
        load(i + 1, nxt).start()

    load(i, cur).wait()

    def issue(r, c):
        src = ubuf.at[cur, :, pl.ds(r, 1), :]
        for kk in range(TOP_K):
            slot = row_ref[(i * tm + r) * TOP_K + kk]
            dst = xs_hbm.at[pl.ds(_slot_block(slot) * ns, ns), pl.ds(_slot_row(slot), 1), :]
            pltpu.make_async_copy(src, dst, out_sems.at[cur]).start(priority=kk % 2)
        return c
    lax.fori_loop(0, tm, issue, 0, unroll=4)

    @pl.when(i == nsteps - 1)
    def _():
        for back in range(nbuf - 1):
            @pl.when(i - back >= 0)
            def _():
                wait_scatter(lax.rem(i - back + nbuf, nbuf))


def _dispatch(slots, pad_blocks, u3, tm, ns):
    ln = u3.shape[2]
    n = u3.shape[0] // ns * tm
    blk = EXPERT_BLOCK
    grid_spec = pltpu.PrefetchScalarGridSpec(
        num_scalar_prefetch=2,
        grid=(n // tm,),
        in_specs=[pl.BlockSpec(memory_space=pl.ANY)],
        out_specs=pl.BlockSpec(memory_space=pl.ANY),
        scratch_shapes=[pltpu.VMEM((_DISPATCH_BUFS, ns, tm, ln), u3.dtype),
                        pltpu.VMEM((ns, blk, ln), u3.dtype),
                        pltpu.SemaphoreType.DMA((_DISPATCH_BUFS,)),
                        pltpu.SemaphoreType.DMA((_DISPATCH_BUFS,)),
                        pltpu.SemaphoreType.DMA(())],
    )
    return pl.pallas_call(
        functools.partial(_dispatch_kernel, tm=tm, ns=ns),
        grid_spec=grid_spec,
        out_shape=jax.ShapeDtypeStruct((pad_blocks.shape[0] * ns, blk, ln), u3.dtype),
        compiler_params=pltpu.CompilerParams(dimension_semantics=("arbitrary",), has_side_effects=True),
        name="moe_dispatch",
    )(slots, pad_blocks, u3)


def _moe_kernel(be_ref, nxt_ref, nused_ref, xs_ref, w1_hbm, b1_ref, w2_hbm, b2_ref, ys_ref,
                w1f, w2f, w1b, w2b, sems):
    i = pl.program_id(0)
    f = w2b.shape[0]
    ns = xs_ref.shape[0]
    e = be_ref[i]
    e_prev = be_ref[jnp.maximum(i - 1, 0)]

    def fetch(expert):
        return (pltpu.make_async_copy(w1_hbm.at[expert], w1f, sems.at[0]),
                pltpu.make_async_copy(w2_hbm.at[expert], w2f, sems.at[1]))

    @pl.when(i == 0)
    def _():
        for cp in fetch(e):
            cp.start()

    @pl.when(jnp.logical_or(i == 0, e != e_prev))
    def _():
        for cp in fetch(e):
            cp.wait()
        w1b[...] = w1f[...].astype(BF16)
        w2b[...] = w2f[...].astype(BF16)

        @pl.when(nxt_ref[i] >= 0)
        def _():
            for cp in fetch(nxt_ref[i]):
                cp.start()

    @pl.when(i < nused_ref[0])
    def _():
        xb = jnp.concatenate([xs_ref[s] for s in range(ns)], axis=1).astype(BF16)
        hgl = jnp.dot(xb, w1b[...], preferred_element_type=F32) + b1_ref[0]
        x_glu = jnp.minimum(hgl[:, :f], SWIGLU_LIMIT)
        x_lin = jnp.clip(hgl[:, f:], -SWIGLU_LIMIT, SWIGLU_LIMIT)
        act = (x_lin + 1.0) * (x_glu * _sigmoid(SWIGLU_ALPHA * x_glu))
        y = jnp.dot(act.astype(BF16), w2b[...], preferred_element_type=F32) + b2_ref[0]
        for s in range(ns):
            ys_ref[s] = y[:, s * LANES:(s + 1) * LANES]

    @pl.when(i >= nused_ref[0])
    def _():
        ys_ref[...] = jnp.zeros_like(ys_ref)


def _moe_experts(block_e, next_e, n_used, xs, w1, b1, w2, b2):
    _, blk, ln = xs.shape
    ne, d, f2 = w1.shape
    f = w2.shape[1]
    nb = block_e.shape[0]
    ns = d // ln
    grid_spec = pltpu.PrefetchScalarGridSpec(
        num_scalar_prefetch=3,
        grid=(nb,),
        in_specs=[
            pl.BlockSpec((ns, blk, ln), lambda i, be, nx, nu: (i, 0, 0)),
            pl.BlockSpec(memory_space=pl.ANY),
            pl.BlockSpec((1, 1, f2), lambda i, be, nx, nu: (be[i], 0, 0)),
            pl.BlockSpec(memory_space=pl.ANY),
            pl.BlockSpec((1, 1, d), lambda i, be, nx, nu: (be[i], 0, 0)),
        ],
        out_specs=pl.BlockSpec((ns, blk, ln), lambda i, be, nx, nu: (i, 0, 0)),
        scratch_shapes=[pltpu.VMEM((d, f2), w1.dtype), pltpu.VMEM((f, d), w2.dtype),
                        pltpu.VMEM((d, f2), BF16), pltpu.VMEM((f, d), BF16),
                        pltpu.SemaphoreType.DMA((2,))],
    )
    return pl.pallas_call(
        _moe_kernel,
        grid_spec=grid_spec,
        out_shape=jax.ShapeDtypeStruct(xs.shape, F32),
        compiler_params=pltpu.CompilerParams(
            dimension_semantics=("arbitrary",), vmem_limit_bytes=VMEM_LIMIT),
        name="moe_experts",
    )(block_e, next_e, n_used, xs, w1, b1.reshape(ne, 1, f2), w2, b2.reshape(ne, 1, d))


def _final_kernel(row_ref, ys_hbm, h_ref, route_ref, p_ref, npl_ref, wg_ref, wp_ref, nfin_ref, out_ref,
                  gbuf, sems, *, last_layer):
    i = pl.program_id(0)
    tm = h_ref.shape[0]
    ns = gbuf.shape[2]
    cur = lax.rem(i, 2)

    def gather(step, buf):
        def issue(r, c):
            for kk in range(TOP_K):
                slot = row_ref[(step * tm + r) * TOP_K + kk]
                src = ys_hbm.at[pl.ds(_slot_block(slot) * ns, ns), pl.ds(_slot_row(slot), 1), :]
                pltpu.make_async_copy(src, gbuf.at[buf, kk, :, pl.ds(r, 1), :],
                                      sems.at[buf]).start(priority=kk % 2)
            return c
        lax.fori_loop(0, tm, issue, 0, unroll=4)

    @pl.when(i == 0)
    def _():
        gather(0, 0)

    @pl.when(i + 1 < pl.num_programs(0))
    def _():
        gather(i + 1, 1 - cur)

    for kk in range(TOP_K):
        pltpu.make_async_copy(ys_hbm.at[pl.ds(0, ns), pl.ds(0, tm), :], gbuf.at[cur, kk], sems.at[cur]).wait()

    route = route_ref[...]
    h = h_ref[...]
    for kk in range(TOP_K):
        yk = jnp.concatenate([gbuf[cur, kk, s] for s in range(ns)], axis=1)
        h = h + route[:, kk:kk + 1] * yk
    u = _rmsnorm(h, npl_ref[...])
    gate = _sigmoid(_mm(u, wg_ref[...]))
    h = h + gate * _mm(p_ref[...], wp_ref[...])
    out_ref[...] = _rmsnorm(h, nfin_ref[...]) if last_layer else h


def _final(slots, ys, h1, route, pf, npl, wg, wp, nfin, tm, last_layer):
    n, d = h1.shape
    pd = pf.shape[1]
    grid_spec = pltpu.PrefetchScalarGridSpec(
        num_scalar_prefetch=1,
        grid=(n // tm,),
        in_specs=[
            pl.BlockSpec(memory_space=pl.ANY),
            pl.BlockSpec((tm, d), lambda i, s: (i, 0)),
            pl.BlockSpec((tm, LANES), lambda i, s: (i, 0)),
            pl.BlockSpec((tm, pd), lambda i, s: (i, 0)),
            pl.BlockSpec((1, d), lambda i, s: (0, 0)),
            pl.BlockSpec((d, d), lambda i, s: (0, 0)),
            pl.BlockSpec((pd, d), lambda i, s: (0, 0)),
            pl.BlockSpec((1, d), lambda i, s: (0, 0)),
        ],
        out_specs=pl.BlockSpec((tm, d), lambda i, s: (i, 0)),
        scratch_shapes=[pltpu.VMEM((2, TOP_K, d // ys.shape[2], tm, ys.shape[2]), F32),
                        pltpu.SemaphoreType.DMA((2,))],
    )
    return pl.pallas_call(
        functools.partial(_final_kernel, last_layer=last_layer),
        grid_spec=grid_spec,
        out_shape=jax.ShapeDtypeStruct((n, d), F32),
        compiler_params=pltpu.CompilerParams(
            dimension_semantics=("arbitrary",), vmem_limit_bytes=VMEM_LIMIT),
        name="final",
    )(slots, ys, h1, route, pf, npl, wg, wp, nfin)


def _routing(route, counts, n_experts):
    n = route.shape[0]
    blk = EXPERT_BLOCK
    nb = (n * TOP_K + n_experts * (blk - 1) + blk - 1) // blk
    counts = counts[0, :n_experts].astype(jnp.int32)
    pcounts = ((counts + blk - 1) // blk) * blk
    pend = jnp.cumsum(pcounts)
    pstart = pend - pcounts
    top_i = route[:, TOP_K:2 * TOP_K].astype(jnp.int32)
    rank = route[:, 2 * TOP_K:3 * TOP_K].astype(jnp.int32)
    onehot = top_i[:, :, None] == jnp.arange(n_experts, dtype=jnp.int32)[None, None, :]
    slots = jnp.sum(jnp.where(onehot, pstart[None, None, :], 0), axis=-1) + rank
    block_rows = jnp.arange(nb, dtype=jnp.int32) * blk
    block_e = jnp.sum(block_rows[:, None] >= pend[None, :], axis=1)
    block_e = jnp.clip(block_e, 0, n_experts - 1).astype(jnp.int32)
    n_used = (pend[-1] // blk).astype(jnp.int32).reshape(1)
    blocks = jnp.arange(nb, dtype=jnp.int32)
    onehot_b = block_e[:, None] == jnp.arange(n_experts, dtype=jnp.int32)[None, :]
    run_end = jnp.sum(jnp.where(onehot_b, pend[None, :], 0), axis=1) // blk
    run_end = jnp.where(blocks >= n_used[0], nb, run_end)
    follow = jnp.sum(jnp.where(blocks[None, :] == run_end[:, None], block_e[None, :], 0), axis=1)
    next_e = jnp.where((run_end < nb) & (follow != block_e), follow, -1).astype(jnp.int32)
    pad_blocks = ((blocks + 1 == run_end) | (blocks >= n_used[0])).astype(jnp.int32)
    return block_e, next_e, n_used, slots.reshape(-1).astype(jnp.int32), pad_blocks


def kernel(x, p, norm_mix, w_in, shift_mu, rw_w0, rw_w2, rw_a0, rw_a2, rw_g2, rw_kk, rw_ka, rw_rk,
           rw_gn_w, rw_gn_b, gla_gk2, gla_gk_b, gla_norm, w_out, norm_ffn, w_router, b_router,
           w1, b1, w2, b2, norm_ple, w_ple_gate, w_ple, norm_final):
    bsz, seq, d = x.shape
    n = bsz * seq
    depth = w_in.shape[0]
    n_experts = w_router.shape[-1]
    tm = 256
    tm_route = min(512, n)
    h = x.reshape(n, d)
    for l in range(depth):
        w_rw = w_in[l][:, :RW_COLS].astype(BF16)
        w_gla = jnp.pad(w_in[l][:, RW_COLS:], ((0, 0), (0, LANES - GLA_GATE_RANK))).astype(BF16)
        rw_proj, gla_proj = _in_proj(h, norm_mix[l].reshape(1, d), w_rw, w_gla, tm_route)
        y_rw = _rwkv(rw_proj.reshape(bsz, seq, -1), shift_mu[l], rw_w0[l], rw_w2[l], rw_a0[l], rw_a2[l],
                     rw_g2[l], rw_kk[l], rw_ka[l], rw_rk[l], rw_gn_w[l], rw_gn_b[l])
        y_gla = _gla(gla_proj.reshape(bsz, seq, -1), gla_gk2[l], gla_gk_b[l], gla_norm[l])

        wr = jnp.pad(w_router[l], ((0, 0), (0, LANES - n_experts)))
        wr_hi = wr.astype(BF16)
        wr_lo = (wr - wr_hi.astype(F32)).astype(BF16)
        br = jnp.pad(b_router[l], (0, LANES - n_experts), constant_values=NEG_BIG).reshape(1, LANES)
        wo = w_out[l].astype(BF16)
        h1, u3, route, counts = _out_proj(y_rw.reshape(n, -1), y_gla.reshape(n, -1), h, wo[:RW_WIDTH],
                                          wo[RW_WIDTH:], norm_ffn[l].reshape(1, d), wr_hi, wr_lo, br, tm_route)

        ns = d // LANES
        block_e, next_e, n_used, slots, pad_blocks = _routing(route, counts, n_experts)
        xs = _dispatch(slots, pad_blocks, u3, tm_route, ns)
        ys = _moe_experts(block_e, next_e, n_used, xs, w1[l], b1[l], w2[l], b2[l])
        h = _final(slots, ys, h1, route, p[l].reshape(n, -1), norm_ple[l].reshape(1, d),
                   w_ple_gate[l].astype(BF16), w_ple[l].astype(BF16), norm_final.reshape(1, d), tm, l == depth - 1)
    return h.reshape(bsz, seq, d)
```

```python
import functools
import math

import jax
import jax.numpy as jnp
from jax import lax
from jax.experimental import pallas as pl
from jax.experimental.pallas import tpu as pltpu

F32 = jnp.float32
BF16 = jnp.bfloat16

CHUNK = 64
RW_HEADS = 8
RW_HEAD_DIM = 64
RW_WIDTH = RW_HEADS * RW_HEAD_DIM
RW_DECAY_LORA = 64
RW_ICLR_LORA = 64
RW_GATE_LORA = 128
RW_COLS = 3 * RW_WIDTH + RW_DECAY_LORA + RW_ICLR_LORA + RW_GATE_LORA
RW_GN_EPS = 64e-5
GLA_HEADS = 4
GLA_KEY_DIM = 64
GLA_VAL_DIM = 128
GLA_QK_WIDTH = GLA_HEADS * GLA_KEY_DIM
GLA_V_WIDTH = GLA_HEADS * GLA_VAL_DIM
GLA_GATE_RANK = 16
GLA_GATE_TEMP = 16.0
GLA_SUB = 8
LANES = 128
GLA_COLS_PAD = 2 * GLA_QK_WIDTH + 2 * GLA_V_WIDTH + LANES
TOP_K = 4
EXPERT_BLOCK = 256
SWIGLU_ALPHA = 1.702
SWIGLU_LIMIT = 7.0
NORM_EPS = 1e-6
NEG_BIG = -1e30
VMEM_LIMIT = 56 * 1024 * 1024


def _mm(a, b):
    return jnp.dot(a.astype(BF16), b.astype(BF16), preferred_element_type=F32)


def _mm_nt(a, b):
    return lax.dot_general(a.astype(BF16), b.astype(BF16), (((1,), (1,)), ((), ())),
                           preferred_element_type=F32)


def _mm_tn(a, b):
    return lax.dot_general(a.astype(BF16), b.astype(BF16), (((0,), (0,)), ((), ())),
                           preferred_element_type=F32)


def _split(a, n):
    parts = []
    rem = a
    for _ in range(n):
        p = rem.astype(BF16)
        parts.append(p)
        rem = rem - p.astype(F32)
    return parts


def _mm_lhs_split(a, b_bf16, n):
    out = None
    for p in _split(a, n):
        t = jnp.dot(p, b_bf16, preferred_element_type=F32)
        out = t if out is None else out + t
    return out


def _mm_rhs_split(a_bf16, b, n):
    out = None
    for p in _split(b, n):
        t = jnp.dot(a_bf16, p, preferred_element_type=F32)
        out = t if out is None else out + t
    return out


def _rmsnorm(x, g):
    return x * lax.rsqrt(jnp.mean(x * x, axis=-1, keepdims=True) + NORM_EPS) * g


def _softplus(x):
    return jnp.maximum(x, 0.0) + jnp.log(1.0 + jnp.exp(-jnp.abs(x)))


def _sigmoid(x):
    return 1.0 / (1.0 + jnp.exp(-x))


def _log2(n):
    assert n & (n - 1) == 0
    return n.bit_length() - 1


def _head_keep(rows, head_lanes, dtype):
    lane_head = lax.shift_right_logical(lax.broadcasted_iota(jnp.int32, (rows, LANES), 1), _log2(head_lanes))
    return [jnp.where(lane_head == j, 1.0, 0.0).astype(dtype) for j in range(LANES // head_lanes)]


def _block_diag_fn(rows, n_heads, head_lanes):
    n_tiles = n_heads * head_lanes // LANES
    zero = jnp.zeros((rows, LANES), BF16)
    keep = _head_keep(rows, head_lanes, BF16) if head_lanes < LANES else None

    def f(y):
        yb = y.astype(BF16)
        blocks = []
        for hd in range(n_heads):
            tiles = [zero] * n_tiles
            if keep is not None:
                t = hd * head_lanes // LANES
                tiles[t] = yb[:, t * LANES:(t + 1) * LANES] * keep[hd % len(keep)]
            else:
                for t in range(hd * head_lanes // LANES, (hd + 1) * head_lanes // LANES):
                    tiles[t] = yb[:, t * LANES:(t + 1) * LANES]
            blocks.append(jnp.concatenate(tiles, axis=1))
        return jnp.concatenate(blocks, axis=0)
    return f


def _diag_blocks_fn(rows, n_heads, head_lanes):
    assert head_lanes < LANES
    keep = _head_keep(rows, head_lanes, F32)
    per = len(keep)

    def f(full):
        tiles = []
        for t in range(n_heads * head_lanes // LANES):
            acc = None
            for j in range(per):
                hd = t * per + j
                term = full[hd * rows:(hd + 1) * rows, t * LANES:(t + 1) * LANES] * keep[j]
                acc = term if acc is None else acc + term
            tiles.append(acc)
        return jnp.concatenate(tiles, axis=1)
    return f


def _tri(n, strict):
    r = lax.broadcasted_iota(jnp.int32, (n, n), 0)
    c = lax.broadcasted_iota(jnp.int32, (n, n), 1)
    return (r > c) if strict else (r >= c)


def _inproj_kernel(x_ref, g_ref, wr_ref, wg_ref, rw_ref, gla_ref):
    u = _rmsnorm(x_ref[...], g_ref[...]).astype(BF16)
    rw_ref[...] = jnp.dot(u, wr_ref[...], preferred_element_type=F32)
    gla_ref[...] = jnp.dot(u, wg_ref[...], preferred_element_type=F32)


def _in_proj(xf, g, w_rw, w_gla, tm):
    n, d = xf.shape
    return pl.pallas_call(
        _inproj_kernel,
        grid=(n // tm,),
        in_specs=[
            pl.BlockSpec((tm, d), lambda i: (i, 0)),
            pl.BlockSpec((1, d), lambda i: (0, 0)),
            pl.BlockSpec(w_rw.shape, lambda i: (0, 0)),
            pl.BlockSpec(w_gla.shape, lambda i: (0, 0)),
        ],
        out_specs=[
            pl.BlockSpec((tm, w_rw.shape[1]), lambda i: (i, 0)),
            pl.BlockSpec((tm, w_gla.shape[1]), lambda i: (i, 0)),
        ],
        out_shape=[
            jax.ShapeDtypeStruct((n, w_rw.shape[1]), F32),
            jax.ShapeDtypeStruct((n, w_gla.shape[1]), F32),
        ],
        compiler_params=pltpu.CompilerParams(
            dimension_semantics=("arbitrary",), vmem_limit_bytes=VMEM_LIMIT),
        name="in_proj",
    )(xf, g, w_rw, w_gla)


def _rwkv_kernel_unpacked(x_ref, mu_ref, w0_ref, w2_ref, a0_ref, a2_ref, g2_ref, kkw_ref, ka_ref,
                          rk_ref, gnw_ref, gnb_ref, bd_ref, y_ref, state_ref, carry_ref, o_ref):
    C, H, D = CHUNK, RW_HEADS, RW_HEAD_DIM

    @pl.when(pl.program_id(1) == 0)
    def _():
        state_ref[...] = jnp.zeros_like(state_ref)
        carry_ref[...] = jnp.zeros_like(carry_ref)

    x = x_ref[0]
    row = lax.broadcasted_iota(jnp.int32, x.shape, 0)
    prev = jnp.where(row == 0, carry_ref[...], pltpu.roll(x, 1, axis=0))
    carry_ref[...] = x[C - 1:C, :]
    h = x + (prev - x) * mu_ref[...]

    W = RW_WIDTH
    r = h[:, 0:W]
    k = h[:, W:2 * W]
    v = h[:, 2 * W:3 * W]
    o0 = 3 * W
    dw = h[:, o0:o0 + RW_DECAY_LORA]
    da = h[:, o0 + RW_DECAY_LORA:o0 + RW_DECAY_LORA + RW_ICLR_LORA]
    dg = h[:, o0 + RW_DECAY_LORA + RW_ICLR_LORA:]

    bd = bd_ref[...]

    def seg_sum(t):
        return _mm_lhs_split(t, bd, 2)

    w_log = -_softplus(-(w0_ref[...] + _mm(jnp.tanh(dw), w2_ref[...]))) - 0.5
    lw = -jnp.exp(w_log)
    iclr = _sigmoid(a0_ref[...] + _mm(da, a2_ref[...]))
    gate = _mm(_sigmoid(dg), g2_ref[...])

    kk = k * kkw_ref[...]
    kk = kk / jnp.maximum(jnp.sqrt(seg_sum(kk * kk)), 1e-12)
    k2 = k * (1.0 + (iclr - 1.0) * ka_ref[...])

    cw = _mm_rhs_split(_tri(C, False).astype(BF16), lw, 3)
    cw_last = cw[C - 1:C, :]
    e_cw = jnp.exp(cw)
    e_ncw = jnp.exp(-cw)
    e_rem = jnp.exp(cw_last - cw)
    kka = kk * iclr
    a_t = -kk * jnp.exp(cw - lw)
    r_t = r * e_cw
    b_t = kka * e_ncw
    k_t = k2 * e_ncw
    b_h = kka * e_rem
    k_h = k2 * e_rem
    w_c = jnp.exp(cw_last)

    strict = _tri(C, True)
    incl = _tri(C, False)
    eye = jnp.where(_tri(C, False) & ~strict, 1.0, 0.0).astype(F32)

    hs = range(H)
    sls = [slice(hd * D, (hd + 1) * D) for hd in hs]
    states = [state_ref[hd] for hd in hs]
    ah = [a_t[:, sl] for sl in sls]
    rh = [r_t[:, sl] for sl in sls]
    bh = [b_t[:, sl] for sl in sls]
    kh = [k_t[:, sl] for sl in sls]
    vh = [v[:, sl] for sl in sls]
    a_ab = [jnp.where(strict, _mm_nt(ah[i], bh[i]), 0.0) for i in hs]
    a_ak = [jnp.where(strict, _mm_nt(ah[i], kh[i]), 0.0) for i in hs]
    a_rb = [jnp.where(incl, _mm_nt(rh[i], bh[i]), 0.0) for i in hs]
    a_rk = [jnp.where(incl, _mm_nt(rh[i], kh[i]), 0.0) for i in hs]
    p = [_mm(a_ab[i], a_ab[i]) for i in hs]
    q = [eye + a_ab[i] for i in hs]
    for _ in range(4):
        pq = [_mm(p[i], q[i]) for i in hs]
        p = [_mm(p[i], p[i]) for i in hs]
        q = [q[i] + pq[i] for i in hs]
    pq = [_mm(p[i], q[i]) for i in hs]
    t_inv = [q[i] + pq[i] for i in hs]
    akv = [_mm(a_ak[i], vh[i]) for i in hs]
    u0 = [_mm(t_inv[i], akv[i]) for i in hs]
    a_hat = [_mm(t_inv[i], ah[i]) for i in hs]
    u = [_mm_nt(a_hat[i], states[i]) + u0[i] for i in hs]
    o_heads = [_mm_nt(rh[i], states[i]) + _mm(a_rb[i], u[i]) + _mm(a_rk[i], vh[i]) for i in hs]
    new_states = [states[i] * w_c[:, sls[i]] + _mm_tn(u[i], b_h[:, sls[i]]) + _mm_tn(vh[i], k_h[:, sls[i]])
                  for i in hs]
    for hd in hs:
        state_ref[hd] = new_states[hd]
        o_ref[:, sls[hd]] = o_heads[hd]

    o = o_ref[...]
    inv_d = 1.0 / D
    mean = seg_sum(o) * inv_d
    dlt = o - mean
    var = seg_sum(dlt * dlt) * inv_d
    o = dlt * lax.rsqrt(var + RW_GN_EPS) * gnw_ref[...] + gnb_ref[...]
    bonus = seg_sum(r * k2 * rk_ref[...]) * v
    y_ref[0] = ((o + bonus) * gate).astype(y_ref.dtype)


def _rwkv_unpacked(rw_proj, mu, w0, w2, a0, a2, g2, kkw, ka, rk, gnw, gnb):
    b, s, cols = rw_proj.shape
    W = RW_WIDTH
    bd = jnp.kron(jnp.eye(RW_HEADS, dtype=F32), jnp.ones((RW_HEAD_DIM, RW_HEAD_DIM), F32)).astype(BF16)
    row = lambda a: a.reshape(1, -1)
    full = lambda a: pl.BlockSpec(a.shape, lambda i, j: (0,) * a.ndim)
    args = [row(mu), row(w0), w2.astype(BF16), row(a0), a2.astype(BF16), g2.astype(BF16),
            row(kkw), row(ka), row(rk), row(gnw), row(gnb), bd]
    return pl.pallas_call(
        _rwkv_kernel_unpacked,
        grid=(b, s // CHUNK),
        in_specs=[pl.BlockSpec((1, CHUNK, cols), lambda i, j: (i, j, 0))] + [full(a) for a in args],
        out_specs=pl.BlockSpec((1, CHUNK, W), lambda i, j: (i, j, 0)),
        out_shape=jax.ShapeDtypeStruct((b, s, W), BF16),
        scratch_shapes=[
            pltpu.VMEM((RW_HEADS, RW_HEAD_DIM, RW_HEAD_DIM), F32),
            pltpu.VMEM((1, cols), F32),
            pltpu.VMEM((CHUNK, W), F32),
        ],
        compiler_params=pltpu.CompilerParams(
            dimension_semantics=("arbitrary", "arbitrary"), vmem_limit_bytes=VMEM_LIMIT),
        name="rwkv7",
    )(rw_proj, *args)


RW_GROUP = 4
RW_GW = RW_GROUP * RW_HEAD_DIM
RW_STEP_CHUNKS = 2
RW_DECAY_SCALE = math.exp(-0.5)


def _rwkv_kernel(x_ref, mu_ref, w0_ref, w2_ref, a0_ref, a2_ref, g2_ref, kkw_ref, ka_ref,
                 rk_ref, gnw_ref, gnb_ref, bd_ref, y_ref, state_ref, carry_ref):
    C, D, W, GW = CHUNK, RW_HEAD_DIM, RW_WIDTH, RW_GW
    nb, ct = x_ref.shape[0], x_ref.shape[1]
    nch = ct // C
    R = nb * ct

    @pl.when(pl.program_id(0) == 0)
    def _():
        state_ref[...] = jnp.zeros_like(state_ref)
        carry_ref[...] = jnp.zeros_like(carry_ref)

    x = x_ref[...].reshape(R, x_ref.shape[2])
    row = lax.broadcasted_iota(jnp.int32, x.shape, 0)
    prev = pltpu.roll(x, 1, axis=0)
    for b in range(nb):
        prev = jnp.where(row == b * ct, carry_ref[b:b + 1, :], prev)
        carry_ref[b:b + 1, :] = x[(b + 1) * ct - 1:(b + 1) * ct, :]
    h = x + (prev - x) * mu_ref[...]

    r = h[:, 0:W]
    k = h[:, W:2 * W]
    v = h[:, 2 * W:3 * W]
    o0 = 3 * W
    dw = h[:, o0:o0 + RW_DECAY_LORA]
    da = h[:, o0 + RW_DECAY_LORA:o0 + RW_DECAY_LORA + RW_ICLR_LORA]
    dg = h[:, o0 + RW_DECAY_LORA + RW_ICLR_LORA:]

    bd_g = bd_ref[...]

    def seg_sum(t):
        return jnp.concatenate(
            [_mm_lhs_split(t[:, g * GW:(g + 1) * GW], bd_g, 2) for g in range(W // GW)], axis=1)

    lw = -RW_DECAY_SCALE * _sigmoid(w0_ref[...] + _mm(jnp.tanh(dw), w2_ref[...]))
    iclr = _sigmoid(a0_ref[...] + _mm(da, a2_ref[...]))
    gate = _mm(_sigmoid(dg), g2_ref[...])

    kk = k * kkw_ref[...]
    kk = kk * jnp.minimum(lax.rsqrt(seg_sum(kk * kk)), 1e12)
    k2 = k * (1.0 + (iclr - 1.0) * ka_ref[...])

    rr = lax.broadcasted_iota(jnp.int32, (R, R), 0)
    cc = lax.broadcasted_iota(jnp.int32, (R, R), 1)
    tri_seq = jnp.where((rr >= cc) & (rr // C == cc // C), 1.0, 0.0).astype(BF16)
    cw = _mm_rhs_split(tri_seq, lw, 3)
    cw_last = jnp.concatenate(
        [jnp.broadcast_to(cw[(j + 1) * C - 1:(j + 1) * C, :], (C, W)) for j in range(R // C)], axis=0)
    e_cw = jnp.exp(cw)
    e_ncw = jnp.exp(-cw)
    e_rem = jnp.exp(cw_last - cw)
    kka = kk * iclr
    a_t = -kk * jnp.exp(cw - lw)
    r_t = r * e_cw
    b_t = kka * e_ncw
    k_t = k2 * e_ncw
    b_h = kka * e_rem
    k_h = k2 * e_rem
    w_c = jnp.exp(cw_last)

    ti = lax.broadcasted_iota(jnp.int32, (C, GW), 0)
    si = jnp.bitwise_and(lax.broadcasted_iota(jnp.int32, (C, GW), 1), D - 1)
    strict = ti > si
    incl = ti >= si
    eye = jnp.where(ti == si, 1.0, 0.0)

    bdiag = _block_diag_fn(C, RW_GROUP, D)
    diag_blocks = _diag_blocks_fn(D, RW_GROUP, D)

    ng = W // GW
    chains = [(b, g, ch) for ch in range(nch) for b in range(nb) for g in range(ng)]

    def part(t, c):
        b, g, ch = c
        r0 = b * ct + ch * C
        return t[r0:r0 + C, g * GW:(g + 1) * GW]

    n = range(len(chains))
    a4 = [part(a_t, c) for c in chains]
    r4 = [part(r_t, c) for c in chains]
    v4 = [part(v, c) for c in chains]
    ar = [jnp.concatenate([a4[i], r4[i]], axis=0) for i in n]
    bd_b = [bdiag(part(b_t, c)) for c in chains]
    bd_k = [bdiag(part(k_t, c)) for c in chains]
    bd_v = [bdiag(v4[i]) for i in n]
    m_b = [_mm_nt(ar[i], bd_b[i]) for i in n]
    m_k = [_mm_nt(ar[i], bd_k[i]) for i in n]
    a_ab = [jnp.where(strict, m_b[i][0:C], 0.0) for i in n]
    a_rb = [jnp.where(incl, m_b[i][C:2 * C], 0.0) for i in n]
    a_ak = [jnp.where(strict, m_k[i][0:C], 0.0) for i in n]
    a_rk = [jnp.where(incl, m_k[i][C:2 * C], 0.0) for i in n]
    akv = [_mm(a_ak[i], bd_v[i]) for i in n]
    o_kv = [_mm(a_rk[i], bd_v[i]) for i in n]
    p = [_mm(a_ab[i], bdiag(a_ab[i])) for i in n]
    q = [eye + a_ab[i] for i in n]
    for _ in range(4):
        qp = [_mm(jnp.concatenate([q[i], p[i]], axis=0), bdiag(p[i])) for i in n]
        q = [q[i] + qp[i][0:C] for i in n]
        p = [qp[i][C:2 * C] for i in n]
    t_inv = [q[i] + _mm(q[i], bdiag(p[i])) for i in n]
    u0 = [_mm(t_inv[i], bdiag(akv[i])) for i in n]
    a_hat = [_mm(t_inv[i], bdiag(a4[i])) for i in n]
    seqs = [(b, g) for b in range(nb) for g in range(ng)]
    s4 = [state_ref[b, g] for b, g in seqs]
    o_g = {}
    for ch in range(nch):
        idx = [chains.index((b, g, ch)) for b, g in seqs]
        bd_s = [bdiag(s) for s in s4]
        o_s = [_mm_nt(r4[i], bd_s[j]) for j, i in enumerate(idx)]
        u = [_mm_nt(a_hat[i], bd_s[j]) + u0[i] for j, i in enumerate(idx)]
        for j, i in enumerate(idx):
            o_g[chains[i]] = o_s[j] + _mm(a_rb[i], bdiag(u[j])) + o_kv[i]
        nxt = []
        for j, i in enumerate(idx):
            c = chains[i]
            uv = jnp.concatenate([u[j], v4[i]], axis=0)
            bk = jnp.concatenate([part(b_h, c), part(k_h, c)], axis=0)
            nxt.append(s4[j] * part(w_c, c)[0:D] + diag_blocks(_mm_tn(uv, bk)))
        s4 = nxt
    for j, (b, g) in enumerate(seqs):
        state_ref[b, g] = s4[j]

    o = jnp.concatenate(
        [jnp.concatenate([o_g[(b, g, ch)] for g in range(ng)], axis=1) for b in range(nb) for ch in range(nch)],
        axis=0)
    inv_d = 1.0 / D
    mean = seg_sum(o) * inv_d
    dlt = o - mean
    var = seg_sum(dlt * dlt) * inv_d
    o = dlt * lax.rsqrt(var + RW_GN_EPS) * gnw_ref[...] + gnb_ref[...]
    bonus = seg_sum(r * k2 * rk_ref[...]) * v
    y_ref[...] = ((o + bonus) * gate).astype(y_ref.dtype).reshape(y_ref.shape)


def _rwkv(rw_proj, mu, w0, w2, a0, a2, g2, kkw, ka, rk, gnw, gnb):
    b, s, cols = rw_proj.shape
    W = RW_WIDTH
    bd = jnp.kron(jnp.eye(RW_GROUP, dtype=F32), jnp.ones((RW_HEAD_DIM, RW_HEAD_DIM), F32)).astype(BF16)
    row = lambda a: a.reshape(1, -1)
    full = lambda a: pl.BlockSpec(a.shape, lambda j: (0,) * a.ndim)
    args = [row(mu), row(w0), w2.astype(BF16), row(a0), a2.astype(BF16), g2.astype(BF16),
            row(kkw), row(ka), row(rk), row(gnw), row(gnb), bd]
    ct = CHUNK * RW_STEP_CHUNKS
    return pl.pallas_call(
        _rwkv_kernel,
        grid=(s // ct,),
        in_specs=[pl.BlockSpec((b, ct, cols), lambda j: (0, j, 0))] + [full(a) for a in args],
        out_specs=pl.BlockSpec((b, ct, W), lambda j: (0, j, 0)),
        out_shape=jax.ShapeDtypeStruct((b, s, W), BF16),
        scratch_shapes=[
            pltpu.VMEM((b, W // RW_GW, RW_HEAD_DIM, RW_GW), F32),
            pltpu.VMEM((b, cols), F32),
        ],
        compiler_params=pltpu.CompilerParams(
            dimension_semantics=("arbitrary",), vmem_limit_bytes=VMEM_LIMIT),
        name="rwkv7",
    )(rw_proj, *args)


GLA_STEP_CHUNKS = 2


def _group_mask(rows, row_group, cols, col_group):
    r = lax.shift_right_logical(lax.broadcasted_iota(jnp.int32, (rows, cols), 0), _log2(row_group))
    c = lax.shift_right_logical(lax.broadcasted_iota(jnp.int32, (rows, cols), 1), _log2(col_group))
    return jnp.where(r == c, 1.0, 0.0).astype(BF16)


def _gla_kernel(x_ref, gk2_ref, gkb_ref, ng_ref, y_ref, state_ref):
    C, H, DK, DV, SB = CHUNK, GLA_HEADS, GLA_KEY_DIM, GLA_VAL_DIM, GLA_SUB
    QW, VW = GLA_QK_WIDTH, GLA_V_WIDTH
    nb, ct = x_ref.shape[0], x_ref.shape[1]
    nch = ct // C
    R = nb * ct

    @pl.when(pl.program_id(0) == 0)
    def _():
        state_ref[...] = jnp.zeros_like(state_ref)

    x = x_ref[...].reshape(R, x_ref.shape[2])
    q = x[:, 0:QW] * (DK ** -0.5)
    k = x[:, QW:2 * QW]
    v = x[:, 2 * QW:2 * QW + VW]
    g = x[:, 2 * QW + VW:2 * QW + 2 * VW]
    dgk = x[:, 2 * QW + 2 * VW:]

    la = -_softplus(-(_mm(dgk, gk2_ref[...]) + gkb_ref[...])) * (1.0 / GLA_GATE_TEMP)
    rr = lax.broadcasted_iota(jnp.int32, (R, R), 0)
    cc = lax.broadcasted_iota(jnp.int32, (R, R), 1)
    tri_seq = jnp.where((rr >= cc) & (rr // C == cc // C), 1.0, 0.0).astype(BF16)
    b = _mm_rhs_split(tri_seq, la, 3)
    b_last = jnp.concatenate(
        [jnp.broadcast_to(b[(j + 1) * C - 1:(j + 1) * C, :], (C, QW)) for j in range(R // C)], axis=0)
    q_e = q * jnp.exp(b)
    k_e = k * jnp.exp(b_last - b)
    w_c = jnp.exp(b_last)

    def tile_rows(y, n):
        return jnp.concatenate([y.astype(BF16)] * n, axis=0)

    units = [(bi, ch) for ch in range(nch) for bi in range(nb)]
    row0 = {u: u[0] * ct + u[1] * C for u in units}

    o_off = {u: None for u in units}
    for s in (C // 2, C // 4, C // 8):
        bd_keys = _block_diag_fn(s, H, DK)
        bd_vals = _block_diag_fn(s, H, DV)
        jobs = [(u, row0[u] + m * 2 * s) for u in units for m in range(C // (2 * s))]
        att = []
        for u, c0 in jobs:
            ref = b[c0 + s - 1:c0 + s]
            q_s = q[c0 + s:c0 + 2 * s] * jnp.exp(b[c0 + s:c0 + 2 * s] - ref)
            k_s = k[c0:c0 + s] * jnp.exp(ref - b[c0:c0 + s])
            att.append(_mm_nt(q_s, bd_keys(k_s)))
        outs = [_mm(att[j], bd_vals(v[c0:c0 + s])) for j, (u, c0) in enumerate(jobs)]
        zero = jnp.zeros((s, VW), F32)
        for u in units:
            pieces = []
            for j, (uj, c0) in enumerate(jobs):
                if uj == u:
                    pieces += [zero, outs[j]]
            level = jnp.concatenate(pieces, axis=0)
            o_off[u] = level if o_off[u] is None else o_off[u] + level

    nblk = C // SB
    ii = lax.broadcasted_iota(jnp.int32, (nblk, SB, SB, 2 * DK), 1)
    jj = lax.broadcasted_iota(jnp.int32, (nblk, SB, SB, 2 * DK), 2)
    causal4 = jj <= ii
    sel = _group_mask(C, 1, C * SB, SB)
    pair_ones = _group_mask(2 * DK, DK, 2 * DV, DV)
    att2 = {}
    for u in units:
        r0 = row0[u]
        for pr in range(H // 2):
            sl2 = slice(2 * pr * DK, 2 * (pr + 1) * DK)
            q2, k2, b2 = q[r0:r0 + C, sl2], k[r0:r0 + C, sl2], b[r0:r0 + C, sl2]
            dec = jnp.exp(jnp.where(
                causal4, b2.reshape(nblk, SB, 1, 2 * DK) - b2.reshape(nblk, 1, SB, 2 * DK), NEG_BIG))
            pw = q2.reshape(nblk, SB, 1, 2 * DK) * k2.reshape(nblk, 1, SB, 2 * DK) * dec
            att2[(u, pr)] = _mm(pw.reshape(C * SB, 2 * DK), pair_ones)
    o_diag = {}
    for u in units:
        r0 = row0[u]
        heads = []
        for hd in range(H):
            v_h = v[r0:r0 + C, hd * DV:(hd + 1) * DV]
            v_rep = jnp.broadcast_to(v_h.reshape(nblk, 1, SB, DV), (nblk, SB, SB, DV)).reshape(C * SB, DV)
            heads.append(_mm(sel, att2[(u, hd // 2)][:, (hd % 2) * DV:(hd % 2 + 1) * DV] * v_rep))
        o_diag[u] = jnp.concatenate(heads, axis=1)

    bd_state = _block_diag_fn(DV, H, DK)
    diag_state = _diag_blocks_fn(DV, H, DK)
    s4 = [state_ref[bi] for bi in range(nb)]
    o_int = {}
    for ch in range(nch):
        for bi in range(nb):
            r0 = row0[(bi, ch)]
            o_int[(bi, ch)] = _mm_nt(q_e[r0:r0 + C], bd_state(s4[bi]))
        nxt = []
        for bi in range(nb):
            r0 = row0[(bi, ch)]
            nxt.append(s4[bi] * w_c[r0:r0 + 1] + diag_state(_mm_tn(v[r0:r0 + C], k_e[r0:r0 + C])))
        s4 = nxt
    for bi in range(nb):
        state_ref[bi] = s4[bi]

    rows = []
    for bi in range(nb):
        for ch in range(nch):
            u = (bi, ch)
            o = o_int[u] + o_diag[u] + o_off[u]
            heads = []
            for hd in range(H):
                oh = o[:, hd * DV:(hd + 1) * DV]
                heads.append(oh * lax.rsqrt(jnp.mean(oh * oh, axis=-1, keepdims=True) + NORM_EPS))
            rows.append(jnp.concatenate(heads, axis=1))
    o = jnp.concatenate(rows, axis=0)
    y = o * ng_ref[...] * (g * _sigmoid(g))
    y_ref[...] = y.astype(y_ref.dtype).reshape(y_ref.shape)


def _gla(gla_proj, gk2, gkb, ng):
    b, s, cols = gla_proj.shape
    gk2p = jnp.zeros((LANES, GLA_QK_WIDTH), F32).at[:GLA_GATE_RANK].set(gk2).astype(BF16)
    args = [gk2p, gkb.reshape(1, -1), ng.reshape(1, -1)]
    full = lambda a: pl.BlockSpec(a.shape, lambda j: (0,) * a.ndim)
    ct = CHUNK * GLA_STEP_CHUNKS
    return pl.pallas_call(
        _gla_kernel,
        grid=(s // ct,),
        in_specs=[pl.BlockSpec((b, ct, cols), lambda j: (0, j, 0))] + [full(a) for a in args],
        out_specs=pl.BlockSpec((b, ct, GLA_V_WIDTH), lambda j: (0, j, 0)),
        out_shape=jax.ShapeDtypeStruct((b, s, GLA_V_WIDTH), BF16),
        scratch_shapes=[pltpu.VMEM((b, GLA_VAL_DIM, GLA_QK_WIDTH), F32)],
        compiler_params=pltpu.CompilerParams(
            dimension_semantics=("arbitrary",), vmem_limit_bytes=VMEM_LIMIT),
        name="gla",
    )(gla_proj, *args)


def _gla_kernel_unpacked(x_ref, gk2_ref, gkb_ref, ng_ref, y_ref, state_ref):
    C, H, DK, DV, SB = CHUNK, GLA_HEADS, GLA_KEY_DIM, GLA_VAL_DIM, GLA_SUB

    @pl.when(pl.program_id(1) == 0)
    def _():
        state_ref[...] = jnp.zeros_like(state_ref)

    x = x_ref[0]
    QW, VW = GLA_QK_WIDTH, GLA_V_WIDTH
    q = x[:, 0:QW] * (DK ** -0.5)
    k = x[:, QW:2 * QW]
    v = x[:, 2 * QW:2 * QW + VW]
    g = x[:, 2 * QW + VW:2 * QW + 2 * VW]
    dgk = x[:, 2 * QW + 2 * VW:]

    la = -_softplus(-(_mm(dgk, gk2_ref[...]) + gkb_ref[...])) * (1.0 / GLA_GATE_TEMP)
    b = _mm_rhs_split(_tri(C, False).astype(BF16), la, 3)
    b_last = b[C - 1:C, :]
    q_e = q * jnp.exp(b)
    k_e = k * jnp.exp(b_last - b)
    w_c = jnp.exp(b_last)

    ii = lax.broadcasted_iota(jnp.int32, (C // SB, SB, SB, 2 * DK), 1)
    jj = lax.broadcasted_iota(jnp.int32, (C // SB, SB, SB, 2 * DK), 2)
    causal4 = jj <= ii

    hs = range(H)
    nblk = C // SB
    sks = [slice(hd * DK, (hd + 1) * DK) for hd in hs]
    svs = [slice(hd * DV, (hd + 1) * DV) for hd in hs]
    states = [state_ref[hd] for hd in hs]
    qh = [q[:, sk] for sk in sks]
    kh = [k[:, sk] for sk in sks]
    bh = [b[:, sk] for sk in sks]
    vh = [v[:, sv] for sv in svs]
    o_inter = [_mm_nt(q_e[:, sks[i]], states[i]) for i in hs]
    new_states = [states[i] * w_c[:, sks[i]] + _mm_tn(vh[i], k_e[:, sks[i]]) for i in hs]
    att_off = {}
    for blk in range(1, nblk):
        r0 = blk * SB
        for i in hs:
            ref = bh[i][r0 - 1:r0]
            q_s = qh[i][r0:r0 + SB] * jnp.exp(bh[i][r0:r0 + SB] - ref)
            k_s = kh[i][0:r0] * jnp.exp(ref - bh[i][0:r0])
            att_off[(i, blk)] = _mm_nt(q_s, k_s)
    o_off = {key: _mm(att, vh[key[0]][0:key[1] * SB]) for key, att in att_off.items()}
    rsel = lax.broadcasted_iota(jnp.int32, (C, C * SB), 0)
    csel = lax.broadcasted_iota(jnp.int32, (C, C * SB), 1)
    sel = jnp.where((csel >= rsel * SB) & (csel < (rsel + 1) * SB), 1.0, 0.0).astype(BF16)
    drow = lax.broadcasted_iota(jnp.int32, (2 * DK, 2 * DV), 0)
    dcol = lax.broadcasted_iota(jnp.int32, (2 * DK, 2 * DV), 1)
    pair_ones = jnp.where((drow >= DK) == (dcol >= DV), 1.0, 0.0).astype(BF16)
    o_diag = []
    for pr in range(H // 2):
        sl2 = slice(2 * pr * DK, 2 * (pr + 1) * DK)
        q2, k2, b2 = q[:, sl2], k[:, sl2], b[:, sl2]
        dec = jnp.exp(jnp.where(causal4, b2.reshape(nblk, SB, 1, 2 * DK) - b2.reshape(nblk, 1, SB, 2 * DK),
                                NEG_BIG))
        pw = q2.reshape(nblk, SB, 1, 2 * DK) * k2.reshape(nblk, 1, SB, 2 * DK) * dec
        att2 = _mm(pw.reshape(C * SB, 2 * DK), pair_ones)
        for t in range(2):
            v_rep = jnp.broadcast_to(vh[2 * pr + t].reshape(nblk, 1, SB, DV), (nblk, SB, SB, DV))
            z = att2[:, t * DV:(t + 1) * DV] * v_rep.reshape(C * SB, DV)
            o_diag.append(_mm(sel, z))
    outs = []
    for i in hs:
        rows = [jnp.zeros((SB, DV), F32)] + [o_off[(i, blk)] for blk in range(1, nblk)]
        o = o_inter[i] + o_diag[i] + jnp.concatenate(rows, axis=0)
        outs.append(o * lax.rsqrt(jnp.mean(o * o, axis=-1, keepdims=True) + NORM_EPS))
    for hd in hs:
        state_ref[hd] = new_states[hd]
    o = jnp.concatenate(outs, axis=1)
    y = o * ng_ref[...] * (g * _sigmoid(g))
    y_ref[0] = y.astype(y_ref.dtype)


def _gla_unpacked(gla_proj, gk2, gkb, ng):
    b, s, cols = gla_proj.shape
    gk2p = jnp.zeros((LANES, GLA_QK_WIDTH), F32).at[:GLA_GATE_RANK].set(gk2).astype(BF16)
    args = [gk2p, gkb.reshape(1, -1), ng.reshape(1, -1)]
    full = lambda a: pl.BlockSpec(a.shape, lambda i, j: (0,) * a.ndim)
    return pl.pallas_call(
        _gla_kernel_unpacked,
        grid=(b, s // CHUNK),
        in_specs=[pl.BlockSpec((1, CHUNK, cols), lambda i, j: (i, j, 0))] + [full(a) for a in args],
        out_specs=pl.BlockSpec((1, CHUNK, GLA_V_WIDTH), lambda i, j: (i, j, 0)),
        out_shape=jax.ShapeDtypeStruct((b, s, GLA_V_WIDTH), BF16),
        scratch_shapes=[pltpu.VMEM((GLA_HEADS, GLA_VAL_DIM, GLA_KEY_DIM), F32)],
        compiler_params=pltpu.CompilerParams(
            dimension_semantics=("arbitrary", "arbitrary"), vmem_limit_bytes=VMEM_LIMIT),
        name="gla",
    )(gla_proj, *args)


def _outproj_kernel(yr_ref, yg_ref, x_ref, wor_ref, wog_ref, nf_ref, wrh_ref, wrl_ref, br_ref,
                    h_ref, u3_ref, route_ref, cnt_ref, carry_ref):
    tm, d = x_ref.shape

    @pl.when(pl.program_id(0) == 0)
    def _():
        carry_ref[...] = jnp.zeros_like(carry_ref)

    h = (x_ref[...] + jnp.dot(yr_ref[...], wor_ref[...], preferred_element_type=F32)
         + jnp.dot(yg_ref[...], wog_ref[...], preferred_element_type=F32))
    h_ref[...] = h
    u = _rmsnorm(h, nf_ref[...])
    for s in range(d // LANES):
        u3_ref[s] = u[:, s * LANES:(s + 1) * LANES]
    u_hi, u_lo = _split(u, 2)
    logits = (jnp.dot(u_hi, wrh_ref[...], preferred_element_type=F32)
              + jnp.dot(u_hi, wrl_ref[...], preferred_element_type=F32)
              + jnp.dot(u_lo, wrh_ref[...], preferred_element_type=F32)) + br_ref[...]
    lane = lax.broadcasted_iota(jnp.int32, logits.shape, 1)
    rest = logits
    picks, idxs, vals = [], [], []
    for r in range(TOP_K):
        m = jnp.max(rest, axis=-1, keepdims=True)
        idx = jnp.min(jnp.where(rest == m, lane, LANES), axis=-1, keepdims=True)
        pick = lane == idx
        picks.append(pick)
        idxs.append(idx)
        vals.append(m)
        rest = jnp.where(pick, -jnp.inf, rest)
    denom = jnp.ones_like(vals[0])
    for r in range(1, TOP_K):
        denom = denom + jnp.exp(vals[r] - vals[0])
    sel = jnp.zeros(logits.shape, F32)
    for pick in picks:
        sel = sel + jnp.where(pick, 1.0, 0.0)
    prefix = _mm(jnp.where(_tri(tm, True), 1.0, 0.0), sel) + carry_ref[...]
    carry_ref[...] = carry_ref[...] + jnp.sum(sel, axis=0, keepdims=True)
    cnt_ref[...] = carry_ref[...]
    route = jnp.zeros(logits.shape, F32)
    for r in range(TOP_K):
        gate = jnp.exp(vals[r] - vals[0]) / denom
        rank = jnp.sum(jnp.where(picks[r], prefix, 0.0), axis=-1, keepdims=True)
        route = jnp.where(lane == r, gate, route)
        route = jnp.where(lane == TOP_K + r, idxs[r].astype(F32), route)
        route = jnp.where(lane == 2 * TOP_K + r, rank, route)
    route_ref[...] = route


def _out_proj(y_rw, y_gla, xf, wo_r, wo_g, nf, wr_hi, wr_lo, br, tm):
    n, d = xf.shape
    full = lambda a: pl.BlockSpec(a.shape, lambda i: (0,) * a.ndim)
    tile = lambda w: pl.BlockSpec((tm, w), lambda i: (i, 0))
    return pl.pallas_call(
        _outproj_kernel,
        grid=(n // tm,),
        in_specs=[tile(y_rw.shape[1]), tile(y_gla.shape[1]), tile(d), full(wo_r), full(wo_g), full(nf),
                  full(wr_hi), full(wr_lo), full(br)],
        out_specs=[tile(d), pl.BlockSpec((d // LANES, tm, LANES), lambda i: (i, 0, 0)), tile(LANES),
                   pl.BlockSpec((1, LANES), lambda i: (0, 0))],
        out_shape=[jax.ShapeDtypeStruct((n, d), F32), jax.ShapeDtypeStruct((n // tm * (d // LANES), tm, LANES), F32),
                   jax.ShapeDtypeStruct((n, LANES), F32), jax.ShapeDtypeStruct((1, LANES), F32)],
        scratch_shapes=[pltpu.VMEM((1, LANES), F32)],
        compiler_params=pltpu.CompilerParams(
            dimension_semantics=("arbitrary",), vmem_limit_bytes=VMEM_LIMIT),
        name="out_proj",
    )(y_rw, y_gla, xf, wo_r, wo_g, nf, wr_hi, wr_lo, br)


assert EXPERT_BLOCK & (EXPERT_BLOCK - 1) == 0
_BLOCK_SHIFT = EXPERT_BLOCK.bit_length() - 1


def _slot_block(slot):
    return lax.shift_right_logical(slot, _BLOCK_SHIFT)


def _slot_row(slot):
    return jnp.bitwise_and(slot, EXPERT_BLOCK - 1)


_DISPATCH_BUFS = 3


def _dispatch_kernel(row_ref, pad_ref, u3_hbm, xs_hbm, ubuf, zbuf, in_sems, out_sems, zsem, *, tm, ns):
    i = pl.program_id(0)
    nsteps = pl.num_programs(0)
    nbuf = _DISPATCH_BUFS
    n_blocks = pad_ref.shape[0]

    @pl.when(i == 0)
    def _():
        zbuf[...] = jnp.zeros_like(zbuf)

        def zero_block(blk_i):
            return pltpu.make_async_copy(zbuf, xs_hbm.at[pl.ds(blk_i * ns, ns)], zsem)

        def start(blk_i, c):
            @pl.when(pad_ref[blk_i] != 0)
            def _():
                zero_block(blk_i).start()
            return c
        lax.fori_loop(0, n_blocks, start, 0)

        def finish(blk_i, c):
            @pl.when(pad_ref[blk_i] != 0)
            def _():
                zero_block(blk_i).wait()
            return c
        lax.fori_loop(0, n_blocks, finish, 0)

    def load(step, slot_):
        return pltpu.make_async_copy(u3_hbm.at[pl.ds(step * ns, ns)], ubuf.at[slot_], in_sems.at[slot_])

    def wait_scatter(slot_):
        for _ in range(TOP_K):
            pltpu.make_async_copy(u3_hbm.at[pl.ds(0, ns)], ubuf.at[slot_], out_sems.at[slot_]).wait()

    cur = lax.rem(i, nbuf)
    nxt = lax.rem(i + 1, nbuf)

    @pl.when(i == 0)
    def _():
        load(0, 0).start()

    @pl.when(i >= nbuf - 1)
    def _():
        wait_scatter(nxt)

    @pl.when(i + 1 < nsteps)
    def _():
        load(i + 1, nxt).start()

    load(i, cur).wait()

    def issue(r, c):
        src = ubuf.at[cur, :, pl.ds(r, 1), :]
        for kk in range(TOP_K):
            slot = row_ref[(i * tm + r) * TOP_K + kk]
            dst = xs_hbm.at[pl.ds(_slot_block(slot) * ns, ns), pl.ds(_slot_row(slot), 1), :]
            pltpu.make_async_copy(src, dst, out_sems.at[cur]).start(priority=kk % 2)
        return c
    lax.fori_loop(0, tm, issue, 0, unroll=4)

    @pl.when(i == nsteps - 1)
    def _():
        for back in range(nbuf - 1):
            @pl.when(i - back >= 0)
            def _():
                wait_scatter(lax.rem(i - back + nbuf, nbuf))


def _dispatch(slots, pad_blocks, u3, tm, ns):
    ln = u3.shape[2]
    n = u3.shape[0] // ns * tm
    blk = EXPERT_BLOCK
    grid_spec = pltpu.PrefetchScalarGridSpec(
        num_scalar_prefetch=2,
        grid=(n // tm,),
        in_specs=[pl.BlockSpec(memory_space=pl.ANY)],
        out_specs=pl.BlockSpec(memory_space=pl.ANY),
        scratch_shapes=[pltpu.VMEM((_DISPATCH_BUFS, ns, tm, ln), u3.dtype),
                        pltpu.VMEM((ns, blk, ln), u3.dtype),
                        pltpu.SemaphoreType.DMA((_DISPATCH_BUFS,)),
                        pltpu.SemaphoreType.DMA((_DISPATCH_BUFS,)),
                        pltpu.SemaphoreType.DMA(())],
    )
    return pl.pallas_call(
        functools.partial(_dispatch_kernel, tm=tm, ns=ns),
        grid_spec=grid_spec,
        out_shape=jax.ShapeDtypeStruct((pad_blocks.shape[0] * ns, blk, ln), u3.dtype),
        compiler_params=pltpu.CompilerParams(dimension_semantics=("arbitrary",), has_side_effects=True),
        name="moe_dispatch",
    )(slots, pad_blocks, u3)


def _moe_kernel(be_ref, nxt_ref, nused_ref, xs_ref, w1_hbm, b1_ref, w2_hbm, b2_ref, ys_ref,
                w1f, w2f, w1b, w2b, sems):
    i = pl.program_id(0)
    f = w2b.shape[0]
    ns = xs_ref.shape[0]
    e = be_ref[i]
    e_prev = be_ref[jnp.maximum(i - 1, 0)]

    def fetch(expert):
        return (pltpu.make_async_copy(w1_hbm.at[expert], w1f, sems.at[0]),
                pltpu.make_async_copy(w2_hbm.at[expert], w2f, sems.at[1]))

    @pl.when(i == 0)
    def _():
        for cp in fetch(e):
            cp.start()

    @pl.when(jnp.logical_or(i == 0, e != e_prev))
    def _():
        for cp in fetch(e):
            cp.wait()
        w1b[...] = w1f[...].astype(BF16)
        w2b[...] = w2f[...].astype(BF16)

        @pl.when(nxt_ref[i] >= 0)
        def _():
            for cp in fetch(nxt_ref[i]):
                cp.start()

    @pl.when(i < nused_ref[0])
    def _():
        xb = jnp.concatenate([xs_ref[s] for s in range(ns)], axis=1).astype(BF16)
        hgl = jnp.dot(xb, w1b[...], preferred_element_type=F32) + b1_ref[0]
        x_glu = jnp.minimum(hgl[:, :f], SWIGLU_LIMIT)
        x_lin = jnp.clip(hgl[:, f:], -SWIGLU_LIMIT, SWIGLU_LIMIT)
        act = (x_lin + 1.0) * (x_glu * _sigmoid(SWIGLU_ALPHA * x_glu))
        y = jnp.dot(act.astype(BF16), w2b[...], preferred_element_type=F32) + b2_ref[0]
        for s in range(ns):
            ys_ref[s] = y[:, s * LANES:(s + 1) * LANES]

    @pl.when(i >= nused_ref[0])
    def _():
        ys_ref[...] = jnp.zeros_like(ys_ref)


def _moe_experts(block_e, next_e, n_used, xs, w1, b1, w2, b2):
    _, blk, ln = xs.shape
    ne, d, f2 = w1.shape
    f = w2.shape[1]
    nb = block_e.shape[0]
    ns = d // ln
    grid_spec = pltpu.PrefetchScalarGridSpec(
        num_scalar_prefetch=3,
        grid=(nb,),
        in_specs=[
            pl.BlockSpec((ns, blk, ln), lambda i, be, nx, nu: (i, 0, 0)),
            pl.BlockSpec(memory_space=pl.ANY),
            pl.BlockSpec((1, 1, f2), lambda i, be, nx, nu: (be[i], 0, 0)),
            pl.BlockSpec(memory_space=pl.ANY),
            pl.BlockSpec((1, 1, d), lambda i, be, nx, nu: (be[i], 0, 0)),
        ],
        out_specs=pl.BlockSpec((ns, blk, ln), lambda i, be, nx, nu: (i, 0, 0)),
        scratch_shapes=[pltpu.VMEM((d, f2), w1.dtype), pltpu.VMEM((f, d), w2.dtype),
                        pltpu.VMEM((d, f2), BF16), pltpu.VMEM((f, d), BF16),
                        pltpu.SemaphoreType.DMA((2,))],
    )
    return pl.pallas_call(
        _moe_kernel,
        grid_spec=grid_spec,
        out_shape=jax.ShapeDtypeStruct(xs.shape, F32),
        compiler_params=pltpu.CompilerParams(
            dimension_semantics=("arbitrary",), vmem_limit_bytes=VMEM_LIMIT),
        name="moe_experts",
    )(block_e, next_e, n_used, xs, w1, b1.reshape(ne, 1, f2), w2, b2.reshape(ne, 1, d))


def _final_kernel(row_ref, ys_hbm, h_ref, route_ref, p_ref, npl_ref, wg_ref, wp_ref, nfin_ref, out_ref,
                  gbuf, sems, *, last_layer):
    i = pl.program_id(0)
    tm = h_ref.shape[0]
    ns = gbuf.shape[2]
    cur = lax.rem(i, 2)

    last = pl.num_programs(0) - 1

    def issue_row(step, buf, r):
        for kk in range(TOP_K):
            slot = row_ref[(step * tm + r) * TOP_K + kk]
            src = ys_hbm.at[pl.ds(_slot_block(slot) * ns, ns), pl.ds(_slot_row(slot), 1), :]
            pltpu.make_async_copy(src, gbuf.at[buf, kk, :, pl.ds(r, 1), :],
                                  sems.at[buf]).start(priority=kk % 2)

    def wait_tile(buf):
        rows = min(tm, EXPERT_BLOCK)
        for kk in range(TOP_K):
            for part in range(tm // rows):
                pltpu.make_async_copy(ys_hbm.at[pl.ds(0, ns), pl.ds(0, rows), :],
                                      gbuf.at[buf, kk, :, pl.ds(part * rows, rows), :], sems.at[buf]).wait()

    def gather(step, buf):
        def issue(r, c):
            issue_row(step, buf, r)
            return c
        lax.fori_loop(0, tm, issue, 0, unroll=4)

    @pl.when(i == 0)
    def _():
        gather(0, 0)

    @pl.when(i < last)
    def _():
        gather(i + 1, 1 - cur)

    wait_tile(cur)

    route = route_ref[...]
    h = h_ref[...]
    for kk in range(TOP_K):
        yk = jnp.concatenate([gbuf[cur, kk, s] for s in range(ns)], axis=1)
        h = h + route[:, kk:kk + 1] * yk
    u = _rmsnorm(h, npl_ref[...])
    gate = _sigmoid(_mm(u, wg_ref[...]))
    h = h + gate * _mm(p_ref[...], wp_ref[...])
    out_ref[...] = _rmsnorm(h, nfin_ref[...]) if last_layer else h


def _final(slots, ys, h1, route, pf, npl, wg, wp, nfin, tm, last_layer):
    n, d = h1.shape
    pd = pf.shape[1]
    grid_spec = pltpu.PrefetchScalarGridSpec(
        num_scalar_prefetch=1,
        grid=(n // tm,),
        in_specs=[
            pl.BlockSpec(memory_space=pl.ANY),
            pl.BlockSpec((tm, d), lambda i, s: (i, 0)),
            pl.BlockSpec((tm, LANES), lambda i, s: (i, 0)),
            pl.BlockSpec((tm, pd), lambda i, s: (i, 0)),
            pl.BlockSpec((1, d), lambda i, s: (0, 0)),
            pl.BlockSpec((d, d), lambda i, s: (0, 0)),
            pl.BlockSpec((pd, d), lambda i, s: (0, 0)),
            pl.BlockSpec((1, d), lambda i, s: (0, 0)),
        ],
        out_specs=pl.BlockSpec((tm, d), lambda i, s: (i, 0)),
        scratch_shapes=[pltpu.VMEM((2, TOP_K, d // ys.shape[2], tm, ys.shape[2]), F32),
                        pltpu.SemaphoreType.DMA((2,))],
    )
    return pl.pallas_call(
        functools.partial(_final_kernel, last_layer=last_layer),
        grid_spec=grid_spec,
        out_shape=jax.ShapeDtypeStruct((n, d), F32),
        compiler_params=pltpu.CompilerParams(
            dimension_semantics=("arbitrary",), vmem_limit_bytes=VMEM_LIMIT),
        name="final",
    )(slots, ys, h1, route, pf, npl, wg, wp, nfin)


def _routing(route, counts, n_experts):
    n = route.shape[0]
    blk = EXPERT_BLOCK
    nb = (n * TOP_K + n_experts * (blk - 1) + blk - 1) // blk
    counts = counts[0, :n_experts].astype(jnp.int32)
    pcounts = ((counts + blk - 1) // blk) * blk
    pend = jnp.cumsum(pcounts)
    pstart = pend - pcounts
    top_i = route[:, TOP_K:2 * TOP_K].astype(jnp.int32)
    rank = route[:, 2 * TOP_K:3 * TOP_K].astype(jnp.int32)
    onehot = top_i[:, :, None] == jnp.arange(n_experts, dtype=jnp.int32)[None, None, :]
    slots = jnp.sum(jnp.where(onehot, pstart[None, None, :], 0), axis=-1) + rank
    block_rows = jnp.arange(nb, dtype=jnp.int32) * blk
    block_e = jnp.sum(block_rows[:, None] >= pend[None, :], axis=1)
    block_e = jnp.clip(block_e, 0, n_experts - 1).astype(jnp.int32)
    n_used = (pend[-1] // blk).astype(jnp.int32).reshape(1)
    blocks = jnp.arange(nb, dtype=jnp.int32)
    onehot_b = block_e[:, None] == jnp.arange(n_experts, dtype=jnp.int32)[None, :]
    run_end = jnp.sum(jnp.where(onehot_b, pend[None, :], 0), axis=1) // blk
    run_end = jnp.where(blocks >= n_used[0], nb, run_end)
    follow = jnp.sum(jnp.where(blocks[None, :] == run_end[:, None], block_e[None, :], 0), axis=1)
    next_e = jnp.where((run_end < nb) & (follow != block_e), follow, -1).astype(jnp.int32)
    pad_blocks = ((blocks + 1 == run_end) | (blocks >= n_used[0])).astype(jnp.int32)
    return block_e, next_e, n_used, slots.reshape(-1).astype(jnp.int32), pad_blocks


def kernel(x, p, norm_mix, w_in, shift_mu, rw_w0, rw_w2, rw_a0, rw_a2, rw_g2, rw_kk, rw_ka, rw_rk,
           rw_gn_w, rw_gn_b, gla_gk2, gla_gk_b, gla_norm, w_out, norm_ffn, w_router, b_router,
           w1, b1, w2, b2, norm_ple, w_ple_gate, w_ple, norm_final):
    bsz, seq, d = x.shape
    n = bsz * seq
    depth = w_in.shape[0]
    n_experts = w_router.shape[-1]
    tm = 256
    tm_route = min(512, n)
    h = x.reshape(n, d)
    for l in range(depth):
        w_rw = w_in[l][:, :RW_COLS].astype(BF16)
        w_gla = jnp.pad(w_in[l][:, RW_COLS:], ((0, 0), (0, LANES - GLA_GATE_RANK))).astype(BF16)
        rw_proj, gla_proj = _in_proj(h, norm_mix[l].reshape(1, d), w_rw, w_gla, tm_route)
        y_rw = _rwkv(rw_proj.reshape(bsz, seq, -1), shift_mu[l], rw_w0[l], rw_w2[l], rw_a0[l], rw_a2[l],
                     rw_g2[l], rw_kk[l], rw_ka[l], rw_rk[l], rw_gn_w[l], rw_gn_b[l])
        y_gla = _gla(gla_proj.reshape(bsz, seq, -1), gla_gk2[l], gla_gk_b[l], gla_norm[l])

        wr = jnp.pad(w_router[l], ((0, 0), (0, LANES - n_experts)))
        wr_hi = wr.astype(BF16)
        wr_lo = (wr - wr_hi.astype(F32)).astype(BF16)
        br = jnp.pad(b_router[l], (0, LANES - n_experts), constant_values=NEG_BIG).reshape(1, LANES)
        wo = w_out[l].astype(BF16)
        h1, u3, route, counts = _out_proj(y_rw.reshape(n, -1), y_gla.reshape(n, -1), h, wo[:RW_WIDTH],
                                          wo[RW_WIDTH:], norm_ffn[l].reshape(1, d), wr_hi, wr_lo, br, tm_route)

        ns = d // LANES
        block_e, next_e, n_used, slots, pad_blocks = _routing(route, counts, n_experts)
        xs = _dispatch(slots, pad_blocks, u3, tm_route, ns)
        ys = _moe_experts(block_e, next_e, n_used, xs, w1[l], b1[l], w2[l], b2[l])
        h = _final(slots, ys, h1, route, p[l].reshape(n, -1), norm_ple[l].reshape(1, d),
                   w_ple_gate[l].astype(BF16), w_ple[l].astype(BF16), norm_final.reshape(1, d), tm_route,
                   l == depth - 1)
    return h.reshape(bsz, seq, d)
```

```python
import functools
import math

import jax
import jax.numpy as jnp
from jax import lax
from jax.experimental import pallas as pl
from jax.experimental.pallas import tpu as pltpu

F32 = jnp.float32
BF16 = jnp.bfloat16

CHUNK = 64
RW_HEADS = 8
RW_HEAD_DIM = 64
RW_WIDTH = RW_HEADS * RW_HEAD_DIM
RW_DECAY_LORA = 64
RW_ICLR_LORA = 64
RW_GATE_LORA = 128
RW_COLS = 3 * RW_WIDTH + RW_DECAY_LORA + RW_ICLR_LORA + RW_GATE_LORA
RW_GN_EPS = 64e-5
GLA_HEADS = 4
GLA_KEY_DIM = 64
GLA_VAL_DIM = 128
GLA_QK_WIDTH = GLA_HEADS * GLA_KEY_DIM
GLA_V_WIDTH = GLA_HEADS * GLA_VAL_DIM
GLA_GATE_RANK = 16
GLA_GATE_TEMP = 16.0
GLA_SUB = 8
LANES = 128
GLA_COLS_PAD = 2 * GLA_QK_WIDTH + 2 * GLA_V_WIDTH + LANES
TOP_K = 4
EXPERT_BLOCK = 256
SWIGLU_ALPHA = 1.702
SWIGLU_LIMIT = 7.0
NORM_EPS = 1e-6
NEG_BIG = -1e30
VMEM_LIMIT = 56 * 1024 * 1024


def _mm(a, b):
    return jnp.dot(a.astype(BF16), b.astype(BF16), preferred_element_type=F32)


def _mm_nt(a, b):
    return lax.dot_general(a.astype(BF16), b.astype(BF16), (((1,), (1,)), ((), ())),
                           preferred_element_type=F32)


def _mm_tn(a, b):
    return lax.dot_general(a.astype(BF16), b.astype(BF16), (((0,), (0,)), ((), ())),
                           preferred_element_type=F32)


def _split(a, n):
    parts = []
    rem = a
    for _ in range(n):
        p = rem.astype(BF16)
        parts.append(p)
        rem = rem - p.astype(F32)
    return parts


def _mm_lhs_split(a, b_bf16, n):
    out = None
    for p in _split(a, n):
        t = jnp.dot(p, b_bf16, preferred_element_type=F32)
        out = t if out is None else out + t
    return out


def _mm_rhs_split(a_bf16, b, n):
    out = None
    for p in _split(b, n):
        t = jnp.dot(a_bf16, p, preferred_element_type=F32)
        out = t if out is None else out + t
    return out


def _rmsnorm(x, g):
    return x * lax.rsqrt(jnp.mean(x * x, axis=-1, keepdims=True) + NORM_EPS) * g


def _softplus(x):
    return jnp.maximum(x, 0.0) + jnp.log(1.0 + jnp.exp(-jnp.abs(x)))


def _sigmoid(x):
    return 1.0 / (1.0 + jnp.exp(-x))


def _log2(n):
    assert n & (n - 1) == 0
    return n.bit_length() - 1


def _head_keep(rows, head_lanes, dtype):
    lane_head = lax.shift_right_logical(lax.broadcasted_iota(jnp.int32, (rows, LANES), 1), _log2(head_lanes))
    return [jnp.where(lane_head == j, 1.0, 0.0).astype(dtype) for j in range(LANES // head_lanes)]


def _block_diag_fn(rows, n_heads, head_lanes):
    n_tiles = n_heads * head_lanes // LANES
    zero = jnp.zeros((rows, LANES), BF16)
    keep = _head_keep(rows, head_lanes, BF16) if head_lanes < LANES else None

    def f(y):
        yb = y.astype(BF16)
        blocks = []
        for hd in range(n_heads):
            tiles = [zero] * n_tiles
            if keep is not None:
                t = hd * head_lanes // LANES
                tiles[t] = yb[:, t * LANES:(t + 1) * LANES] * keep[hd % len(keep)]
            else:
                for t in range(hd * head_lanes // LANES, (hd + 1) * head_lanes // LANES):
                    tiles[t] = yb[:, t * LANES:(t + 1) * LANES]
            blocks.append(jnp.concatenate(tiles, axis=1))
        return jnp.concatenate(blocks, axis=0)
    return f


def _diag_blocks_fn(rows, n_heads, head_lanes):
    assert head_lanes < LANES
    keep = _head_keep(rows, head_lanes, F32)
    per = len(keep)

    def f(full):
        tiles = []
        for t in range(n_heads * head_lanes // LANES):
            acc = None
            for j in range(per):
                hd = t * per + j
                term = full[hd * rows:(hd + 1) * rows, t * LANES:(t + 1) * LANES] * keep[j]
                acc = term if acc is None else acc + term
            tiles.append(acc)
        return jnp.concatenate(tiles, axis=1)
    return f


def _tri(n, strict):
    r = lax.broadcasted_iota(jnp.int32, (n, n), 0)
    c = lax.broadcasted_iota(jnp.int32, (n, n), 1)
    return (r > c) if strict else (r >= c)


def _inproj_kernel(x_ref, g_ref, wr_ref, wg_ref, rw_ref, gla_ref):
    u = _rmsnorm(x_ref[...], g_ref[...]).astype(BF16)
    rw_ref[...] = jnp.dot(u, wr_ref[...], preferred_element_type=F32)
    gla_ref[...] = jnp.dot(u, wg_ref[...], preferred_element_type=F32)


def _in_proj(xf, g, w_rw, w_gla, tm):
    n, d = xf.shape
    return pl.pallas_call(
        _inproj_kernel,
        grid=(n // tm,),
        in_specs=[
            pl.BlockSpec((tm, d), lambda i: (i, 0)),
            pl.BlockSpec((1, d), lambda i: (0, 0)),
            pl.BlockSpec(w_rw.shape, lambda i: (0, 0)),
            pl.BlockSpec(w_gla.shape, lambda i: (0, 0)),
        ],
        out_specs=[
            pl.BlockSpec((tm, w_rw.shape[1]), lambda i: (i, 0)),
            pl.BlockSpec((tm, w_gla.shape[1]), lambda i: (i, 0)),
        ],
        out_shape=[
            jax.ShapeDtypeStruct((n, w_rw.shape[1]), F32),
            jax.ShapeDtypeStruct((n, w_gla.shape[1]), F32),
        ],
        compiler_params=pltpu.CompilerParams(
            dimension_semantics=("arbitrary",), vmem_limit_bytes=VMEM_LIMIT),
        name="in_proj",
    )(xf, g, w_rw, w_gla)


RW_GROUP = 4
RW_GW = RW_GROUP * RW_HEAD_DIM
RW_STEP_CHUNKS = 2
RW_DECAY_SCALE = math.exp(-0.5)


def _rwkv_kernel(x_ref, mu_ref, w0_ref, w2_ref, a0_ref, a2_ref, g2_ref, kkw_ref, ka_ref,
                 rk_ref, gnw_ref, gnb_ref, bd_ref, y_ref, state_ref, carry_ref):
    C, D, W, GW = CHUNK, RW_HEAD_DIM, RW_WIDTH, RW_GW
    nb, ct = x_ref.shape[0], x_ref.shape[1]
    nch = ct // C
    R = nb * ct

    @pl.when(pl.program_id(0) == 0)
    def _():
        state_ref[...] = jnp.zeros_like(state_ref)
        carry_ref[...] = jnp.zeros_like(carry_ref)

    x = x_ref[...].reshape(R, x_ref.shape[2])
    row = lax.broadcasted_iota(jnp.int32, x.shape, 0)
    prev = pltpu.roll(x, 1, axis=0)
    for b in range(nb):
        prev = jnp.where(row == b * ct, carry_ref[b:b + 1, :], prev)
        carry_ref[b:b + 1, :] = x[(b + 1) * ct - 1:(b + 1) * ct, :]
    h = x + (prev - x) * mu_ref[...]

    r = h[:, 0:W]
    k = h[:, W:2 * W]
    v = h[:, 2 * W:3 * W]
    o0 = 3 * W
    dw = h[:, o0:o0 + RW_DECAY_LORA]
    da = h[:, o0 + RW_DECAY_LORA:o0 + RW_DECAY_LORA + RW_ICLR_LORA]
    dg = h[:, o0 + RW_DECAY_LORA + RW_ICLR_LORA:]

    bd_g = bd_ref[...]

    def seg_sum(t):
        return jnp.concatenate(
            [_mm_lhs_split(t[:, g * GW:(g + 1) * GW], bd_g, 2) for g in range(W // GW)], axis=1)

    lw = -RW_DECAY_SCALE * _sigmoid(w0_ref[...] + _mm(jnp.tanh(dw), w2_ref[...]))
    iclr = _sigmoid(a0_ref[...] + _mm(da, a2_ref[...]))
    gate = _mm(_sigmoid(dg), g2_ref[...])

    kk = k * kkw_ref[...]
    kk = kk * jnp.minimum(lax.rsqrt(seg_sum(kk * kk)), 1e12)
    k2 = k * (1.0 + (iclr - 1.0) * ka_ref[...])

    rr = lax.broadcasted_iota(jnp.int32, (R, R), 0)
    cc = lax.broadcasted_iota(jnp.int32, (R, R), 1)
    tri_seq = jnp.where((rr >= cc) & (rr // C == cc // C), 1.0, 0.0).astype(BF16)
    cw = _mm_rhs_split(tri_seq, lw, 3)
    cw_last = jnp.concatenate(
        [jnp.broadcast_to(cw[(j + 1) * C - 1:(j + 1) * C, :], (C, W)) for j in range(R // C)], axis=0)
    e_cw = jnp.exp(cw)
    e_ncw = jnp.exp(-cw)
    e_rem = jnp.exp(cw_last - cw)
    kka = kk * iclr
    a_t = -kk * jnp.exp(cw - lw)
    r_t = r * e_cw
    b_t = kka * e_ncw
    k_t = k2 * e_ncw
    b_h = kka * e_rem
    k_h = k2 * e_rem
    w_c = jnp.exp(cw_last)

    ti = lax.broadcasted_iota(jnp.int32, (C, GW), 0)
    si = jnp.bitwise_and(lax.broadcasted_iota(jnp.int32, (C, GW), 1), D - 1)
    strict = ti > si
    incl = ti >= si
    eye = jnp.where(ti == si, 1.0, 0.0)

    bdiag = _block_diag_fn(C, RW_GROUP, D)
    diag_blocks = _diag_blocks_fn(D, RW_GROUP, D)

    ng = W // GW
    chains = [(b, g, ch) for ch in range(nch) for b in range(nb) for g in range(ng)]

    def part(t, c):
        b, g, ch = c
        r0 = b * ct + ch * C
        return t[r0:r0 + C, g * GW:(g + 1) * GW]

    n = range(len(chains))
    a4 = [part(a_t, c) for c in chains]
    r4 = [part(r_t, c) for c in chains]
    v4 = [part(v, c) for c in chains]
    ar = [jnp.concatenate([a4[i], r4[i]], axis=0) for i in n]
    bd_b = [bdiag(part(b_t, c)) for c in chains]
    bd_k = [bdiag(part(k_t, c)) for c in chains]
    bd_v = [bdiag(v4[i]) for i in n]
    m_b = [_mm_nt(ar[i], bd_b[i]) for i in n]
    m_k = [_mm_nt(ar[i], bd_k[i]) for i in n]
    a_ab = [jnp.where(strict, m_b[i][0:C], 0.0) for i in n]
    a_rb = [jnp.where(incl, m_b[i][C:2 * C], 0.0) for i in n]
    a_ak = [jnp.where(strict, m_k[i][0:C], 0.0) for i in n]
    a_rk = [jnp.where(incl, m_k[i][C:2 * C], 0.0) for i in n]
    akv = [_mm(a_ak[i], bd_v[i]) for i in n]
    o_kv = [_mm(a_rk[i], bd_v[i]) for i in n]
    p = [_mm(a_ab[i], bdiag(a_ab[i])) for i in n]
    q = [eye + a_ab[i] for i in n]
    for _ in range(4):
        qp = [_mm(jnp.concatenate([q[i], p[i]], axis=0), bdiag(p[i])) for i in n]
        q = [q[i] + qp[i][0:C] for i in n]
        p = [qp[i][C:2 * C] for i in n]
    t_inv = [q[i] + _mm(q[i], bdiag(p[i])) for i in n]
    u0 = [_mm(t_inv[i], bdiag(akv[i])) for i in n]
    a_hat = [_mm(t_inv[i], bdiag(a4[i])) for i in n]
    seqs = [(b, g) for b in range(nb) for g in range(ng)]
    s4 = [state_ref[b, g] for b, g in seqs]
    o_g = {}
    for ch in range(nch):
        idx = [chains.index((b, g, ch)) for b, g in seqs]
        bd_s = [bdiag(s) for s in s4]
        o_s = [_mm_nt(r4[i], bd_s[j]) for j, i in enumerate(idx)]
        u = [_mm_nt(a_hat[i], bd_s[j]) + u0[i] for j, i in enumerate(idx)]
        for j, i in enumerate(idx):
            o_g[chains[i]] = o_s[j] + _mm(a_rb[i], bdiag(u[j])) + o_kv[i]
        nxt = []
        for j, i in enumerate(idx):
            c = chains[i]
            uv = jnp.concatenate([u[j], v4[i]], axis=0)
            bk = jnp.concatenate([part(b_h, c), part(k_h, c)], axis=0)
            nxt.append(s4[j] * part(w_c, c)[0:D] + diag_blocks(_mm_tn(uv, bk)))
        s4 = nxt
    for j, (b, g) in enumerate(seqs):
        state_ref[b, g] = s4[j]

    o = jnp.concatenate(
        [jnp.concatenate([o_g[(b, g, ch)] for g in range(ng)], axis=1) for b in range(nb) for ch in range(nch)],
        axis=0)
    inv_d = 1.0 / D
    mean = seg_sum(o) * inv_d
    dlt = o - mean
    var = seg_sum(dlt * dlt) * inv_d
    o = dlt * lax.rsqrt(var + RW_GN_EPS) * gnw_ref[...] + gnb_ref[...]
    bonus = seg_sum(r * k2 * rk_ref[...]) * v
    y_ref[...] = ((o + bonus) * gate).astype(y_ref.dtype).reshape(y_ref.shape)


def _rwkv(rw_proj, mu, w0, w2, a0, a2, g2, kkw, ka, rk, gnw, gnb):
    b, s, cols = rw_proj.shape
    W = RW_WIDTH
    bd = jnp.kron(jnp.eye(RW_GROUP, dtype=F32), jnp.ones((RW_HEAD_DIM, RW_HEAD_DIM), F32)).astype(BF16)
    row = lambda a: a.reshape(1, -1)
    full = lambda a: pl.BlockSpec(a.shape, lambda j: (0,) * a.ndim)
    args = [row(mu), row(w0), w2.astype(BF16), row(a0), a2.astype(BF16), g2.astype(BF16),
            row(kkw), row(ka), row(rk), row(gnw), row(gnb), bd]
    ct = CHUNK * RW_STEP_CHUNKS
    return pl.pallas_call(
        _rwkv_kernel,
        grid=(s // ct,),
        in_specs=[pl.BlockSpec((b, ct, cols), lambda j: (0, j, 0))] + [full(a) for a in args],
        out_specs=pl.BlockSpec((b, ct, W), lambda j: (0, j, 0)),
        out_shape=jax.ShapeDtypeStruct((b, s, W), BF16),
        scratch_shapes=[
            pltpu.VMEM((b, W // RW_GW, RW_HEAD_DIM, RW_GW), F32),
            pltpu.VMEM((b, cols), F32),
        ],
        compiler_params=pltpu.CompilerParams(
            dimension_semantics=("arbitrary",), vmem_limit_bytes=VMEM_LIMIT),
        name="rwkv7",
    )(rw_proj, *args)


GLA_STEP_CHUNKS = 2


def _group_mask(rows, row_group, cols, col_group):
    r = lax.shift_right_logical(lax.broadcasted_iota(jnp.int32, (rows, cols), 0), _log2(row_group))
    c = lax.shift_right_logical(lax.broadcasted_iota(jnp.int32, (rows, cols), 1), _log2(col_group))
    return jnp.where(r == c, 1.0, 0.0).astype(BF16)


def _gla_kernel(x_ref, gk2_ref, gkb_ref, ng_ref, y_ref, state_ref):
    C, H, DK, DV, SB = CHUNK, GLA_HEADS, GLA_KEY_DIM, GLA_VAL_DIM, GLA_SUB
    QW, VW = GLA_QK_WIDTH, GLA_V_WIDTH
    nb, ct = x_ref.shape[0], x_ref.shape[1]
    nch = ct // C
    R = nb * ct

    @pl.when(pl.program_id(0) == 0)
    def _():
        state_ref[...] = jnp.zeros_like(state_ref)

    x = x_ref[...].reshape(R, x_ref.shape[2])
    q = x[:, 0:QW] * (DK ** -0.5)
    k = x[:, QW:2 * QW]
    v = x[:, 2 * QW:2 * QW + VW]
    g = x[:, 2 * QW + VW:2 * QW + 2 * VW]
    dgk = x[:, 2 * QW + 2 * VW:]

    la = -_softplus(-(_mm(dgk, gk2_ref[...]) + gkb_ref[...])) * (1.0 / GLA_GATE_TEMP)
    rr = lax.broadcasted_iota(jnp.int32, (R, R), 0)
    cc = lax.broadcasted_iota(jnp.int32, (R, R), 1)
    tri_seq = jnp.where((rr >= cc) & (rr // C == cc // C), 1.0, 0.0).astype(BF16)
    b = _mm_rhs_split(tri_seq, la, 3)
    b_last = jnp.concatenate(
        [jnp.broadcast_to(b[(j + 1) * C - 1:(j + 1) * C, :], (C, QW)) for j in range(R // C)], axis=0)
    q_e = q * jnp.exp(b)
    k_e = k * jnp.exp(b_last - b)
    w_c = jnp.exp(b_last)

    units = [(bi, ch) for ch in range(nch) for bi in range(nb)]
    row0 = {u: u[0] * ct + u[1] * C for u in units}

    o_off = {u: None for u in units}
    for s in (C // 2, C // 4, C // 8):
        bd_keys = _block_diag_fn(s, H, DK)
        bd_vals = _block_diag_fn(s, H, DV)
        jobs = [(u, row0[u] + m * 2 * s) for u in units for m in range(C // (2 * s))]
        att = []
        for u, c0 in jobs:
            ref = b[c0 + s - 1:c0 + s]
            q_s = q[c0 + s:c0 + 2 * s] * jnp.exp(b[c0 + s:c0 + 2 * s] - ref)
            k_s = k[c0:c0 + s] * jnp.exp(ref - b[c0:c0 + s])
            att.append(_mm_nt(q_s, bd_keys(k_s)))
        outs = [_mm(att[j], bd_vals(v[c0:c0 + s])) for j, (u, c0) in enumerate(jobs)]
        zero = jnp.zeros((s, VW), F32)
        for u in units:
            pieces = []
            for j, (uj, c0) in enumerate(jobs):
                if uj == u:
                    pieces += [zero, outs[j]]
            level = jnp.concatenate(pieces, axis=0)
            o_off[u] = level if o_off[u] is None else o_off[u] + level

    nblk = C // SB
    ii = lax.broadcasted_iota(jnp.int32, (nblk, SB, SB, 2 * DK), 1)
    jj = lax.broadcasted_iota(jnp.int32, (nblk, SB, SB, 2 * DK), 2)
    causal4 = jj <= ii
    sel = _group_mask(C, 1, C * SB, SB)
    pair_ones = _group_mask(2 * DK, DK, 2 * DV, DV)
    att2 = {}
    for u in units:
        r0 = row0[u]
        for pr in range(H // 2):
            sl2 = slice(2 * pr * DK, 2 * (pr + 1) * DK)
            q2, k2, b2 = q[r0:r0 + C, sl2], k[r0:r0 + C, sl2], b[r0:r0 + C, sl2]
            dec = jnp.exp(jnp.where(
                causal4, b2.reshape(nblk, SB, 1, 2 * DK) - b2.reshape(nblk, 1, SB, 2 * DK), NEG_BIG))
            pw = q2.reshape(nblk, SB, 1, 2 * DK) * k2.reshape(nblk, 1, SB, 2 * DK) * dec
            att2[(u, pr)] = _mm(pw.reshape(C * SB, 2 * DK), pair_ones)
    o_diag = {}
    for u in units:
        r0 = row0[u]
        heads = []
        for hd in range(H):
            v_h = v[r0:r0 + C, hd * DV:(hd + 1) * DV]
            v_rep = jnp.broadcast_to(v_h.reshape(nblk, 1, SB, DV), (nblk, SB, SB, DV)).reshape(C * SB, DV)
            heads.append(_mm(sel, att2[(u, hd // 2)][:, (hd % 2) * DV:(hd % 2 + 1) * DV] * v_rep))
        o_diag[u] = jnp.concatenate(heads, axis=1)

    bd_state = _block_diag_fn(DV, H, DK)
    diag_state = _diag_blocks_fn(DV, H, DK)
    s4 = [state_ref[bi] for bi in range(nb)]
    o_int = {}
    for ch in range(nch):
        for bi in range(nb):
            r0 = row0[(bi, ch)]
            o_int[(bi, ch)] = _mm_nt(q_e[r0:r0 + C], bd_state(s4[bi]))
        nxt = []
        for bi in range(nb):
            r0 = row0[(bi, ch)]
            nxt.append(s4[bi] * w_c[r0:r0 + 1] + diag_state(_mm_tn(v[r0:r0 + C], k_e[r0:r0 + C])))
        s4 = nxt
    for bi in range(nb):
        state_ref[bi] = s4[bi]

    rows = []
    for bi in range(nb):
        for ch in range(nch):
            u = (bi, ch)
            o = o_int[u] + o_diag[u] + o_off[u]
            heads = []
            for hd in range(H):
                oh = o[:, hd * DV:(hd + 1) * DV]
                heads.append(oh * lax.rsqrt(jnp.mean(oh * oh, axis=-1, keepdims=True) + NORM_EPS))
            rows.append(jnp.concatenate(heads, axis=1))
    o = jnp.concatenate(rows, axis=0)
    y = o * ng_ref[...] * (g * _sigmoid(g))
    y_ref[...] = y.astype(y_ref.dtype).reshape(y_ref.shape)


def _gla(gla_proj, gk2, gkb, ng):
    b, s, cols = gla_proj.shape
    gk2p = jnp.zeros((LANES, GLA_QK_WIDTH), F32).at[:GLA_GATE_RANK].set(gk2).astype(BF16)
    args = [gk2p, gkb.reshape(1, -1), ng.reshape(1, -1)]
    full = lambda a: pl.BlockSpec(a.shape, lambda j: (0,) * a.ndim)
    ct = CHUNK * GLA_STEP_CHUNKS
    return pl.pallas_call(
        _gla_kernel,
        grid=(s // ct,),
        in_specs=[pl.BlockSpec((b, ct, cols), lambda j: (0, j, 0))] + [full(a) for a in args],
        out_specs=pl.BlockSpec((b, ct, GLA_V_WIDTH), lambda j: (0, j, 0)),
        out_shape=jax.ShapeDtypeStruct((b, s, GLA_V_WIDTH), BF16),
        scratch_shapes=[pltpu.VMEM((b, GLA_VAL_DIM, GLA_QK_WIDTH), F32)],
        compiler_params=pltpu.CompilerParams(
            dimension_semantics=("arbitrary",), vmem_limit_bytes=VMEM_LIMIT),
        name="gla",
    )(gla_proj, *args)


def _outproj_kernel(yr_ref, yg_ref, x_ref, wor_ref, wog_ref, nf_ref, wrh_ref, wrl_ref, br_ref,
                    h_ref, u3_ref, route_ref, cnt_ref, carry_ref):
    tm, d = x_ref.shape

    @pl.when(pl.program_id(0) == 0)
    def _():
        carry_ref[...] = jnp.zeros_like(carry_ref)

    h = (x_ref[...] + jnp.dot(yr_ref[...], wor_ref[...], preferred_element_type=F32)
         + jnp.dot(yg_ref[...], wog_ref[...], preferred_element_type=F32))
    h_ref[...] = h
    u = _rmsnorm(h, nf_ref[...])
    for s in range(d // LANES):
        u3_ref[s] = u[:, s * LANES:(s + 1) * LANES]
    u_hi, u_lo = _split(u, 2)
    logits = (jnp.dot(u_hi, wrh_ref[...], preferred_element_type=F32)
              + jnp.dot(u_hi, wrl_ref[...], preferred_element_type=F32)
              + jnp.dot(u_lo, wrh_ref[...], preferred_element_type=F32)) + br_ref[...]
    lane = lax.broadcasted_iota(jnp.int32, logits.shape, 1)
    rest = logits
    picks, idxs, vals = [], [], []
    for r in range(TOP_K):
        m = jnp.max(rest, axis=-1, keepdims=True)
        idx = jnp.min(jnp.where(rest == m, lane, LANES), axis=-1, keepdims=True)
        pick = lane == idx
        picks.append(pick)
        idxs.append(idx)
        vals.append(m)
        rest = jnp.where(pick, -jnp.inf, rest)
    denom = jnp.ones_like(vals[0])
    for r in range(1, TOP_K):
        denom = denom + jnp.exp(vals[r] - vals[0])
    sel = jnp.zeros(logits.shape, F32)
    for pick in picks:
        sel = sel + jnp.where(pick, 1.0, 0.0)
    prefix = _mm(jnp.where(_tri(tm, True), 1.0, 0.0), sel) + carry_ref[...]
    carry_ref[...] = carry_ref[...] + jnp.sum(sel, axis=0, keepdims=True)
    cnt_ref[...] = carry_ref[...]
    route = jnp.zeros(logits.shape, F32)
    for r in range(TOP_K):
        gate = jnp.exp(vals[r] - vals[0]) / denom
        rank = jnp.sum(jnp.where(picks[r], prefix, 0.0), axis=-1, keepdims=True)
        route = jnp.where(lane == r, gate, route)
        route = jnp.where(lane == TOP_K + r, idxs[r].astype(F32), route)
        route = jnp.where(lane == 2 * TOP_K + r, rank, route)
    route_ref[...] = route


def _out_proj(y_rw, y_gla, xf, wo_r, wo_g, nf, wr_hi, wr_lo, br, tm):
    n, d = xf.shape
    full = lambda a: pl.BlockSpec(a.shape, lambda i: (0,) * a.ndim)
    tile = lambda w: pl.BlockSpec((tm, w), lambda i: (i, 0))
    return pl.pallas_call(
        _outproj_kernel,
        grid=(n // tm,),
        in_specs=[tile(y_rw.shape[1]), tile(y_gla.shape[1]), tile(d), full(wo_r), full(wo_g), full(nf),
                  full(wr_hi), full(wr_lo), full(br)],
        out_specs=[tile(d), pl.BlockSpec((d // LANES, tm, LANES), lambda i: (0, i, 0)), tile(LANES),
                   pl.BlockSpec((1, LANES), lambda i: (0, 0))],
        out_shape=[jax.ShapeDtypeStruct((n, d), F32), jax.ShapeDtypeStruct((d // LANES, n, LANES), F32),
                   jax.ShapeDtypeStruct((n, LANES), F32), jax.ShapeDtypeStruct((1, LANES), F32)],
        scratch_shapes=[pltpu.VMEM((1, LANES), F32)],
        compiler_params=pltpu.CompilerParams(
            dimension_semantics=("arbitrary",), vmem_limit_bytes=VMEM_LIMIT),
        name="out_proj",
    )(y_rw, y_gla, xf, wo_r, wo_g, nf, wr_hi, wr_lo, br)


_DISPATCH_BUFS = 3


def _dispatch_kernel(row_ref, pad_ref, u3_hbm, xs_hbm, ubuf, zbuf, in_sems, out_sems, zsem, *, tm):
    i = pl.program_id(0)
    nsteps = pl.num_programs(0)
    nbuf = _DISPATCH_BUFS
    n_blocks = pad_ref.shape[0]

    @pl.when(i == 0)
    def _():
        zbuf[...] = jnp.zeros_like(zbuf)

        def zero_block(blk_i):
            return pltpu.make_async_copy(zbuf, xs_hbm.at[:, pl.ds(blk_i * EXPERT_BLOCK, EXPERT_BLOCK), :], zsem)

        def start(blk_i, c):
            @pl.when(pad_ref[blk_i] != 0)
            def _():
                zero_block(blk_i).start()
            return c
        lax.fori_loop(0, n_blocks, start, 0)

        def finish(blk_i, c):
            @pl.when(pad_ref[blk_i] != 0)
            def _():
                zero_block(blk_i).wait()
            return c
        lax.fori_loop(0, n_blocks, finish, 0)

    def load(step, slot_):
        return pltpu.make_async_copy(u3_hbm.at[:, pl.ds(step * tm, tm), :], ubuf.at[slot_], in_sems.at[slot_])

    def wait_scatter(slot_):
        for _ in range(TOP_K):
            pltpu.make_async_copy(u3_hbm.at[:, pl.ds(0, tm), :], ubuf.at[slot_], out_sems.at[slot_]).wait()

    cur = lax.rem(i, nbuf)
    nxt = lax.rem(i + 1, nbuf)

    @pl.when(i == 0)
    def _():
        load(0, 0).start()

    @pl.when(i >= nbuf - 1)
    def _():
        wait_scatter(nxt)

    @pl.when(i + 1 < nsteps)
    def _():
        load(i + 1, nxt).start()

    load(i, cur).wait()

    def issue(r, c):
        src = ubuf.at[cur, :, pl.ds(r, 1), :]
        for kk in range(TOP_K):
            slot = row_ref[(i * tm + r) * TOP_K + kk]
            dst = xs_hbm.at[:, pl.ds(slot, 1), :]
            pltpu.make_async_copy(src, dst, out_sems.at[cur]).start(priority=kk % 2)
        return c
    lax.fori_loop(0, tm, issue, 0, unroll=4)

    @pl.when(i == nsteps - 1)
    def _():
        for back in range(nbuf - 1):
            @pl.when(i - back >= 0)
            def _():
                wait_scatter(lax.rem(i - back + nbuf, nbuf))


def _dispatch(slots, pad_blocks, u3, tm):
    ns, n, ln = u3.shape
    blk = EXPERT_BLOCK
    grid_spec = pltpu.PrefetchScalarGridSpec(
        num_scalar_prefetch=2,
        grid=(n // tm,),
        in_specs=[pl.BlockSpec(memory_space=pl.ANY)],
        out_specs=pl.BlockSpec(memory_space=pl.ANY),
        scratch_shapes=[pltpu.VMEM((_DISPATCH_BUFS, ns, tm, ln), u3.dtype),
                        pltpu.VMEM((ns, blk, ln), u3.dtype),
                        pltpu.SemaphoreType.DMA((_DISPATCH_BUFS,)),
                        pltpu.SemaphoreType.DMA((_DISPATCH_BUFS,)),
                        pltpu.SemaphoreType.DMA(())],
    )
    return pl.pallas_call(
        functools.partial(_dispatch_kernel, tm=tm),
        grid_spec=grid_spec,
        out_shape=jax.ShapeDtypeStruct((ns, pad_blocks.shape[0] * blk, ln), u3.dtype),
        compiler_params=pltpu.CompilerParams(dimension_semantics=("arbitrary",), has_side_effects=True),
        name="moe_dispatch",
    )(slots, pad_blocks, u3)


def _moe_kernel(be_ref, nxt_ref, nused_ref, xs_ref, w1_hbm, b1_ref, w2_hbm, b2_ref, ys_ref,
                w1f, w2f, w1b, w2b, sems):
    i = pl.program_id(0)
    f = w2b.shape[0]
    ns = xs_ref.shape[0]
    e = be_ref[i]
    e_prev = be_ref[jnp.maximum(i - 1, 0)]

    def fetch(expert):
        return (pltpu.make_async_copy(w1_hbm.at[expert], w1f, sems.at[0]),
                pltpu.make_async_copy(w2_hbm.at[expert], w2f, sems.at[1]))

    @pl.when(i == 0)
    def _():
        for cp in fetch(e):
            cp.start()

    @pl.when(jnp.logical_or(i == 0, e != e_prev))
    def _():
        for cp in fetch(e):
            cp.wait()
        w1b[...] = w1f[...].astype(BF16)
        w2b[...] = w2f[...].astype(BF16)

        @pl.when(nxt_ref[i] >= 0)
        def _():
            for cp in fetch(nxt_ref[i]):
                cp.start()

    @pl.when(i < nused_ref[0])
    def _():
        xb = jnp.concatenate([xs_ref[s] for s in range(ns)], axis=1).astype(BF16)
        hgl = jnp.dot(xb, w1b[...], preferred_element_type=F32) + b1_ref[0]
        x_glu = jnp.minimum(hgl[:, :f], SWIGLU_LIMIT)
        x_lin = jnp.clip(hgl[:, f:], -SWIGLU_LIMIT, SWIGLU_LIMIT)
        act = (x_lin + 1.0) * (x_glu * _sigmoid(SWIGLU_ALPHA * x_glu))
        y = jnp.dot(act.astype(BF16), w2b[...], preferred_element_type=F32) + b2_ref[0]
        for s in range(ns):
            ys_ref[s] = y[:, s * LANES:(s + 1) * LANES]

    @pl.when(i >= nused_ref[0])
    def _():
        ys_ref[...] = jnp.zeros_like(ys_ref)


def _moe_experts(block_e, next_e, n_used, xs, w1, b1, w2, b2):
    ns, _, ln = xs.shape
    ne, d, f2 = w1.shape
    f = w2.shape[1]
    nb = block_e.shape[0]
    blk = EXPERT_BLOCK
    grid_spec = pltpu.PrefetchScalarGridSpec(
        num_scalar_prefetch=3,
        grid=(nb,),
        in_specs=[
            pl.BlockSpec((ns, blk, ln), lambda i, be, nx, nu: (0, i, 0)),
            pl.BlockSpec(memory_space=pl.ANY),
            pl.BlockSpec((1, 1, f2), lambda i, be, nx, nu: (be[i], 0, 0)),
            pl.BlockSpec(memory_space=pl.ANY),
            pl.BlockSpec((1, 1, d), lambda i, be, nx, nu: (be[i], 0, 0)),
        ],
        out_specs=pl.BlockSpec((ns, blk, ln), lambda i, be, nx, nu: (0, i, 0)),
        scratch_shapes=[pltpu.VMEM((d, f2), w1.dtype), pltpu.VMEM((f, d), w2.dtype),
                        pltpu.VMEM((d, f2), BF16), pltpu.VMEM((f, d), BF16),
                        pltpu.SemaphoreType.DMA((2,))],
    )
    return pl.pallas_call(
        _moe_kernel,
        grid_spec=grid_spec,
        out_shape=jax.ShapeDtypeStruct(xs.shape, F32),
        compiler_params=pltpu.CompilerParams(
            dimension_semantics=("arbitrary",), vmem_limit_bytes=VMEM_LIMIT),
        name="moe_experts",
    )(block_e, next_e, n_used, xs, w1, b1.reshape(ne, 1, f2), w2, b2.reshape(ne, 1, d))


def _final_kernel(row_ref, ys_hbm, h_ref, route_ref, p_ref, npl_ref, wg_ref, wp_ref, nfin_ref, out_ref,
                  gbuf, sems, *, last_layer):
    i = pl.program_id(0)
    tm = h_ref.shape[0]
    ns = gbuf.shape[2]
    cur = lax.rem(i, 2)

    last = pl.num_programs(0) - 1

    def issue_row(step, buf, r):
        for kk in range(TOP_K):
            slot = row_ref[(step * tm + r) * TOP_K + kk]
            src = ys_hbm.at[:, pl.ds(slot, 1), :]
            pltpu.make_async_copy(src, gbuf.at[buf, kk, :, pl.ds(r, 1), :],
                                  sems.at[buf]).start(priority=kk % 2)

    def wait_tile(buf):
        for kk in range(TOP_K):
            pltpu.make_async_copy(ys_hbm.at[:, pl.ds(0, tm), :], gbuf.at[buf, kk], sems.at[buf]).wait()

    def gather(step, buf):
        def issue(r, c):
            issue_row(step, buf, r)
            return c
        lax.fori_loop(0, tm, issue, 0, unroll=4)

    @pl.when(i == 0)
    def _():
        gather(0, 0)

    wait_tile(cur)
    nxt_step = jnp.minimum(i + 1, last)
    for r in range(tm):
        issue_row(nxt_step, 1 - cur, r)

    route = route_ref[...]
    h = h_ref[...]
    for kk in range(TOP_K):
        yk = jnp.concatenate([gbuf[cur, kk, s] for s in range(ns)], axis=1)
        h = h + route[:, kk:kk + 1] * yk
    u = _rmsnorm(h, npl_ref[...])
    gate = _sigmoid(_mm(u, wg_ref[...]))
    h = h + gate * _mm(p_ref[...], wp_ref[...])
    out_ref[...] = _rmsnorm(h, nfin_ref[...]) if last_layer else h

    @pl.when(i == last)
    def _():
        wait_tile(1 - cur)


def _final(slots, ys, h1, route, pf, npl, wg, wp, nfin, tm, last_layer):
    n, d = h1.shape
    pd = pf.shape[1]
    grid_spec = pltpu.PrefetchScalarGridSpec(
        num_scalar_prefetch=1,
        grid=(n // tm,),
        in_specs=[
            pl.BlockSpec(memory_space=pl.ANY),
            pl.BlockSpec((tm, d), lambda i, s: (i, 0)),
            pl.BlockSpec((tm, LANES), lambda i, s: (i, 0)),
            pl.BlockSpec((tm, pd), lambda i, s: (i, 0)),
            pl.BlockSpec((1, d), lambda i, s: (0, 0)),
            pl.BlockSpec((d, d), lambda i, s: (0, 0)),
            pl.BlockSpec((pd, d), lambda i, s: (0, 0)),
            pl.BlockSpec((1, d), lambda i, s: (0, 0)),
        ],
        out_specs=pl.BlockSpec((tm, d), lambda i, s: (i, 0)),
        scratch_shapes=[pltpu.VMEM((2, TOP_K, d // ys.shape[2], tm, ys.shape[2]), F32),
                        pltpu.SemaphoreType.DMA((2,))],
    )
    return pl.pallas_call(
        functools.partial(_final_kernel, last_layer=last_layer),
        grid_spec=grid_spec,
        out_shape=jax.ShapeDtypeStruct((n, d), F32),
        compiler_params=pltpu.CompilerParams(
            dimension_semantics=("arbitrary",), vmem_limit_bytes=VMEM_LIMIT),
        name="final",
    )(slots, ys, h1, route, pf, npl, wg, wp, nfin)


def _routing(route, counts, n_experts):
    n = route.shape[0]
    blk = EXPERT_BLOCK
    nb = (n * TOP_K + n_experts * (blk - 1) + blk - 1) // blk
    counts = counts[0, :n_experts].astype(jnp.int32)
    pcounts = ((counts + blk - 1) // blk) * blk
    pend = jnp.cumsum(pcounts)
    pstart = pend - pcounts
    top_i = route[:, TOP_K:2 * TOP_K].astype(jnp.int32)
    rank = route[:, 2 * TOP_K:3 * TOP_K].astype(jnp.int32)
    onehot = top_i[:, :, None] == jnp.arange(n_experts, dtype=jnp.int32)[None, None, :]
    slots = jnp.sum(jnp.where(onehot, pstart[None, None, :], 0), axis=-1) + rank
    block_rows = jnp.arange(nb, dtype=jnp.int32) * blk
    block_e = jnp.sum(block_rows[:, None] >= pend[None, :], axis=1)
    block_e = jnp.clip(block_e, 0, n_experts - 1).astype(jnp.int32)
    n_used = (pend[-1] // blk).astype(jnp.int32).reshape(1)
    blocks = jnp.arange(nb, dtype=jnp.int32)
    onehot_b = block_e[:, None] == jnp.arange(n_experts, dtype=jnp.int32)[None, :]
    run_end = jnp.sum(jnp.where(onehot_b, pend[None, :], 0), axis=1) // blk
    run_end = jnp.where(blocks >= n_used[0], nb, run_end)
    follow = jnp.sum(jnp.where(blocks[None, :] == run_end[:, None], block_e[None, :], 0), axis=1)
    next_e = jnp.where((run_end < nb) & (follow != block_e), follow, -1).astype(jnp.int32)
    pad_blocks = ((blocks + 1 == run_end) | (blocks >= n_used[0])).astype(jnp.int32)
    return block_e, next_e, n_used, slots.reshape(-1).astype(jnp.int32), pad_blocks


def kernel(x, p, norm_mix, w_in, shift_mu, rw_w0, rw_w2, rw_a0, rw_a2, rw_g2, rw_kk, rw_ka, rw_rk,
           rw_gn_w, rw_gn_b, gla_gk2, gla_gk_b, gla_norm, w_out, norm_ffn, w_router, b_router,
           w1, b1, w2, b2, norm_ple, w_ple_gate, w_ple, norm_final):
    bsz, seq, d = x.shape
    n = bsz * seq
    depth = w_in.shape[0]
    n_experts = w_router.shape[-1]
    tm = 256
    tm_route = min(512, n)
    h = x.reshape(n, d)
    for l in range(depth):
        w_rw = w_in[l][:, :RW_COLS].astype(BF16)
        w_gla = jnp.pad(w_in[l][:, RW_COLS:], ((0, 0), (0, LANES - GLA_GATE_RANK))).astype(BF16)
        rw_proj, gla_proj = _in_proj(h, norm_mix[l].reshape(1, d), w_rw, w_gla, tm_route)
        y_rw = _rwkv(rw_proj.reshape(bsz, seq, -1), shift_mu[l], rw_w0[l], rw_w2[l], rw_a0[l], rw_a2[l],
                     rw_g2[l], rw_kk[l], rw_ka[l], rw_rk[l], rw_gn_w[l], rw_gn_b[l])
        y_gla = _gla(gla_proj.reshape(bsz, seq, -1), gla_gk2[l], gla_gk_b[l], gla_norm[l])

        wr = jnp.pad(w_router[l], ((0, 0), (0, LANES - n_experts)))
        wr_hi = wr.astype(BF16)
        wr_lo = (wr - wr_hi.astype(F32)).astype(BF16)
        br = jnp.pad(b_router[l], (0, LANES - n_experts), constant_values=NEG_BIG).reshape(1, LANES)
        wo = w_out[l].astype(BF16)
        h1, u3, route, counts = _out_proj(y_rw.reshape(n, -1), y_gla.reshape(n, -1), h, wo[:RW_WIDTH],
                                          wo[RW_WIDTH:], norm_ffn[l].reshape(1, d), wr_hi, wr_lo, br, tm_route)

        block_e, next_e, n_used, slots, pad_blocks = _routing(route, counts, n_experts)
        xs = _dispatch(slots, pad_blocks, u3, tm_route)
        ys = _moe_experts(block_e, next_e, n_used, xs, w1[l], b1[l], w2[l], b2[l])
        h = _final(slots, ys, h1, route, p[l].reshape(n, -1), norm_ple[l].reshape(1, d),
                   w_ple_gate[l].astype(BF16), w_ple[l].astype(BF16), norm_final.reshape(1, d), tm,
                   l == depth - 1)
    return h.reshape(bsz, seq, d)
```

```python
import functools
import math

import jax
import jax.numpy as jnp
from jax import lax
from jax.experimental import pallas as pl
from jax.experimental.pallas import tpu as pltpu

F32 = jnp.float32
BF16 = jnp.bfloat16

CHUNK = 64
RW_HEADS = 8
RW_HEAD_DIM = 64
RW_WIDTH = RW_HEADS * RW_HEAD_DIM
RW_DECAY_LORA = 64
RW_ICLR_LORA = 64
RW_GATE_LORA = 128
RW_COLS = 3 * RW_WIDTH + RW_DECAY_LORA + RW_ICLR_LORA + RW_GATE_LORA
RW_GN_EPS = 64e-5
GLA_HEADS = 4
GLA_KEY_DIM = 64
GLA_VAL_DIM = 128
GLA_QK_WIDTH = GLA_HEADS * GLA_KEY_DIM
GLA_V_WIDTH = GLA_HEADS * GLA_VAL_DIM
GLA_GATE_RANK = 16
GLA_GATE_TEMP = 16.0
GLA_SUB = 8
LANES = 128
SUBLANES = 8
GLA_COLS_PAD = 2 * GLA_QK_WIDTH + 2 * GLA_V_WIDTH + LANES
TOP_K = 4
EXPERT_BLOCK = 256
SWIGLU_ALPHA = 1.702
SWIGLU_LIMIT = 7.0
NORM_EPS = 1e-6
NEG_BIG = -1e30
VMEM_LIMIT = 56 * 1024 * 1024


def _mm(a, b):
    return jnp.dot(a.astype(BF16), b.astype(BF16), preferred_element_type=F32)


def _mm_nt(a, b):
    return lax.dot_general(a.astype(BF16), b.astype(BF16), (((1,), (1,)), ((), ())),
                           preferred_element_type=F32)


def _mm_tn(a, b):
    return lax.dot_general(a.astype(BF16), b.astype(BF16), (((0,), (0,)), ((), ())),
                           preferred_element_type=F32)


def _split(a, n):
    parts = []
    rem = a
    for _ in range(n):
        p = rem.astype(BF16)
        parts.append(p)
        rem = rem - p.astype(F32)
    return parts


def _mm_lhs_split(a, b_bf16, n):
    out = None
    for p in _split(a, n):
        t = jnp.dot(p, b_bf16, preferred_element_type=F32)
        out = t if out is None else out + t
    return out


def _mm_rhs_split(a_bf16, b, n):
    out = None
    for p in _split(b, n):
        t = jnp.dot(a_bf16, p, preferred_element_type=F32)
        out = t if out is None else out + t
    return out


def _rmsnorm(x, g):
    return x * lax.rsqrt(jnp.mean(x * x, axis=-1, keepdims=True) + NORM_EPS) * g


def _softplus(x):
    return jnp.maximum(x, 0.0) + jnp.log(1.0 + jnp.exp(-jnp.abs(x)))


def _sigmoid(x):
    return 1.0 / (1.0 + jnp.exp(-x))


def _log2(n):
    assert n & (n - 1) == 0
    return n.bit_length() - 1


def _head_keep(rows, head_lanes, dtype):
    lane_head = lax.shift_right_logical(lax.broadcasted_iota(jnp.int32, (rows, LANES), 1), _log2(head_lanes))
    return [jnp.where(lane_head == j, 1.0, 0.0).astype(dtype) for j in range(LANES // head_lanes)]


def _block_diag_fn(rows, n_heads, head_lanes):
    n_tiles = n_heads * head_lanes // LANES
    zero = jnp.zeros((rows, LANES), BF16)
    keep = _head_keep(rows, head_lanes, BF16) if head_lanes < LANES else None

    def f(y):
        yb = y.astype(BF16)
        blocks = []
        for hd in range(n_heads):
            tiles = [zero] * n_tiles
            if keep is not None:
                t = hd * head_lanes // LANES
                tiles[t] = yb[:, t * LANES:(t + 1) * LANES] * keep[hd % len(keep)]
            else:
                for t in range(hd * head_lanes // LANES, (hd + 1) * head_lanes // LANES):
                    tiles[t] = yb[:, t * LANES:(t + 1) * LANES]
            blocks.append(jnp.concatenate(tiles, axis=1))
        return jnp.concatenate(blocks, axis=0)
    return f


def _diag_blocks_fn(rows, n_heads, head_lanes):
    assert head_lanes < LANES
    keep = _head_keep(rows, head_lanes, F32)
    per = len(keep)

    def f(full):
        tiles = []
        for t in range(n_heads * head_lanes // LANES):
            acc = None
            for j in range(per):
                hd = t * per + j
                term = full[hd * rows:(hd + 1) * rows, t * LANES:(t + 1) * LANES] * keep[j]
                acc = term if acc is None else acc + term
            tiles.append(acc)
        return jnp.concatenate(tiles, axis=1)
    return f


def _tri(n, strict):
    r = lax.broadcasted_iota(jnp.int32, (n, n), 0)
    c = lax.broadcasted_iota(jnp.int32, (n, n), 1)
    return (r > c) if strict else (r >= c)


def _inproj_kernel(x_ref, g_ref, wr_ref, wg_ref, rw_ref, gla_ref):
    u = _rmsnorm(x_ref[...], g_ref[...]).astype(BF16)
    rw_ref[...] = jnp.dot(u, wr_ref[...], preferred_element_type=F32)
    gla_ref[...] = jnp.dot(u, wg_ref[...], preferred_element_type=F32)


def _in_proj(xf, g, w_rw, w_gla, tm):
    n, d = xf.shape
    return pl.pallas_call(
        _inproj_kernel,
        grid=(n // tm,),
        in_specs=[
            pl.BlockSpec((tm, d), lambda i: (i, 0)),
            pl.BlockSpec((1, d), lambda i: (0, 0)),
            pl.BlockSpec(w_rw.shape, lambda i: (0, 0)),
            pl.BlockSpec(w_gla.shape, lambda i: (0, 0)),
        ],
        out_specs=[
            pl.BlockSpec((tm, w_rw.shape[1]), lambda i: (i, 0)),
            pl.BlockSpec((tm, w_gla.shape[1]), lambda i: (i, 0)),
        ],
        out_shape=[
            jax.ShapeDtypeStruct((n, w_rw.shape[1]), F32),
            jax.ShapeDtypeStruct((n, w_gla.shape[1]), F32),
        ],
        compiler_params=pltpu.CompilerParams(
            dimension_semantics=("arbitrary",), vmem_limit_bytes=VMEM_LIMIT),
        name="in_proj",
    )(xf, g, w_rw, w_gla)


RW_GROUP = 4
RW_GW = RW_GROUP * RW_HEAD_DIM
RW_STEP_CHUNKS = 2
RW_DECAY_SCALE = math.exp(-0.5)


def _rwkv_kernel(x_ref, mu_ref, w0_ref, w2_ref, a0_ref, a2_ref, g2_ref, kkw_ref, ka_ref,
                 rk_ref, gnw_ref, gnb_ref, bd_ref, y_ref, state_ref, carry_ref):
    C, D, W, GW = CHUNK, RW_HEAD_DIM, RW_WIDTH, RW_GW
    nb, ct = x_ref.shape[0], x_ref.shape[1]
    nch = ct // C
    R = nb * ct

    @pl.when(pl.program_id(0) == 0)
    def _():
        state_ref[...] = jnp.zeros_like(state_ref)
        carry_ref[...] = jnp.zeros_like(carry_ref)

    x = x_ref[...].reshape(R, x_ref.shape[2])
    row = lax.broadcasted_iota(jnp.int32, x.shape, 0)
    prev = pltpu.roll(x, 1, axis=0)
    for b in range(nb):
        prev = jnp.where(row == b * ct, carry_ref[b:b + 1, :], prev)
        carry_ref[b:b + 1, :] = x[(b + 1) * ct - 1:(b + 1) * ct, :]
    h = x + (prev - x) * mu_ref[...]

    r = h[:, 0:W]
    k = h[:, W:2 * W]
    v = h[:, 2 * W:3 * W]
    o0 = 3 * W
    dw = h[:, o0:o0 + RW_DECAY_LORA]
    da = h[:, o0 + RW_DECAY_LORA:o0 + RW_DECAY_LORA + RW_ICLR_LORA]
    dg = h[:, o0 + RW_DECAY_LORA + RW_ICLR_LORA:]

    bd_g = bd_ref[...]

    def seg_sum(t):
        return jnp.concatenate(
            [_mm_lhs_split(t[:, g * GW:(g + 1) * GW], bd_g, 2) for g in range(W // GW)], axis=1)

    lw = -RW_DECAY_SCALE * _sigmoid(w0_ref[...] + _mm(jnp.tanh(dw), w2_ref[...]))
    iclr = _sigmoid(a0_ref[...] + _mm(da, a2_ref[...]))
    gate = _mm(_sigmoid(dg), g2_ref[...])

    kk = k * kkw_ref[...]
    kk = kk * jnp.minimum(lax.rsqrt(seg_sum(kk * kk)), 1e12)
    k2 = k * (1.0 + (iclr - 1.0) * ka_ref[...])

    rr = lax.broadcasted_iota(jnp.int32, (R, R), 0)
    cc = lax.broadcasted_iota(jnp.int32, (R, R), 1)
    tri_seq = jnp.where((rr >= cc) & (rr // C == cc // C), 1.0, 0.0).astype(BF16)
    cw = _mm_rhs_split(tri_seq, lw, 3)
    cw_last = jnp.concatenate(
        [jnp.broadcast_to(cw[(j + 1) * C - 1:(j + 1) * C, :], (C, W)) for j in range(R // C)], axis=0)
    e_cw = jnp.exp(cw)
    e_ncw = jnp.exp(-cw)
    e_rem = jnp.exp(cw_last - cw)
    kka = kk * iclr
    a_t = -kk * jnp.exp(cw - lw)
    r_t = r * e_cw
    b_t = kka * e_ncw
    k_t = k2 * e_ncw
    b_h = kka * e_rem
    k_h = k2 * e_rem
    w_c = jnp.exp(cw_last)

    ti = lax.broadcasted_iota(jnp.int32, (C, GW), 0)
    si = jnp.bitwise_and(lax.broadcasted_iota(jnp.int32, (C, GW), 1), D - 1)
    strict = ti > si
    incl = ti >= si
    eye = jnp.where(ti == si, 1.0, 0.0)

    bdiag = _block_diag_fn(C, RW_GROUP, D)
    diag_blocks = _diag_blocks_fn(D, RW_GROUP, D)

    ng = W // GW
    chains = [(b, g, ch) for ch in range(nch) for b in range(nb) for g in range(ng)]

    def part(t, c):
        b, g, ch = c
        r0 = b * ct + ch * C
        return t[r0:r0 + C, g * GW:(g + 1) * GW]

    n = range(len(chains))
    a4 = [part(a_t, c) for c in chains]
    r4 = [part(r_t, c) for c in chains]
    v4 = [part(v, c) for c in chains]
    ar = [jnp.concatenate([a4[i], r4[i]], axis=0) for i in n]
    bd_b = [bdiag(part(b_t, c)) for c in chains]
    bd_k = [bdiag(part(k_t, c)) for c in chains]
    bd_v = [bdiag(v4[i]) for i in n]
    m_b = [_mm_nt(ar[i], bd_b[i]) for i in n]
    m_k = [_mm_nt(ar[i], bd_k[i]) for i in n]
    a_ab = [jnp.where(strict, m_b[i][0:C], 0.0) for i in n]
    a_rb = [jnp.where(incl, m_b[i][C:2 * C], 0.0) for i in n]
    a_ak = [jnp.where(strict, m_k[i][0:C], 0.0) for i in n]
    a_rk = [jnp.where(incl, m_k[i][C:2 * C], 0.0) for i in n]
    akv = [_mm(a_ak[i], bd_v[i]) for i in n]
    o_kv = [_mm(a_rk[i], bd_v[i]) for i in n]
    p = [_mm(a_ab[i], bdiag(a_ab[i])) for i in n]
    q = [eye + a_ab[i] for i in n]
    for _ in range(4):
        qp = [_mm(jnp.concatenate([q[i], p[i]], axis=0), bdiag(p[i])) for i in n]
        q = [q[i] + qp[i][0:C] for i in n]
        p = [qp[i][C:2 * C] for i in n]
    t_inv = [q[i] + _mm(q[i], bdiag(p[i])) for i in n]
    u0 = [_mm(t_inv[i], bdiag(akv[i])) for i in n]
    a_hat = [_mm(t_inv[i], bdiag(a4[i])) for i in n]
    seqs = [(b, g) for b in range(nb) for g in range(ng)]
    s4 = [state_ref[b, g] for b, g in seqs]
    o_g = {}
    for ch in range(nch):
        idx = [chains.index((b, g, ch)) for b, g in seqs]
        bd_s = [bdiag(s) for s in s4]
        o_s = [_mm_nt(r4[i], bd_s[j]) for j, i in enumerate(idx)]
        u = [_mm_nt(a_hat[i], bd_s[j]) + u0[i] for j, i in enumerate(idx)]
        for j, i in enumerate(idx):
            o_g[chains[i]] = o_s[j] + _mm(a_rb[i], bdiag(u[j])) + o_kv[i]
        nxt = []
        for j, i in enumerate(idx):
            c = chains[i]
            uv = jnp.concatenate([u[j], v4[i]], axis=0)
            bk = jnp.concatenate([part(b_h, c), part(k_h, c)], axis=0)
            nxt.append(s4[j] * part(w_c, c)[0:D] + diag_blocks(_mm_tn(uv, bk)))
        s4 = nxt
    for j, (b, g) in enumerate(seqs):
        state_ref[b, g] = s4[j]

    o = jnp.concatenate(
        [jnp.concatenate([o_g[(b, g, ch)] for g in range(ng)], axis=1) for b in range(nb) for ch in range(nch)],
        axis=0)
    inv_d = 1.0 / D
    mean = seg_sum(o) * inv_d
    dlt = o - mean
    var = seg_sum(dlt * dlt) * inv_d
    o = dlt * lax.rsqrt(var + RW_GN_EPS) * gnw_ref[...] + gnb_ref[...]
    bonus = seg_sum(r * k2 * rk_ref[...]) * v
    y_ref[...] = ((o + bonus) * gate).astype(y_ref.dtype).reshape(y_ref.shape)


def _rwkv(rw_proj, mu, w0, w2, a0, a2, g2, kkw, ka, rk, gnw, gnb):
    b, s, cols = rw_proj.shape
    W = RW_WIDTH
    bd = jnp.kron(jnp.eye(RW_GROUP, dtype=F32), jnp.ones((RW_HEAD_DIM, RW_HEAD_DIM), F32)).astype(BF16)
    row = lambda a: a.reshape(1, -1)
    full = lambda a: pl.BlockSpec(a.shape, lambda j: (0,) * a.ndim)
    args = [row(mu), row(w0), w2.astype(BF16), row(a0), a2.astype(BF16), g2.astype(BF16),
            row(kkw), row(ka), row(rk), row(gnw), row(gnb), bd]
    ct = CHUNK * RW_STEP_CHUNKS
    return pl.pallas_call(
        _rwkv_kernel,
        grid=(s // ct,),
        in_specs=[pl.BlockSpec((b, ct, cols), lambda j: (0, j, 0))] + [full(a) for a in args],
        out_specs=pl.BlockSpec((b, ct, W), lambda j: (0, j, 0)),
        out_shape=jax.ShapeDtypeStruct((b, s, W), BF16),
        scratch_shapes=[
            pltpu.VMEM((b, W // RW_GW, RW_HEAD_DIM, RW_GW), F32),
            pltpu.VMEM((b, cols), F32),
        ],
        compiler_params=pltpu.CompilerParams(
            dimension_semantics=("arbitrary",), vmem_limit_bytes=VMEM_LIMIT),
        name="rwkv7",
    )(rw_proj, *args)


GLA_STEP_CHUNKS = 2


def _group_mask(rows, row_group, cols, col_group):
    r = lax.shift_right_logical(lax.broadcasted_iota(jnp.int32, (rows, cols), 0), _log2(row_group))
    c = lax.shift_right_logical(lax.broadcasted_iota(jnp.int32, (rows, cols), 1), _log2(col_group))
    return jnp.where(r == c, 1.0, 0.0).astype(BF16)


def _gla_kernel(x_ref, gk2_ref, gkb_ref, ng_ref, y_ref, state_ref):
    C, H, DK, DV, SB = CHUNK, GLA_HEADS, GLA_KEY_DIM, GLA_VAL_DIM, GLA_SUB
    QW, VW = GLA_QK_WIDTH, GLA_V_WIDTH
    nb, ct = x_ref.shape[0], x_ref.shape[1]
    nch = ct // C
    R = nb * ct

    @pl.when(pl.program_id(0) == 0)
    def _():
        state_ref[...] = jnp.zeros_like(state_ref)

    x = x_ref[...].reshape(R, x_ref.shape[2])
    q = x[:, 0:QW] * (DK ** -0.5)
    k = x[:, QW:2 * QW]
    v = x[:, 2 * QW:2 * QW + VW]
    g = x[:, 2 * QW + VW:2 * QW + 2 * VW]
    dgk = x[:, 2 * QW + 2 * VW:]

    la = -_softplus(-(_mm(dgk, gk2_ref[...]) + gkb_ref[...])) * (1.0 / GLA_GATE_TEMP)
    rr = lax.broadcasted_iota(jnp.int32, (R, R), 0)
    cc = lax.broadcasted_iota(jnp.int32, (R, R), 1)
    tri_seq = jnp.where((rr >= cc) & (rr // C == cc // C), 1.0, 0.0).astype(BF16)
    b = _mm_rhs_split(tri_seq, la, 3)
    b_last = jnp.concatenate(
        [jnp.broadcast_to(b[(j + 1) * C - 1:(j + 1) * C, :], (C, QW)) for j in range(R // C)], axis=0)
    q_e = q * jnp.exp(b)
    k_e = k * jnp.exp(b_last - b)
    w_c = jnp.exp(b_last)

    units = [(bi, ch) for ch in range(nch) for bi in range(nb)]
    row0 = {u: u[0] * ct + u[1] * C for u in units}

    o_off = {u: None for u in units}
    for s in (C // 2, C // 4, C // 8):
        bd_keys = _block_diag_fn(s, H, DK)
        bd_vals = _block_diag_fn(s, H, DV)
        jobs = [(u, row0[u] + m * 2 * s) for u in units for m in range(C // (2 * s))]
        att = []
        for u, c0 in jobs:
            ref = b[c0 + s - 1:c0 + s]
            q_s = q[c0 + s:c0 + 2 * s] * jnp.exp(b[c0 + s:c0 + 2 * s] - ref)
            k_s = k[c0:c0 + s] * jnp.exp(ref - b[c0:c0 + s])
            att.append(_mm_nt(q_s, bd_keys(k_s)))
        outs = [_mm(att[j], bd_vals(v[c0:c0 + s])) for j, (u, c0) in enumerate(jobs)]
        zero = jnp.zeros((s, VW), F32)
        for u in units:
            pieces = []
            for j, (uj, c0) in enumerate(jobs):
                if uj == u:
                    pieces += [zero, outs[j]]
            level = jnp.concatenate(pieces, axis=0)
            o_off[u] = level if o_off[u] is None else o_off[u] + level

    nblk = C // SB
    ii = lax.broadcasted_iota(jnp.int32, (nblk, SB, SB, 2 * DK), 1)
    jj = lax.broadcasted_iota(jnp.int32, (nblk, SB, SB, 2 * DK), 2)
    causal4 = jj <= ii
    sel = _group_mask(C, 1, C * SB, SB)
    pair_ones = _group_mask(2 * DK, DK, 2 * DV, DV)
    att2 = {}
    for u in units:
        r0 = row0[u]
        for pr in range(H // 2):
            sl2 = slice(2 * pr * DK, 2 * (pr + 1) * DK)
            q2, k2, b2 = q[r0:r0 + C, sl2], k[r0:r0 + C, sl2], b[r0:r0 + C, sl2]
            dec = jnp.exp(jnp.where(
                causal4, b2.reshape(nblk, SB, 1, 2 * DK) - b2.reshape(nblk, 1, SB, 2 * DK), NEG_BIG))
            pw = q2.reshape(nblk, SB, 1, 2 * DK) * k2.reshape(nblk, 1, SB, 2 * DK) * dec
            att2[(u, pr)] = _mm(pw.reshape(C * SB, 2 * DK), pair_ones)
    o_diag = {}
    for u in units:
        r0 = row0[u]
        heads = []
        for hd in range(H):
            v_h = v[r0:r0 + C, hd * DV:(hd + 1) * DV]
            v_rep = jnp.broadcast_to(v_h.reshape(nblk, 1, SB, DV), (nblk, SB, SB, DV)).reshape(C * SB, DV)
            heads.append(_mm(sel, att2[(u, hd // 2)][:, (hd % 2) * DV:(hd % 2 + 1) * DV] * v_rep))
        o_diag[u] = jnp.concatenate(heads, axis=1)

    bd_state = _block_diag_fn(DV, H, DK)
    diag_state = _diag_blocks_fn(DV, H, DK)
    s4 = [state_ref[bi] for bi in range(nb)]
    o_int = {}
    for ch in range(nch):
        for bi in range(nb):
            r0 = row0[(bi, ch)]
            o_int[(bi, ch)] = _mm_nt(q_e[r0:r0 + C], bd_state(s4[bi]))
        nxt = []
        for bi in range(nb):
            r0 = row0[(bi, ch)]
            nxt.append(s4[bi] * w_c[r0:r0 + 1] + diag_state(_mm_tn(v[r0:r0 + C], k_e[r0:r0 + C])))
        s4 = nxt
    for bi in range(nb):
        state_ref[bi] = s4[bi]

    rows = []
    for bi in range(nb):
        for ch in range(nch):
            u = (bi, ch)
            o = o_int[u] + o_diag[u] + o_off[u]
            heads = []
            for hd in range(H):
                oh = o[:, hd * DV:(hd + 1) * DV]
                heads.append(oh * lax.rsqrt(jnp.mean(oh * oh, axis=-1, keepdims=True) + NORM_EPS))
            rows.append(jnp.concatenate(heads, axis=1))
    o = jnp.concatenate(rows, axis=0)
    y = o * ng_ref[...] * (g * _sigmoid(g))
    y_ref[...] = y.astype(y_ref.dtype).reshape(y_ref.shape)


def _gla(gla_proj, gk2, gkb, ng):
    b, s, cols = gla_proj.shape
    gk2p = jnp.zeros((LANES, GLA_QK_WIDTH), F32).at[:GLA_GATE_RANK].set(gk2).astype(BF16)
    args = [gk2p, gkb.reshape(1, -1), ng.reshape(1, -1)]
    full = lambda a: pl.BlockSpec(a.shape, lambda j: (0,) * a.ndim)
    ct = CHUNK * GLA_STEP_CHUNKS
    return pl.pallas_call(
        _gla_kernel,
        grid=(s // ct,),
        in_specs=[pl.BlockSpec((b, ct, cols), lambda j: (0, j, 0))] + [full(a) for a in args],
        out_specs=pl.BlockSpec((b, ct, GLA_V_WIDTH), lambda j: (0, j, 0)),
        out_shape=jax.ShapeDtypeStruct((b, s, GLA_V_WIDTH), BF16),
        scratch_shapes=[pltpu.VMEM((b, GLA_VAL_DIM, GLA_QK_WIDTH), F32)],
        compiler_params=pltpu.CompilerParams(
            dimension_semantics=("arbitrary",), vmem_limit_bytes=VMEM_LIMIT),
        name="gla",
    )(gla_proj, *args)


def _outproj_kernel(yr_ref, yg_ref, x_ref, wor_ref, wog_ref, nf_ref, wrh_ref, wrl_ref, br_ref,
                    h_ref, route_ref, cnt_ref, rec_ref, xs_hbm,
                    carry_ref, open_ref, free_ref, ubuf, slot_vmem, slot_smem, cnt_vmem, cnt_smem, zrow, zblk,
                    sc_sems, misc_sem, zsem, *, n_experts, n_blocks):
    tm, d = x_ref.shape
    ns = d // LANES
    blk = EXPERT_BLOCK
    i = pl.program_id(0)
    last = pl.num_programs(0) - 1
    cur = lax.rem(i, 2)

    def wait_scatter(buf):
        for _ in range(TOP_K):
            pltpu.make_async_copy(ubuf.at[buf], xs_hbm.at[:, pl.ds(0, tm), :], sc_sems.at[buf]).wait()

    @pl.when(i == 0)
    def _():
        carry_ref[...] = jnp.zeros_like(carry_ref)
        open_ref[...] = jnp.zeros_like(open_ref)
        free_ref[...] = jnp.zeros_like(free_ref)

    @pl.when(i >= 2)
    def _():
        wait_scatter(cur)

    h = (x_ref[...] + jnp.dot(yr_ref[...], wor_ref[...], preferred_element_type=F32)
         + jnp.dot(yg_ref[...], wog_ref[...], preferred_element_type=F32))
    h_ref[...] = h
    u = _rmsnorm(h, nf_ref[...])
    for s in range(ns):
        ubuf[cur, s] = u[:, s * LANES:(s + 1) * LANES]
    u_hi, u_lo = _split(u, 2)
    logits = (jnp.dot(u_hi, wrh_ref[...], preferred_element_type=F32)
              + jnp.dot(u_hi, wrl_ref[...], preferred_element_type=F32)
              + jnp.dot(u_lo, wrh_ref[...], preferred_element_type=F32)) + br_ref[...]
    lane = lax.broadcasted_iota(jnp.int32, logits.shape, 1)
    rest = logits
    picks, idxs, vals = [], [], []
    for r in range(TOP_K):
        m = jnp.max(rest, axis=-1, keepdims=True)
        idx = jnp.min(jnp.where(rest == m, lane, LANES), axis=-1, keepdims=True)
        pick = lane == idx
        picks.append(pick)
        idxs.append(idx)
        vals.append(m)
        rest = jnp.where(pick, -jnp.inf, rest)
    denom = jnp.ones_like(vals[0])
    for r in range(1, TOP_K):
        denom = denom + jnp.exp(vals[r] - vals[0])
    sel = jnp.zeros(logits.shape, F32)
    for pick in picks:
        sel = sel + jnp.where(pick, 1.0, 0.0)
    seen = carry_ref[...]
    prefix = _mm(jnp.where(_tri(tm, True), 1.0, 0.0), sel) + seen
    seen_after = seen + jnp.sum(sel, axis=0, keepdims=True)
    carry_ref[...] = seen_after
    cnt_ref[...] = seen_after
    inv_blk = 1.0 / blk
    had = jnp.floor((seen + (blk - 1)) * inv_blk)
    need = jnp.floor((seen_after + (blk - 1)) * inv_blk)
    opened = need - had
    lower = jnp.where(lax.broadcasted_iota(jnp.int32, (LANES, LANES), 0)
                      < lax.broadcasted_iota(jnp.int32, (LANES, LANES), 1), 1.0, 0.0)
    first_id = free_ref[...] + _mm(jnp.broadcast_to(opened, (SUBLANES, LANES)), lower)[0:1]
    logical = jnp.floor(prefix * inv_blk)
    block_id = jnp.where(logical < had, open_ref[...], first_id + (logical - had))
    row_id = block_id * blk + (prefix - logical * blk)
    open_ref[...] = jnp.where(opened > 0.0, first_id + opened - 1.0, open_ref[...])
    free_ref[...] = free_ref[...] + jnp.sum(opened, axis=-1, keepdims=True)
    rec_ref[0] = jnp.concatenate([opened, first_id, had, jnp.zeros((SUBLANES - 3, LANES), F32)], axis=0)
    route = jnp.zeros(logits.shape, F32)
    for r in range(TOP_K):
        gate = jnp.exp(vals[r] - vals[0]) / denom
        rank = jnp.sum(jnp.where(picks[r], prefix, 0.0), axis=-1, keepdims=True)
        row = jnp.sum(jnp.where(picks[r], row_id, 0.0), axis=-1, keepdims=True)
        route = jnp.where(lane == r, gate, route)
        route = jnp.where(lane == TOP_K + r, idxs[r].astype(F32), route)
        route = jnp.where(lane == 2 * TOP_K + r, rank, route)
        route = jnp.where(lane == 3 * TOP_K + r, row, route)
    route_ref[...] = route

    slot_vmem[...] = route.astype(jnp.int32)
    to_smem = pltpu.make_async_copy(slot_vmem, slot_smem, misc_sem)
    to_smem.start()
    to_smem.wait()

    def issue(r, c):
        src = ubuf.at[cur, :, pl.ds(r, 1), :]
        for kk in range(TOP_K):
            dst = xs_hbm.at[:, pl.ds(slot_smem[r, 3 * TOP_K + kk], 1), :]
            pltpu.make_async_copy(src, dst, sc_sems.at[cur]).start(priority=kk % 2)
        return c
    lax.fori_loop(0, tm, issue, 0, unroll=4)

    @pl.when(i == last)
    def _():
        wait_scatter(cur)

        @pl.when(i >= 1)
        def _():
            wait_scatter(1 - cur)

        zrow[...] = jnp.zeros_like(zrow)
        zblk[...] = jnp.zeros_like(zblk)
        table = jnp.concatenate([carry_ref[...], open_ref[...], free_ref[...],
                                 jnp.zeros((SUBLANES - 3, LANES), F32)], axis=0)
        cnt_vmem[...] = table.astype(jnp.int32)
        table_to_smem = pltpu.make_async_copy(cnt_vmem, cnt_smem, misc_sem)
        table_to_smem.start()
        table_to_smem.wait()

        def pad_copy(row):
            return pltpu.make_async_copy(zrow, xs_hbm.at[:, pl.ds(row, 1), :], zsem)

        def pad_block(b):
            return pltpu.make_async_copy(zblk, xs_hbm.at[:, pl.ds(b * blk, blk), :], zsem)

        def run(lo, hi, make):
            def start(j, c):
                make(j).start()
                return c
            lax.fori_loop(lo, hi, start, 0)

            def finish(j, c):
                make(j).wait()
                return c
            lax.fori_loop(lo, hi, finish, 0)

        for e in range(n_experts):
            used = jnp.bitwise_and(cnt_smem[0, e], blk - 1)
            lo = cnt_smem[1, e] * blk + used
            run(lo, lo + jnp.bitwise_and(blk - used, blk - 1), pad_copy)
        run(cnt_smem[2, 0], n_blocks, pad_block)


def _out_proj(y_rw, y_gla, xf, wo_r, wo_g, nf, wr_hi, wr_lo, br, tm, n_experts):
    n, d = xf.shape
    ns = d // LANES
    blk = EXPERT_BLOCK
    assert blk & (blk - 1) == 0
    n_blocks = (n * TOP_K + n_experts * (blk - 1) + blk - 1) // blk
    full = lambda a: pl.BlockSpec(a.shape, lambda i: (0,) * a.ndim)
    tile = lambda w: pl.BlockSpec((tm, w), lambda i: (i, 0))
    return pl.pallas_call(
        functools.partial(_outproj_kernel, n_experts=n_experts, n_blocks=n_blocks),
        grid=(n // tm,),
        in_specs=[tile(y_rw.shape[1]), tile(y_gla.shape[1]), tile(d), full(wo_r), full(wo_g), full(nf),
                  full(wr_hi), full(wr_lo), full(br)],
        out_specs=[tile(d), tile(LANES), pl.BlockSpec((1, LANES), lambda i: (0, 0)),
                   pl.BlockSpec((1, SUBLANES, LANES), lambda i: (i, 0, 0)),
                   pl.BlockSpec(memory_space=pl.ANY)],
        out_shape=[jax.ShapeDtypeStruct((n, d), F32), jax.ShapeDtypeStruct((n, LANES), F32),
                   jax.ShapeDtypeStruct((1, LANES), F32),
                   jax.ShapeDtypeStruct((n // tm, SUBLANES, LANES), F32),
                   jax.ShapeDtypeStruct((ns, n_blocks * blk, LANES), F32)],
        scratch_shapes=[
            pltpu.VMEM((1, LANES), F32),
            pltpu.VMEM((1, LANES), F32),
            pltpu.VMEM((1, LANES), F32),
            pltpu.VMEM((2, ns, tm, LANES), F32),
            pltpu.VMEM((tm, LANES), jnp.int32),
            pltpu.SMEM((tm, LANES), jnp.int32),
            pltpu.VMEM((SUBLANES, LANES), jnp.int32),
            pltpu.SMEM((SUBLANES, LANES), jnp.int32),
            pltpu.VMEM((ns, 1, LANES), F32),
            pltpu.VMEM((ns, blk, LANES), F32),
            pltpu.SemaphoreType.DMA((2,)),
            pltpu.SemaphoreType.DMA(()),
            pltpu.SemaphoreType.DMA(()),
        ],
        compiler_params=pltpu.CompilerParams(
            dimension_semantics=("arbitrary",), vmem_limit_bytes=VMEM_LIMIT, has_side_effects=True),
        name="out_proj",
    )(y_rw, y_gla, xf, wo_r, wo_g, nf, wr_hi, wr_lo, br)


_DISPATCH_BUFS = 3


def _dispatch_kernel(row_ref, pad_ref, u3_hbm, xs_hbm, ubuf, zbuf, in_sems, out_sems, zsem, *, tm):
    i = pl.program_id(0)
    nsteps = pl.num_programs(0)
    nbuf = _DISPATCH_BUFS
    n_blocks = pad_ref.shape[0]

    @pl.when(i == 0)
    def _():
        zbuf[...] = jnp.zeros_like(zbuf)

        def zero_block(blk_i):
            return pltpu.make_async_copy(zbuf, xs_hbm.at[:, pl.ds(blk_i * EXPERT_BLOCK, EXPERT_BLOCK), :], zsem)

        def start(blk_i, c):
            @pl.when(pad_ref[blk_i] != 0)
            def _():
                zero_block(blk_i).start()
            return c
        lax.fori_loop(0, n_blocks, start, 0)

        def finish(blk_i, c):
            @pl.when(pad_ref[blk_i] != 0)
            def _():
                zero_block(blk_i).wait()
            return c
        lax.fori_loop(0, n_blocks, finish, 0)

    def load(step, slot_):
        return pltpu.make_async_copy(u3_hbm.at[:, pl.ds(step * tm, tm), :], ubuf.at[slot_], in_sems.at[slot_])

    def wait_scatter(slot_):
        for _ in range(TOP_K):
            pltpu.make_async_copy(u3_hbm.at[:, pl.ds(0, tm), :], ubuf.at[slot_], out_sems.at[slot_]).wait()

    cur = lax.rem(i, nbuf)
    nxt = lax.rem(i + 1, nbuf)

    @pl.when(i == 0)
    def _():
        load(0, 0).start()

    @pl.when(i >= nbuf - 1)
    def _():
        wait_scatter(nxt)

    @pl.when(i + 1 < nsteps)
    def _():
        load(i + 1, nxt).start()

    load(i, cur).wait()

    def issue(r, c):
        src = ubuf.at[cur, :, pl.ds(r, 1), :]
        for kk in range(TOP_K):
            slot = row_ref[(i * tm + r) * TOP_K + kk]
            dst = xs_hbm.at[:, pl.ds(slot, 1), :]
            pltpu.make_async_copy(src, dst, out_sems.at[cur]).start(priority=kk % 2)
        return c
    lax.fori_loop(0, tm, issue, 0, unroll=4)

    @pl.when(i == nsteps - 1)
    def _():
        for back in range(nbuf - 1):
            @pl.when(i - back >= 0)
            def _():
                wait_scatter(lax.rem(i - back + nbuf, nbuf))


def _dispatch(slots, pad_blocks, u3, tm):
    ns, n, ln = u3.shape
    blk = EXPERT_BLOCK
    grid_spec = pltpu.PrefetchScalarGridSpec(
        num_scalar_prefetch=2,
        grid=(n // tm,),
        in_specs=[pl.BlockSpec(memory_space=pl.ANY)],
        out_specs=pl.BlockSpec(memory_space=pl.ANY),
        scratch_shapes=[pltpu.VMEM((_DISPATCH_BUFS, ns, tm, ln), u3.dtype),
                        pltpu.VMEM((ns, blk, ln), u3.dtype),
                        pltpu.SemaphoreType.DMA((_DISPATCH_BUFS,)),
                        pltpu.SemaphoreType.DMA((_DISPATCH_BUFS,)),
                        pltpu.SemaphoreType.DMA(())],
    )
    return pl.pallas_call(
        functools.partial(_dispatch_kernel, tm=tm),
        grid_spec=grid_spec,
        out_shape=jax.ShapeDtypeStruct((ns, pad_blocks.shape[0] * blk, ln), u3.dtype),
        compiler_params=pltpu.CompilerParams(dimension_semantics=("arbitrary",), has_side_effects=True),
        name="moe_dispatch",
    )(slots, pad_blocks, u3)


def _moe_kernel(be_ref, nxt_ref, nused_ref, src_ref, xs_ref, w1_hbm, b1_ref, w2_hbm, b2_ref, ys_ref,
                w1f, w2f, w1b, w2b, sems):
    del src_ref
    i = pl.program_id(0)
    f = w2b.shape[0]
    ns = xs_ref.shape[0]
    e = be_ref[i]
    e_prev = be_ref[jnp.maximum(i - 1, 0)]

    def fetch(expert):
        return (pltpu.make_async_copy(w1_hbm.at[expert], w1f, sems.at[0]),
                pltpu.make_async_copy(w2_hbm.at[expert], w2f, sems.at[1]))

    @pl.when(i == 0)
    def _():
        for cp in fetch(e):
            cp.start()

    @pl.when(jnp.logical_or(i == 0, e != e_prev))
    def _():
        for cp in fetch(e):
            cp.wait()
        w1b[...] = w1f[...].astype(BF16)
        w2b[...] = w2f[...].astype(BF16)

        @pl.when(nxt_ref[i] >= 0)
        def _():
            for cp in fetch(nxt_ref[i]):
                cp.start()

    @pl.when(i < nused_ref[0])
    def _():
        xb = jnp.concatenate([xs_ref[s] for s in range(ns)], axis=1).astype(BF16)
        hgl = jnp.dot(xb, w1b[...], preferred_element_type=F32) + b1_ref[0]
        x_glu = jnp.minimum(hgl[:, :f], SWIGLU_LIMIT)
        x_lin = jnp.clip(hgl[:, f:], -SWIGLU_LIMIT, SWIGLU_LIMIT)
        act = (x_lin + 1.0) * (x_glu * _sigmoid(SWIGLU_ALPHA * x_glu))
        y = jnp.dot(act.astype(BF16), w2b[...], preferred_element_type=F32) + b2_ref[0]
        for s in range(ns):
            ys_ref[s] = y[:, s * LANES:(s + 1) * LANES]

    @pl.when(i >= nused_ref[0])
    def _():
        ys_ref[...] = jnp.zeros_like(ys_ref)


def _moe_experts(block_e, next_e, n_used, src_block, xs, w1, b1, w2, b2):
    ns, _, ln = xs.shape
    ne, d, f2 = w1.shape
    f = w2.shape[1]
    nb = block_e.shape[0]
    blk = EXPERT_BLOCK
    grid_spec = pltpu.PrefetchScalarGridSpec(
        num_scalar_prefetch=4,
        grid=(nb,),
        in_specs=[
            pl.BlockSpec((ns, blk, ln), lambda i, be, nx, nu, sb: (0, sb[i], 0)),
            pl.BlockSpec(memory_space=pl.ANY),
            pl.BlockSpec((1, 1, f2), lambda i, be, nx, nu, sb: (be[i], 0, 0)),
            pl.BlockSpec(memory_space=pl.ANY),
            pl.BlockSpec((1, 1, d), lambda i, be, nx, nu, sb: (be[i], 0, 0)),
        ],
        out_specs=pl.BlockSpec((ns, blk, ln), lambda i, be, nx, nu, sb: (0, sb[i], 0)),
        scratch_shapes=[pltpu.VMEM((d, f2), w1.dtype), pltpu.VMEM((f, d), w2.dtype),
                        pltpu.VMEM((d, f2), BF16), pltpu.VMEM((f, d), BF16),
                        pltpu.SemaphoreType.DMA((2,))],
    )
    return pl.pallas_call(
        _moe_kernel,
        grid_spec=grid_spec,
        out_shape=jax.ShapeDtypeStruct((ns, nb * blk, ln), F32),
        compiler_params=pltpu.CompilerParams(
            dimension_semantics=("arbitrary",), vmem_limit_bytes=VMEM_LIMIT),
        name="moe_experts",
    )(block_e, next_e, n_used, src_block, xs, w1, b1.reshape(ne, 1, f2), w2, b2.reshape(ne, 1, d))


def _final_kernel(row_ref, ys_hbm, h_ref, route_ref, p_ref, npl_ref, wg_ref, wp_ref, nfin_ref, out_ref,
                  gbuf, sems, *, last_layer):
    i = pl.program_id(0)
    tm = h_ref.shape[0]
    ns = gbuf.shape[2]
    cur = lax.rem(i, 2)

    last = pl.num_programs(0) - 1

    def issue_row(step, buf, r):
        for kk in range(TOP_K):
            slot = row_ref[(step * tm + r) * TOP_K + kk]
            src = ys_hbm.at[:, pl.ds(slot, 1), :]
            pltpu.make_async_copy(src, gbuf.at[buf, kk, :, pl.ds(r, 1), :],
                                  sems.at[buf]).start(priority=kk % 2)

    def wait_tile(buf):
        for kk in range(TOP_K):
            pltpu.make_async_copy(ys_hbm.at[:, pl.ds(0, tm), :], gbuf.at[buf, kk], sems.at[buf]).wait()

    def gather(step, buf):
        def issue(r, c):
            issue_row(step, buf, r)
            return c
        lax.fori_loop(0, tm, issue, 0, unroll=4)

    @pl.when(i == 0)
    def _():
        gather(0, 0)

    wait_tile(cur)
    nxt_step = jnp.minimum(i + 1, last)
    for r in range(tm):
        issue_row(nxt_step, 1 - cur, r)

    route = route_ref[...]
    h = h_ref[...]
    for kk in range(TOP_K):
        yk = jnp.concatenate([gbuf[cur, kk, s] for s in range(ns)], axis=1)
        h = h + route[:, kk:kk + 1] * yk
    u = _rmsnorm(h, npl_ref[...])
    gate = _sigmoid(_mm(u, wg_ref[...]))
    h = h + gate * _mm(p_ref[...], wp_ref[...])
    out_ref[...] = _rmsnorm(h, nfin_ref[...]) if last_layer else h

    @pl.when(i == last)
    def _():
        wait_tile(1 - cur)


def _final(slots, ys, h1, route, pf, npl, wg, wp, nfin, tm, last_layer):
    n, d = h1.shape
    pd = pf.shape[1]
    grid_spec = pltpu.PrefetchScalarGridSpec(
        num_scalar_prefetch=1,
        grid=(n // tm,),
        in_specs=[
            pl.BlockSpec(memory_space=pl.ANY),
            pl.BlockSpec((tm, d), lambda i, s: (i, 0)),
            pl.BlockSpec((tm, LANES), lambda i, s: (i, 0)),
            pl.BlockSpec((tm, pd), lambda i, s: (i, 0)),
            pl.BlockSpec((1, d), lambda i, s: (0, 0)),
            pl.BlockSpec((d, d), lambda i, s: (0, 0)),
            pl.BlockSpec((pd, d), lambda i, s: (0, 0)),
            pl.BlockSpec((1, d), lambda i, s: (0, 0)),
        ],
        out_specs=pl.BlockSpec((tm, d), lambda i, s: (i, 0)),
        scratch_shapes=[pltpu.VMEM((2, TOP_K, d // ys.shape[2], tm, ys.shape[2]), F32),
                        pltpu.SemaphoreType.DMA((2,))],
    )
    return pl.pallas_call(
        functools.partial(_final_kernel, last_layer=last_layer),
        grid_spec=grid_spec,
        out_shape=jax.ShapeDtypeStruct((n, d), F32),
        compiler_params=pltpu.CompilerParams(
            dimension_semantics=("arbitrary",), vmem_limit_bytes=VMEM_LIMIT),
        name="final",
    )(slots, ys, h1, route, pf, npl, wg, wp, nfin)


def _routing(route, counts, rec, n_experts, nb):
    blk = EXPERT_BLOCK
    counts = counts[0, :n_experts].astype(jnp.int32)
    pcounts = ((counts + blk - 1) // blk) * blk
    pend = jnp.cumsum(pcounts)
    pstart = pend - pcounts
    slots = route[:, 3 * TOP_K:4 * TOP_K].astype(jnp.int32)
    block_rows = jnp.arange(nb, dtype=jnp.int32) * blk
    block_e = jnp.sum(block_rows[:, None] >= pend[None, :], axis=1)
    block_e = jnp.clip(block_e, 0, n_experts - 1).astype(jnp.int32)
    n_used = (pend[-1] // blk).astype(jnp.int32).reshape(1)
    blocks = jnp.arange(nb, dtype=jnp.int32)
    onehot_b = block_e[:, None] == jnp.arange(n_experts, dtype=jnp.int32)[None, :]
    run_end = jnp.sum(jnp.where(onehot_b, pend[None, :], 0), axis=1) // blk
    run_end = jnp.where(blocks >= n_used[0], nb, run_end)
    follow = jnp.sum(jnp.where(blocks[None, :] == run_end[:, None], block_e[None, :], 0), axis=1)
    next_e = jnp.where((run_end < nb) & (follow != block_e), follow, -1).astype(jnp.int32)
    run_start = jnp.sum(jnp.where(onehot_b, pstart[None, :], 0), axis=1) // blk
    j = (blocks - run_start)[None, :]
    pick = onehot_b.astype(F32).T
    opened, first, had = (jnp.dot(rec[:, r, :n_experts], pick, precision=lax.Precision.HIGHEST).astype(jnp.int32)
                          for r in range(3))
    covers = (had <= j) & (j < had + opened)
    phys = jnp.sum(jnp.where(covers, first + j - had, 0), axis=0)
    phys = jnp.where(blocks < n_used[0], phys, blocks).astype(jnp.int32)
    return block_e, next_e, n_used, phys, slots.reshape(-1)


def kernel(x, p, norm_mix, w_in, shift_mu, rw_w0, rw_w2, rw_a0, rw_a2, rw_g2, rw_kk, rw_ka, rw_rk,
           rw_gn_w, rw_gn_b, gla_gk2, gla_gk_b, gla_norm, w_out, norm_ffn, w_router, b_router,
           w1, b1, w2, b2, norm_ple, w_ple_gate, w_ple, norm_final):
    bsz, seq, d = x.shape
    n = bsz * seq
    depth = w_in.shape[0]
    n_experts = w_router.shape[-1]
    tm = 256
    tm_route = min(512, n)
    h = x.reshape(n, d)
    for l in range(depth):
        w_rw = w_in[l][:, :RW_COLS].astype(BF16)
        w_gla = jnp.pad(w_in[l][:, RW_COLS:], ((0, 0), (0, LANES - GLA_GATE_RANK))).astype(BF16)
        rw_proj, gla_proj = _in_proj(h, norm_mix[l].reshape(1, d), w_rw, w_gla, tm_route)
        y_rw = _rwkv(rw_proj.reshape(bsz, seq, -1), shift_mu[l], rw_w0[l], rw_w2[l], rw_a0[l], rw_a2[l],
                     rw_g2[l], rw_kk[l], rw_ka[l], rw_rk[l], rw_gn_w[l], rw_gn_b[l])
        y_gla = _gla(gla_proj.reshape(bsz, seq, -1), gla_gk2[l], gla_gk_b[l], gla_norm[l])

        wr = jnp.pad(w_router[l], ((0, 0), (0, LANES - n_experts)))
        wr_hi = wr.astype(BF16)
        wr_lo = (wr - wr_hi.astype(F32)).astype(BF16)
        br = jnp.pad(b_router[l], (0, LANES - n_experts), constant_values=NEG_BIG).reshape(1, LANES)
        wo = w_out[l].astype(BF16)
        h1, route, counts, rec, xs = _out_proj(y_rw.reshape(n, -1), y_gla.reshape(n, -1), h, wo[:RW_WIDTH],
                                               wo[RW_WIDTH:], norm_ffn[l].reshape(1, d), wr_hi, wr_lo, br,
                                               tm_route, n_experts)

        block_e, next_e, n_used, src_block, slots = _routing(route, counts, rec, n_experts,
                                                             xs.shape[1] // EXPERT_BLOCK)
        ys = _moe_experts(block_e, next_e, n_used, src_block, xs, w1[l], b1[l], w2[l], b2[l])
        h = _final(slots, ys, h1, route, p[l].reshape(n, -1), norm_ple[l].reshape(1, d),
                   w_ple_gate[l].astype(BF16), w_ple[l].astype(BF16), norm_final.reshape(1, d), tm,
                   l == depth - 1)
    return h.reshape(bsz, seq, d)
```

```python
import functools
import math

import jax
import jax.numpy as jnp
from jax import lax
from jax.experimental import pallas as pl
from jax.experimental.pallas import tpu as pltpu

F32 = jnp.float32
BF16 = jnp.bfloat16

CHUNK = 64
RW_HEADS = 8
RW_HEAD_DIM = 64
RW_WIDTH = RW_HEADS * RW_HEAD_DIM
RW_DECAY_LORA = 64
RW_ICLR_LORA = 64
RW_GATE_LORA = 128
RW_COLS = 3 * RW_WIDTH + RW_DECAY_LORA + RW_ICLR_LORA + RW_GATE_LORA
RW_GN_EPS = 64e-5
GLA_HEADS = 4
GLA_KEY_DIM = 64
GLA_VAL_DIM = 128
GLA_QK_WIDTH = GLA_HEADS * GLA_KEY_DIM
GLA_V_WIDTH = GLA_HEADS * GLA_VAL_DIM
GLA_GATE_RANK = 16
GLA_GATE_TEMP = 16.0
GLA_SUB = 8
LANES = 128
SUBLANES = 8
GLA_COLS_PAD = 2 * GLA_QK_WIDTH + 2 * GLA_V_WIDTH + LANES
TOP_K = 4
EXPERT_BLOCK = 256
SWIGLU_ALPHA = 1.702
SWIGLU_LIMIT = 7.0
NORM_EPS = 1e-6
NEG_BIG = -1e30
VMEM_LIMIT = 56 * 1024 * 1024


def _mm(a, b):
    return jnp.dot(a.astype(BF16), b.astype(BF16), preferred_element_type=F32)


def _mm_nt(a, b):
    return lax.dot_general(a.astype(BF16), b.astype(BF16), (((1,), (1,)), ((), ())),
                           preferred_element_type=F32)


def _mm_tn(a, b):
    return lax.dot_general(a.astype(BF16), b.astype(BF16), (((0,), (0,)), ((), ())),
                           preferred_element_type=F32)


def _split(a, n):
    parts = []
    rem = a
    for _ in range(n):
        p = rem.astype(BF16)
        parts.append(p)
        rem = rem - p.astype(F32)
    return parts


def _mm_lhs_split(a, b_bf16, n):
    out = None
    for p in _split(a, n):
        t = jnp.dot(p, b_bf16, preferred_element_type=F32)
        out = t if out is None else out + t
    return out


def _mm_rhs_split(a_bf16, b, n):
    out = None
    for p in _split(b, n):
        t = jnp.dot(a_bf16, p, preferred_element_type=F32)
        out = t if out is None else out + t
    return out


def _rmsnorm(x, g):
    return x * lax.rsqrt(jnp.mean(x * x, axis=-1, keepdims=True) + NORM_EPS) * g


def _softplus(x):
    return jnp.maximum(x, 0.0) + jnp.log(1.0 + jnp.exp(-jnp.abs(x)))


def _sigmoid(x):
    return 1.0 / (1.0 + jnp.exp(-x))


def _log2(n):
    assert n & (n - 1) == 0
    return n.bit_length() - 1


def _head_keep(rows, head_lanes, dtype):
    lane_head = lax.shift_right_logical(lax.broadcasted_iota(jnp.int32, (rows, LANES), 1), _log2(head_lanes))
    return [jnp.where(lane_head == j, 1.0, 0.0).astype(dtype) for j in range(LANES // head_lanes)]


def _block_diag_fn(rows, n_heads, head_lanes):
    n_tiles = n_heads * head_lanes // LANES
    zero = jnp.zeros((rows, LANES), BF16)
    keep = _head_keep(rows, head_lanes, BF16) if head_lanes < LANES else None

    def f(y):
        yb = y.astype(BF16)
        blocks = []
        for hd in range(n_heads):
            tiles = [zero] * n_tiles
            if keep is not None:
                t = hd * head_lanes // LANES
                tiles[t] = yb[:, t * LANES:(t + 1) * LANES] * keep[hd % len(keep)]
            else:
                for t in range(hd * head_lanes // LANES, (hd + 1) * head_lanes // LANES):
                    tiles[t] = yb[:, t * LANES:(t + 1) * LANES]
            blocks.append(jnp.concatenate(tiles, axis=1))
        return jnp.concatenate(blocks, axis=0)
    return f


def _diag_blocks_fn(rows, n_heads, head_lanes):
    assert head_lanes < LANES
    keep = _head_keep(rows, head_lanes, F32)
    per = len(keep)

    def f(full):
        tiles = []
        for t in range(n_heads * head_lanes // LANES):
            acc = None
            for j in range(per):
                hd = t * per + j
                term = full[hd * rows:(hd + 1) * rows, t * LANES:(t + 1) * LANES] * keep[j]
                acc = term if acc is None else acc + term
            tiles.append(acc)
        return jnp.concatenate(tiles, axis=1)
    return f


def _tri(n, strict):
    r = lax.broadcasted_iota(jnp.int32, (n, n), 0)
    c = lax.broadcasted_iota(jnp.int32, (n, n), 1)
    return (r > c) if strict else (r >= c)


def _inproj_kernel(x_ref, g_ref, wr_ref, wg_ref, rw_ref, gla_ref):
    u = _rmsnorm(x_ref[...], g_ref[...]).astype(BF16)
    rw_ref[...] = jnp.dot(u, wr_ref[...], preferred_element_type=F32)
    gla_ref[...] = jnp.dot(u, wg_ref[...], preferred_element_type=F32)


def _in_proj(xf, g, w_rw, w_gla, tm):
    n, d = xf.shape
    return pl.pallas_call(
        _inproj_kernel,
        grid=(n // tm,),
        in_specs=[
            pl.BlockSpec((tm, d), lambda i: (i, 0)),
            pl.BlockSpec((1, d), lambda i: (0, 0)),
            pl.BlockSpec(w_rw.shape, lambda i: (0, 0)),
            pl.BlockSpec(w_gla.shape, lambda i: (0, 0)),
        ],
        out_specs=[
            pl.BlockSpec((tm, w_rw.shape[1]), lambda i: (i, 0)),
            pl.BlockSpec((tm, w_gla.shape[1]), lambda i: (i, 0)),
        ],
        out_shape=[
            jax.ShapeDtypeStruct((n, w_rw.shape[1]), F32),
            jax.ShapeDtypeStruct((n, w_gla.shape[1]), F32),
        ],
        compiler_params=pltpu.CompilerParams(
            dimension_semantics=("arbitrary",), vmem_limit_bytes=VMEM_LIMIT),
        name="in_proj",
    )(xf, g, w_rw, w_gla)


RW_GROUP = 4
RW_GW = RW_GROUP * RW_HEAD_DIM
RW_STEP_CHUNKS = 2
RW_DECAY_SCALE = math.exp(-0.5)


def _rwkv_kernel(x_ref, mu_ref, w0_ref, w2_ref, a0_ref, a2_ref, g2_ref, kkw_ref, ka_ref,
                 rk_ref, gnw_ref, gnb_ref, bd_ref, y_ref, state_ref, carry_ref):
    C, D, W, GW = CHUNK, RW_HEAD_DIM, RW_WIDTH, RW_GW
    nb, ct = x_ref.shape[0], x_ref.shape[1]
    nch = ct // C
    R = nb * ct

    @pl.when(pl.program_id(0) == 0)
    def _():
        state_ref[...] = jnp.zeros_like(state_ref)
        carry_ref[...] = jnp.zeros_like(carry_ref)

    x = x_ref[...].reshape(R, x_ref.shape[2])
    row = lax.broadcasted_iota(jnp.int32, x.shape, 0)
    prev = pltpu.roll(x, 1, axis=0)
    for b in range(nb):
        prev = jnp.where(row == b * ct, carry_ref[b:b + 1, :], prev)
        carry_ref[b:b + 1, :] = x[(b + 1) * ct - 1:(b + 1) * ct, :]
    h = x + (prev - x) * mu_ref[...]

    r = h[:, 0:W]
    k = h[:, W:2 * W]
    v = h[:, 2 * W:3 * W]
    o0 = 3 * W
    dw = h[:, o0:o0 + RW_DECAY_LORA]
    da = h[:, o0 + RW_DECAY_LORA:o0 + RW_DECAY_LORA + RW_ICLR_LORA]
    dg = h[:, o0 + RW_DECAY_LORA + RW_ICLR_LORA:]

    bd_g = bd_ref[...]

    def seg_sum(t):
        return jnp.concatenate(
            [_mm_lhs_split(t[:, g * GW:(g + 1) * GW], bd_g, 2) for g in range(W // GW)], axis=1)

    lw = -RW_DECAY_SCALE * _sigmoid(w0_ref[...] + _mm(jnp.tanh(dw), w2_ref[...]))
    iclr = _sigmoid(a0_ref[...] + _mm(da, a2_ref[...]))
    gate = _mm(_sigmoid(dg), g2_ref[...])

    kk = k * kkw_ref[...]
    kk = kk * jnp.minimum(lax.rsqrt(seg_sum(kk * kk)), 1e12)
    k2 = k * (1.0 + (iclr - 1.0) * ka_ref[...])

    rr = lax.broadcasted_iota(jnp.int32, (R, R), 0)
    cc = lax.broadcasted_iota(jnp.int32, (R, R), 1)
    tri_seq = jnp.where((rr >= cc) & (rr // C == cc // C), 1.0, 0.0).astype(BF16)
    cw = _mm_rhs_split(tri_seq, lw, 3)
    cw_last = jnp.concatenate(
        [jnp.broadcast_to(cw[(j + 1) * C - 1:(j + 1) * C, :], (C, W)) for j in range(R // C)], axis=0)
    e_cw = jnp.exp(cw)
    e_ncw = jnp.exp(-cw)
    e_rem = jnp.exp(cw_last - cw)
    kka = kk * iclr
    a_t = -kk * jnp.exp(cw - lw)
    r_t = r * e_cw
    b_t = kka * e_ncw
    k_t = k2 * e_ncw
    b_h = kka * e_rem
    k_h = k2 * e_rem
    w_c = jnp.exp(cw_last)

    ti = lax.broadcasted_iota(jnp.int32, (C, GW), 0)
    si = jnp.bitwise_and(lax.broadcasted_iota(jnp.int32, (C, GW), 1), D - 1)
    strict = ti > si
    incl = ti >= si
    eye = jnp.where(ti == si, 1.0, 0.0)

    bdiag = _block_diag_fn(C, RW_GROUP, D)
    diag_blocks = _diag_blocks_fn(D, RW_GROUP, D)

    ng = W // GW
    chains = [(b, g, ch) for ch in range(nch) for b in range(nb) for g in range(ng)]

    def part(t, c):
        b, g, ch = c
        r0 = b * ct + ch * C
        return t[r0:r0 + C, g * GW:(g + 1) * GW]

    n = range(len(chains))
    a4 = [part(a_t, c) for c in chains]
    r4 = [part(r_t, c) for c in chains]
    v4 = [part(v, c) for c in chains]
    ar = [jnp.concatenate([a4[i], r4[i]], axis=0) for i in n]
    bd_b = [bdiag(part(b_t, c)) for c in chains]
    bd_k = [bdiag(part(k_t, c)) for c in chains]
    bd_v = [bdiag(v4[i]) for i in n]
    m_b = [_mm_nt(ar[i], bd_b[i]) for i in n]
    m_k = [_mm_nt(ar[i], bd_k[i]) for i in n]
    a_ab = [jnp.where(strict, m_b[i][0:C], 0.0) for i in n]
    a_rb = [jnp.where(incl, m_b[i][C:2 * C], 0.0) for i in n]
    a_ak = [jnp.where(strict, m_k[i][0:C], 0.0) for i in n]
    a_rk = [jnp.where(incl, m_k[i][C:2 * C], 0.0) for i in n]
    akv = [_mm(a_ak[i], bd_v[i]) for i in n]
    o_kv = [_mm(a_rk[i], bd_v[i]) for i in n]
    p = [_mm(a_ab[i], bdiag(a_ab[i])) for i in n]
    q = [eye + a_ab[i] for i in n]
    for _ in range(4):
        qp = [_mm(jnp.concatenate([q[i], p[i]], axis=0), bdiag(p[i])) for i in n]
        q = [q[i] + qp[i][0:C] for i in n]
        p = [qp[i][C:2 * C] for i in n]
    t_inv = [q[i] + _mm(q[i], bdiag(p[i])) for i in n]
    u0 = [_mm(t_inv[i], bdiag(akv[i])) for i in n]
    a_hat = [_mm(t_inv[i], bdiag(a4[i])) for i in n]
    seqs = [(b, g) for b in range(nb) for g in range(ng)]
    s4 = [state_ref[b, g] for b, g in seqs]
    o_g = {}
    for ch in range(nch):
        idx = [chains.index((b, g, ch)) for b, g in seqs]
        bd_s = [bdiag(s) for s in s4]
        o_s = [_mm_nt(r4[i], bd_s[j]) for j, i in enumerate(idx)]
        u = [_mm_nt(a_hat[i], bd_s[j]) + u0[i] for j, i in enumerate(idx)]
        for j, i in enumerate(idx):
            o_g[chains[i]] = o_s[j] + _mm(a_rb[i], bdiag(u[j])) + o_kv[i]
        nxt = []
        for j, i in enumerate(idx):
            c = chains[i]
            uv = jnp.concatenate([u[j], v4[i]], axis=0)
            bk = jnp.concatenate([part(b_h, c), part(k_h, c)], axis=0)
            nxt.append(s4[j] * part(w_c, c)[0:D] + diag_blocks(_mm_tn(uv, bk)))
        s4 = nxt
    for j, (b, g) in enumerate(seqs):
        state_ref[b, g] = s4[j]

    o = jnp.concatenate(
        [jnp.concatenate([o_g[(b, g, ch)] for g in range(ng)], axis=1) for b in range(nb) for ch in range(nch)],
        axis=0)
    inv_d = 1.0 / D
    mean = seg_sum(o) * inv_d
    dlt = o - mean
    var = seg_sum(dlt * dlt) * inv_d
    o = dlt * lax.rsqrt(var + RW_GN_EPS) * gnw_ref[...] + gnb_ref[...]
    bonus = seg_sum(r * k2 * rk_ref[...]) * v
    y_ref[...] = ((o + bonus) * gate).astype(y_ref.dtype).reshape(y_ref.shape)


def _rwkv(rw_proj, mu, w0, w2, a0, a2, g2, kkw, ka, rk, gnw, gnb):
    b, s, cols = rw_proj.shape
    W = RW_WIDTH
    bd = jnp.kron(jnp.eye(RW_GROUP, dtype=F32), jnp.ones((RW_HEAD_DIM, RW_HEAD_DIM), F32)).astype(BF16)
    row = lambda a: a.reshape(1, -1)
    full = lambda a: pl.BlockSpec(a.shape, lambda j: (0,) * a.ndim)
    args = [row(mu), row(w0), w2.astype(BF16), row(a0), a2.astype(BF16), g2.astype(BF16),
            row(kkw), row(ka), row(rk), row(gnw), row(gnb), bd]
    ct = CHUNK * RW_STEP_CHUNKS
    return pl.pallas_call(
        _rwkv_kernel,
        grid=(s // ct,),
        in_specs=[pl.BlockSpec((b, ct, cols), lambda j: (0, j, 0))] + [full(a) for a in args],
        out_specs=pl.BlockSpec((b, ct, W), lambda j: (0, j, 0)),
        out_shape=jax.ShapeDtypeStruct((b, s, W), BF16),
        scratch_shapes=[
            pltpu.VMEM((b, W // RW_GW, RW_HEAD_DIM, RW_GW), F32),
            pltpu.VMEM((b, cols), F32),
        ],
        compiler_params=pltpu.CompilerParams(
            dimension_semantics=("arbitrary",), vmem_limit_bytes=VMEM_LIMIT),
        name="rwkv7",
    )(rw_proj, *args)


GLA_STEP_CHUNKS = 2


def _group_mask(rows, row_group, cols, col_group):
    r = lax.shift_right_logical(lax.broadcasted_iota(jnp.int32, (rows, cols), 0), _log2(row_group))
    c = lax.shift_right_logical(lax.broadcasted_iota(jnp.int32, (rows, cols), 1), _log2(col_group))
    return jnp.where(r == c, 1.0, 0.0).astype(BF16)


def _gla_kernel(x_ref, gk2_ref, gkb_ref, ng_ref, y_ref, state_ref):
    C, H, DK, DV, SB = CHUNK, GLA_HEADS, GLA_KEY_DIM, GLA_VAL_DIM, GLA_SUB
    QW, VW = GLA_QK_WIDTH, GLA_V_WIDTH
    nb, ct = x_ref.shape[0], x_ref.shape[1]
    nch = ct // C
    R = nb * ct

    @pl.when(pl.program_id(0) == 0)
    def _():
        state_ref[...] = jnp.zeros_like(state_ref)

    x = x_ref[...].reshape(R, x_ref.shape[2])
    q = x[:, 0:QW] * (DK ** -0.5)
    k = x[:, QW:2 * QW]
    v = x[:, 2 * QW:2 * QW + VW]
    g = x[:, 2 * QW + VW:2 * QW + 2 * VW]
    dgk = x[:, 2 * QW + 2 * VW:]

    la = -_softplus(-(_mm(dgk, gk2_ref[...]) + gkb_ref[...])) * (1.0 / GLA_GATE_TEMP)
    rr = lax.broadcasted_iota(jnp.int32, (R, R), 0)
    cc = lax.broadcasted_iota(jnp.int32, (R, R), 1)
    tri_seq = jnp.where((rr >= cc) & (rr // C == cc // C), 1.0, 0.0).astype(BF16)
    b = _mm_rhs_split(tri_seq, la, 3)
    b_last = jnp.concatenate(
        [jnp.broadcast_to(b[(j + 1) * C - 1:(j + 1) * C, :], (C, QW)) for j in range(R // C)], axis=0)
    q_e = q * jnp.exp(b)
    k_e = k * jnp.exp(b_last - b)
    w_c = jnp.exp(b_last)

    units = [(bi, ch) for ch in range(nch) for bi in range(nb)]
    row0 = {u: u[0] * ct + u[1] * C for u in units}

    o_off = {u: None for u in units}
    for s in (C // 2, C // 4, C // 8):
        bd_keys = _block_diag_fn(s, H, DK)
        bd_vals = _block_diag_fn(s, H, DV)
        jobs = [(u, row0[u] + m * 2 * s) for u in units for m in range(C // (2 * s))]
        att = []
        for u, c0 in jobs:
            ref = b[c0 + s - 1:c0 + s]
            q_s = q[c0 + s:c0 + 2 * s] * jnp.exp(b[c0 + s:c0 + 2 * s] - ref)
            k_s = k[c0:c0 + s] * jnp.exp(ref - b[c0:c0 + s])
            att.append(_mm_nt(q_s, bd_keys(k_s)))
        outs = [_mm(att[j], bd_vals(v[c0:c0 + s])) for j, (u, c0) in enumerate(jobs)]
        zero = jnp.zeros((s, VW), F32)
        for u in units:
            pieces = []
            for j, (uj, c0) in enumerate(jobs):
                if uj == u:
                    pieces += [zero, outs[j]]
            level = jnp.concatenate(pieces, axis=0)
            o_off[u] = level if o_off[u] is None else o_off[u] + level

    nblk = C // SB
    ii = lax.broadcasted_iota(jnp.int32, (nblk, SB, SB, 2 * DK), 1)
    jj = lax.broadcasted_iota(jnp.int32, (nblk, SB, SB, 2 * DK), 2)
    causal4 = jj <= ii
    sel = _group_mask(C, 1, C * SB, SB)
    pair_ones = _group_mask(2 * DK, DK, 2 * DV, DV)
    att2 = {}
    for u in units:
        r0 = row0[u]
        for pr in range(H // 2):
            sl2 = slice(2 * pr * DK, 2 * (pr + 1) * DK)
            q2, k2, b2 = q[r0:r0 + C, sl2], k[r0:r0 + C, sl2], b[r0:r0 + C, sl2]
            dec = jnp.exp(jnp.where(
                causal4, b2.reshape(nblk, SB, 1, 2 * DK) - b2.reshape(nblk, 1, SB, 2 * DK), NEG_BIG))
            pw = q2.reshape(nblk, SB, 1, 2 * DK) * k2.reshape(nblk, 1, SB, 2 * DK) * dec
            att2[(u, pr)] = _mm(pw.reshape(C * SB, 2 * DK), pair_ones)
    o_diag = {}
    for u in units:
        r0 = row0[u]
        heads = []
        for hd in range(H):
            v_h = v[r0:r0 + C, hd * DV:(hd + 1) * DV]
            v_rep = jnp.broadcast_to(v_h.reshape(nblk, 1, SB, DV), (nblk, SB, SB, DV)).reshape(C * SB, DV)
            heads.append(_mm(sel, att2[(u, hd // 2)][:, (hd % 2) * DV:(hd % 2 + 1) * DV] * v_rep))
        o_diag[u] = jnp.concatenate(heads, axis=1)

    bd_state = _block_diag_fn(DV, H, DK)
    diag_state = _diag_blocks_fn(DV, H, DK)
    s4 = [state_ref[bi] for bi in range(nb)]
    o_int = {}
    for ch in range(nch):
        for bi in range(nb):
            r0 = row0[(bi, ch)]
            o_int[(bi, ch)] = _mm_nt(q_e[r0:r0 + C], bd_state(s4[bi]))
        nxt = []
        for bi in range(nb):
            r0 = row0[(bi, ch)]
            nxt.append(s4[bi] * w_c[r0:r0 + 1] + diag_state(_mm_tn(v[r0:r0 + C], k_e[r0:r0 + C])))
        s4 = nxt
    for bi in range(nb):
        state_ref[bi] = s4[bi]

    rows = []
    for bi in range(nb):
        for ch in range(nch):
            u = (bi, ch)
            o = o_int[u] + o_diag[u] + o_off[u]
            heads = []
            for hd in range(H):
                oh = o[:, hd * DV:(hd + 1) * DV]
                heads.append(oh * lax.rsqrt(jnp.mean(oh * oh, axis=-1, keepdims=True) + NORM_EPS))
            rows.append(jnp.concatenate(heads, axis=1))
    o = jnp.concatenate(rows, axis=0)
    y = o * ng_ref[...] * (g * _sigmoid(g))
    y_ref[...] = y.astype(y_ref.dtype).reshape(y_ref.shape)


def _gla(gla_proj, gk2, gkb, ng):
    b, s, cols = gla_proj.shape
    gk2p = jnp.zeros((LANES, GLA_QK_WIDTH), F32).at[:GLA_GATE_RANK].set(gk2).astype(BF16)
    args = [gk2p, gkb.reshape(1, -1), ng.reshape(1, -1)]
    full = lambda a: pl.BlockSpec(a.shape, lambda j: (0,) * a.ndim)
    ct = CHUNK * GLA_STEP_CHUNKS
    return pl.pallas_call(
        _gla_kernel,
        grid=(s // ct,),
        in_specs=[pl.BlockSpec((b, ct, cols), lambda j: (0, j, 0))] + [full(a) for a in args],
        out_specs=pl.BlockSpec((b, ct, GLA_V_WIDTH), lambda j: (0, j, 0)),
        out_shape=jax.ShapeDtypeStruct((b, s, GLA_V_WIDTH), BF16),
        scratch_shapes=[pltpu.VMEM((b, GLA_VAL_DIM, GLA_QK_WIDTH), F32)],
        compiler_params=pltpu.CompilerParams(
            dimension_semantics=("arbitrary",), vmem_limit_bytes=VMEM_LIMIT),
        name="gla",
    )(gla_proj, *args)


def _outproj_kernel(yr_ref, yg_ref, x_ref, wor_ref, wog_ref, nf_ref, wrh_ref, wrl_ref, br_ref,
                    h_ref, route_ref, cnt_ref, rec_ref, xs_hbm,
                    carry_ref, open_ref, free_ref, ubuf, slot_vmem, slot_smem, cnt_vmem, cnt_smem, zrow, zblk,
                    sc_sems, rec_sems, misc_sem, zsem, *, n_experts, n_blocks):
    tm, d = x_ref.shape
    ns = d // LANES
    blk = EXPERT_BLOCK
    i = pl.program_id(0)
    last = pl.num_programs(0) - 1
    nbuf = ubuf.shape[0]
    cur = lax.rem(i, nbuf)
    par = lax.rem(i, 2)

    def wait_scatter(buf):
        for _ in range(TOP_K):
            pltpu.make_async_copy(ubuf.at[buf], xs_hbm.at[:, pl.ds(0, tm), :], sc_sems.at[buf]).wait()

    def record_to_smem(p):
        return pltpu.make_async_copy(slot_vmem.at[p], slot_smem.at[p], rec_sems.at[p])

    def issue_row(buf, p, r):
        src = ubuf.at[buf, :, pl.ds(r, 1), :]
        for kk in range(TOP_K):
            dst = xs_hbm.at[:, pl.ds(slot_smem[p, r, 3 * TOP_K + kk], 1), :]
            pltpu.make_async_copy(src, dst, sc_sems.at[buf]).start(priority=kk % 2)

    @pl.when(i == 0)
    def _():
        carry_ref[...] = jnp.zeros_like(carry_ref)
        open_ref[...] = jnp.zeros_like(open_ref)
        free_ref[...] = jnp.zeros_like(free_ref)
        ubuf[nbuf - 1] = jnp.zeros(ubuf.shape[1:], F32)
        rr = lax.broadcasted_iota(jnp.int32, (tm, LANES), 0)
        ll = lax.broadcasted_iota(jnp.int32, (tm, LANES), 1)
        slot_vmem[1] = n_blocks * blk + rr * TOP_K + jnp.clip(ll - 3 * TOP_K, 0, TOP_K - 1)
        record_to_smem(1).start()

    @pl.when(i >= nbuf - 1)
    def _():
        wait_scatter(cur)

    record_to_smem(1 - par).wait()
    prev = lax.rem(i + nbuf - 1, nbuf)
    for r in range(tm):
        issue_row(prev, 1 - par, r)

    h = (x_ref[...] + jnp.dot(yr_ref[...], wor_ref[...], preferred_element_type=F32)
         + jnp.dot(yg_ref[...], wog_ref[...], preferred_element_type=F32))
    h_ref[...] = h
    u = _rmsnorm(h, nf_ref[...])
    for s in range(ns):
        ubuf[cur, s] = u[:, s * LANES:(s + 1) * LANES]
    u_hi, u_lo = _split(u, 2)
    logits = (jnp.dot(u_hi, wrh_ref[...], preferred_element_type=F32)
              + jnp.dot(u_hi, wrl_ref[...], preferred_element_type=F32)
              + jnp.dot(u_lo, wrh_ref[...], preferred_element_type=F32)) + br_ref[...]
    lane = lax.broadcasted_iota(jnp.int32, logits.shape, 1)
    rest = logits
    picks, idxs, vals = [], [], []
    for r in range(TOP_K):
        m = jnp.max(rest, axis=-1, keepdims=True)
        idx = jnp.min(jnp.where(rest == m, lane, LANES), axis=-1, keepdims=True)
        pick = lane == idx
        picks.append(pick)
        idxs.append(idx)
        vals.append(m)
        rest = jnp.where(pick, -jnp.inf, rest)
    denom = jnp.ones_like(vals[0])
    for r in range(1, TOP_K):
        denom = denom + jnp.exp(vals[r] - vals[0])
    sel = jnp.zeros(logits.shape, F32)
    for pick in picks:
        sel = sel + jnp.where(pick, 1.0, 0.0)
    seen = carry_ref[...]
    prefix = _mm(jnp.where(_tri(tm, True), 1.0, 0.0), sel) + seen
    seen_after = seen + jnp.sum(sel, axis=0, keepdims=True)
    carry_ref[...] = seen_after
    cnt_ref[...] = seen_after
    inv_blk = 1.0 / blk
    had = jnp.floor((seen + (blk - 1)) * inv_blk)
    need = jnp.floor((seen_after + (blk - 1)) * inv_blk)
    opened = need - had
    lower = jnp.where(lax.broadcasted_iota(jnp.int32, (LANES, LANES), 0)
                      < lax.broadcasted_iota(jnp.int32, (LANES, LANES), 1), 1.0, 0.0)
    first_id = free_ref[...] + _mm(jnp.broadcast_to(opened, (SUBLANES, LANES)), lower)[0:1]
    logical = jnp.floor(prefix * inv_blk)
    block_id = jnp.where(logical < had, open_ref[...], first_id + (logical - had))
    row_id = block_id * blk + (prefix - logical * blk)
    open_ref[...] = jnp.where(opened > 0.0, first_id + opened - 1.0, open_ref[...])
    free_ref[...] = free_ref[...] + jnp.sum(opened, axis=-1, keepdims=True)
    rec_ref[0] = jnp.concatenate([opened, first_id, had, jnp.zeros((SUBLANES - 3, LANES), F32)], axis=0)
    route = jnp.zeros(logits.shape, F32)
    for r in range(TOP_K):
        gate = jnp.exp(vals[r] - vals[0]) / denom
        rank = jnp.sum(jnp.where(picks[r], prefix, 0.0), axis=-1, keepdims=True)
        row = jnp.sum(jnp.where(picks[r], row_id, 0.0), axis=-1, keepdims=True)
        route = jnp.where(lane == r, gate, route)
        route = jnp.where(lane == TOP_K + r, idxs[r].astype(F32), route)
        route = jnp.where(lane == 2 * TOP_K + r, rank, route)
        route = jnp.where(lane == 3 * TOP_K + r, row, route)
    route_ref[...] = route

    slot_vmem[par] = route.astype(jnp.int32)
    record_to_smem(par).start()

    @pl.when(i == last)
    def _():
        record_to_smem(par).wait()

        def issue(r, c):
            issue_row(cur, par, r)
            return c
        lax.fori_loop(0, tm, issue, 0, unroll=4)
        for back in range(nbuf):
            @pl.when(i - back >= -1)
            def _():
                wait_scatter(lax.rem(i + nbuf - back, nbuf))

        zrow[...] = jnp.zeros_like(zrow)
        zblk[...] = jnp.zeros_like(zblk)
        table = jnp.concatenate([carry_ref[...], open_ref[...], free_ref[...],
                                 jnp.zeros((SUBLANES - 3, LANES), F32)], axis=0)
        cnt_vmem[...] = table.astype(jnp.int32)
        table_to_smem = pltpu.make_async_copy(cnt_vmem, cnt_smem, misc_sem)
        table_to_smem.start()
        table_to_smem.wait()

        def pad_copy(row):
            return pltpu.make_async_copy(zrow, xs_hbm.at[:, pl.ds(row, 1), :], zsem)

        def pad_block(b):
            return pltpu.make_async_copy(zblk, xs_hbm.at[:, pl.ds(b * blk, blk), :], zsem)

        def run(lo, hi, make):
            def start(j, c):
                make(j).start()
                return c
            lax.fori_loop(lo, hi, start, 0)

            def finish(j, c):
                make(j).wait()
                return c
            lax.fori_loop(lo, hi, finish, 0)

        for e in range(n_experts):
            used = jnp.bitwise_and(cnt_smem[0, e], blk - 1)
            lo = cnt_smem[1, e] * blk + used
            run(lo, lo + jnp.bitwise_and(blk - used, blk - 1), pad_copy)
        run(cnt_smem[2, 0], n_blocks, pad_block)


def _num_blocks(n, n_experts):
    return (n * TOP_K + n_experts * (EXPERT_BLOCK - 1) + EXPERT_BLOCK - 1) // EXPERT_BLOCK


def _out_proj(y_rw, y_gla, xf, wo_r, wo_g, nf, wr_hi, wr_lo, br, tm, n_experts):
    n, d = xf.shape
    ns = d // LANES
    blk = EXPERT_BLOCK
    assert blk & (blk - 1) == 0
    n_blocks = _num_blocks(n, n_experts)
    full = lambda a: pl.BlockSpec(a.shape, lambda i: (0,) * a.ndim)
    tile = lambda w: pl.BlockSpec((tm, w), lambda i: (i, 0))
    return pl.pallas_call(
        functools.partial(_outproj_kernel, n_experts=n_experts, n_blocks=n_blocks),
        grid=(n // tm,),
        in_specs=[tile(y_rw.shape[1]), tile(y_gla.shape[1]), tile(d), full(wo_r), full(wo_g), full(nf),
                  full(wr_hi), full(wr_lo), full(br)],
        out_specs=[tile(d), tile(LANES), pl.BlockSpec((1, LANES), lambda i: (0, 0)),
                   pl.BlockSpec((1, SUBLANES, LANES), lambda i: (i, 0, 0)),
                   pl.BlockSpec(memory_space=pl.ANY)],
        out_shape=[jax.ShapeDtypeStruct((n, d), F32), jax.ShapeDtypeStruct((n, LANES), F32),
                   jax.ShapeDtypeStruct((1, LANES), F32),
                   jax.ShapeDtypeStruct((n // tm, SUBLANES, LANES), F32),
                   jax.ShapeDtypeStruct((ns, n_blocks * blk + tm * TOP_K, LANES), F32)],
        scratch_shapes=[
            pltpu.VMEM((1, LANES), F32),
            pltpu.VMEM((1, LANES), F32),
            pltpu.VMEM((1, LANES), F32),
            pltpu.VMEM((3, ns, tm, LANES), F32),
            pltpu.VMEM((2, tm, LANES), jnp.int32),
            pltpu.SMEM((2, tm, LANES), jnp.int32),
            pltpu.VMEM((SUBLANES, LANES), jnp.int32),
            pltpu.SMEM((SUBLANES, LANES), jnp.int32),
            pltpu.VMEM((ns, 1, LANES), F32),
            pltpu.VMEM((ns, blk, LANES), F32),
            pltpu.SemaphoreType.DMA((3,)),
            pltpu.SemaphoreType.DMA((2,)),
            pltpu.SemaphoreType.DMA(()),
            pltpu.SemaphoreType.DMA(()),
        ],
        compiler_params=pltpu.CompilerParams(
            dimension_semantics=("arbitrary",), vmem_limit_bytes=VMEM_LIMIT, has_side_effects=True),
        name="out_proj",
    )(y_rw, y_gla, xf, wo_r, wo_g, nf, wr_hi, wr_lo, br)


_DISPATCH_BUFS = 3


def _dispatch_kernel(row_ref, pad_ref, u3_hbm, xs_hbm, ubuf, zbuf, in_sems, out_sems, zsem, *, tm):
    i = pl.program_id(0)
    nsteps = pl.num_programs(0)
    nbuf = _DISPATCH_BUFS
    n_blocks = pad_ref.shape[0]

    @pl.when(i == 0)
    def _():
        zbuf[...] = jnp.zeros_like(zbuf)

        def zero_block(blk_i):
            return pltpu.make_async_copy(zbuf, xs_hbm.at[:, pl.ds(blk_i * EXPERT_BLOCK, EXPERT_BLOCK), :], zsem)

        def start(blk_i, c):
            @pl.when(pad_ref[blk_i] != 0)
            def _():
                zero_block(blk_i).start()
            return c
        lax.fori_loop(0, n_blocks, start, 0)

        def finish(blk_i, c):
            @pl.when(pad_ref[blk_i] != 0)
            def _():
                zero_block(blk_i).wait()
            return c
        lax.fori_loop(0, n_blocks, finish, 0)

    def load(step, slot_):
        return pltpu.make_async_copy(u3_hbm.at[:, pl.ds(step * tm, tm), :], ubuf.at[slot_], in_sems.at[slot_])

    def wait_scatter(slot_):
        for _ in range(TOP_K):
            pltpu.make_async_copy(u3_hbm.at[:, pl.ds(0, tm), :], ubuf.at[slot_], out_sems.at[slot_]).wait()

    cur = lax.rem(i, nbuf)
    nxt = lax.rem(i + 1, nbuf)

    @pl.when(i == 0)
    def _():
        load(0, 0).start()

    @pl.when(i >= nbuf - 1)
    def _():
        wait_scatter(nxt)

    @pl.when(i + 1 < nsteps)
    def _():
        load(i + 1, nxt).start()

    load(i, cur).wait()

    def issue(r, c):
        src = ubuf.at[cur, :, pl.ds(r, 1), :]
        for kk in range(TOP_K):
            slot = row_ref[(i * tm + r) * TOP_K + kk]
            dst = xs_hbm.at[:, pl.ds(slot, 1), :]
            pltpu.make_async_copy(src, dst, out_sems.at[cur]).start(priority=kk % 2)
        return c
    lax.fori_loop(0, tm, issue, 0, unroll=4)

    @pl.when(i == nsteps - 1)
    def _():
        for back in range(nbuf - 1):
            @pl.when(i - back >= 0)
            def _():
                wait_scatter(lax.rem(i - back + nbuf, nbuf))


def _dispatch(slots, pad_blocks, u3, tm):
    ns, n, ln = u3.shape
    blk = EXPERT_BLOCK
    grid_spec = pltpu.PrefetchScalarGridSpec(
        num_scalar_prefetch=2,
        grid=(n // tm,),
        in_specs=[pl.BlockSpec(memory_space=pl.ANY)],
        out_specs=pl.BlockSpec(memory_space=pl.ANY),
        scratch_shapes=[pltpu.VMEM((_DISPATCH_BUFS, ns, tm, ln), u3.dtype),
                        pltpu.VMEM((ns, blk, ln), u3.dtype),
                        pltpu.SemaphoreType.DMA((_DISPATCH_BUFS,)),
                        pltpu.SemaphoreType.DMA((_DISPATCH_BUFS,)),
                        pltpu.SemaphoreType.DMA(())],
    )
    return pl.pallas_call(
        functools.partial(_dispatch_kernel, tm=tm),
        grid_spec=grid_spec,
        out_shape=jax.ShapeDtypeStruct((ns, pad_blocks.shape[0] * blk, ln), u3.dtype),
        compiler_params=pltpu.CompilerParams(dimension_semantics=("arbitrary",), has_side_effects=True),
        name="moe_dispatch",
    )(slots, pad_blocks, u3)


def _moe_kernel(be_ref, nxt_ref, nused_ref, src_ref, xs_ref, w1_hbm, b1_ref, w2_hbm, b2_ref, ys_ref,
                w1f, w2f, w1b, w2b, sems):
    del src_ref
    i = pl.program_id(0)
    f = w2b.shape[0]
    ns = xs_ref.shape[0]
    e = be_ref[i]
    e_prev = be_ref[jnp.maximum(i - 1, 0)]

    def fetch(expert):
        return (pltpu.make_async_copy(w1_hbm.at[expert], w1f, sems.at[0]),
                pltpu.make_async_copy(w2_hbm.at[expert], w2f, sems.at[1]))

    @pl.when(i == 0)
    def _():
        for cp in fetch(e):
            cp.start()

    @pl.when(jnp.logical_or(i == 0, e != e_prev))
    def _():
        for cp in fetch(e):
            cp.wait()
        w1b[...] = w1f[...].astype(BF16)
        w2b[...] = w2f[...].astype(BF16)

        @pl.when(nxt_ref[i] >= 0)
        def _():
            for cp in fetch(nxt_ref[i]):
                cp.start()

    @pl.when(i < nused_ref[0])
    def _():
        xb = jnp.concatenate([xs_ref[s] for s in range(ns)], axis=1).astype(BF16)
        hgl = jnp.dot(xb, w1b[...], preferred_element_type=F32) + b1_ref[0]
        x_glu = jnp.minimum(hgl[:, :f], SWIGLU_LIMIT)
        x_lin = jnp.clip(hgl[:, f:], -SWIGLU_LIMIT, SWIGLU_LIMIT)
        act = (x_lin + 1.0) * (x_glu * _sigmoid(SWIGLU_ALPHA * x_glu))
        y = jnp.dot(act.astype(BF16), w2b[...], preferred_element_type=F32) + b2_ref[0]
        for s in range(ns):
            ys_ref[s] = y[:, s * LANES:(s + 1) * LANES]

    @pl.when(i >= nused_ref[0])
    def _():
        ys_ref[...] = jnp.zeros_like(ys_ref)


def _moe_experts(block_e, next_e, n_used, src_block, xs, w1, b1, w2, b2):
    ns, _, ln = xs.shape
    ne, d, f2 = w1.shape
    f = w2.shape[1]
    nb = block_e.shape[0]
    blk = EXPERT_BLOCK
    grid_spec = pltpu.PrefetchScalarGridSpec(
        num_scalar_prefetch=4,
        grid=(nb,),
        in_specs=[
            pl.BlockSpec((ns, blk, ln), lambda i, be, nx, nu, sb: (0, sb[i], 0)),
            pl.BlockSpec(memory_space=pl.ANY),
            pl.BlockSpec((1, 1, f2), lambda i, be, nx, nu, sb: (be[i], 0, 0)),
            pl.BlockSpec(memory_space=pl.ANY),
            pl.BlockSpec((1, 1, d), lambda i, be, nx, nu, sb: (be[i], 0, 0)),
        ],
        out_specs=pl.BlockSpec((ns, blk, ln), lambda i, be, nx, nu, sb: (0, sb[i], 0)),
        scratch_shapes=[pltpu.VMEM((d, f2), w1.dtype), pltpu.VMEM((f, d), w2.dtype),
                        pltpu.VMEM((d, f2), BF16), pltpu.VMEM((f, d), BF16),
                        pltpu.SemaphoreType.DMA((2,))],
    )
    return pl.pallas_call(
        _moe_kernel,
        grid_spec=grid_spec,
        out_shape=jax.ShapeDtypeStruct((ns, nb * blk, ln), F32),
        compiler_params=pltpu.CompilerParams(
            dimension_semantics=("arbitrary",), vmem_limit_bytes=VMEM_LIMIT),
        name="moe_experts",
    )(block_e, next_e, n_used, src_block, xs, w1, b1.reshape(ne, 1, f2), w2, b2.reshape(ne, 1, d))


def _final_kernel(row_ref, ys_hbm, h_ref, route_ref, p_ref, npl_ref, wg_ref, wp_ref, nfin_ref, out_ref,
                  gbuf, sems, *, last_layer):
    i = pl.program_id(0)
    tm = h_ref.shape[0]
    ns = gbuf.shape[2]
    cur = lax.rem(i, 2)

    last = pl.num_programs(0) - 1

    def issue_row(step, buf, r):
        for kk in range(TOP_K):
            slot = row_ref[(step * tm + r) * TOP_K + kk]
            src = ys_hbm.at[:, pl.ds(slot, 1), :]
            pltpu.make_async_copy(src, gbuf.at[buf, kk, :, pl.ds(r, 1), :],
                                  sems.at[buf]).start(priority=kk % 2)

    def wait_tile(buf):
        for kk in range(TOP_K):
            pltpu.make_async_copy(ys_hbm.at[:, pl.ds(0, tm), :], gbuf.at[buf, kk], sems.at[buf]).wait()

    def gather(step, buf):
        def issue(r, c):
            issue_row(step, buf, r)
            return c
        lax.fori_loop(0, tm, issue, 0, unroll=4)

    @pl.when(i == 0)
    def _():
        gather(0, 0)

    wait_tile(cur)
    nxt_step = jnp.minimum(i + 1, last)
    for r in range(tm):
        issue_row(nxt_step, 1 - cur, r)

    route = route_ref[...]
    h = h_ref[...]
    for kk in range(TOP_K):
        yk = jnp.concatenate([gbuf[cur, kk, s] for s in range(ns)], axis=1)
        h = h + route[:, kk:kk + 1] * yk
    u = _rmsnorm(h, npl_ref[...])
    gate = _sigmoid(_mm(u, wg_ref[...]))
    h = h + gate * _mm(p_ref[...], wp_ref[...])
    out_ref[...] = _rmsnorm(h, nfin_ref[...]) if last_layer else h

    @pl.when(i == last)
    def _():
        wait_tile(1 - cur)


def _final(slots, ys, h1, route, pf, npl, wg, wp, nfin, tm, last_layer):
    n, d = h1.shape
    pd = pf.shape[1]
    grid_spec = pltpu.PrefetchScalarGridSpec(
        num_scalar_prefetch=1,
        grid=(n // tm,),
        in_specs=[
            pl.BlockSpec(memory_space=pl.ANY),
            pl.BlockSpec((tm, d), lambda i, s: (i, 0)),
            pl.BlockSpec((tm, LANES), lambda i, s: (i, 0)),
            pl.BlockSpec((tm, pd), lambda i, s: (i, 0)),
            pl.BlockSpec((1, d), lambda i, s: (0, 0)),
            pl.BlockSpec((d, d), lambda i, s: (0, 0)),
            pl.BlockSpec((pd, d), lambda i, s: (0, 0)),
            pl.BlockSpec((1, d), lambda i, s: (0, 0)),
        ],
        out_specs=pl.BlockSpec((tm, d), lambda i, s: (i, 0)),
        scratch_shapes=[pltpu.VMEM((2, TOP_K, d // ys.shape[2], tm, ys.shape[2]), F32),
                        pltpu.SemaphoreType.DMA((2,))],
    )
    return pl.pallas_call(
        functools.partial(_final_kernel, last_layer=last_layer),
        grid_spec=grid_spec,
        out_shape=jax.ShapeDtypeStruct((n, d), F32),
        compiler_params=pltpu.CompilerParams(
            dimension_semantics=("arbitrary",), vmem_limit_bytes=VMEM_LIMIT),
        name="final",
    )(slots, ys, h1, route, pf, npl, wg, wp, nfin)


def _routing(route, counts, rec, n_experts, nb):
    blk = EXPERT_BLOCK
    counts = counts[0, :n_experts].astype(jnp.int32)
    pcounts = ((counts + blk - 1) // blk) * blk
    pend = jnp.cumsum(pcounts)
    pstart = pend - pcounts
    slots = route[:, 3 * TOP_K:4 * TOP_K].astype(jnp.int32)
    block_rows = jnp.arange(nb, dtype=jnp.int32) * blk
    block_e = jnp.sum(block_rows[:, None] >= pend[None, :], axis=1)
    block_e = jnp.clip(block_e, 0, n_experts - 1).astype(jnp.int32)
    n_used = (pend[-1] // blk).astype(jnp.int32).reshape(1)
    blocks = jnp.arange(nb, dtype=jnp.int32)
    onehot_b = block_e[:, None] == jnp.arange(n_experts, dtype=jnp.int32)[None, :]
    run_end = jnp.sum(jnp.where(onehot_b, pend[None, :], 0), axis=1) // blk
    run_end = jnp.where(blocks >= n_used[0], nb, run_end)
    follow = jnp.sum(jnp.where(blocks[None, :] == run_end[:, None], block_e[None, :], 0), axis=1)
    next_e = jnp.where((run_end < nb) & (follow != block_e), follow, -1).astype(jnp.int32)
    run_start = jnp.sum(jnp.where(onehot_b, pstart[None, :], 0), axis=1) // blk
    j = (blocks - run_start)[None, :]
    pick = onehot_b.astype(F32).T
    opened, first, had = (jnp.dot(rec[:, r, :n_experts], pick, precision=lax.Precision.HIGHEST).astype(jnp.int32)
                          for r in range(3))
    covers = (had <= j) & (j < had + opened)
    phys = jnp.sum(jnp.where(covers, first + j - had, 0), axis=0)
    phys = jnp.where(blocks < n_used[0], phys, blocks).astype(jnp.int32)
    return block_e, next_e, n_used, phys, slots.reshape(-1)


def kernel(x, p, norm_mix, w_in, shift_mu, rw_w0, rw_w2, rw_a0, rw_a2, rw_g2, rw_kk, rw_ka, rw_rk,
           rw_gn_w, rw_gn_b, gla_gk2, gla_gk_b, gla_norm, w_out, norm_ffn, w_router, b_router,
           w1, b1, w2, b2, norm_ple, w_ple_gate, w_ple, norm_final):
    bsz, seq, d = x.shape
    n = bsz * seq
    depth = w_in.shape[0]
    n_experts = w_router.shape[-1]
    tm = 256
    tm_route = min(512, n)
    h = x.reshape(n, d)
    for l in range(depth):
        w_rw = w_in[l][:, :RW_COLS].astype(BF16)
        w_gla = jnp.pad(w_in[l][:, RW_COLS:], ((0, 0), (0, LANES - GLA_GATE_RANK))).astype(BF16)
        rw_proj, gla_proj = _in_proj(h, norm_mix[l].reshape(1, d), w_rw, w_gla, tm_route)
        y_rw = _rwkv(rw_proj.reshape(bsz, seq, -1), shift_mu[l], rw_w0[l], rw_w2[l], rw_a0[l], rw_a2[l],
                     rw_g2[l], rw_kk[l], rw_ka[l], rw_rk[l], rw_gn_w[l], rw_gn_b[l])
        y_gla = _gla(gla_proj.reshape(bsz, seq, -1), gla_gk2[l], gla_gk_b[l], gla_norm[l])

        wr = jnp.pad(w_router[l], ((0, 0), (0, LANES - n_experts)))
        wr_hi = wr.astype(BF16)
        wr_lo = (wr - wr_hi.astype(F32)).astype(BF16)
        br = jnp.pad(b_router[l], (0, LANES - n_experts), constant_values=NEG_BIG).reshape(1, LANES)
        wo = w_out[l].astype(BF16)
        h1, route, counts, rec, xs = _out_proj(y_rw.reshape(n, -1), y_gla.reshape(n, -1), h, wo[:RW_WIDTH],
                                               wo[RW_WIDTH:], norm_ffn[l].reshape(1, d), wr_hi, wr_lo, br,
                                               tm_route, n_experts)

        block_e, next_e, n_used, src_block, slots = _routing(route, counts, rec, n_experts,
                                                             _num_blocks(n, n_experts))
        ys = _moe_experts(block_e, next_e, n_used, src_block, xs, w1[l], b1[l], w2[l], b2[l])
        h = _final(slots, ys, h1, route, p[l].reshape(n, -1), norm_ple[l].reshape(1, d),
                   w_ple_gate[l].astype(BF16), w_ple[l].astype(BF16), norm_final.reshape(1, d), tm,
                   l == depth - 1)
    return h.reshape(bsz, seq, d)
```

```python
import functools
import math

import jax
import jax.numpy as jnp
from jax import lax
from jax.experimental import pallas as pl
from jax.experimental.pallas import tpu as pltpu

F32 = jnp.float32
BF16 = jnp.bfloat16

CHUNK = 64
RW_HEADS = 8
RW_HEAD_DIM = 64
RW_WIDTH = RW_HEADS * RW_HEAD_DIM
RW_DECAY_LORA = 64
RW_ICLR_LORA = 64
RW_GATE_LORA = 128
RW_COLS = 3 * RW_WIDTH + RW_DECAY_LORA + RW_ICLR_LORA + RW_GATE_LORA
RW_GN_EPS = 64e-5
GLA_HEADS = 4
GLA_KEY_DIM = 64
GLA_VAL_DIM = 128
GLA_QK_WIDTH = GLA_HEADS * GLA_KEY_DIM
GLA_V_WIDTH = GLA_HEADS * GLA_VAL_DIM
GLA_GATE_RANK = 16
GLA_GATE_TEMP = 16.0
GLA_SUB = 8
LANES = 128
GLA_COLS_PAD = 2 * GLA_QK_WIDTH + 2 * GLA_V_WIDTH + LANES
TOP_K = 4
EXPERT_BLOCK = 256
SWIGLU_ALPHA = 1.702
SWIGLU_LIMIT = 7.0
NORM_EPS = 1e-6
NEG_BIG = -1e30
VMEM_LIMIT = 56 * 1024 * 1024


def _mm(a, b):
    return jnp.dot(a.astype(BF16), b.astype(BF16), preferred_element_type=F32)


def _mm_nt(a, b):
    return lax.dot_general(a.astype(BF16), b.astype(BF16), (((1,), (1,)), ((), ())),
                           preferred_element_type=F32)


def _mm_tn(a, b):
    return lax.dot_general(a.astype(BF16), b.astype(BF16), (((0,), (0,)), ((), ())),
                           preferred_element_type=F32)


def _split(a, n):
    parts = []
    rem = a
    for _ in range(n):
        p = rem.astype(BF16)
        parts.append(p)
        rem = rem - p.astype(F32)
    return parts


def _mm_lhs_split(a, b_bf16, n):
    out = None
    for p in _split(a, n):
        t = jnp.dot(p, b_bf16, preferred_element_type=F32)
        out = t if out is None else out + t
    return out


def _mm_rhs_split(a_bf16, b, n):
    out = None
    for p in _split(b, n):
        t = jnp.dot(a_bf16, p, preferred_element_type=F32)
        out = t if out is None else out + t
    return out


def _rmsnorm(x, g):
    return x * lax.rsqrt(jnp.mean(x * x, axis=-1, keepdims=True) + NORM_EPS) * g


def _softplus(x):
    return jnp.maximum(x, 0.0) + jnp.log(1.0 + jnp.exp(-jnp.abs(x)))


def _sigmoid(x):
    return 1.0 / (1.0 + jnp.exp(-x))


def _log2(n):
    assert n & (n - 1) == 0
    return n.bit_length() - 1


def _head_keep(rows, head_lanes, dtype):
    lane_head = lax.shift_right_logical(lax.broadcasted_iota(jnp.int32, (rows, LANES), 1), _log2(head_lanes))
    return [jnp.where(lane_head == j, 1.0, 0.0).astype(dtype) for j in range(LANES // head_lanes)]


def _block_diag_fn(rows, n_heads, head_lanes):
    n_tiles = n_heads * head_lanes // LANES
    zero = jnp.zeros((rows, LANES), BF16)
    keep = _head_keep(rows, head_lanes, BF16) if head_lanes < LANES else None

    def f(y):
        yb = y.astype(BF16)
        blocks = []
        for hd in range(n_heads):
            tiles = [zero] * n_tiles
            if keep is not None:
                t = hd * head_lanes // LANES
                tiles[t] = yb[:, t * LANES:(t + 1) * LANES] * keep[hd % len(keep)]
            else:
                for t in range(hd * head_lanes // LANES, (hd + 1) * head_lanes // LANES):
                    tiles[t] = yb[:, t * LANES:(t + 1) * LANES]
            blocks.append(jnp.concatenate(tiles, axis=1))
        return jnp.concatenate(blocks, axis=0)
    return f


def _diag_blocks_fn(rows, n_heads, head_lanes):
    assert head_lanes < LANES
    keep = _head_keep(rows, head_lanes, F32)
    per = len(keep)

    def f(full):
        tiles = []
        for t in range(n_heads * head_lanes // LANES):
            acc = None
            for j in range(per):
                hd = t * per + j
                term = full[hd * rows:(hd + 1) * rows, t * LANES:(t + 1) * LANES] * keep[j]
                acc = term if acc is None else acc + term
            tiles.append(acc)
        return jnp.concatenate(tiles, axis=1)
    return f


def _tri(n, strict):
    r = lax.broadcasted_iota(jnp.int32, (n, n), 0)
    c = lax.broadcasted_iota(jnp.int32, (n, n), 1)
    return (r > c) if strict else (r >= c)


def _inproj_kernel(x_ref, g_ref, wr_ref, wg_ref, rw_ref, gla_ref):
    u = _rmsnorm(x_ref[...], g_ref[...]).astype(BF16)
    rw_ref[...] = jnp.dot(u, wr_ref[...], preferred_element_type=F32)
    gla_ref[...] = jnp.dot(u, wg_ref[...], preferred_element_type=F32)


def _in_proj(xf, g, w_rw, w_gla, tm):
    n, d = xf.shape
    return pl.pallas_call(
        _inproj_kernel,
        grid=(n // tm,),
        in_specs=[
            pl.BlockSpec((tm, d), lambda i: (i, 0)),
            pl.BlockSpec((1, d), lambda i: (0, 0)),
            pl.BlockSpec(w_rw.shape, lambda i: (0, 0)),
            pl.BlockSpec(w_gla.shape, lambda i: (0, 0)),
        ],
        out_specs=[
            pl.BlockSpec((tm, w_rw.shape[1]), lambda i: (i, 0)),
            pl.BlockSpec((tm, w_gla.shape[1]), lambda i: (i, 0)),
        ],
        out_shape=[
            jax.ShapeDtypeStruct((n, w_rw.shape[1]), F32),
            jax.ShapeDtypeStruct((n, w_gla.shape[1]), F32),
        ],
        compiler_params=pltpu.CompilerParams(
            dimension_semantics=("arbitrary",), vmem_limit_bytes=VMEM_LIMIT),
        name="in_proj",
    )(xf, g, w_rw, w_gla)


RW_GROUP = 4
RW_GW = RW_GROUP * RW_HEAD_DIM
RW_STEP_CHUNKS = 2
RW_DECAY_SCALE = math.exp(-0.5)


def _rwkv_kernel(x_ref, mu_ref, w0_ref, w2_ref, a0_ref, a2_ref, g2_ref, kkw_ref, ka_ref,
                 rk_ref, gnw_ref, gnb_ref, bd_ref, y_ref, state_ref, carry_ref):
    C, D, W, GW = CHUNK, RW_HEAD_DIM, RW_WIDTH, RW_GW
    nb, ct = x_ref.shape[0], x_ref.shape[1]
    nch = ct // C
    R = nb * ct

    @pl.when(pl.program_id(0) == 0)
    def _():
        state_ref[...] = jnp.zeros_like(state_ref)
        carry_ref[...] = jnp.zeros_like(carry_ref)

    x = x_ref[...].reshape(R, x_ref.shape[2])
    row = lax.broadcasted_iota(jnp.int32, x.shape, 0)
    prev = pltpu.roll(x, 1, axis=0)
    for b in range(nb):
        prev = jnp.where(row == b * ct, carry_ref[b:b + 1, :], prev)
        carry_ref[b:b + 1, :] = x[(b + 1) * ct - 1:(b + 1) * ct, :]
    h = x + (prev - x) * mu_ref[...]

    r = h[:, 0:W]
    k = h[:, W:2 * W]
    v = h[:, 2 * W:3 * W]
    o0 = 3 * W
    dw = h[:, o0:o0 + RW_DECAY_LORA]
    da = h[:, o0 + RW_DECAY_LORA:o0 + RW_DECAY_LORA + RW_ICLR_LORA]
    dg = h[:, o0 + RW_DECAY_LORA + RW_ICLR_LORA:]

    bd_g = bd_ref[...]

    def seg_sum(t):
        return jnp.concatenate(
            [_mm_lhs_split(t[:, g * GW:(g + 1) * GW], bd_g, 2) for g in range(W // GW)], axis=1)

    lw = -RW_DECAY_SCALE * _sigmoid(w0_ref[...] + _mm(jnp.tanh(dw), w2_ref[...]))
    iclr = _sigmoid(a0_ref[...] + _mm(da, a2_ref[...]))
    gate = _mm(_sigmoid(dg), g2_ref[...])

    kk = k * kkw_ref[...]
    kk = kk * jnp.minimum(lax.rsqrt(seg_sum(kk * kk)), 1e12)
    k2 = k * (1.0 + (iclr - 1.0) * ka_ref[...])

    rr = lax.broadcasted_iota(jnp.int32, (R, R), 0)
    cc = lax.broadcasted_iota(jnp.int32, (R, R), 1)
    tri_seq = jnp.where((rr >= cc) & (rr // C == cc // C), 1.0, 0.0).astype(BF16)
    cw = _mm_rhs_split(tri_seq, lw, 3)
    cw_last = jnp.concatenate(
        [jnp.broadcast_to(cw[(j + 1) * C - 1:(j + 1) * C, :], (C, W)) for j in range(R // C)], axis=0)
    e_cw = jnp.exp(cw)
    e_ncw = jnp.exp(-cw)
    e_rem = jnp.exp(cw_last - cw)
    kka = kk * iclr
    a_t = -kk * jnp.exp(cw - lw)
    r_t = r * e_cw
    b_t = kka * e_ncw
    k_t = k2 * e_ncw
    b_h = kka * e_rem
    k_h = k2 * e_rem
    w_c = jnp.exp(cw_last)

    ti = lax.broadcasted_iota(jnp.int32, (C, GW), 0)
    si = jnp.bitwise_and(lax.broadcasted_iota(jnp.int32, (C, GW), 1), D - 1)
    strict = ti > si
    incl = ti >= si
    eye = jnp.where(ti == si, 1.0, 0.0)

    bdiag = _block_diag_fn(C, RW_GROUP, D)
    diag_blocks = _diag_blocks_fn(D, RW_GROUP, D)

    ng = W // GW
    chains = [(b, g, ch) for ch in range(nch) for b in range(nb) for g in range(ng)]

    def part(t, c):
        b, g, ch = c
        r0 = b * ct + ch * C
        return t[r0:r0 + C, g * GW:(g + 1) * GW]

    n = range(len(chains))
    a4 = [part(a_t, c) for c in chains]
    r4 = [part(r_t, c) for c in chains]
    v4 = [part(v, c) for c in chains]
    ar = [jnp.concatenate([a4[i], r4[i]], axis=0) for i in n]
    bd_b = [bdiag(part(b_t, c)) for c in chains]
    bd_k = [bdiag(part(k_t, c)) for c in chains]
    bd_v = [bdiag(v4[i]) for i in n]
    m_b = [_mm_nt(ar[i], bd_b[i]) for i in n]
    m_k = [_mm_nt(ar[i], bd_k[i]) for i in n]
    a_ab = [jnp.where(strict, m_b[i][0:C], 0.0) for i in n]
    a_rb = [jnp.where(incl, m_b[i][C:2 * C], 0.0) for i in n]
    a_ak = [jnp.where(strict, m_k[i][0:C], 0.0) for i in n]
    a_rk = [jnp.where(incl, m_k[i][C:2 * C], 0.0) for i in n]
    akv = [_mm(a_ak[i], bd_v[i]) for i in n]
    o_kv = [_mm(a_rk[i], bd_v[i]) for i in n]
    p = [_mm(a_ab[i], bdiag(a_ab[i])) for i in n]
    q = [eye + a_ab[i] for i in n]
    for _ in range(4):
        qp = [_mm(jnp.concatenate([q[i], p[i]], axis=0), bdiag(p[i])) for i in n]
        q = [q[i] + qp[i][0:C] for i in n]
        p = [qp[i][C:2 * C] for i in n]
    t_inv = [q[i] + _mm(q[i], bdiag(p[i])) for i in n]
    u0 = [_mm(t_inv[i], bdiag(akv[i])) for i in n]
    a_hat = [_mm(t_inv[i], bdiag(a4[i])) for i in n]
    seqs = [(b, g) for b in range(nb) for g in range(ng)]
    s4 = [state_ref[b, g] for b, g in seqs]
    o_g = {}
    for ch in range(nch):
        idx = [chains.index((b, g, ch)) for b, g in seqs]
        bd_s = [bdiag(s) for s in s4]
        o_s = [_mm_nt(r4[i], bd_s[j]) for j, i in enumerate(idx)]
        u = [_mm_nt(a_hat[i], bd_s[j]) + u0[i] for j, i in enumerate(idx)]
        for j, i in enumerate(idx):
            o_g[chains[i]] = o_s[j] + _mm(a_rb[i], bdiag(u[j])) + o_kv[i]
        nxt = []
        for j, i in enumerate(idx):
            c = chains[i]
            uv = jnp.concatenate([u[j], v4[i]], axis=0)
            bk = jnp.concatenate([part(b_h, c), part(k_h, c)], axis=0)
            nxt.append(s4[j] * part(w_c, c)[0:D] + diag_blocks(_mm_tn(uv, bk)))
        s4 = nxt
    for j, (b, g) in enumerate(seqs):
        state_ref[b, g] = s4[j]

    o = jnp.concatenate(
        [jnp.concatenate([o_g[(b, g, ch)] for g in range(ng)], axis=1) for b in range(nb) for ch in range(nch)],
        axis=0)
    inv_d = 1.0 / D
    mean = seg_sum(o) * inv_d
    dlt = o - mean
    var = seg_sum(dlt * dlt) * inv_d
    o = dlt * lax.rsqrt(var + RW_GN_EPS) * gnw_ref[...] + gnb_ref[...]
    bonus = seg_sum(r * k2 * rk_ref[...]) * v
    y_ref[...] = ((o + bonus) * gate).astype(y_ref.dtype).reshape(y_ref.shape)


def _rwkv(rw_proj, mu, w0, w2, a0, a2, g2, kkw, ka, rk, gnw, gnb):
    b, s, cols = rw_proj.shape
    W = RW_WIDTH
    bd = jnp.kron(jnp.eye(RW_GROUP, dtype=F32), jnp.ones((RW_HEAD_DIM, RW_HEAD_DIM), F32)).astype(BF16)
    row = lambda a: a.reshape(1, -1)
    full = lambda a: pl.BlockSpec(a.shape, lambda j: (0,) * a.ndim)
    args = [row(mu), row(w0), w2.astype(BF16), row(a0), a2.astype(BF16), g2.astype(BF16),
            row(kkw), row(ka), row(rk), row(gnw), row(gnb), bd]
    ct = CHUNK * RW_STEP_CHUNKS
    return pl.pallas_call(
        _rwkv_kernel,
        grid=(s // ct,),
        in_specs=[pl.BlockSpec((b, ct, cols), lambda j: (0, j, 0))] + [full(a) for a in args],
        out_specs=pl.BlockSpec((b, ct, W), lambda j: (0, j, 0)),
        out_shape=jax.ShapeDtypeStruct((b, s, W), BF16),
        scratch_shapes=[
            pltpu.VMEM((b, W // RW_GW, RW_HEAD_DIM, RW_GW), F32),
            pltpu.VMEM((b, cols), F32),
        ],
        compiler_params=pltpu.CompilerParams(
            dimension_semantics=("arbitrary",), vmem_limit_bytes=VMEM_LIMIT),
        name="rwkv7",
    )(rw_proj, *args)


GLA_STEP_CHUNKS = 2


def _group_mask(rows, row_group, cols, col_group):
    r = lax.shift_right_logical(lax.broadcasted_iota(jnp.int32, (rows, cols), 0), _log2(row_group))
    c = lax.shift_right_logical(lax.broadcasted_iota(jnp.int32, (rows, cols), 1), _log2(col_group))
    return jnp.where(r == c, 1.0, 0.0).astype(BF16)


def _gla_kernel(x_ref, gk2_ref, gkb_ref, ng_ref, y_ref, state_ref):
    C, H, DK, DV, SB = CHUNK, GLA_HEADS, GLA_KEY_DIM, GLA_VAL_DIM, GLA_SUB
    QW, VW = GLA_QK_WIDTH, GLA_V_WIDTH
    nb, ct = x_ref.shape[0], x_ref.shape[1]
    nch = ct // C
    R = nb * ct

    @pl.when(pl.program_id(0) == 0)
    def _():
        state_ref[...] = jnp.zeros_like(state_ref)

    x = x_ref[...].reshape(R, x_ref.shape[2])
    q = x[:, 0:QW] * (DK ** -0.5)
    k = x[:, QW:2 * QW]
    v = x[:, 2 * QW:2 * QW + VW]
    g = x[:, 2 * QW + VW:2 * QW + 2 * VW]
    dgk = x[:, 2 * QW + 2 * VW:]

    la = -_softplus(-(_mm(dgk, gk2_ref[...]) + gkb_ref[...])) * (1.0 / GLA_GATE_TEMP)
    rr = lax.broadcasted_iota(jnp.int32, (R, R), 0)
    cc = lax.broadcasted_iota(jnp.int32, (R, R), 1)
    tri_seq = jnp.where((rr >= cc) & (rr // C == cc // C), 1.0, 0.0).astype(BF16)
    b = _mm_rhs_split(tri_seq, la, 3)
    b_last = jnp.concatenate(
        [jnp.broadcast_to(b[(j + 1) * C - 1:(j + 1) * C, :], (C, QW)) for j in range(R // C)], axis=0)
    q_e = q * jnp.exp(b)
    k_e = k * jnp.exp(b_last - b)
    w_c = jnp.exp(b_last)

    units = [(bi, ch) for ch in range(nch) for bi in range(nb)]
    row0 = {u: u[0] * ct + u[1] * C for u in units}

    o_off = {u: None for u in units}
    for s in (C // 2, C // 4, C // 8):
        bd_keys = _block_diag_fn(s, H, DK)
        bd_vals = _block_diag_fn(s, H, DV)
        jobs = [(u, row0[u] + m * 2 * s) for u in units for m in range(C // (2 * s))]
        att = []
        for u, c0 in jobs:
            ref = b[c0 + s - 1:c0 + s]
            q_s = q[c0 + s:c0 + 2 * s] * jnp.exp(b[c0 + s:c0 + 2 * s] - ref)
            k_s = k[c0:c0 + s] * jnp.exp(ref - b[c0:c0 + s])
            att.append(_mm_nt(q_s, bd_keys(k_s)))
        outs = [_mm(att[j], bd_vals(v[c0:c0 + s])) for j, (u, c0) in enumerate(jobs)]
        zero = jnp.zeros((s, VW), F32)
        for u in units:
            pieces = []
            for j, (uj, c0) in enumerate(jobs):
                if uj == u:
                    pieces += [zero, outs[j]]
            level = jnp.concatenate(pieces, axis=0)
            o_off[u] = level if o_off[u] is None else o_off[u] + level

    nblk = C // SB
    ii = lax.broadcasted_iota(jnp.int32, (nblk, SB, SB, 2 * DK), 1)
    jj = lax.broadcasted_iota(jnp.int32, (nblk, SB, SB, 2 * DK), 2)
    causal4 = jj <= ii
    sel = _group_mask(C, 1, C * SB, SB)
    pair_ones = _group_mask(2 * DK, DK, 2 * DV, DV)
    att2 = {}
    for u in units:
        r0 = row0[u]
        for pr in range(H // 2):
            sl2 = slice(2 * pr * DK, 2 * (pr + 1) * DK)
            q2, k2, b2 = q[r0:r0 + C, sl2], k[r0:r0 + C, sl2], b[r0:r0 + C, sl2]
            dec = jnp.exp(jnp.where(
                causal4, b2.reshape(nblk, SB, 1, 2 * DK) - b2.reshape(nblk, 1, SB, 2 * DK), NEG_BIG))
            pw = q2.reshape(nblk, SB, 1, 2 * DK) * k2.reshape(nblk, 1, SB, 2 * DK) * dec
            att2[(u, pr)] = _mm(pw.reshape(C * SB, 2 * DK), pair_ones)
    o_diag = {}
    for u in units:
        r0 = row0[u]
        heads = []
        for hd in range(H):
            v_h = v[r0:r0 + C, hd * DV:(hd + 1) * DV]
            v_rep = jnp.broadcast_to(v_h.reshape(nblk, 1, SB, DV), (nblk, SB, SB, DV)).reshape(C * SB, DV)
            heads.append(_mm(sel, att2[(u, hd // 2)][:, (hd % 2) * DV:(hd % 2 + 1) * DV] * v_rep))
        o_diag[u] = jnp.concatenate(heads, axis=1)

    bd_state = _block_diag_fn(DV, H, DK)
    diag_state = _diag_blocks_fn(DV, H, DK)
    s4 = [state_ref[bi] for bi in range(nb)]
    o_int = {}
    for ch in range(nch):
        for bi in range(nb):
            r0 = row0[(bi, ch)]
            o_int[(bi, ch)] = _mm_nt(q_e[r0:r0 + C], bd_state(s4[bi]))
        nxt = []
        for bi in range(nb):
            r0 = row0[(bi, ch)]
            nxt.append(s4[bi] * w_c[r0:r0 + 1] + diag_state(_mm_tn(v[r0:r0 + C], k_e[r0:r0 + C])))
        s4 = nxt
    for bi in range(nb):
        state_ref[bi] = s4[bi]

    rows = []
    for bi in range(nb):
        for ch in range(nch):
            u = (bi, ch)
            o = o_int[u] + o_diag[u] + o_off[u]
            heads = []
            for hd in range(H):
                oh = o[:, hd * DV:(hd + 1) * DV]
                heads.append(oh * lax.rsqrt(jnp.mean(oh * oh, axis=-1, keepdims=True) + NORM_EPS))
            rows.append(jnp.concatenate(heads, axis=1))
    o = jnp.concatenate(rows, axis=0)
    y = o * ng_ref[...] * (g * _sigmoid(g))
    y_ref[...] = y.astype(y_ref.dtype).reshape(y_ref.shape)


def _gla(gla_proj, gk2, gkb, ng):
    b, s, cols = gla_proj.shape
    gk2p = jnp.zeros((LANES, GLA_QK_WIDTH), F32).at[:GLA_GATE_RANK].set(gk2).astype(BF16)
    args = [gk2p, gkb.reshape(1, -1), ng.reshape(1, -1)]
    full = lambda a: pl.BlockSpec(a.shape, lambda j: (0,) * a.ndim)
    ct = CHUNK * GLA_STEP_CHUNKS
    return pl.pallas_call(
        _gla_kernel,
        grid=(s // ct,),
        in_specs=[pl.BlockSpec((b, ct, cols), lambda j: (0, j, 0))] + [full(a) for a in args],
        out_specs=pl.BlockSpec((b, ct, GLA_V_WIDTH), lambda j: (0, j, 0)),
        out_shape=jax.ShapeDtypeStruct((b, s, GLA_V_WIDTH), BF16),
        scratch_shapes=[pltpu.VMEM((b, GLA_VAL_DIM, GLA_QK_WIDTH), F32)],
        compiler_params=pltpu.CompilerParams(
            dimension_semantics=("arbitrary",), vmem_limit_bytes=VMEM_LIMIT),
        name="gla",
    )(gla_proj, *args)


def _outproj_kernel(yr_ref, yg_ref, x_ref, wor_ref, wog_ref, nf_ref, wrh_ref, wrl_ref, br_ref,
                    h_ref, u3_ref, route_ref, cnt_ref, carry_ref):
    tm, d = x_ref.shape

    @pl.when(pl.program_id(0) == 0)
    def _():
        carry_ref[...] = jnp.zeros_like(carry_ref)

    h = (x_ref[...] + jnp.dot(yr_ref[...], wor_ref[...], preferred_element_type=F32)
         + jnp.dot(yg_ref[...], wog_ref[...], preferred_element_type=F32))
    h_ref[...] = h
    u = _rmsnorm(h, nf_ref[...])
    for s in range(d // LANES):
        u3_ref[s] = u[:, s * LANES:(s + 1) * LANES]
    u_hi, u_lo = _split(u, 2)
    logits = (jnp.dot(u_hi, wrh_ref[...], preferred_element_type=F32)
              + jnp.dot(u_hi, wrl_ref[...], preferred_element_type=F32)
              + jnp.dot(u_lo, wrh_ref[...], preferred_element_type=F32)) + br_ref[...]
    lane = lax.broadcasted_iota(jnp.int32, logits.shape, 1)
    rest = logits
    picks, idxs, vals = [], [], []
    for r in range(TOP_K):
        m = jnp.max(rest, axis=-1, keepdims=True)
        idx = jnp.min(jnp.where(rest == m, lane, LANES), axis=-1, keepdims=True)
        pick = lane == idx
        picks.append(pick)
        idxs.append(idx)
        vals.append(m)
        rest = jnp.where(pick, -jnp.inf, rest)
    denom = jnp.ones_like(vals[0])
    for r in range(1, TOP_K):
        denom = denom + jnp.exp(vals[r] - vals[0])
    sel = jnp.zeros(logits.shape, F32)
    for pick in picks:
        sel = sel + jnp.where(pick, 1.0, 0.0)
    prefix = _mm(jnp.where(_tri(tm, True), 1.0, 0.0), sel) + carry_ref[...]
    carry_ref[...] = carry_ref[...] + jnp.sum(sel, axis=0, keepdims=True)
    cnt_ref[...] = carry_ref[...]
    route = jnp.zeros(logits.shape, F32)
    for r in range(TOP_K):
        gate = jnp.exp(vals[r] - vals[0]) / denom
        rank = jnp.sum(jnp.where(picks[r], prefix, 0.0), axis=-1, keepdims=True)
        route = jnp.where(lane == r, gate, route)
        route = jnp.where(lane == TOP_K + r, idxs[r].astype(F32), route)
        route = jnp.where(lane == 2 * TOP_K + r, rank, route)
    route_ref[...] = route


def _out_proj(y_rw, y_gla, xf, wo_r, wo_g, nf, wr_hi, wr_lo, br, tm):
    n, d = xf.shape
    full = lambda a: pl.BlockSpec(a.shape, lambda i: (0,) * a.ndim)
    tile = lambda w: pl.BlockSpec((tm, w), lambda i: (i, 0))
    return pl.pallas_call(
        _outproj_kernel,
        grid=(n // tm,),
        in_specs=[tile(y_rw.shape[1]), tile(y_gla.shape[1]), tile(d), full(wo_r), full(wo_g), full(nf),
                  full(wr_hi), full(wr_lo), full(br)],
        out_specs=[tile(d), pl.BlockSpec((d // LANES, tm, LANES), lambda i: (0, i, 0)), tile(LANES),
                   pl.BlockSpec((1, LANES), lambda i: (0, 0))],
        out_shape=[jax.ShapeDtypeStruct((n, d), F32), jax.ShapeDtypeStruct((d // LANES, n, LANES), F32),
                   jax.ShapeDtypeStruct((n, LANES), F32), jax.ShapeDtypeStruct((1, LANES), F32)],
        scratch_shapes=[pltpu.VMEM((1, LANES), F32)],
        compiler_params=pltpu.CompilerParams(
            dimension_semantics=("arbitrary",), vmem_limit_bytes=VMEM_LIMIT),
        name="out_proj",
    )(y_rw, y_gla, xf, wo_r, wo_g, nf, wr_hi, wr_lo, br)


_DISPATCH_BUFS = 3


def _dispatch_kernel(row_ref, pad_ref, u3_hbm, xs_hbm, ubuf, zbuf, in_sems, out_sems, zsem, *, tm):
    i = pl.program_id(0)
    nsteps = pl.num_programs(0)
    nbuf = _DISPATCH_BUFS
    n_blocks = pad_ref.shape[0]

    @pl.when(i == 0)
    def _():
        zbuf[...] = jnp.zeros_like(zbuf)

        def zero_block(blk_i):
            return pltpu.make_async_copy(zbuf, xs_hbm.at[pl.ds(blk_i * EXPERT_BLOCK, EXPERT_BLOCK)], zsem)

        def start(blk_i, c):
            @pl.when(pad_ref[blk_i] != 0)
            def _():
                zero_block(blk_i).start()
            return c
        lax.fori_loop(0, n_blocks, start, 0)

        def finish(blk_i, c):
            @pl.when(pad_ref[blk_i] != 0)
            def _():
                zero_block(blk_i).wait()
            return c
        lax.fori_loop(0, n_blocks, finish, 0)

    def load(step, slot_):
        return pltpu.make_async_copy(u3_hbm.at[:, pl.ds(step * tm, tm), :], ubuf.at[slot_], in_sems.at[slot_])

    def wait_scatter(slot_):
        for _ in range(TOP_K):
            pltpu.make_async_copy(u3_hbm.at[:, pl.ds(0, tm), :], ubuf.at[slot_], out_sems.at[slot_]).wait()

    cur = lax.rem(i, nbuf)
    nxt = lax.rem(i + 1, nbuf)

    @pl.when(i == 0)
    def _():
        load(0, 0).start()

    @pl.when(i >= nbuf - 1)
    def _():
        wait_scatter(nxt)

    @pl.when(i + 1 < nsteps)
    def _():
        load(i + 1, nxt).start()

    load(i, cur).wait()

    def issue(r, c):
        src = ubuf.at[cur, :, r, :]
        for kk in range(TOP_K):
            slot = row_ref[(i * tm + r) * TOP_K + kk]
            dst = xs_hbm.at[slot]
            pltpu.make_async_copy(src, dst, out_sems.at[cur]).start(priority=kk % 2)
        return c
    lax.fori_loop(0, tm, issue, 0, unroll=4)

    @pl.when(i == nsteps - 1)
    def _():
        for back in range(nbuf - 1):
            @pl.when(i - back >= 0)
            def _():
                wait_scatter(lax.rem(i - back + nbuf, nbuf))


def _dispatch(slots, pad_blocks, u3, tm):
    ns, n, ln = u3.shape
    blk = EXPERT_BLOCK
    grid_spec = pltpu.PrefetchScalarGridSpec(
        num_scalar_prefetch=2,
        grid=(n // tm,),
        in_specs=[pl.BlockSpec(memory_space=pl.ANY)],
        out_specs=pl.BlockSpec(memory_space=pl.ANY),
        scratch_shapes=[pltpu.VMEM((_DISPATCH_BUFS, ns, tm, ln), u3.dtype),
                        pltpu.VMEM((blk, ns, ln), u3.dtype),
                        pltpu.SemaphoreType.DMA((_DISPATCH_BUFS,)),
                        pltpu.SemaphoreType.DMA((_DISPATCH_BUFS,)),
                        pltpu.SemaphoreType.DMA(())],
    )
    return pl.pallas_call(
        functools.partial(_dispatch_kernel, tm=tm),
        grid_spec=grid_spec,
        out_shape=jax.ShapeDtypeStruct((pad_blocks.shape[0] * blk, ns, ln), u3.dtype),
        compiler_params=pltpu.CompilerParams(dimension_semantics=("arbitrary",), has_side_effects=True),
        name="moe_dispatch",
    )(slots, pad_blocks, u3)


def _moe_kernel(be_ref, nxt_ref, nused_ref, xs_ref, w1_hbm, b1_ref, w2_hbm, b2_ref, ys_ref,
                w1f, w2f, w1b, w2b, sems):
    i = pl.program_id(0)
    f = w2b.shape[0]
    ns = xs_ref.shape[1]
    e = be_ref[i]
    e_prev = be_ref[jnp.maximum(i - 1, 0)]

    def fetch(expert):
        return (pltpu.make_async_copy(w1_hbm.at[expert], w1f, sems.at[0]),
                pltpu.make_async_copy(w2_hbm.at[expert], w2f, sems.at[1]))

    @pl.when(i == 0)
    def _():
        for cp in fetch(e):
            cp.start()

    @pl.when(jnp.logical_or(i == 0, e != e_prev))
    def _():
        for cp in fetch(e):
            cp.wait()
        w1b[...] = w1f[...].astype(BF16)
        w2b[...] = w2f[...].astype(BF16)

        @pl.when(nxt_ref[i] >= 0)
        def _():
            for cp in fetch(nxt_ref[i]):
                cp.start()

    @pl.when(i < nused_ref[0])
    def _():
        xb = jnp.concatenate([xs_ref[:, s, :] for s in range(ns)], axis=1).astype(BF16)
        hgl = jnp.dot(xb, w1b[...], preferred_element_type=F32) + b1_ref[0]
        x_glu = jnp.minimum(hgl[:, :f], SWIGLU_LIMIT)
        x_lin = jnp.clip(hgl[:, f:], -SWIGLU_LIMIT, SWIGLU_LIMIT)
        act = (x_lin + 1.0) * (x_glu * _sigmoid(SWIGLU_ALPHA * x_glu))
        y = jnp.dot(act.astype(BF16), w2b[...], preferred_element_type=F32) + b2_ref[0]
        for s in range(ns):
            ys_ref[s] = y[:, s * LANES:(s + 1) * LANES]

    @pl.when(i >= nused_ref[0])
    def _():
        ys_ref[...] = jnp.zeros_like(ys_ref)


def _moe_experts(block_e, next_e, n_used, xs, w1, b1, w2, b2):
    n_rows, ns, ln = xs.shape
    ne, d, f2 = w1.shape
    f = w2.shape[1]
    nb = block_e.shape[0]
    blk = EXPERT_BLOCK
    grid_spec = pltpu.PrefetchScalarGridSpec(
        num_scalar_prefetch=3,
        grid=(nb,),
        in_specs=[
            pl.BlockSpec((blk, ns, ln), lambda i, be, nx, nu: (i, 0, 0)),
            pl.BlockSpec(memory_space=pl.ANY),
            pl.BlockSpec((1, 1, f2), lambda i, be, nx, nu: (be[i], 0, 0)),
            pl.BlockSpec(memory_space=pl.ANY),
            pl.BlockSpec((1, 1, d), lambda i, be, nx, nu: (be[i], 0, 0)),
        ],
        out_specs=pl.BlockSpec((ns, blk, ln), lambda i, be, nx, nu: (0, i, 0)),
        scratch_shapes=[pltpu.VMEM((d, f2), w1.dtype), pltpu.VMEM((f, d), w2.dtype),
                        pltpu.VMEM((d, f2), BF16), pltpu.VMEM((f, d), BF16),
                        pltpu.SemaphoreType.DMA((2,))],
    )
    return pl.pallas_call(
        _moe_kernel,
        grid_spec=grid_spec,
        out_shape=jax.ShapeDtypeStruct((ns, n_rows, ln), F32),
        compiler_params=pltpu.CompilerParams(
            dimension_semantics=("arbitrary",), vmem_limit_bytes=VMEM_LIMIT),
        name="moe_experts",
    )(block_e, next_e, n_used, xs, w1, b1.reshape(ne, 1, f2), w2, b2.reshape(ne, 1, d))


def _final_kernel(row_ref, ys_hbm, h_ref, route_ref, p_ref, npl_ref, wg_ref, wp_ref, nfin_ref, out_ref,
                  gbuf, sems, *, last_layer):
    i = pl.program_id(0)
    tm = h_ref.shape[0]
    ns = gbuf.shape[2]
    cur = lax.rem(i, 2)

    last = pl.num_programs(0) - 1

    def issue_row(step, buf, r):
        for kk in range(TOP_K):
            slot = row_ref[(step * tm + r) * TOP_K + kk]
            src = ys_hbm.at[:, pl.ds(slot, 1), :]
            pltpu.make_async_copy(src, gbuf.at[buf, kk, :, pl.ds(r, 1), :],
                                  sems.at[buf]).start(priority=kk % 2)

    def wait_tile(buf):
        for kk in range(TOP_K):
            pltpu.make_async_copy(ys_hbm.at[:, pl.ds(0, tm), :], gbuf.at[buf, kk], sems.at[buf]).wait()

    def gather(step, buf):
        def issue(r, c):
            issue_row(step, buf, r)
            return c
        lax.fori_loop(0, tm, issue, 0, unroll=4)

    @pl.when(i == 0)
    def _():
        gather(0, 0)

    wait_tile(cur)
    nxt_step = jnp.minimum(i + 1, last)
    for r in range(tm):
        issue_row(nxt_step, 1 - cur, r)

    route = route_ref[...]
    h = h_ref[...]
    for kk in range(TOP_K):
        yk = jnp.concatenate([gbuf[cur, kk, s] for s in range(ns)], axis=1)
        h = h + route[:, kk:kk + 1] * yk
    u = _rmsnorm(h, npl_ref[...])
    gate = _sigmoid(_mm(u, wg_ref[...]))
    h = h + gate * _mm(p_ref[...], wp_ref[...])
    out_ref[...] = _rmsnorm(h, nfin_ref[...]) if last_layer else h

    @pl.when(i == last)
    def _():
        wait_tile(1 - cur)


def _final(slots, ys, h1, route, pf, npl, wg, wp, nfin, tm, last_layer):
    n, d = h1.shape
    pd = pf.shape[1]
    grid_spec = pltpu.PrefetchScalarGridSpec(
        num_scalar_prefetch=1,
        grid=(n // tm,),
        in_specs=[
            pl.BlockSpec(memory_space=pl.ANY),
            pl.BlockSpec((tm, d), lambda i, s: (i, 0)),
            pl.BlockSpec((tm, LANES), lambda i, s: (i, 0)),
            pl.BlockSpec((tm, pd), lambda i, s: (i, 0)),
            pl.BlockSpec((1, d), lambda i, s: (0, 0)),
            pl.BlockSpec((d, d), lambda i, s: (0, 0)),
            pl.BlockSpec((pd, d), lambda i, s: (0, 0)),
            pl.BlockSpec((1, d), lambda i, s: (0, 0)),
        ],
        out_specs=pl.BlockSpec((tm, d), lambda i, s: (i, 0)),
        scratch_shapes=[pltpu.VMEM((2, TOP_K, d // ys.shape[2], tm, ys.shape[2]), F32),
                        pltpu.SemaphoreType.DMA((2,))],
    )
    return pl.pallas_call(
        functools.partial(_final_kernel, last_layer=last_layer),
        grid_spec=grid_spec,
        out_shape=jax.ShapeDtypeStruct((n, d), F32),
        compiler_params=pltpu.CompilerParams(
            dimension_semantics=("arbitrary",), vmem_limit_bytes=VMEM_LIMIT),
        name="final",
    )(slots, ys, h1, route, pf, npl, wg, wp, nfin)


def _routing(route, counts, n_experts):
    n = route.shape[0]
    blk = EXPERT_BLOCK
    nb = (n * TOP_K + n_experts * (blk - 1) + blk - 1) // blk
    counts = counts[0, :n_experts].astype(jnp.int32)
    pcounts = ((counts + blk - 1) // blk) * blk
    pend = jnp.cumsum(pcounts)
    pstart = pend - pcounts
    top_i = route[:, TOP_K:2 * TOP_K].astype(jnp.int32)
    rank = route[:, 2 * TOP_K:3 * TOP_K].astype(jnp.int32)
    onehot = top_i[:, :, None] == jnp.arange(n_experts, dtype=jnp.int32)[None, None, :]
    slots = jnp.sum(jnp.where(onehot, pstart[None, None, :], 0), axis=-1) + rank
    block_rows = jnp.arange(nb, dtype=jnp.int32) * blk
    block_e = jnp.sum(block_rows[:, None] >= pend[None, :], axis=1)
    block_e = jnp.clip(block_e, 0, n_experts - 1).astype(jnp.int32)
    n_used = (pend[-1] // blk).astype(jnp.int32).reshape(1)
    blocks = jnp.arange(nb, dtype=jnp.int32)
    onehot_b = block_e[:, None] == jnp.arange(n_experts, dtype=jnp.int32)[None, :]
    run_end = jnp.sum(jnp.where(onehot_b, pend[None, :], 0), axis=1) // blk
    run_end = jnp.where(blocks >= n_used[0], nb, run_end)
    follow = jnp.sum(jnp.where(blocks[None, :] == run_end[:, None], block_e[None, :], 0), axis=1)
    next_e = jnp.where((run_end < nb) & (follow != block_e), follow, -1).astype(jnp.int32)
    pad_blocks = ((blocks + 1 == run_end) | (blocks >= n_used[0])).astype(jnp.int32)
    return block_e, next_e, n_used, slots.reshape(-1).astype(jnp.int32), pad_blocks


def kernel(x, p, norm_mix, w_in, shift_mu, rw_w0, rw_w2, rw_a0, rw_a2, rw_g2, rw_kk, rw_ka, rw_rk,
           rw_gn_w, rw_gn_b, gla_gk2, gla_gk_b, gla_norm, w_out, norm_ffn, w_router, b_router,
           w1, b1, w2, b2, norm_ple, w_ple_gate, w_ple, norm_final):
    bsz, seq, d = x.shape
    n = bsz * seq
    depth = w_in.shape[0]
    n_experts = w_router.shape[-1]
    tm = 256
    tm_route = min(512, n)
    h = x.reshape(n, d)
    for l in range(depth):
        w_rw = w_in[l][:, :RW_COLS].astype(BF16)
        w_gla = jnp.pad(w_in[l][:, RW_COLS:], ((0, 0), (0, LANES - GLA_GATE_RANK))).astype(BF16)
        rw_proj, gla_proj = _in_proj(h, norm_mix[l].reshape(1, d), w_rw, w_gla, tm_route)
        y_rw = _rwkv(rw_proj.reshape(bsz, seq, -1), shift_mu[l], rw_w0[l], rw_w2[l], rw_a0[l], rw_a2[l],
                     rw_g2[l], rw_kk[l], rw_ka[l], rw_rk[l], rw_gn_w[l], rw_gn_b[l])
        y_gla = _gla(gla_proj.reshape(bsz, seq, -1), gla_gk2[l], gla_gk_b[l], gla_norm[l])

        wr = jnp.pad(w_router[l], ((0, 0), (0, LANES - n_experts)))
        wr_hi = wr.astype(BF16)
        wr_lo = (wr - wr_hi.astype(F32)).astype(BF16)
        br = jnp.pad(b_router[l], (0, LANES - n_experts), constant_values=NEG_BIG).reshape(1, LANES)
        wo = w_out[l].astype(BF16)
        h1, u3, route, counts = _out_proj(y_rw.reshape(n, -1), y_gla.reshape(n, -1), h, wo[:RW_WIDTH],
                                          wo[RW_WIDTH:], norm_ffn[l].reshape(1, d), wr_hi, wr_lo, br, tm_route)

        block_e, next_e, n_used, slots, pad_blocks = _routing(route, counts, n_experts)
        xs = _dispatch(slots, pad_blocks, u3, tm_route)
        ys = _moe_experts(block_e, next_e, n_used, xs, w1[l], b1[l], w2[l], b2[l])
        h = _final(slots, ys, h1, route, p[l].reshape(n, -1), norm_ple[l].reshape(1, d),
                   w_ple_gate[l].astype(BF16), w_ple[l].astype(BF16), norm_final.reshape(1, d), tm,
                   l == depth - 1)
    return h.reshape(bsz, seq, d)
```

```python
import functools
import math

import jax
import jax.numpy as jnp
from jax import lax
from jax.experimental import pallas as pl
from jax.experimental.pallas import tpu as pltpu

F32 = jnp.float32
BF16 = jnp.bfloat16

CHUNK = 64
RW_HEADS = 8
RW_HEAD_DIM = 64
RW_WIDTH = RW_HEADS * RW_HEAD_DIM
RW_DECAY_LORA = 64
RW_ICLR_LORA = 64
RW_GATE_LORA = 128
RW_COLS = 3 * RW_WIDTH + RW_DECAY_LORA + RW_ICLR_LORA + RW_GATE_LORA
RW_GN_EPS = 64e-5
GLA_HEADS = 4
GLA_KEY_DIM = 64
GLA_VAL_DIM = 128
GLA_QK_WIDTH = GLA_HEADS * GLA_KEY_DIM
GLA_V_WIDTH = GLA_HEADS * GLA_VAL_DIM
GLA_GATE_RANK = 16
GLA_GATE_TEMP = 16.0
GLA_SUB = 8
LANES = 128
GLA_COLS_PAD = 2 * GLA_QK_WIDTH + 2 * GLA_V_WIDTH + LANES
TOP_K = 4
EXPERT_BLOCK = 256
SWIGLU_ALPHA = 1.702
SWIGLU_LIMIT = 7.0
NORM_EPS = 1e-6
NEG_BIG = -1e30
VMEM_LIMIT = 56 * 1024 * 1024


def _mm(a, b):
    return jnp.dot(a.astype(BF16), b.astype(BF16), preferred_element_type=F32)


def _mm_nt(a, b):
    return lax.dot_general(a.astype(BF16), b.astype(BF16), (((1,), (1,)), ((), ())),
                           preferred_element_type=F32)


def _mm_tn(a, b):
    return lax.dot_general(a.astype(BF16), b.astype(BF16), (((0,), (0,)), ((), ())),
                           preferred_element_type=F32)


def _split(a, n):
    parts = []
    rem = a
    for _ in range(n):
        p = rem.astype(BF16)
        parts.append(p)
        rem = rem - p.astype(F32)
    return parts


def _mm_lhs_split(a, b_bf16, n):
    out = None
    for p in _split(a, n):
        t = jnp.dot(p, b_bf16, preferred_element_type=F32)
        out = t if out is None else out + t
    return out


def _mm_rhs_split(a_bf16, b, n):
    out = None
    for p in _split(b, n):
        t = jnp.dot(a_bf16, p, preferred_element_type=F32)
        out = t if out is None else out + t
    return out


def _rmsnorm(x, g):
    return x * lax.rsqrt(jnp.mean(x * x, axis=-1, keepdims=True) + NORM_EPS) * g


def _softplus(x):
    return jnp.maximum(x, 0.0) + jnp.log(1.0 + jnp.exp(-jnp.abs(x)))


def _sigmoid(x):
    return 1.0 / (1.0 + jnp.exp(-x))


def _log2(n):
    assert n & (n - 1) == 0
    return n.bit_length() - 1


def _head_keep(rows, head_lanes, dtype):
    lane_head = lax.shift_right_logical(lax.broadcasted_iota(jnp.int32, (rows, LANES), 1), _log2(head_lanes))
    return [jnp.where(lane_head == j, 1.0, 0.0).astype(dtype) for j in range(LANES // head_lanes)]


def _block_diag_fn(rows, n_heads, head_lanes):
    n_tiles = n_heads * head_lanes // LANES
    zero = jnp.zeros((rows, LANES), BF16)
    keep = _head_keep(rows, head_lanes, BF16) if head_lanes < LANES else None

    def f(y):
        yb = y.astype(BF16)
        blocks = []
        for hd in range(n_heads):
            tiles = [zero] * n_tiles
            if keep is not None:
                t = hd * head_lanes // LANES
                tiles[t] = yb[:, t * LANES:(t + 1) * LANES] * keep[hd % len(keep)]
            else:
                for t in range(hd * head_lanes // LANES, (hd + 1) * head_lanes // LANES):
                    tiles[t] = yb[:, t * LANES:(t + 1) * LANES]
            blocks.append(jnp.concatenate(tiles, axis=1))
        return jnp.concatenate(blocks, axis=0)
    return f


def _diag_blocks_fn(rows, n_heads, head_lanes):
    assert head_lanes < LANES
    keep = _head_keep(rows, head_lanes, F32)
    per = len(keep)

    def f(full):
        tiles = []
        for t in range(n_heads * head_lanes // LANES):
            acc = None
            for j in range(per):
                hd = t * per + j
                term = full[hd * rows:(hd + 1) * rows, t * LANES:(t + 1) * LANES] * keep[j]
                acc = term if acc is None else acc + term
            tiles.append(acc)
        return jnp.concatenate(tiles, axis=1)
    return f


def _tri(n, strict):
    r = lax.broadcasted_iota(jnp.int32, (n, n), 0)
    c = lax.broadcasted_iota(jnp.int32, (n, n), 1)
    return (r > c) if strict else (r >= c)


def _inproj_kernel(x_ref, g_ref, wr_ref, wg_ref, rw_ref, gla_ref):
    u = _rmsnorm(x_ref[...], g_ref[...]).astype(BF16)
    rw_ref[...] = jnp.dot(u, wr_ref[...], preferred_element_type=F32)
    gla_ref[...] = jnp.dot(u, wg_ref[...], preferred_element_type=F32)


def _seq_tile(tm, width, tiles_per_seq):
    return pl.BlockSpec((None, tm, width), lambda i: (i // tiles_per_seq, i % tiles_per_seq, 0))


def _in_proj(x3, g, w_rw, w_gla, tm):
    b, s, d = x3.shape
    assert s % tm == 0
    tps = s // tm
    return pl.pallas_call(
        _inproj_kernel,
        grid=(b * tps,),
        in_specs=[
            _seq_tile(tm, d, tps),
            pl.BlockSpec((1, d), lambda i: (0, 0)),
            pl.BlockSpec(w_rw.shape, lambda i: (0, 0)),
            pl.BlockSpec(w_gla.shape, lambda i: (0, 0)),
        ],
        out_specs=[_seq_tile(tm, w_rw.shape[1], tps), _seq_tile(tm, w_gla.shape[1], tps)],
        out_shape=[
            jax.ShapeDtypeStruct((b, s, w_rw.shape[1]), F32),
            jax.ShapeDtypeStruct((b, s, w_gla.shape[1]), F32),
        ],
        compiler_params=pltpu.CompilerParams(
            dimension_semantics=("arbitrary",), vmem_limit_bytes=VMEM_LIMIT),
        name="in_proj",
    )(x3, g, w_rw, w_gla)


RW_GROUP = 4
RW_GW = RW_GROUP * RW_HEAD_DIM
RW_STEP_CHUNKS = 2
RW_DECAY_SCALE = math.exp(-0.5)


def _rwkv_kernel(x_ref, mu_ref, w0_ref, w2_ref, a0_ref, a2_ref, g2_ref, kkw_ref, ka_ref,
                 rk_ref, gnw_ref, gnb_ref, bd_ref, y_ref, state_ref, carry_ref):
    C, D, W, GW = CHUNK, RW_HEAD_DIM, RW_WIDTH, RW_GW
    nb, ct = x_ref.shape[0], x_ref.shape[1]
    nch = ct // C
    R = nb * ct

    @pl.when(pl.program_id(0) == 0)
    def _():
        state_ref[...] = jnp.zeros_like(state_ref)
        carry_ref[...] = jnp.zeros_like(carry_ref)

    x = x_ref[...].reshape(R, x_ref.shape[2])
    row = lax.broadcasted_iota(jnp.int32, x.shape, 0)
    prev = pltpu.roll(x, 1, axis=0)
    for b in range(nb):
        prev = jnp.where(row == b * ct, carry_ref[b:b + 1, :], prev)
        carry_ref[b:b + 1, :] = x[(b + 1) * ct - 1:(b + 1) * ct, :]
    h = x + (prev - x) * mu_ref[...]

    r = h[:, 0:W]
    k = h[:, W:2 * W]
    v = h[:, 2 * W:3 * W]
    o0 = 3 * W
    dw = h[:, o0:o0 + RW_DECAY_LORA]
    da = h[:, o0 + RW_DECAY_LORA:o0 + RW_DECAY_LORA + RW_ICLR_LORA]
    dg = h[:, o0 + RW_DECAY_LORA + RW_ICLR_LORA:]

    bd_g = bd_ref[...]

    def seg_sum(t):
        return jnp.concatenate(
            [_mm_lhs_split(t[:, g * GW:(g + 1) * GW], bd_g, 2) for g in range(W // GW)], axis=1)

    lw = -RW_DECAY_SCALE * _sigmoid(w0_ref[...] + _mm(jnp.tanh(dw), w2_ref[...]))
    iclr = _sigmoid(a0_ref[...] + _mm(da, a2_ref[...]))
    gate = _mm(_sigmoid(dg), g2_ref[...])

    kk = k * kkw_ref[...]
    kk = kk * jnp.minimum(lax.rsqrt(seg_sum(kk * kk)), 1e12)
    k2 = k * (1.0 + (iclr - 1.0) * ka_ref[...])

    rr = lax.broadcasted_iota(jnp.int32, (R, R), 0)
    cc = lax.broadcasted_iota(jnp.int32, (R, R), 1)
    tri_seq = jnp.where((rr >= cc) & (rr // C == cc // C), 1.0, 0.0).astype(BF16)
    cw = _mm_rhs_split(tri_seq, lw, 3)
    cw_last = jnp.concatenate(
        [jnp.broadcast_to(cw[(j + 1) * C - 1:(j + 1) * C, :], (C, W)) for j in range(R // C)], axis=0)
    e_cw = jnp.exp(cw)
    e_ncw = jnp.exp(-cw)
    e_rem = jnp.exp(cw_last - cw)
    kka = kk * iclr
    a_t = -kk * jnp.exp(cw - lw)
    r_t = r * e_cw
    b_t = kka * e_ncw
    k_t = k2 * e_ncw
    b_h = kka * e_rem
    k_h = k2 * e_rem
    w_c = jnp.exp(cw_last)

    ti = lax.broadcasted_iota(jnp.int32, (C, GW), 0)
    si = jnp.bitwise_and(lax.broadcasted_iota(jnp.int32, (C, GW), 1), D - 1)
    strict = ti > si
    incl = ti >= si
    eye = jnp.where(ti == si, 1.0, 0.0)

    bdiag = _block_diag_fn(C, RW_GROUP, D)
    diag_blocks = _diag_blocks_fn(D, RW_GROUP, D)

    ng = W // GW
    chains = [(b, g, ch) for ch in range(nch) for b in range(nb) for g in range(ng)]

    def part(t, c):
        b, g, ch = c
        r0 = b * ct + ch * C
        return t[r0:r0 + C, g * GW:(g + 1) * GW]

    n = range(len(chains))
    a4 = [part(a_t, c) for c in chains]
    r4 = [part(r_t, c) for c in chains]
    v4 = [part(v, c) for c in chains]
    ar = [jnp.concatenate([a4[i], r4[i]], axis=0) for i in n]
    bd_b = [bdiag(part(b_t, c)) for c in chains]
    bd_k = [bdiag(part(k_t, c)) for c in chains]
    bd_v = [bdiag(v4[i]) for i in n]
    m_b = [_mm_nt(ar[i], bd_b[i]) for i in n]
    m_k = [_mm_nt(ar[i], bd_k[i]) for i in n]
    a_ab = [jnp.where(strict, m_b[i][0:C], 0.0) for i in n]
    a_rb = [jnp.where(incl, m_b[i][C:2 * C], 0.0) for i in n]
    a_ak = [jnp.where(strict, m_k[i][0:C], 0.0) for i in n]
    a_rk = [jnp.where(incl, m_k[i][C:2 * C], 0.0) for i in n]
    akv = [_mm(a_ak[i], bd_v[i]) for i in n]
    o_kv = [_mm(a_rk[i], bd_v[i]) for i in n]
    p = [_mm(a_ab[i], bdiag(a_ab[i])) for i in n]
    q = [eye + a_ab[i] for i in n]
    for _ in range(4):
        qp = [_mm(jnp.concatenate([q[i], p[i]], axis=0), bdiag(p[i])) for i in n]
        q = [q[i] + qp[i][0:C] for i in n]
        p = [qp[i][C:2 * C] for i in n]
    t_inv = [q[i] + _mm(q[i], bdiag(p[i])) for i in n]
    u0 = [_mm(t_inv[i], bdiag(akv[i])) for i in n]
    a_hat = [_mm(t_inv[i], bdiag(a4[i])) for i in n]
    seqs = [(b, g) for b in range(nb) for g in range(ng)]
    s4 = [state_ref[b, g] for b, g in seqs]
    o_g = {}
    for ch in range(nch):
        idx = [chains.index((b, g, ch)) for b, g in seqs]
        bd_s = [bdiag(s) for s in s4]
        o_s = [_mm_nt(r4[i], bd_s[j]) for j, i in enumerate(idx)]
        u = [_mm_nt(a_hat[i], bd_s[j]) + u0[i] for j, i in enumerate(idx)]
        for j, i in enumerate(idx):
            o_g[chains[i]] = o_s[j] + _mm(a_rb[i], bdiag(u[j])) + o_kv[i]
        nxt = []
        for j, i in enumerate(idx):
            c = chains[i]
            uv = jnp.concatenate([u[j], v4[i]], axis=0)
            bk = jnp.concatenate([part(b_h, c), part(k_h, c)], axis=0)
            nxt.append(s4[j] * part(w_c, c)[0:D] + diag_blocks(_mm_tn(uv, bk)))
        s4 = nxt
    for j, (b, g) in enumerate(seqs):
        state_ref[b, g] = s4[j]

    o = jnp.concatenate(
        [jnp.concatenate([o_g[(b, g, ch)] for g in range(ng)], axis=1) for b in range(nb) for ch in range(nch)],
        axis=0)
    inv_d = 1.0 / D
    mean = seg_sum(o) * inv_d
    dlt = o - mean
    var = seg_sum(dlt * dlt) * inv_d
    o = dlt * lax.rsqrt(var + RW_GN_EPS) * gnw_ref[...] + gnb_ref[...]
    bonus = seg_sum(r * k2 * rk_ref[...]) * v
    y_ref[...] = ((o + bonus) * gate).astype(y_ref.dtype).reshape(y_ref.shape)


def _rwkv(rw_proj, mu, w0, w2, a0, a2, g2, kkw, ka, rk, gnw, gnb):
    b, s, cols = rw_proj.shape
    W = RW_WIDTH
    bd = jnp.kron(jnp.eye(RW_GROUP, dtype=F32), jnp.ones((RW_HEAD_DIM, RW_HEAD_DIM), F32)).astype(BF16)
    row = lambda a: a.reshape(1, -1)
    full = lambda a: pl.BlockSpec(a.shape, lambda j: (0,) * a.ndim)
    args = [row(mu), row(w0), w2.astype(BF16), row(a0), a2.astype(BF16), g2.astype(BF16),
            row(kkw), row(ka), row(rk), row(gnw), row(gnb), bd]
    ct = CHUNK * RW_STEP_CHUNKS
    return pl.pallas_call(
        _rwkv_kernel,
        grid=(s // ct,),
        in_specs=[pl.BlockSpec((b, ct, cols), lambda j: (0, j, 0))] + [full(a) for a in args],
        out_specs=pl.BlockSpec((b, ct, W), lambda j: (0, j, 0)),
        out_shape=jax.ShapeDtypeStruct((b, s, W), BF16),
        scratch_shapes=[
            pltpu.VMEM((b, W // RW_GW, RW_HEAD_DIM, RW_GW), F32),
            pltpu.VMEM((b, cols), F32),
        ],
        compiler_params=pltpu.CompilerParams(
            dimension_semantics=("arbitrary",), vmem_limit_bytes=VMEM_LIMIT),
        name="rwkv7",
    )(rw_proj, *args)


GLA_STEP_CHUNKS = 2


def _group_mask(rows, row_group, cols, col_group):
    r = lax.shift_right_logical(lax.broadcasted_iota(jnp.int32, (rows, cols), 0), _log2(row_group))
    c = lax.shift_right_logical(lax.broadcasted_iota(jnp.int32, (rows, cols), 1), _log2(col_group))
    return jnp.where(r == c, 1.0, 0.0).astype(BF16)


def _gla_kernel(x_ref, gk2_ref, gkb_ref, ng_ref, y_ref, state_ref):
    C, H, DK, DV, SB = CHUNK, GLA_HEADS, GLA_KEY_DIM, GLA_VAL_DIM, GLA_SUB
    QW, VW = GLA_QK_WIDTH, GLA_V_WIDTH
    nb, ct = x_ref.shape[0], x_ref.shape[1]
    nch = ct // C
    R = nb * ct

    @pl.when(pl.program_id(0) == 0)
    def _():
        state_ref[...] = jnp.zeros_like(state_ref)

    x = x_ref[...].reshape(R, x_ref.shape[2])
    q = x[:, 0:QW] * (DK ** -0.5)
    k = x[:, QW:2 * QW]
    v = x[:, 2 * QW:2 * QW + VW]
    g = x[:, 2 * QW + VW:2 * QW + 2 * VW]
    dgk = x[:, 2 * QW + 2 * VW:]

    la = -_softplus(-(_mm(dgk, gk2_ref[...]) + gkb_ref[...])) * (1.0 / GLA_GATE_TEMP)
    rr = lax.broadcasted_iota(jnp.int32, (R, R), 0)
    cc = lax.broadcasted_iota(jnp.int32, (R, R), 1)
    tri_seq = jnp.where((rr >= cc) & (rr // C == cc // C), 1.0, 0.0).astype(BF16)
    b = _mm_rhs_split(tri_seq, la, 3)
    b_last = jnp.concatenate(
        [jnp.broadcast_to(b[(j + 1) * C - 1:(j + 1) * C, :], (C, QW)) for j in range(R // C)], axis=0)
    q_e = q * jnp.exp(b)
    k_e = k * jnp.exp(b_last - b)
    w_c = jnp.exp(b_last)

    units = [(bi, ch) for ch in range(nch) for bi in range(nb)]
    row0 = {u: u[0] * ct + u[1] * C for u in units}

    o_off = {u: None for u in units}
    for s in (C // 2, C // 4, C // 8):
        bd_keys = _block_diag_fn(s, H, DK)
        bd_vals = _block_diag_fn(s, H, DV)
        jobs = [(u, row0[u] + m * 2 * s) for u in units for m in range(C // (2 * s))]
        att = []
        for u, c0 in jobs:
            ref = b[c0 + s - 1:c0 + s]
            q_s = q[c0 + s:c0 + 2 * s] * jnp.exp(b[c0 + s:c0 + 2 * s] - ref)
            k_s = k[c0:c0 + s] * jnp.exp(ref - b[c0:c0 + s])
            att.append(_mm_nt(q_s, bd_keys(k_s)))
        outs = [_mm(att[j], bd_vals(v[c0:c0 + s])) for j, (u, c0) in enumerate(jobs)]
        zero = jnp.zeros((s, VW), F32)
        for u in units:
            pieces = []
            for j, (uj, c0) in enumerate(jobs):
                if uj == u:
                    pieces += [zero, outs[j]]
            level = jnp.concatenate(pieces, axis=0)
            o_off[u] = level if o_off[u] is None else o_off[u] + level

    nblk = C // SB
    ii = lax.broadcasted_iota(jnp.int32, (nblk, SB, SB, 2 * DK), 1)
    jj = lax.broadcasted_iota(jnp.int32, (nblk, SB, SB, 2 * DK), 2)
    causal4 = jj <= ii
    sel = _group_mask(C, 1, C * SB, SB)
    pair_ones = _group_mask(2 * DK, DK, 2 * DV, DV)
    att2 = {}
    for u in units:
        r0 = row0[u]
        for pr in range(H // 2):
            sl2 = slice(2 * pr * DK, 2 * (pr + 1) * DK)
            q2, k2, b2 = q[r0:r0 + C, sl2], k[r0:r0 + C, sl2], b[r0:r0 + C, sl2]
            dec = jnp.exp(jnp.where(
                causal4, b2.reshape(nblk, SB, 1, 2 * DK) - b2.reshape(nblk, 1, SB, 2 * DK), NEG_BIG))
            pw = q2.reshape(nblk, SB, 1, 2 * DK) * k2.reshape(nblk, 1, SB, 2 * DK) * dec
            att2[(u, pr)] = _mm(pw.reshape(C * SB, 2 * DK), pair_ones)
    o_diag = {}
    for u in units:
        r0 = row0[u]
        heads = []
        for hd in range(H):
            v_h = v[r0:r0 + C, hd * DV:(hd + 1) * DV]
            v_rep = jnp.broadcast_to(v_h.reshape(nblk, 1, SB, DV), (nblk, SB, SB, DV)).reshape(C * SB, DV)
            heads.append(_mm(sel, att2[(u, hd // 2)][:, (hd % 2) * DV:(hd % 2 + 1) * DV] * v_rep))
        o_diag[u] = jnp.concatenate(heads, axis=1)

    bd_state = _block_diag_fn(DV, H, DK)
    diag_state = _diag_blocks_fn(DV, H, DK)
    s4 = [state_ref[bi] for bi in range(nb)]
    o_int = {}
    for ch in range(nch):
        for bi in range(nb):
            r0 = row0[(bi, ch)]
            o_int[(bi, ch)] = _mm_nt(q_e[r0:r0 + C], bd_state(s4[bi]))
        nxt = []
        for bi in range(nb):
            r0 = row0[(bi, ch)]
            nxt.append(s4[bi] * w_c[r0:r0 + 1] + diag_state(_mm_tn(v[r0:r0 + C], k_e[r0:r0 + C])))
        s4 = nxt
    for bi in range(nb):
        state_ref[bi] = s4[bi]

    rows = []
    for bi in range(nb):
        for ch in range(nch):
            u = (bi, ch)
            o = o_int[u] + o_diag[u] + o_off[u]
            heads = []
            for hd in range(H):
                oh = o[:, hd * DV:(hd + 1) * DV]
                heads.append(oh * lax.rsqrt(jnp.mean(oh * oh, axis=-1, keepdims=True) + NORM_EPS))
            rows.append(jnp.concatenate(heads, axis=1))
    o = jnp.concatenate(rows, axis=0)
    y = o * ng_ref[...] * (g * _sigmoid(g))
    y_ref[...] = y.astype(y_ref.dtype).reshape(y_ref.shape)


def _gla(gla_proj, gk2, gkb, ng):
    b, s, cols = gla_proj.shape
    gk2p = jnp.zeros((LANES, GLA_QK_WIDTH), F32).at[:GLA_GATE_RANK].set(gk2).astype(BF16)
    args = [gk2p, gkb.reshape(1, -1), ng.reshape(1, -1)]
    full = lambda a: pl.BlockSpec(a.shape, lambda j: (0,) * a.ndim)
    ct = CHUNK * GLA_STEP_CHUNKS
    return pl.pallas_call(
        _gla_kernel,
        grid=(s // ct,),
        in_specs=[pl.BlockSpec((b, ct, cols), lambda j: (0, j, 0))] + [full(a) for a in args],
        out_specs=pl.BlockSpec((b, ct, GLA_V_WIDTH), lambda j: (0, j, 0)),
        out_shape=jax.ShapeDtypeStruct((b, s, GLA_V_WIDTH), BF16),
        scratch_shapes=[pltpu.VMEM((b, GLA_VAL_DIM, GLA_QK_WIDTH), F32)],
        compiler_params=pltpu.CompilerParams(
            dimension_semantics=("arbitrary",), vmem_limit_bytes=VMEM_LIMIT),
        name="gla",
    )(gla_proj, *args)


def _outproj_kernel(yr_ref, yg_ref, x_ref, wor_ref, wog_ref, nf_ref, wrh_ref, wrl_ref, br_ref,
                    h_ref, u3_ref, route_ref, cnt_ref, carry_ref):
    tm, d = x_ref.shape

    @pl.when(pl.program_id(0) == 0)
    def _():
        carry_ref[...] = jnp.zeros_like(carry_ref)

    h = (x_ref[...] + jnp.dot(yr_ref[...], wor_ref[...], preferred_element_type=F32)
         + jnp.dot(yg_ref[...], wog_ref[...], preferred_element_type=F32))
    h_ref[...] = h
    u = _rmsnorm(h, nf_ref[...])
    bits = pltpu.bitcast(u.astype(BF16).astype(F32), jnp.uint32)
    planes = d // LANES // 2
    for s in range(planes):
        lo = lax.shift_right_logical(bits[:, s * LANES:(s + 1) * LANES], jnp.uint32(16))
        hi = jnp.bitwise_and(bits[:, (planes + s) * LANES:(planes + s + 1) * LANES], jnp.uint32(0xFFFF0000))
        u3_ref[s] = jnp.bitwise_or(lo, hi)
    u_hi, u_lo = _split(u, 2)
    logits = (jnp.dot(u_hi, wrh_ref[...], preferred_element_type=F32)
              + jnp.dot(u_hi, wrl_ref[...], preferred_element_type=F32)
              + jnp.dot(u_lo, wrh_ref[...], preferred_element_type=F32)) + br_ref[...]
    lane = lax.broadcasted_iota(jnp.int32, logits.shape, 1)
    rest = logits
    picks, idxs, vals = [], [], []
    for r in range(TOP_K):
        m = jnp.max(rest, axis=-1, keepdims=True)
        idx = jnp.min(jnp.where(rest == m, lane, LANES), axis=-1, keepdims=True)
        pick = lane == idx
        picks.append(pick)
        idxs.append(idx)
        vals.append(m)
        rest = jnp.where(pick, -jnp.inf, rest)
    denom = jnp.ones_like(vals[0])
    for r in range(1, TOP_K):
        denom = denom + jnp.exp(vals[r] - vals[0])
    sel = jnp.zeros(logits.shape, F32)
    for pick in picks:
        sel = sel + jnp.where(pick, 1.0, 0.0)
    prefix = _mm(jnp.where(_tri(tm, True), 1.0, 0.0), sel) + carry_ref[...]
    carry_ref[...] = carry_ref[...] + jnp.sum(sel, axis=0, keepdims=True)
    cnt_ref[...] = carry_ref[...]
    route = jnp.zeros(logits.shape, F32)
    for r in range(TOP_K):
        gate = jnp.exp(vals[r] - vals[0]) / denom
        rank = jnp.sum(jnp.where(picks[r], prefix, 0.0), axis=-1, keepdims=True)
        route = jnp.where(lane == r, gate, route)
        route = jnp.where(lane == TOP_K + r, idxs[r].astype(F32), route)
        route = jnp.where(lane == 2 * TOP_K + r, rank, route)
    route_ref[...] = route


def _out_proj(y_rw, y_gla, x3, wo_r, wo_g, nf, wr_hi, wr_lo, br, tm):
    b, s, d = x3.shape
    assert s % tm == 0
    n, tps = b * s, s // tm
    full = lambda a: pl.BlockSpec(a.shape, lambda i: (0,) * a.ndim)
    tile = lambda w: pl.BlockSpec((tm, w), lambda i: (i, 0))
    return pl.pallas_call(
        _outproj_kernel,
        grid=(n // tm,),
        in_specs=[_seq_tile(tm, y_rw.shape[2], tps), _seq_tile(tm, y_gla.shape[2], tps), _seq_tile(tm, d, tps),
                  full(wo_r), full(wo_g), full(nf), full(wr_hi), full(wr_lo), full(br)],
        out_specs=[tile(d), pl.BlockSpec((d // LANES // 2, tm, LANES), lambda i: (0, i, 0)), tile(LANES),
                   pl.BlockSpec((1, LANES), lambda i: (0, 0))],
        out_shape=[jax.ShapeDtypeStruct((n, d), F32), jax.ShapeDtypeStruct((d // LANES // 2, n, LANES), jnp.uint32),
                   jax.ShapeDtypeStruct((n, LANES), F32), jax.ShapeDtypeStruct((1, LANES), F32)],
        scratch_shapes=[pltpu.VMEM((1, LANES), F32)],
        compiler_params=pltpu.CompilerParams(
            dimension_semantics=("arbitrary",), vmem_limit_bytes=VMEM_LIMIT),
        name="out_proj",
    )(y_rw, y_gla, x3, wo_r, wo_g, nf, wr_hi, wr_lo, br)


_DISPATCH_BUFS = 3


def _dispatch_kernel(row_ref, pad_ref, u3_hbm, xs_hbm, ubuf, zbuf, in_sems, out_sems, zsem, *, tm):
    i = pl.program_id(0)
    nsteps = pl.num_programs(0)
    nbuf = _DISPATCH_BUFS
    n_blocks = pad_ref.shape[0]
    n_tokens = row_ref.shape[0] // TOP_K

    @pl.when(i == 0)
    def _():
        zbuf[...] = jnp.zeros_like(zbuf)

        def zero_block(blk_i):
            return pltpu.make_async_copy(zbuf, xs_hbm.at[:, pl.ds(blk_i * EXPERT_BLOCK, EXPERT_BLOCK), :], zsem)

        def start(blk_i, c):
            @pl.when(pad_ref[blk_i] != 0)
            def _():
                zero_block(blk_i).start()
            return c
        lax.fori_loop(0, n_blocks, start, 0)

        def finish(blk_i, c):
            @pl.when(pad_ref[blk_i] != 0)
            def _():
                zero_block(blk_i).wait()
            return c
        lax.fori_loop(0, n_blocks, finish, 0)

    def load(step, slot_):
        return pltpu.make_async_copy(u3_hbm.at[:, pl.ds(step * tm, tm), :], ubuf.at[slot_], in_sems.at[slot_])

    def wait_scatter(slot_):
        for _ in range(TOP_K):
            pltpu.make_async_copy(u3_hbm.at[:, pl.ds(0, tm), :], ubuf.at[slot_], out_sems.at[slot_]).wait()

    cur = lax.rem(i, nbuf)
    nxt = lax.rem(i + 1, nbuf)

    @pl.when(i == 0)
    def _():
        load(0, 0).start()

    @pl.when(i >= nbuf - 1)
    def _():
        wait_scatter(nxt)

    @pl.when(i + 1 < nsteps)
    def _():
        load(i + 1, nxt).start()

    load(i, cur).wait()

    def issue(r, c):
        src = ubuf.at[cur, :, pl.ds(r, 1), :]
        for kk in range(TOP_K):
            slot = row_ref[kk * n_tokens + i * tm + r]
            dst = xs_hbm.at[:, pl.ds(slot, 1), :]
            pltpu.make_async_copy(src, dst, out_sems.at[cur]).start(priority=kk % 2)
        return c
    lax.fori_loop(0, tm, issue, 0, unroll=4)

    @pl.when(i == nsteps - 1)
    def _():
        for back in range(nbuf - 1):
            @pl.when(i - back >= 0)
            def _():
                wait_scatter(lax.rem(i - back + nbuf, nbuf))


def _dispatch(slots, pad_blocks, u3, tm):
    ns, n, ln = u3.shape
    blk = EXPERT_BLOCK
    grid_spec = pltpu.PrefetchScalarGridSpec(
        num_scalar_prefetch=2,
        grid=(n // tm,),
        in_specs=[pl.BlockSpec(memory_space=pl.ANY)],
        out_specs=pl.BlockSpec(memory_space=pl.ANY),
        scratch_shapes=[pltpu.VMEM((_DISPATCH_BUFS, ns, tm, ln), u3.dtype),
                        pltpu.VMEM((ns, blk, ln), u3.dtype),
                        pltpu.SemaphoreType.DMA((_DISPATCH_BUFS,)),
                        pltpu.SemaphoreType.DMA((_DISPATCH_BUFS,)),
                        pltpu.SemaphoreType.DMA(())],
    )
    return pl.pallas_call(
        functools.partial(_dispatch_kernel, tm=tm),
        grid_spec=grid_spec,
        out_shape=jax.ShapeDtypeStruct((ns, pad_blocks.shape[0] * blk, ln), u3.dtype),
        compiler_params=pltpu.CompilerParams(dimension_semantics=("arbitrary",), has_side_effects=True),
        name="moe_dispatch",
    )(slots, pad_blocks, u3)


def _moe_kernel(be_ref, nxt_ref, nused_ref, xs_ref, w1_hbm, b1_ref, w2_hbm, b2_ref, ys_ref,
                w1f, w2f, w1b, w2b, sems):
    i = pl.program_id(0)
    f = w2b.shape[0]
    ns = ys_ref.shape[0]
    e = be_ref[i]
    e_prev = be_ref[jnp.maximum(i - 1, 0)]

    def fetch(expert):
        return (pltpu.make_async_copy(w1_hbm.at[expert], w1f, sems.at[0]),
                pltpu.make_async_copy(w2_hbm.at[expert], w2f, sems.at[1]))

    @pl.when(i == 0)
    def _():
        for cp in fetch(e):
            cp.start()

    @pl.when(jnp.logical_or(i == 0, e != e_prev))
    def _():
        for cp in fetch(e):
            cp.wait()
        w1b[...] = w1f[...].astype(BF16)
        w2b[...] = w2f[...].astype(BF16)

        @pl.when(nxt_ref[i] >= 0)
        def _():
            for cp in fetch(nxt_ref[i]):
                cp.start()

    @pl.when(i < nused_ref[0])
    def _():
        words = [xs_ref[s] for s in range(xs_ref.shape[0])]
        lo = [pltpu.bitcast(lax.shift_left(w, jnp.uint32(16)), F32) for w in words]
        hi = [pltpu.bitcast(jnp.bitwise_and(w, jnp.uint32(0xFFFF0000)), F32) for w in words]
        xb = jnp.concatenate(lo + hi, axis=1).astype(BF16)
        hgl = jnp.dot(xb, w1b[...], preferred_element_type=F32) + b1_ref[0]
        x_glu = jnp.minimum(hgl[:, :f], SWIGLU_LIMIT)
        x_lin = jnp.clip(hgl[:, f:], -SWIGLU_LIMIT, SWIGLU_LIMIT)
        act = (x_lin + 1.0) * (x_glu * _sigmoid(SWIGLU_ALPHA * x_glu))
        y = jnp.dot(act.astype(BF16), w2b[...], preferred_element_type=F32) + b2_ref[0]
        for s in range(ns):
            ys_ref[s] = y[:, s * LANES:(s + 1) * LANES]

    @pl.when(i >= nused_ref[0])
    def _():
        ys_ref[...] = jnp.zeros_like(ys_ref)


def _moe_experts(block_e, next_e, n_used, xs, w1, b1, w2, b2):
    ns_in, n_rows, ln = xs.shape
    ns = 2 * ns_in
    ne, d, f2 = w1.shape
    f = w2.shape[1]
    nb = block_e.shape[0]
    blk = EXPERT_BLOCK
    grid_spec = pltpu.PrefetchScalarGridSpec(
        num_scalar_prefetch=3,
        grid=(nb,),
        in_specs=[
            pl.BlockSpec((ns_in, blk, ln), lambda i, be, nx, nu: (0, i, 0)),
            pl.BlockSpec(memory_space=pl.ANY),
            pl.BlockSpec((1, 1, f2), lambda i, be, nx, nu: (be[i], 0, 0)),
            pl.BlockSpec(memory_space=pl.ANY),
            pl.BlockSpec((1, 1, d), lambda i, be, nx, nu: (be[i], 0, 0)),
        ],
        out_specs=pl.BlockSpec((ns, blk, ln), lambda i, be, nx, nu: (0, i, 0)),
        scratch_shapes=[pltpu.VMEM((d, f2), w1.dtype), pltpu.VMEM((f, d), w2.dtype),
                        pltpu.VMEM((d, f2), BF16), pltpu.VMEM((f, d), BF16),
                        pltpu.SemaphoreType.DMA((2,))],
    )
    return pl.pallas_call(
        _moe_kernel,
        grid_spec=grid_spec,
        out_shape=jax.ShapeDtypeStruct((ns, n_rows, ln), F32),
        compiler_params=pltpu.CompilerParams(
            dimension_semantics=("arbitrary",), vmem_limit_bytes=VMEM_LIMIT),
        name="moe_experts",
    )(block_e, next_e, n_used, xs, w1, b1.reshape(ne, 1, f2), w2, b2.reshape(ne, 1, d))


def _final_kernel(row_ref, ys_hbm, h_ref, route_ref, p_ref, npl_ref, wg_ref, wp_ref, nfin_ref, out_ref,
                  gbuf, sems, *, last_layer):
    i = pl.program_id(0)
    tm = h_ref.shape[0]
    ns = gbuf.shape[2]
    n_tokens = row_ref.shape[0] // TOP_K
    cur = lax.rem(i, 2)

    last = pl.num_programs(0) - 1

    def issue_row(step, buf, r):
        for kk in range(TOP_K):
            slot = row_ref[kk * n_tokens + step * tm + r]
            src = ys_hbm.at[:, pl.ds(slot, 1), :]
            pltpu.make_async_copy(src, gbuf.at[buf, kk, :, pl.ds(r, 1), :],
                                  sems.at[buf]).start(priority=kk % 2)

    def wait_tile(buf):
        for kk in range(TOP_K):
            pltpu.make_async_copy(ys_hbm.at[:, pl.ds(0, tm), :], gbuf.at[buf, kk], sems.at[buf]).wait()

    def gather(step, buf):
        def issue(r, c):
            issue_row(step, buf, r)
            return c
        lax.fori_loop(0, tm, issue, 0, unroll=4)

    @pl.when(i == 0)
    def _():
        gather(0, 0)

    wait_tile(cur)
    nxt_step = jnp.minimum(i + 1, last)
    for r in range(tm):
        issue_row(nxt_step, 1 - cur, r)

    route = route_ref[...]
    h = h_ref[...]
    for kk in range(TOP_K):
        yk = jnp.concatenate([gbuf[cur, kk, s] for s in range(ns)], axis=1)
        h = h + route[:, kk:kk + 1] * yk
    u = _rmsnorm(h, npl_ref[...])
    gate = _sigmoid(_mm(u, wg_ref[...]))
    h = h + gate * _mm(p_ref[...], wp_ref[...])
    out_ref[...] = _rmsnorm(h, nfin_ref[...]) if last_layer else h

    @pl.when(i == last)
    def _():
        wait_tile(1 - cur)


def _final(slots, ys, h1, route, pf, npl, wg, wp, nfin, tm, last_layer):
    n, d = h1.shape
    pd = pf.shape[1]
    grid_spec = pltpu.PrefetchScalarGridSpec(
        num_scalar_prefetch=1,
        grid=(n // tm,),
        in_specs=[
            pl.BlockSpec(memory_space=pl.ANY),
            pl.BlockSpec((tm, d), lambda i, s: (i, 0)),
            pl.BlockSpec((tm, LANES), lambda i, s: (i, 0)),
            pl.BlockSpec((tm, pd), lambda i, s: (i, 0)),
            pl.BlockSpec((1, d), lambda i, s: (0, 0)),
            pl.BlockSpec((d, d), lambda i, s: (0, 0)),
            pl.BlockSpec((pd, d), lambda i, s: (0, 0)),
            pl.BlockSpec((1, d), lambda i, s: (0, 0)),
        ],
        out_specs=pl.BlockSpec((tm, d), lambda i, s: (i, 0)),
        scratch_shapes=[pltpu.VMEM((2, TOP_K, d // ys.shape[2], tm, ys.shape[2]), F32),
                        pltpu.SemaphoreType.DMA((2,))],
    )
    return pl.pallas_call(
        functools.partial(_final_kernel, last_layer=last_layer),
        grid_spec=grid_spec,
        out_shape=jax.ShapeDtypeStruct((n, d), F32),
        compiler_params=pltpu.CompilerParams(
            dimension_semantics=("arbitrary",), vmem_limit_bytes=VMEM_LIMIT),
        name="final",
    )(slots, ys, h1, route, pf, npl, wg, wp, nfin)


def _routing(route, counts, n_experts):
    n = route.shape[0]
    blk = EXPERT_BLOCK
    nb = (n * TOP_K + n_experts * (blk - 1) + blk - 1) // blk
    counts = counts[0, :n_experts].astype(jnp.int32)
    pcounts = ((counts + blk - 1) // blk) * blk
    pend = jnp.cumsum(pcounts)
    pstart = pend - pcounts
    top_i = route[:, TOP_K:2 * TOP_K].T.astype(jnp.int32)
    rank = route[:, 2 * TOP_K:3 * TOP_K].T.astype(jnp.int32)
    onehot = top_i[None, :, :] == jnp.arange(n_experts, dtype=jnp.int32)[:, None, None]
    slots = jnp.sum(jnp.where(onehot, pstart[:, None, None], 0), axis=0) + rank
    block_rows = jnp.arange(nb, dtype=jnp.int32) * blk
    block_e = jnp.sum(block_rows[:, None] >= pend[None, :], axis=1)
    block_e = jnp.clip(block_e, 0, n_experts - 1).astype(jnp.int32)
    n_used = (pend[-1] // blk).astype(jnp.int32).reshape(1)
    blocks = jnp.arange(nb, dtype=jnp.int32)
    onehot_b = block_e[:, None] == jnp.arange(n_experts, dtype=jnp.int32)[None, :]
    run_end = jnp.sum(jnp.where(onehot_b, pend[None, :], 0), axis=1) // blk
    run_end = jnp.where(blocks >= n_used[0], nb, run_end)
    follow = jnp.sum(jnp.where(blocks[None, :] == run_end[:, None], block_e[None, :], 0), axis=1)
    next_e = jnp.where((run_end < nb) & (follow != block_e), follow, -1).astype(jnp.int32)
    pad_blocks = ((blocks + 1 == run_end) | (blocks >= n_used[0])).astype(jnp.int32)
    return block_e, next_e, n_used, slots.reshape(-1).astype(jnp.int32), pad_blocks


def kernel(x, p, norm_mix, w_in, shift_mu, rw_w0, rw_w2, rw_a0, rw_a2, rw_g2, rw_kk, rw_ka, rw_rk,
           rw_gn_w, rw_gn_b, gla_gk2, gla_gk_b, gla_norm, w_out, norm_ffn, w_router, b_router,
           w1, b1, w2, b2, norm_ple, w_ple_gate, w_ple, norm_final):
    bsz, seq, d = x.shape
    n = bsz * seq
    depth = w_in.shape[0]
    n_experts = w_router.shape[-1]
    tm = 256
    tm_route = min(512, n)
    tm_route = min(tm_route, seq)
    h = x
    for l in range(depth):
        w_rw = w_in[l][:, :RW_COLS].astype(BF16)
        w_gla = jnp.pad(w_in[l][:, RW_COLS:], ((0, 0), (0, LANES - GLA_GATE_RANK))).astype(BF16)
        rw_proj, gla_proj = _in_proj(h, norm_mix[l].reshape(1, d), w_rw, w_gla, tm_route)
        y_rw = _rwkv(rw_proj, shift_mu[l], rw_w0[l], rw_w2[l], rw_a0[l], rw_a2[l],
                     rw_g2[l], rw_kk[l], rw_ka[l], rw_rk[l], rw_gn_w[l], rw_gn_b[l])
        y_gla = _gla(gla_proj, gla_gk2[l], gla_gk_b[l], gla_norm[l])

        wr = jnp.pad(w_router[l], ((0, 0), (0, LANES - n_experts)))
        wr_hi = wr.astype(BF16)
        wr_lo = (wr - wr_hi.astype(F32)).astype(BF16)
        br = jnp.pad(b_router[l], (0, LANES - n_experts), constant_values=NEG_BIG).reshape(1, LANES)
        wo = w_out[l].astype(BF16)
        h1, u3, route, counts = _out_proj(y_rw, y_gla, h, wo[:RW_WIDTH],
                                          wo[RW_WIDTH:], norm_ffn[l].reshape(1, d), wr_hi, wr_lo, br, tm_route)

        block_e, next_e, n_used, slots, pad_blocks = _routing(route, counts, n_experts)
        xs = _dispatch(slots, pad_blocks, u3, tm_route)
        ys = _moe_experts(block_e, next_e, n_used, xs, w1[l], b1[l], w2[l], b2[l])
        h = _final(slots, ys, h1, route, p[l].reshape(n, -1), norm_ple[l].reshape(1, d),
                   w_ple_gate[l].astype(BF16), w_ple[l].astype(BF16), norm_final.reshape(1, d), tm,
                   l == depth - 1).reshape(bsz, seq, d)
    return h
```

```python
import functools
import math

import jax
import jax.numpy as jnp
from jax import lax
from jax.experimental import pallas as pl
from jax.experimental.pallas import tpu as pltpu

F32 = jnp.float32
BF16 = jnp.bfloat16

CHUNK = 64
RW_HEADS = 8
RW_HEAD_DIM = 64
RW_WIDTH = RW_HEADS * RW_HEAD_DIM
RW_DECAY_LORA = 64
RW_ICLR_LORA = 64
RW_GATE_LORA = 128
RW_COLS = 3 * RW_WIDTH + RW_DECAY_LORA + RW_ICLR_LORA + RW_GATE_LORA
RW_GN_EPS = 64e-5
GLA_HEADS = 4
GLA_KEY_DIM = 64
GLA_VAL_DIM = 128
GLA_QK_WIDTH = GLA_HEADS * GLA_KEY_DIM
GLA_V_WIDTH = GLA_HEADS * GLA_VAL_DIM
GLA_GATE_RANK = 16
GLA_GATE_TEMP = 16.0
GLA_SUB = 8
LANES = 128
GLA_COLS_PAD = 2 * GLA_QK_WIDTH + 2 * GLA_V_WIDTH + LANES
TOP_K = 4
EXPERT_BLOCK = 256
SWIGLU_ALPHA = 1.702
SWIGLU_LIMIT = 7.0
NORM_EPS = 1e-6
NEG_BIG = -1e30
VMEM_LIMIT = 56 * 1024 * 1024


def _mm(a, b):
    return jnp.dot(a.astype(BF16), b.astype(BF16), preferred_element_type=F32)


def _mm_nt(a, b):
    return lax.dot_general(a.astype(BF16), b.astype(BF16), (((1,), (1,)), ((), ())),
                           preferred_element_type=F32)


def _mm_tn(a, b):
    return lax.dot_general(a.astype(BF16), b.astype(BF16), (((0,), (0,)), ((), ())),
                           preferred_element_type=F32)


def _split(a, n):
    parts = []
    rem = a
    for _ in range(n):
        p = rem.astype(BF16)
        parts.append(p)
        rem = rem - p.astype(F32)
    return parts


def _mm_lhs_split(a, b_bf16, n):
    out = None
    for p in _split(a, n):
        t = jnp.dot(p, b_bf16, preferred_element_type=F32)
        out = t if out is None else out + t
    return out


def _mm_rhs_split(a_bf16, b, n):
    out = None
    for p in _split(b, n):
        t = jnp.dot(a_bf16, p, preferred_element_type=F32)
        out = t if out is None else out + t
    return out


def _rmsnorm(x, g):
    return x * lax.rsqrt(jnp.mean(x * x, axis=-1, keepdims=True) + NORM_EPS) * g


def _softplus(x):
    return jnp.maximum(x, 0.0) + jnp.log(1.0 + jnp.exp(-jnp.abs(x)))


def _sigmoid(x):
    return 1.0 / (1.0 + jnp.exp(-x))


def _log2(n):
    assert n & (n - 1) == 0
    return n.bit_length() - 1


def _head_keep(rows, head_lanes, dtype):
    lane_head = lax.shift_right_logical(lax.broadcasted_iota(jnp.int32, (rows, LANES), 1), _log2(head_lanes))
    return [jnp.where(lane_head == j, 1.0, 0.0).astype(dtype) for j in range(LANES // head_lanes)]


def _block_diag_fn(rows, n_heads, head_lanes):
    n_tiles = n_heads * head_lanes // LANES
    zero = jnp.zeros((rows, LANES), BF16)
    keep = _head_keep(rows, head_lanes, BF16) if head_lanes < LANES else None

    def f(y):
        yb = y.astype(BF16)
        blocks = []
        for hd in range(n_heads):
            tiles = [zero] * n_tiles
            if keep is not None:
                t = hd * head_lanes // LANES
                tiles[t] = yb[:, t * LANES:(t + 1) * LANES] * keep[hd % len(keep)]
            else:
                for t in range(hd * head_lanes // LANES, (hd + 1) * head_lanes // LANES):
                    tiles[t] = yb[:, t * LANES:(t + 1) * LANES]
            blocks.append(jnp.concatenate(tiles, axis=1))
        return jnp.concatenate(blocks, axis=0)
    return f


def _diag_blocks_fn(rows, n_heads, head_lanes):
    assert head_lanes < LANES
    keep = _head_keep(rows, head_lanes, F32)
    per = len(keep)

    def f(full):
        tiles = []
        for t in range(n_heads * head_lanes // LANES):
            acc = None
            for j in range(per):
                hd = t * per + j
                term = full[hd * rows:(hd + 1) * rows, t * LANES:(t + 1) * LANES] * keep[j]
                acc = term if acc is None else acc + term
            tiles.append(acc)
        return jnp.concatenate(tiles, axis=1)
    return f


def _tri(n, strict):
    r = lax.broadcasted_iota(jnp.int32, (n, n), 0)
    c = lax.broadcasted_iota(jnp.int32, (n, n), 1)
    return (r > c) if strict else (r >= c)


def _inproj_kernel(x_ref, g_ref, wr_ref, wg_ref, rw_ref, gla_ref):
    u = _rmsnorm(x_ref[...], g_ref[...]).astype(BF16)
    rw_ref[...] = jnp.dot(u, wr_ref[...], preferred_element_type=F32)
    gla_ref[...] = jnp.dot(u, wg_ref[...], preferred_element_type=F32)


def _seq_tile(tm, width, tiles_per_seq):
    return pl.BlockSpec((None, tm, width), lambda i: (i // tiles_per_seq, i % tiles_per_seq, 0))


def _in_proj(x3, g, w_rw, w_gla, tm):
    b, s, d = x3.shape
    assert s % tm == 0
    tps = s // tm
    return pl.pallas_call(
        _inproj_kernel,
        grid=(b * tps,),
        in_specs=[
            _seq_tile(tm, d, tps),
            pl.BlockSpec((1, d), lambda i: (0, 0)),
            pl.BlockSpec(w_rw.shape, lambda i: (0, 0)),
            pl.BlockSpec(w_gla.shape, lambda i: (0, 0)),
        ],
        out_specs=[_seq_tile(tm, w_rw.shape[1], tps), _seq_tile(tm, w_gla.shape[1], tps)],
        out_shape=[
            jax.ShapeDtypeStruct((b, s, w_rw.shape[1]), F32),
            jax.ShapeDtypeStruct((b, s, w_gla.shape[1]), F32),
        ],
        compiler_params=pltpu.CompilerParams(
            dimension_semantics=("arbitrary",), vmem_limit_bytes=VMEM_LIMIT,
            allow_input_fusion=[False, False, True, True]),
        name="in_proj",
    )(x3, g, w_rw, w_gla)


RW_GROUP = 4
RW_GW = RW_GROUP * RW_HEAD_DIM
RW_STEP_CHUNKS = 2
RW_DECAY_SCALE = math.exp(-0.5)


def _rwkv_kernel(x_ref, mu_ref, w0_ref, w2_ref, a0_ref, a2_ref, g2_ref, kkw_ref, ka_ref,
                 rk_ref, gnw_ref, gnb_ref, bd_ref, y_ref, state_ref, carry_ref):
    C, D, W, GW = CHUNK, RW_HEAD_DIM, RW_WIDTH, RW_GW
    nb, ct = x_ref.shape[0], x_ref.shape[1]
    nch = ct // C
    R = nb * ct

    @pl.when(pl.program_id(0) == 0)
    def _():
        state_ref[...] = jnp.zeros_like(state_ref)
        carry_ref[...] = jnp.zeros_like(carry_ref)

    x = x_ref[...].reshape(R, x_ref.shape[2])
    row = lax.broadcasted_iota(jnp.int32, x.shape, 0)
    prev = pltpu.roll(x, 1, axis=0)
    for b in range(nb):
        prev = jnp.where(row == b * ct, carry_ref[b:b + 1, :], prev)
        carry_ref[b:b + 1, :] = x[(b + 1) * ct - 1:(b + 1) * ct, :]
    h = x + (prev - x) * mu_ref[...]

    r = h[:, 0:W]
    k = h[:, W:2 * W]
    v = h[:, 2 * W:3 * W]
    o0 = 3 * W
    dw = h[:, o0:o0 + RW_DECAY_LORA]
    da = h[:, o0 + RW_DECAY_LORA:o0 + RW_DECAY_LORA + RW_ICLR_LORA]
    dg = h[:, o0 + RW_DECAY_LORA + RW_ICLR_LORA:]

    bd_g = bd_ref[...]

    def seg_sum(t):
        return jnp.concatenate(
            [_mm_lhs_split(t[:, g * GW:(g + 1) * GW], bd_g, 2) for g in range(W // GW)], axis=1)

    lw = -RW_DECAY_SCALE * _sigmoid(w0_ref[...] + _mm(jnp.tanh(dw), w2_ref[...]))
    iclr = _sigmoid(a0_ref[...] + _mm(da, a2_ref[...]))
    gate = _mm(_sigmoid(dg), g2_ref[...])

    kk = k * kkw_ref[...]
    kk = kk * jnp.minimum(lax.rsqrt(seg_sum(kk * kk)), 1e12)
    k2 = k * (1.0 + (iclr - 1.0) * ka_ref[...])

    rr = lax.broadcasted_iota(jnp.int32, (R, R), 0)
    cc = lax.broadcasted_iota(jnp.int32, (R, R), 1)
    tri_seq = jnp.where((rr >= cc) & (rr // C == cc // C), 1.0, 0.0).astype(BF16)
    cw = _mm_rhs_split(tri_seq, lw, 3)
    cw_last = jnp.concatenate(
        [jnp.broadcast_to(cw[(j + 1) * C - 1:(j + 1) * C, :], (C, W)) for j in range(R // C)], axis=0)
    e_cw = jnp.exp(cw)
    e_ncw = jnp.exp(-cw)
    e_rem = jnp.exp(cw_last - cw)
    kka = kk * iclr
    a_t = -kk * jnp.exp(cw - lw)
    r_t = r * e_cw
    b_t = kka * e_ncw
    k_t = k2 * e_ncw
    b_h = kka * e_rem
    k_h = k2 * e_rem
    w_c = jnp.exp(cw_last)

    ti = lax.broadcasted_iota(jnp.int32, (C, GW), 0)
    si = jnp.bitwise_and(lax.broadcasted_iota(jnp.int32, (C, GW), 1), D - 1)
    strict = ti > si
    incl = ti >= si
    eye = jnp.where(ti == si, 1.0, 0.0)

    bdiag = _block_diag_fn(C, RW_GROUP, D)
    diag_blocks = _diag_blocks_fn(D, RW_GROUP, D)

    ng = W // GW
    chains = [(b, g, ch) for ch in range(nch) for b in range(nb) for g in range(ng)]

    def part(t, c):
        b, g, ch = c
        r0 = b * ct + ch * C
        return t[r0:r0 + C, g * GW:(g + 1) * GW]

    n = range(len(chains))
    a4 = [part(a_t, c) for c in chains]
    r4 = [part(r_t, c) for c in chains]
    v4 = [part(v, c) for c in chains]
    ar = [jnp.concatenate([a4[i], r4[i]], axis=0) for i in n]
    bd_b = [bdiag(part(b_t, c)) for c in chains]
    bd_k = [bdiag(part(k_t, c)) for c in chains]
    bd_v = [bdiag(v4[i]) for i in n]
    m_b = [_mm_nt(ar[i], bd_b[i]) for i in n]
    m_k = [_mm_nt(ar[i], bd_k[i]) for i in n]
    a_ab = [jnp.where(strict, m_b[i][0:C], 0.0) for i in n]
    a_rb = [jnp.where(incl, m_b[i][C:2 * C], 0.0) for i in n]
    a_ak = [jnp.where(strict, m_k[i][0:C], 0.0) for i in n]
    a_rk = [jnp.where(incl, m_k[i][C:2 * C], 0.0) for i in n]
    akv = [_mm(a_ak[i], bd_v[i]) for i in n]
    o_kv = [_mm(a_rk[i], bd_v[i]) for i in n]
    p = [_mm(a_ab[i], bdiag(a_ab[i])) for i in n]
    q = [eye + a_ab[i] for i in n]
    for _ in range(4):
        qp = [_mm(jnp.concatenate([q[i], p[i]], axis=0), bdiag(p[i])) for i in n]
        q = [q[i] + qp[i][0:C] for i in n]
        p = [qp[i][C:2 * C] for i in n]
    t_inv = [q[i] + _mm(q[i], bdiag(p[i])) for i in n]
    u0 = [_mm(t_inv[i], bdiag(akv[i])) for i in n]
    a_hat = [_mm(t_inv[i], bdiag(a4[i])) for i in n]
    seqs = [(b, g) for b in range(nb) for g in range(ng)]
    s4 = [state_ref[b, g] for b, g in seqs]
    o_g = {}
    for ch in range(nch):
        idx = [chains.index((b, g, ch)) for b, g in seqs]
        bd_s = [bdiag(s) for s in s4]
        o_s = [_mm_nt(r4[i], bd_s[j]) for j, i in enumerate(idx)]
        u = [_mm_nt(a_hat[i], bd_s[j]) + u0[i] for j, i in enumerate(idx)]
        for j, i in enumerate(idx):
            o_g[chains[i]] = o_s[j] + _mm(a_rb[i], bdiag(u[j])) + o_kv[i]
        nxt = []
        for j, i in enumerate(idx):
            c = chains[i]
            uv = jnp.concatenate([u[j], v4[i]], axis=0)
            bk = jnp.concatenate([part(b_h, c), part(k_h, c)], axis=0)
            nxt.append(s4[j] * part(w_c, c)[0:D] + diag_blocks(_mm_tn(uv, bk)))
        s4 = nxt
    for j, (b, g) in enumerate(seqs):
        state_ref[b, g] = s4[j]

    o = jnp.concatenate(
        [jnp.concatenate([o_g[(b, g, ch)] for g in range(ng)], axis=1) for b in range(nb) for ch in range(nch)],
        axis=0)
    inv_d = 1.0 / D
    mean = seg_sum(o) * inv_d
    dlt = o - mean
    var = seg_sum(dlt * dlt) * inv_d
    o = dlt * lax.rsqrt(var + RW_GN_EPS) * gnw_ref[...] + gnb_ref[...]
    bonus = seg_sum(r * k2 * rk_ref[...]) * v
    y_ref[...] = ((o + bonus) * gate).astype(y_ref.dtype).reshape(y_ref.shape)


def _rwkv(rw_proj, mu, w0, w2, a0, a2, g2, kkw, ka, rk, gnw, gnb):
    b, s, cols = rw_proj.shape
    W = RW_WIDTH
    bd = jnp.kron(jnp.eye(RW_GROUP, dtype=F32), jnp.ones((RW_HEAD_DIM, RW_HEAD_DIM), F32)).astype(BF16)
    row = lambda a: a.reshape(1, -1)
    full = lambda a: pl.BlockSpec(a.shape, lambda j: (0,) * a.ndim)
    args = [row(mu), row(w0), w2.astype(BF16), row(a0), a2.astype(BF16), g2.astype(BF16),
            row(kkw), row(ka), row(rk), row(gnw), row(gnb), bd]
    ct = CHUNK * RW_STEP_CHUNKS
    return pl.pallas_call(
        _rwkv_kernel,
        grid=(s // ct,),
        in_specs=[pl.BlockSpec((b, ct, cols), lambda j: (0, j, 0))] + [full(a) for a in args],
        out_specs=pl.BlockSpec((b, ct, W), lambda j: (0, j, 0)),
        out_shape=jax.ShapeDtypeStruct((b, s, W), BF16),
        scratch_shapes=[
            pltpu.VMEM((b, W // RW_GW, RW_HEAD_DIM, RW_GW), F32),
            pltpu.VMEM((b, cols), F32),
        ],
        compiler_params=pltpu.CompilerParams(
            dimension_semantics=("arbitrary",), vmem_limit_bytes=VMEM_LIMIT),
        name="rwkv7",
    )(rw_proj, *args)


GLA_STEP_CHUNKS = 2


def _group_mask(rows, row_group, cols, col_group):
    r = lax.shift_right_logical(lax.broadcasted_iota(jnp.int32, (rows, cols), 0), _log2(row_group))
    c = lax.shift_right_logical(lax.broadcasted_iota(jnp.int32, (rows, cols), 1), _log2(col_group))
    return jnp.where(r == c, 1.0, 0.0).astype(BF16)


def _gla_kernel(x_ref, gk2_ref, gkb_ref, ng_ref, y_ref, state_ref):
    C, H, DK, DV, SB = CHUNK, GLA_HEADS, GLA_KEY_DIM, GLA_VAL_DIM, GLA_SUB
    QW, VW = GLA_QK_WIDTH, GLA_V_WIDTH
    nb, ct = x_ref.shape[0], x_ref.shape[1]
    nch = ct // C
    R = nb * ct

    @pl.when(pl.program_id(0) == 0)
    def _():
        state_ref[...] = jnp.zeros_like(state_ref)

    x = x_ref[...].reshape(R, x_ref.shape[2])
    q = x[:, 0:QW] * (DK ** -0.5)
    k = x[:, QW:2 * QW]
    v = x[:, 2 * QW:2 * QW + VW]
    g = x[:, 2 * QW + VW:2 * QW + 2 * VW]
    dgk = x[:, 2 * QW + 2 * VW:]

    la = -_softplus(-(_mm(dgk, gk2_ref[...]) + gkb_ref[...])) * (1.0 / GLA_GATE_TEMP)
    rr = lax.broadcasted_iota(jnp.int32, (R, R), 0)
    cc = lax.broadcasted_iota(jnp.int32, (R, R), 1)
    tri_seq = jnp.where((rr >= cc) & (rr // C == cc // C), 1.0, 0.0).astype(BF16)
    b = _mm_rhs_split(tri_seq, la, 3)
    b_last = jnp.concatenate(
        [jnp.broadcast_to(b[(j + 1) * C - 1:(j + 1) * C, :], (C, QW)) for j in range(R // C)], axis=0)
    q_e = q * jnp.exp(b)
    k_e = k * jnp.exp(b_last - b)
    w_c = jnp.exp(b_last)

    units = [(bi, ch) for ch in range(nch) for bi in range(nb)]
    row0 = {u: u[0] * ct + u[1] * C for u in units}

    o_off = {u: None for u in units}
    for s in (C // 2, C // 4, C // 8):
        bd_keys = _block_diag_fn(s, H, DK)
        bd_vals = _block_diag_fn(s, H, DV)
        jobs = [(u, row0[u] + m * 2 * s) for u in units for m in range(C // (2 * s))]
        att = []
        for u, c0 in jobs:
            ref = b[c0 + s - 1:c0 + s]
            q_s = q[c0 + s:c0 + 2 * s] * jnp.exp(b[c0 + s:c0 + 2 * s] - ref)
            k_s = k[c0:c0 + s] * jnp.exp(ref - b[c0:c0 + s])
            att.append(_mm_nt(q_s, bd_keys(k_s)))
        outs = [_mm(att[j], bd_vals(v[c0:c0 + s])) for j, (u, c0) in enumerate(jobs)]
        zero = jnp.zeros((s, VW), F32)
        for u in units:
            pieces = []
            for j, (uj, c0) in enumerate(jobs):
                if uj == u:
                    pieces += [zero, outs[j]]
            level = jnp.concatenate(pieces, axis=0)
            o_off[u] = level if o_off[u] is None else o_off[u] + level

    nblk = C // SB
    ii = lax.broadcasted_iota(jnp.int32, (nblk, SB, SB, 2 * DK), 1)
    jj = lax.broadcasted_iota(jnp.int32, (nblk, SB, SB, 2 * DK), 2)
    causal4 = jj <= ii
    sel = _group_mask(C, 1, C * SB, SB)
    pair_ones = _group_mask(2 * DK, DK, 2 * DV, DV)
    att2 = {}
    for u in units:
        r0 = row0[u]
        for pr in range(H // 2):
            sl2 = slice(2 * pr * DK, 2 * (pr + 1) * DK)
            q2, k2, b2 = q[r0:r0 + C, sl2], k[r0:r0 + C, sl2], b[r0:r0 + C, sl2]
            dec = jnp.exp(jnp.where(
                causal4, b2.reshape(nblk, SB, 1, 2 * DK) - b2.reshape(nblk, 1, SB, 2 * DK), NEG_BIG))
            pw = q2.reshape(nblk, SB, 1, 2 * DK) * k2.reshape(nblk, 1, SB, 2 * DK) * dec
            att2[(u, pr)] = _mm(pw.reshape(C * SB, 2 * DK), pair_ones)
    o_diag = {}
    for u in units:
        r0 = row0[u]
        heads = []
        for hd in range(H):
            v_h = v[r0:r0 + C, hd * DV:(hd + 1) * DV]
            v_rep = jnp.broadcast_to(v_h.reshape(nblk, 1, SB, DV), (nblk, SB, SB, DV)).reshape(C * SB, DV)
            heads.append(_mm(sel, att2[(u, hd // 2)][:, (hd % 2) * DV:(hd % 2 + 1) * DV] * v_rep))
        o_diag[u] = jnp.concatenate(heads, axis=1)

    bd_state = _block_diag_fn(DV, H, DK)
    diag_state = _diag_blocks_fn(DV, H, DK)
    s4 = [state_ref[bi] for bi in range(nb)]
    o_int = {}
    for ch in range(nch):
        for bi in range(nb):
            r0 = row0[(bi, ch)]
            o_int[(bi, ch)] = _mm_nt(q_e[r0:r0 + C], bd_state(s4[bi]))
        nxt = []
        for bi in range(nb):
            r0 = row0[(bi, ch)]
            nxt.append(s4[bi] * w_c[r0:r0 + 1] + diag_state(_mm_tn(v[r0:r0 + C], k_e[r0:r0 + C])))
        s4 = nxt
    for bi in range(nb):
        state_ref[bi] = s4[bi]

    rows = []
    for bi in range(nb):
        for ch in range(nch):
            u = (bi, ch)
            o = o_int[u] + o_diag[u] + o_off[u]
            heads = []
            for hd in range(H):
                oh = o[:, hd * DV:(hd + 1) * DV]
                heads.append(oh * lax.rsqrt(jnp.mean(oh * oh, axis=-1, keepdims=True) + NORM_EPS))
            rows.append(jnp.concatenate(heads, axis=1))
    o = jnp.concatenate(rows, axis=0)
    y = o * ng_ref[...] * (g * _sigmoid(g))
    y_ref[...] = y.astype(y_ref.dtype).reshape(y_ref.shape)


def _gla(gla_proj, gk2, gkb, ng):
    b, s, cols = gla_proj.shape
    gk2p = jnp.zeros((LANES, GLA_QK_WIDTH), F32).at[:GLA_GATE_RANK].set(gk2).astype(BF16)
    args = [gk2p, gkb.reshape(1, -1), ng.reshape(1, -1)]
    full = lambda a: pl.BlockSpec(a.shape, lambda j: (0,) * a.ndim)
    ct = CHUNK * GLA_STEP_CHUNKS
    return pl.pallas_call(
        _gla_kernel,
        grid=(s // ct,),
        in_specs=[pl.BlockSpec((b, ct, cols), lambda j: (0, j, 0))] + [full(a) for a in args],
        out_specs=pl.BlockSpec((b, ct, GLA_V_WIDTH), lambda j: (0, j, 0)),
        out_shape=jax.ShapeDtypeStruct((b, s, GLA_V_WIDTH), BF16),
        scratch_shapes=[pltpu.VMEM((b, GLA_VAL_DIM, GLA_QK_WIDTH), F32)],
        compiler_params=pltpu.CompilerParams(
            dimension_semantics=("arbitrary",), vmem_limit_bytes=VMEM_LIMIT),
        name="gla",
    )(gla_proj, *args)


def _outproj_kernel(yr_ref, yg_ref, x_ref, wor_ref, wog_ref, nf_ref, wrh_ref, wrl_ref, br_ref,
                    h_ref, u3_ref, route_ref, cnt_ref, carry_ref):
    tm, d = x_ref.shape

    @pl.when(pl.program_id(0) == 0)
    def _():
        carry_ref[...] = jnp.zeros_like(carry_ref)

    h = (x_ref[...] + jnp.dot(yr_ref[...], wor_ref[...], preferred_element_type=F32)
         + jnp.dot(yg_ref[...], wog_ref[...], preferred_element_type=F32))
    h_ref[...] = h
    u = _rmsnorm(h, nf_ref[...])
    bits = pltpu.bitcast(u.astype(BF16).astype(F32), jnp.uint32)
    planes = d // LANES // 2
    for s in range(planes):
        lo = lax.shift_right_logical(bits[:, s * LANES:(s + 1) * LANES], jnp.uint32(16))
        hi = jnp.bitwise_and(bits[:, (planes + s) * LANES:(planes + s + 1) * LANES], jnp.uint32(0xFFFF0000))
        u3_ref[s] = jnp.bitwise_or(lo, hi)
    u_hi, u_lo = _split(u, 2)
    logits = (jnp.dot(u_hi, wrh_ref[...], preferred_element_type=F32)
              + jnp.dot(u_hi, wrl_ref[...], preferred_element_type=F32)
              + jnp.dot(u_lo, wrh_ref[...], preferred_element_type=F32)) + br_ref[...]
    lane = lax.broadcasted_iota(jnp.int32, logits.shape, 1)
    rest = logits
    picks, idxs, vals = [], [], []
    for r in range(TOP_K):
        m = jnp.max(rest, axis=-1, keepdims=True)
        idx = jnp.min(jnp.where(rest == m, lane, LANES), axis=-1, keepdims=True)
        pick = lane == idx
        picks.append(pick)
        idxs.append(idx)
        vals.append(m)
        rest = jnp.where(pick, -jnp.inf, rest)
    denom = jnp.ones_like(vals[0])
    for r in range(1, TOP_K):
        denom = denom + jnp.exp(vals[r] - vals[0])
    sel = jnp.zeros(logits.shape, F32)
    for pick in picks:
        sel = sel + jnp.where(pick, 1.0, 0.0)
    prefix = _mm(jnp.where(_tri(tm, True), 1.0, 0.0), sel) + carry_ref[...]
    carry_ref[...] = carry_ref[...] + jnp.sum(sel, axis=0, keepdims=True)
    cnt_ref[...] = carry_ref[...]
    route = jnp.zeros(logits.shape, F32)
    for r in range(TOP_K):
        gate = jnp.exp(vals[r] - vals[0]) / denom
        rank = jnp.sum(jnp.where(picks[r], prefix, 0.0), axis=-1, keepdims=True)
        route = jnp.where(lane == r, gate, route)
        route = jnp.where(lane == TOP_K + r, idxs[r].astype(F32), route)
        route = jnp.where(lane == 2 * TOP_K + r, rank, route)
    route_ref[...] = route


def _out_proj(y_rw, y_gla, x3, wo_r, wo_g, nf, wr_hi, wr_lo, br, tm):
    b, s, d = x3.shape
    assert s % tm == 0
    n, tps = b * s, s // tm
    full = lambda a: pl.BlockSpec(a.shape, lambda i: (0,) * a.ndim)
    tile = lambda w: pl.BlockSpec((tm, w), lambda i: (i, 0))
    return pl.pallas_call(
        _outproj_kernel,
        grid=(n // tm,),
        in_specs=[_seq_tile(tm, y_rw.shape[2], tps), _seq_tile(tm, y_gla.shape[2], tps), _seq_tile(tm, d, tps),
                  full(wo_r), full(wo_g), full(nf), full(wr_hi), full(wr_lo), full(br)],
        out_specs=[tile(d), pl.BlockSpec((d // LANES // 2, tm, LANES), lambda i: (0, i, 0)), tile(LANES),
                   pl.BlockSpec((1, LANES), lambda i: (0, 0))],
        out_shape=[jax.ShapeDtypeStruct((n, d), F32), jax.ShapeDtypeStruct((d // LANES // 2, n, LANES), jnp.uint32),
                   jax.ShapeDtypeStruct((n, LANES), F32), jax.ShapeDtypeStruct((1, LANES), F32)],
        scratch_shapes=[pltpu.VMEM((1, LANES), F32)],
        compiler_params=pltpu.CompilerParams(
            dimension_semantics=("arbitrary",), vmem_limit_bytes=VMEM_LIMIT),
        name="out_proj",
    )(y_rw, y_gla, x3, wo_r, wo_g, nf, wr_hi, wr_lo, br)


_DISPATCH_BUFS = 3


def _dispatch_kernel(row_ref, pad_ref, u3_hbm, xs_hbm, ubuf, zbuf, in_sems, out_sems, zsem, *, tm):
    i = pl.program_id(0)
    nsteps = pl.num_programs(0)
    nbuf = _DISPATCH_BUFS
    n_blocks = pad_ref.shape[0]
    n_tokens = row_ref.shape[0] // TOP_K

    @pl.when(i == 0)
    def _():
        zbuf[...] = jnp.zeros_like(zbuf)

        def zero_block(blk_i):
            return pltpu.make_async_copy(zbuf, xs_hbm.at[:, pl.ds(blk_i * EXPERT_BLOCK, EXPERT_BLOCK), :], zsem)

        def start(blk_i, c):
            @pl.when(pad_ref[blk_i] != 0)
            def _():
                zero_block(blk_i).start()
            return c
        lax.fori_loop(0, n_blocks, start, 0)

        def finish(blk_i, c):
            @pl.when(pad_ref[blk_i] != 0)
            def _():
                zero_block(blk_i).wait()
            return c
        lax.fori_loop(0, n_blocks, finish, 0)

    def load(step, slot_):
        return pltpu.make_async_copy(u3_hbm.at[:, pl.ds(step * tm, tm), :], ubuf.at[slot_], in_sems.at[slot_])

    def wait_scatter(slot_):
        for _ in range(TOP_K):
            pltpu.make_async_copy(u3_hbm.at[:, pl.ds(0, tm), :], ubuf.at[slot_], out_sems.at[slot_]).wait()

    cur = lax.rem(i, nbuf)
    nxt = lax.rem(i + 1, nbuf)

    @pl.when(i == 0)
    def _():
        load(0, 0).start()

    @pl.when(i >= nbuf - 1)
    def _():
        wait_scatter(nxt)

    @pl.when(i + 1 < nsteps)
    def _():
        load(i + 1, nxt).start()

    load(i, cur).wait()

    def issue(r, c):
        src = ubuf.at[cur, :, pl.ds(r, 1), :]
        for kk in range(TOP_K):
            slot = row_ref[kk * n_tokens + i * tm + r]
            dst = xs_hbm.at[:, pl.ds(slot, 1), :]
            pltpu.make_async_copy(src, dst, out_sems.at[cur]).start(priority=kk % 2)
        return c
    lax.fori_loop(0, tm, issue, 0, unroll=4)

    @pl.when(i == nsteps - 1)
    def _():
        for back in range(nbuf - 1):
            @pl.when(i - back >= 0)
            def _():
                wait_scatter(lax.rem(i - back + nbuf, nbuf))


def _dispatch(slots, pad_blocks, u3, tm):
    ns, n, ln = u3.shape
    blk = EXPERT_BLOCK
    grid_spec = pltpu.PrefetchScalarGridSpec(
        num_scalar_prefetch=2,
        grid=(n // tm,),
        in_specs=[pl.BlockSpec(memory_space=pl.ANY)],
        out_specs=pl.BlockSpec(memory_space=pl.ANY),
        scratch_shapes=[pltpu.VMEM((_DISPATCH_BUFS, ns, tm, ln), u3.dtype),
                        pltpu.VMEM((ns, blk, ln), u3.dtype),
                        pltpu.SemaphoreType.DMA((_DISPATCH_BUFS,)),
                        pltpu.SemaphoreType.DMA((_DISPATCH_BUFS,)),
                        pltpu.SemaphoreType.DMA(())],
    )
    return pl.pallas_call(
        functools.partial(_dispatch_kernel, tm=tm),
        grid_spec=grid_spec,
        out_shape=jax.ShapeDtypeStruct((ns, pad_blocks.shape[0] * blk, ln), u3.dtype),
        compiler_params=pltpu.CompilerParams(dimension_semantics=("arbitrary",), has_side_effects=True),
        name="moe_dispatch",
    )(slots, pad_blocks, u3)


def _moe_kernel(be_ref, nxt_ref, nused_ref, xs_ref, w1_hbm, b1_ref, w2_hbm, b2_ref, ys_ref,
                w1f, w2f, w1b, w2b, sems):
    i = pl.program_id(0)
    f = w2b.shape[0]
    ns = ys_ref.shape[0]
    e = be_ref[i]
    e_prev = be_ref[jnp.maximum(i - 1, 0)]

    def fetch(expert):
        return (pltpu.make_async_copy(w1_hbm.at[expert], w1f, sems.at[0]),
                pltpu.make_async_copy(w2_hbm.at[expert], w2f, sems.at[1]))

    @pl.when(i == 0)
    def _():
        for cp in fetch(e):
            cp.start()

    @pl.when(jnp.logical_or(i == 0, e != e_prev))
    def _():
        for cp in fetch(e):
            cp.wait()
        w1b[...] = w1f[...].astype(BF16)
        w2b[...] = w2f[...].astype(BF16)

        @pl.when(nxt_ref[i] >= 0)
        def _():
            for cp in fetch(nxt_ref[i]):
                cp.start()

    @pl.when(i < nused_ref[0])
    def _():
        words = [xs_ref[s] for s in range(xs_ref.shape[0])]
        lo = [pltpu.bitcast(lax.shift_left(w, jnp.uint32(16)), F32) for w in words]
        hi = [pltpu.bitcast(jnp.bitwise_and(w, jnp.uint32(0xFFFF0000)), F32) for w in words]
        xb = jnp.concatenate(lo + hi, axis=1).astype(BF16)
        hgl = jnp.dot(xb, w1b[...], preferred_element_type=F32) + b1_ref[0]
        x_glu = jnp.minimum(hgl[:, :f], SWIGLU_LIMIT)
        x_lin = jnp.clip(hgl[:, f:], -SWIGLU_LIMIT, SWIGLU_LIMIT)
        act = (x_lin + 1.0) * (x_glu * _sigmoid(SWIGLU_ALPHA * x_glu))
        y = jnp.dot(act.astype(BF16), w2b[...], preferred_element_type=F32) + b2_ref[0]
        for s in range(ns):
            ys_ref[s] = y[:, s * LANES:(s + 1) * LANES]

    @pl.when(i >= nused_ref[0])
    def _():
        ys_ref[...] = jnp.zeros_like(ys_ref)


def _moe_experts(block_e, next_e, n_used, xs, w1, b1, w2, b2):
    ns_in, n_rows, ln = xs.shape
    ns = 2 * ns_in
    ne, d, f2 = w1.shape
    f = w2.shape[1]
    nb = block_e.shape[0]
    blk = EXPERT_BLOCK
    grid_spec = pltpu.PrefetchScalarGridSpec(
        num_scalar_prefetch=3,
        grid=(nb,),
        in_specs=[
            pl.BlockSpec((ns_in, blk, ln), lambda i, be, nx, nu: (0, i, 0)),
            pl.BlockSpec(memory_space=pl.ANY),
            pl.BlockSpec((1, 1, f2), lambda i, be, nx, nu: (be[i], 0, 0)),
            pl.BlockSpec(memory_space=pl.ANY),
            pl.BlockSpec((1, 1, d), lambda i, be, nx, nu: (be[i], 0, 0)),
        ],
        out_specs=pl.BlockSpec((ns, blk, ln), lambda i, be, nx, nu: (0, i, 0)),
        scratch_shapes=[pltpu.VMEM((d, f2), w1.dtype), pltpu.VMEM((f, d), w2.dtype),
                        pltpu.VMEM((d, f2), BF16), pltpu.VMEM((f, d), BF16),
                        pltpu.SemaphoreType.DMA((2,))],
    )
    return pl.pallas_call(
        _moe_kernel,
        grid_spec=grid_spec,
        out_shape=jax.ShapeDtypeStruct((ns, n_rows, ln), F32),
        compiler_params=pltpu.CompilerParams(
            dimension_semantics=("arbitrary",), vmem_limit_bytes=VMEM_LIMIT),
        name="moe_experts",
    )(block_e, next_e, n_used, xs, w1, b1.reshape(ne, 1, f2), w2, b2.reshape(ne, 1, d))


def _final_kernel(row_ref, ys_hbm, h_ref, route_ref, p_ref, npl_ref, wg_ref, wp_ref, nfin_ref, out_ref,
                  gbuf, sems, *, last_layer):
    i = pl.program_id(0)
    tm = h_ref.shape[0]
    ns = gbuf.shape[2]
    n_tokens = row_ref.shape[0] // TOP_K
    cur = lax.rem(i, 2)

    last = pl.num_programs(0) - 1

    def issue_row(step, buf, r):
        for kk in range(TOP_K):
            slot = row_ref[kk * n_tokens + step * tm + r]
            src = ys_hbm.at[:, pl.ds(slot, 1), :]
            pltpu.make_async_copy(src, gbuf.at[buf, kk, :, pl.ds(r, 1), :],
                                  sems.at[buf]).start(priority=kk % 2)

    def wait_tile(buf):
        for kk in range(TOP_K):
            pltpu.make_async_copy(ys_hbm.at[:, pl.ds(0, tm), :], gbuf.at[buf, kk], sems.at[buf]).wait()

    def gather(step, buf):
        def issue(r, c):
            issue_row(step, buf, r)
            return c
        lax.fori_loop(0, tm, issue, 0, unroll=4)

    @pl.when(i == 0)
    def _():
        gather(0, 0)

    wait_tile(cur)
    nxt_step = jnp.minimum(i + 1, last)
    for r in range(tm):
        issue_row(nxt_step, 1 - cur, r)

    route = route_ref[...]
    h = h_ref[...]
    for kk in range(TOP_K):
        yk = jnp.concatenate([gbuf[cur, kk, s] for s in range(ns)], axis=1)
        h = h + route[:, kk:kk + 1] * yk
    u = _rmsnorm(h, npl_ref[...])
    gate = _sigmoid(_mm(u, wg_ref[...]))
    h = h + gate * _mm(p_ref[...], wp_ref[...])
    out_ref[...] = _rmsnorm(h, nfin_ref[...]) if last_layer else h

    @pl.when(i == last)
    def _():
        wait_tile(1 - cur)


def _final(slots, ys, h1, route, pf, npl, wg, wp, nfin, tm, last_layer):
    n, d = h1.shape
    pd = pf.shape[1]
    grid_spec = pltpu.PrefetchScalarGridSpec(
        num_scalar_prefetch=1,
        grid=(n // tm,),
        in_specs=[
            pl.BlockSpec(memory_space=pl.ANY),
            pl.BlockSpec((tm, d), lambda i, s: (i, 0)),
            pl.BlockSpec((tm, LANES), lambda i, s: (i, 0)),
            pl.BlockSpec((tm, pd), lambda i, s: (i, 0)),
            pl.BlockSpec((1, d), lambda i, s: (0, 0)),
            pl.BlockSpec((d, d), lambda i, s: (0, 0)),
            pl.BlockSpec((pd, d), lambda i, s: (0, 0)),
            pl.BlockSpec((1, d), lambda i, s: (0, 0)),
        ],
        out_specs=pl.BlockSpec((tm, d), lambda i, s: (i, 0)),
        scratch_shapes=[pltpu.VMEM((2, TOP_K, d // ys.shape[2], tm, ys.shape[2]), F32),
                        pltpu.SemaphoreType.DMA((2,))],
    )
    return pl.pallas_call(
        functools.partial(_final_kernel, last_layer=last_layer),
        grid_spec=grid_spec,
        out_shape=jax.ShapeDtypeStruct((n, d), F32),
        compiler_params=pltpu.CompilerParams(
            dimension_semantics=("arbitrary",), vmem_limit_bytes=VMEM_LIMIT),
        name="final",
    )(slots, ys, h1, route, pf, npl, wg, wp, nfin)


def _routing(route, counts, n_experts):
    n = route.shape[0]
    blk = EXPERT_BLOCK
    nb = (n * TOP_K + n_experts * (blk - 1) + blk - 1) // blk
    counts = counts[0, :n_experts].astype(jnp.int32)
    pcounts = ((counts + blk - 1) // blk) * blk
    pend = jnp.cumsum(pcounts)
    pstart = pend - pcounts
    top_i = route[:, TOP_K:2 * TOP_K].T.astype(jnp.int32)
    rank = route[:, 2 * TOP_K:3 * TOP_K].T.astype(jnp.int32)
    onehot = top_i[None, :, :] == jnp.arange(n_experts, dtype=jnp.int32)[:, None, None]
    slots = jnp.sum(jnp.where(onehot, pstart[:, None, None], 0), axis=0) + rank
    block_rows = jnp.arange(nb, dtype=jnp.int32) * blk
    block_e = jnp.sum(block_rows[:, None] >= pend[None, :], axis=1)
    block_e = jnp.clip(block_e, 0, n_experts - 1).astype(jnp.int32)
    n_used = (pend[-1] // blk).astype(jnp.int32).reshape(1)
    blocks = jnp.arange(nb, dtype=jnp.int32)
    onehot_b = block_e[:, None] == jnp.arange(n_experts, dtype=jnp.int32)[None, :]
    run_end = jnp.sum(jnp.where(onehot_b, pend[None, :], 0), axis=1) // blk
    run_end = jnp.where(blocks >= n_used[0], nb, run_end)
    follow = jnp.sum(jnp.where(blocks[None, :] == run_end[:, None], block_e[None, :], 0), axis=1)
    next_e = jnp.where((run_end < nb) & (follow != block_e), follow, -1).astype(jnp.int32)
    pad_blocks = ((blocks + 1 == run_end) | (blocks >= n_used[0])).astype(jnp.int32)
    return block_e, next_e, n_used, slots.reshape(-1).astype(jnp.int32), pad_blocks


def kernel(x, p, norm_mix, w_in, shift_mu, rw_w0, rw_w2, rw_a0, rw_a2, rw_g2, rw_kk, rw_ka, rw_rk,
           rw_gn_w, rw_gn_b, gla_gk2, gla_gk_b, gla_norm, w_out, norm_ffn, w_router, b_router,
           w1, b1, w2, b2, norm_ple, w_ple_gate, w_ple, norm_final):
    bsz, seq, d = x.shape
    n = bsz * seq
    depth = w_in.shape[0]
    n_experts = w_router.shape[-1]
    tm = 256
    tm_route = min(512, n)
    tm_route = min(tm_route, seq)
    h = x
    for l in range(depth):
        w_rw = w_in[l][:, :RW_COLS].astype(BF16)
        w_gla = jnp.pad(w_in[l][:, RW_COLS:], ((0, 0), (0, LANES - GLA_GATE_RANK))).astype(BF16)
        rw_proj, gla_proj = _in_proj(h, norm_mix[l].reshape(1, d), w_rw, w_gla, tm_route)
        y_rw = _rwkv(rw_proj, shift_mu[l], rw_w0[l], rw_w2[l], rw_a0[l], rw_a2[l],
                     rw_g2[l], rw_kk[l], rw_ka[l], rw_rk[l], rw_gn_w[l], rw_gn_b[l])
        y_gla = _gla(gla_proj, gla_gk2[l], gla_gk_b[l], gla_norm[l])

        wr = jnp.pad(w_router[l], ((0, 0), (0, LANES - n_experts)))
        wr_hi = wr.astype(BF16)
        wr_lo = (wr - wr_hi.astype(F32)).astype(BF16)
        br = jnp.pad(b_router[l], (0, LANES - n_experts), constant_values=NEG_BIG).reshape(1, LANES)
        wo = w_out[l].astype(BF16)
        h1, u3, route, counts = _out_proj(y_rw, y_gla, h, wo[:RW_WIDTH],
                                          wo[RW_WIDTH:], norm_ffn[l].reshape(1, d), wr_hi, wr_lo, br, tm_route)

        block_e, next_e, n_used, slots, pad_blocks = _routing(route, counts, n_experts)
        xs = _dispatch(slots, pad_blocks, u3, tm_route)
        ys = _moe_experts(block_e, next_e, n_used, xs, w1[l], b1[l], w2[l], b2[l])
        h = _final(slots, ys, h1, route, p[l].reshape(n, -1), norm_ple[l].reshape(1, d),
                   w_ple_gate[l].astype(BF16), w_ple[l].astype(BF16), norm_final.reshape(1, d), tm,
                   l == depth - 1).reshape(bsz, seq, d)
    return h
```
